```python
import jax, jax.numpy as jnp
from jax import lax
import numpy as np

D_MODEL = 1024
BATCH = 8
SEQ = 4096
DEPTH = 1

CONV_WIDTH = 1024
CONV_KERNEL = 31
N_HEADS = 8
QK_NOPE_DIM = 128
QK_ROPE_DIM = 64
V_DIM = 128
Q_LORA_RANK = 256
KV_LORA_RANK = 256
ATTN_WIDTH = N_HEADS * V_DIM
ROPE_THETA = 10000.0
Q_BLOCK = 128
EPS = 1e-6

IN_SIZES = (
    CONV_WIDTH,
    CONV_WIDTH,
    CONV_WIDTH,
    Q_LORA_RANK,
    KV_LORA_RANK,
    QK_ROPE_DIM,
    ATTN_WIDTH,
    D_MODEL,
    D_MODEL,
)
IN_COLS = 3 * CONV_WIDTH + Q_LORA_RANK + KV_LORA_RANK + QK_ROPE_DIM + ATTN_WIDTH + 2 * D_MODEL

kernel_name = 'hybrid_conformer_mla_block'


def rms_norm(x, w):
    xf = x.astype(jnp.float32)
    y = xf * lax.rsqrt(jnp.mean(xf * xf, axis=-1, keepdims=True) + EPS)
    return (y * w.astype(jnp.float32)).astype(x.dtype)


def layer_norm(x, w, b):
    xf = x.astype(jnp.float32)
    mu = jnp.mean(xf, axis=-1, keepdims=True)
    xc = xf - mu
    var = jnp.mean(xc * xc, axis=-1, keepdims=True)
    y = xc * lax.rsqrt(var + EPS)
    return (y * w.astype(jnp.float32) + b.astype(jnp.float32)).astype(x.dtype)


def apply_rope(x, cos, sin):
    x1, x2 = jnp.split(x, 2, axis=-1)
    return jnp.concatenate([x1 * cos - x2 * sin, x1 * sin + x2 * cos], axis=-1).astype(x.dtype)


def causal_depthwise_conv(u, w, b):
    k, ch = w.shape
    out = lax.conv_general_dilated(
        u, w[:, None, :].astype(u.dtype), window_strides=(1,), padding=[(k - 1, 0)],
        dimension_numbers=('NWC', 'WIO', 'NWC'), feature_group_count=ch)
    return out + b


def mla_attention(q_nope, q_rope, k_nope, k_rope, v):
    b, s, h, _ = q_nope.shape
    nb = s // Q_BLOCK
    scale = (QK_NOPE_DIM + QK_ROPE_DIM) ** -0.5
    qn = q_nope.reshape(b, nb, Q_BLOCK, h, QK_NOPE_DIM).transpose(1, 0, 2, 3, 4)
    qr = q_rope.reshape(b, nb, Q_BLOCK, h, QK_ROPE_DIM).transpose(1, 0, 2, 3, 4)
    key_idx = jnp.arange(s)

    def block(args):
        qn_b, qr_b, i = args
        scores = (jnp.einsum('bqhd,bkhd->bhqk', qn_b, k_nope)
                  + jnp.einsum('bqhd,bkd->bhqk', qr_b, k_rope)).astype(jnp.float32) * scale
        q_idx = i * Q_BLOCK + jnp.arange(Q_BLOCK)
        mask = key_idx[None, :] <= q_idx[:, None]
        scores = jnp.where(mask[None, None], scores, -jnp.inf)
        p = jax.nn.softmax(scores, axis=-1).astype(v.dtype)
        return jnp.einsum('bhqk,bkhd->bqhd', p, v)

    out = lax.map(block, (qn, qr, jnp.arange(nb)))
    return out.transpose(1, 0, 2, 3, 4).reshape(b, s, h * V_DIM)


def _fwd_setup_inputs(seed: int = 0) -> dict:
    key = jax.random.key(seed)
    ks = jax.random.split(key, 24)
    f32 = jnp.float32

    def nrm(k, shape, fan_in, mult=1.0):
        return jax.random.normal(k, shape, f32) * (mult * fan_in ** -0.5)

    def gain(k, shape):
        return 1.0 + 0.02 * jax.random.normal(k, shape, f32)

    x = jax.random.normal(ks[0], (BATCH, SEQ, D_MODEL), f32)
    c = jax.random.normal(ks[1], (BATCH, D_MODEL), f32)
    offsets = jax.random.randint(ks[2], (BATCH, 1), 0, 2048, jnp.int32)
    positions = (offsets + jnp.arange(SEQ, dtype=jnp.int32)[None, :]).astype(jnp.int32)
    return {
        'x': x,
        'c': c,
        'positions': positions,
        'w_ada': nrm(ks[3], (DEPTH, D_MODEL, 3 * D_MODEL), D_MODEL, 0.5),
        'b_ada': 0.02 * jax.random.normal(ks[4], (DEPTH, 3 * D_MODEL), f32),
        'norm_w': gain(ks[5], (DEPTH, D_MODEL)),
        'w_in': nrm(ks[6], (DEPTH, D_MODEL, IN_COLS), D_MODEL),
        'conv_w': nrm(ks[7], (DEPTH, CONV_KERNEL, CONV_WIDTH), CONV_KERNEL),
        'conv_b': 0.02 * jax.random.normal(ks[8], (DEPTH, CONV_WIDTH), f32),
        'conv_ln_w': gain(ks[9], (DEPTH, CONV_WIDTH)),
        'conv_ln_b': 0.02 * jax.random.normal(ks[10], (DEPTH, CONV_WIDTH), f32),
        'w_conv_out': nrm(ks[11], (DEPTH, CONV_WIDTH, D_MODEL), CONV_WIDTH),
        'q_norm_w': gain(ks[12], (DEPTH, Q_LORA_RANK)),
        'w_uq': nrm(ks[13], (DEPTH, Q_LORA_RANK, N_HEADS * (QK_NOPE_DIM + QK_ROPE_DIM)), Q_LORA_RANK),
        'kv_norm_w': gain(ks[14], (DEPTH, KV_LORA_RANK)),
        'w_ukv': nrm(ks[15], (DEPTH, KV_LORA_RANK, N_HEADS * (QK_NOPE_DIM + V_DIM)), KV_LORA_RANK),
        'w_attn_out': nrm(ks[16], (DEPTH, ATTN_WIDTH, D_MODEL), ATTN_WIDTH),
        'w_out': nrm(ks[17], (DEPTH, D_MODEL, D_MODEL), D_MODEL),
        'final_norm_w': gain(ks[18], (D_MODEL,)),
    }


def _fwd_reference(x, c, positions, w_ada, b_ada, norm_w, w_in, conv_w, conv_b, conv_ln_w,
              conv_ln_b, w_conv_out, q_norm_w, w_uq, kv_norm_w, w_ukv, w_attn_out,
              w_out, final_norm_w):
    b, s, _ = x.shape
    split_points = []
    acc = 0
    for sz in IN_SIZES[:-1]:
        acc += sz
        split_points.append(acc)

    inv_freq = ROPE_THETA ** (-jnp.arange(0, QK_ROPE_DIM, 2, dtype=jnp.float32) / QK_ROPE_DIM)
    ang = positions.astype(jnp.float32)[..., None] * inv_freq
    cos, sin = jnp.cos(ang), jnp.sin(ang)

    c_act = jax.nn.silu(c)
    for l in range(DEPTH):
        shift, scale, gate = jnp.split(c_act @ w_ada[l] + b_ada[l], 3, axis=-1)
        h = rms_norm(x, norm_w[l]) * (1.0 + scale[:, None, :]) + shift[:, None, :]

        proj = h @ w_in[l]
        a_val, a_glu, a_gate, cq, ckv, k_pe, b_gate, g_a, g_b = jnp.split(proj, split_points, axis=-1)

        u = a_val * jax.nn.sigmoid(a_glu)
        u = causal_depthwise_conv(u, conv_w[l], conv_b[l])
        u = jax.nn.silu(layer_norm(u, conv_ln_w[l], conv_ln_b[l]))
        y_a = (u * jax.nn.silu(a_gate)) @ w_conv_out[l]

        q = (rms_norm(cq, q_norm_w[l]) @ w_uq[l]).reshape(b, s, N_HEADS, QK_NOPE_DIM + QK_ROPE_DIM)
        q_nope, q_rope = q[..., :QK_NOPE_DIM], q[..., QK_NOPE_DIM:]
        kv = (rms_norm(ckv, kv_norm_w[l]) @ w_ukv[l]).reshape(b, s, N_HEADS, QK_NOPE_DIM + V_DIM)
        k_nope, v = kv[..., :QK_NOPE_DIM], kv[..., QK_NOPE_DIM:]
        q_rope = apply_rope(q_rope, cos[:, :, None, :], sin[:, :, None, :])
        k_rope = apply_rope(k_pe, cos, sin)
        o = mla_attention(q_nope, q_rope, k_nope, k_rope, v)
        y_b = (o * jax.nn.silu(b_gate)) @ w_attn_out[l]

        merged = jax.nn.sigmoid(g_a) * y_a + jax.nn.sigmoid(g_b) * y_b
        x = x + gate[:, None, :] * (merged @ w_out[l])

    return rms_norm(x, final_norm_w)


import jax as _jax
import jax.numpy as _jnp

TWIN_FORMAT = 'train_step'
FWD_PARAMS = ['x', 'c', 'positions', 'w_ada', 'b_ada', 'norm_w', 'w_in', 'conv_w', 'conv_b', 'conv_ln_w', 'conv_ln_b', 'w_conv_out', 'q_norm_w', 'w_uq', 'kv_norm_w', 'w_ukv', 'w_attn_out', 'w_out', 'final_norm_w']
TWIN_WEIGHTS = ['w_ada', 'b_ada', 'norm_w', 'w_in', 'conv_w', 'conv_b', 'conv_ln_w', 'conv_ln_b', 'w_conv_out', 'q_norm_w', 'w_uq', 'kv_norm_w', 'w_ukv', 'w_attn_out', 'w_out', 'final_norm_w']
TWIN_DIFF_INPUT = 'x'
TWIN_INPUTS = ['x', 'c', 'positions', 'w_ada', 'b_ada', 'norm_w', 'w_in', 'conv_w', 'conv_b', 'conv_ln_w', 'conv_ln_b', 'w_conv_out', 'q_norm_w', 'w_uq', 'kv_norm_w', 'w_ukv', 'w_attn_out', 'w_out', 'final_norm_w', 'loss_target', 'm_w_ada', 'm_b_ada', 'm_norm_w', 'm_w_in', 'm_conv_w', 'm_conv_b', 'm_conv_ln_w', 'm_conv_ln_b', 'm_w_conv_out', 'm_q_norm_w', 'm_w_uq', 'm_kv_norm_w', 'm_w_ukv', 'm_w_attn_out', 'm_w_out', 'm_final_norm_w', 'v_w_ada', 'v_b_ada', 'v_norm_w', 'v_w_in', 'v_conv_w', 'v_conv_b', 'v_conv_ln_w', 'v_conv_ln_b', 'v_w_conv_out', 'v_q_norm_w', 'v_w_uq', 'v_kv_norm_w', 'v_w_ukv', 'v_w_attn_out', 'v_w_out', 'v_final_norm_w']
TWIN_OUTPUTS = ['loss', 'grad_x', 'grad_w_ada', 'grad_b_ada', 'grad_norm_w', 'grad_w_in', 'grad_conv_w', 'grad_conv_b', 'grad_conv_ln_w', 'grad_conv_ln_b', 'grad_w_conv_out', 'grad_q_norm_w', 'grad_w_uq', 'grad_kv_norm_w', 'grad_w_ukv', 'grad_w_attn_out', 'grad_w_out', 'grad_final_norm_w', 'delta_w_ada', 'delta_b_ada', 'delta_norm_w', 'delta_w_in', 'delta_conv_w', 'delta_conv_b', 'delta_conv_ln_w', 'delta_conv_ln_b', 'delta_w_conv_out', 'delta_q_norm_w', 'delta_w_uq', 'delta_kv_norm_w', 'delta_w_ukv', 'delta_w_attn_out', 'delta_w_out', 'delta_final_norm_w', 'new_m_w_ada', 'new_m_b_ada', 'new_m_norm_w', 'new_m_w_in', 'new_m_conv_w', 'new_m_conv_b', 'new_m_conv_ln_w', 'new_m_conv_ln_b', 'new_m_w_conv_out', 'new_m_q_norm_w', 'new_m_w_uq', 'new_m_kv_norm_w', 'new_m_w_ukv', 'new_m_w_attn_out', 'new_m_w_out', 'new_m_final_norm_w', 'new_v_w_ada', 'new_v_b_ada', 'new_v_norm_w', 'new_v_w_in', 'new_v_conv_w', 'new_v_conv_b', 'new_v_conv_ln_w', 'new_v_conv_ln_b', 'new_v_w_conv_out', 'new_v_q_norm_w', 'new_v_w_uq', 'new_v_kv_norm_w', 'new_v_w_ukv', 'new_v_w_attn_out', 'new_v_w_out', 'new_v_final_norm_w']
TWIN_LEAF_KINDS = {'loss': 'loss', 'grad_x': 'grad_x', 'grad_w_ada': 'grad_w', 'grad_b_ada': 'grad_w', 'grad_norm_w': 'grad_w', 'grad_w_in': 'grad_w', 'grad_conv_w': 'grad_w', 'grad_conv_b': 'grad_w', 'grad_conv_ln_w': 'grad_w', 'grad_conv_ln_b': 'grad_w', 'grad_w_conv_out': 'grad_w', 'grad_q_norm_w': 'grad_w', 'grad_w_uq': 'grad_w', 'grad_kv_norm_w': 'grad_w', 'grad_w_ukv': 'grad_w', 'grad_w_attn_out': 'grad_w', 'grad_w_out': 'grad_w', 'grad_final_norm_w': 'grad_w', 'delta_w_ada': 'delta_w', 'delta_b_ada': 'delta_w', 'delta_norm_w': 'delta_w', 'delta_w_in': 'delta_w', 'delta_conv_w': 'delta_w', 'delta_conv_b': 'delta_w', 'delta_conv_ln_w': 'delta_w', 'delta_conv_ln_b': 'delta_w', 'delta_w_conv_out': 'delta_w', 'delta_q_norm_w': 'delta_w', 'delta_w_uq': 'delta_w', 'delta_kv_norm_w': 'delta_w', 'delta_w_ukv': 'delta_w', 'delta_w_attn_out': 'delta_w', 'delta_w_out': 'delta_w', 'delta_final_norm_w': 'delta_w', 'new_m_w_ada': 'new_m', 'new_m_b_ada': 'new_m', 'new_m_norm_w': 'new_m', 'new_m_w_in': 'new_m', 'new_m_conv_w': 'new_m', 'new_m_conv_b': 'new_m', 'new_m_conv_ln_w': 'new_m', 'new_m_conv_ln_b': 'new_m', 'new_m_w_conv_out': 'new_m', 'new_m_q_norm_w': 'new_m', 'new_m_w_uq': 'new_m', 'new_m_kv_norm_w': 'new_m', 'new_m_w_ukv': 'new_m', 'new_m_w_attn_out': 'new_m', 'new_m_w_out': 'new_m', 'new_m_final_norm_w': 'new_m', 'new_v_w_ada': 'new_v', 'new_v_b_ada': 'new_v', 'new_v_norm_w': 'new_v', 'new_v_w_in': 'new_v', 'new_v_conv_w': 'new_v', 'new_v_conv_b': 'new_v', 'new_v_conv_ln_w': 'new_v', 'new_v_conv_ln_b': 'new_v', 'new_v_w_conv_out': 'new_v', 'new_v_q_norm_w': 'new_v', 'new_v_w_uq': 'new_v', 'new_v_kv_norm_w': 'new_v', 'new_v_w_ukv': 'new_v', 'new_v_w_attn_out': 'new_v', 'new_v_w_out': 'new_v', 'new_v_final_norm_w': 'new_v'}


def _forward(args):
    return _fwd_reference(*[args[k] for k in FWD_PARAMS])


def _output_shape():
    def fwd():
        inp = _fwd_setup_inputs(0)
        return _fwd_reference(*[inp[k] for k in FWD_PARAMS])
    out = _jax.eval_shape(fwd)
    return out.shape, out.dtype

N_MICROBATCH = 1
ADAM_LR = 0.001
ADAM_B1 = 0.9
ADAM_B2 = 0.999
ADAM_EPS = 1e-08
ADAM_WD = 0.01
ADAM_STEP = 10
PER_EXAMPLE_BATCH_AXIS = {'x': 0, 'c': 0, 'positions': 0, 'loss_target': 0}
SHARED_INPUTS = []
_WEIGHT_DTYPES = {'w_ada': _jnp.float32, 'b_ada': _jnp.float32, 'norm_w': _jnp.float32, 'w_in': _jnp.float32, 'conv_w': _jnp.float32, 'conv_b': _jnp.float32, 'conv_ln_w': _jnp.float32, 'conv_ln_b': _jnp.float32, 'w_conv_out': _jnp.float32, 'q_norm_w': _jnp.float32, 'w_uq': _jnp.float32, 'kv_norm_w': _jnp.float32, 'w_ukv': _jnp.float32, 'w_attn_out': _jnp.float32, 'w_out': _jnp.float32, 'final_norm_w': _jnp.float32}
MOMENT_SCALE = {'w_ada': 1.896416e-02, 'b_ada': 3.140904e-02, 'norm_w': 1.866853e-02, 'w_in': 8.357197e-03, 'conv_w': 1.201337e-02, 'conv_b': 2.239675e-02, 'conv_ln_w': 1.407663e-02, 'conv_ln_b': 1.213207e-02, 'w_conv_out': 1.157047e-02, 'q_norm_w': 6.844030e-03, 'w_uq': 2.781546e-03, 'kv_norm_w': 1.568786e-02, 'w_ukv': 5.111653e-03, 'w_attn_out': 6.777557e-03, 'w_out': 1.341560e-02, 'final_norm_w': 3.201216e+01}


def _to_microbatches(a, axis):
    t = _jnp.moveaxis(a, axis, 0)
    t = t.reshape((N_MICROBATCH, t.shape[0] // N_MICROBATCH) + t.shape[1:])
    return _jnp.moveaxis(t, 1, axis + 1)


def setup_inputs(seed: int = 0) -> dict:
    inp = _fwd_setup_inputs(seed)
    key = _jax.random.fold_in(_jax.random.key(seed), 7919)
    shape, _ = _output_shape()
    out = dict(inp)
    out["loss_target"] = _jax.random.normal(_jax.random.fold_in(key, 0), shape, _jnp.float32)
    for i, name in enumerate(TWIN_WEIGHTS):
        w = inp[name].astype(_jnp.float32)
        if MOMENT_SCALE is None:
            s = _jnp.sqrt(_jnp.mean(_jnp.square(w)) + 1e-30)
        else:
            s = MOMENT_SCALE[name]
        km, kv = _jax.random.split(_jax.random.fold_in(key, i + 1))
        out[name] = w
        out["m_" + name] = s * _jax.random.normal(km, w.shape, _jnp.float32)
        out["v_" + name] = (s * s) * _jax.random.uniform(kv, w.shape, _jnp.float32, 0.5, 1.5)
    if N_MICROBATCH > 1:
        for name, axis in PER_EXAMPLE_BATCH_AXIS.items():
            out[name] = _to_microbatches(out[name], axis)
    return {'x': out['x'], 'c': out['c'], 'positions': out['positions'], 'w_ada': out['w_ada'], 'b_ada': out['b_ada'], 'norm_w': out['norm_w'], 'w_in': out['w_in'], 'conv_w': out['conv_w'], 'conv_b': out['conv_b'], 'conv_ln_w': out['conv_ln_w'], 'conv_ln_b': out['conv_ln_b'], 'w_conv_out': out['w_conv_out'], 'q_norm_w': out['q_norm_w'], 'w_uq': out['w_uq'], 'kv_norm_w': out['kv_norm_w'], 'w_ukv': out['w_ukv'], 'w_attn_out': out['w_attn_out'], 'w_out': out['w_out'], 'final_norm_w': out['final_norm_w'], 'loss_target': out['loss_target'], 'm_w_ada': out['m_w_ada'], 'm_b_ada': out['m_b_ada'], 'm_norm_w': out['m_norm_w'], 'm_w_in': out['m_w_in'], 'm_conv_w': out['m_conv_w'], 'm_conv_b': out['m_conv_b'], 'm_conv_ln_w': out['m_conv_ln_w'], 'm_conv_ln_b': out['m_conv_ln_b'], 'm_w_conv_out': out['m_w_conv_out'], 'm_q_norm_w': out['m_q_norm_w'], 'm_w_uq': out['m_w_uq'], 'm_kv_norm_w': out['m_kv_norm_w'], 'm_w_ukv': out['m_w_ukv'], 'm_w_attn_out': out['m_w_attn_out'], 'm_w_out': out['m_w_out'], 'm_final_norm_w': out['m_final_norm_w'], 'v_w_ada': out['v_w_ada'], 'v_b_ada': out['v_b_ada'], 'v_norm_w': out['v_norm_w'], 'v_w_in': out['v_w_in'], 'v_conv_w': out['v_conv_w'], 'v_conv_b': out['v_conv_b'], 'v_conv_ln_w': out['v_conv_ln_w'], 'v_conv_ln_b': out['v_conv_ln_b'], 'v_w_conv_out': out['v_w_conv_out'], 'v_q_norm_w': out['v_q_norm_w'], 'v_w_uq': out['v_w_uq'], 'v_kv_norm_w': out['v_kv_norm_w'], 'v_w_ukv': out['v_w_ukv'], 'v_w_attn_out': out['v_w_attn_out'], 'v_w_out': out['v_w_out'], 'v_final_norm_w': out['v_final_norm_w']}


def _loss(weights, diff, rest, loss_target):
    with _jax.named_scope("forward"):
        args = {**rest, TWIN_DIFF_INPUT: diff, **{k: w.astype(_WEIGHT_DTYPES[k]) for k, w in weights.items()}}
        y = _forward(args)
    with _jax.named_scope("loss_head"):
        err = _jnp.square(y.astype(_jnp.float32) - loss_target)
        return 0.5 * _jnp.sum(_jnp.mean(err, axis=-1)) if err.ndim else 0.5 * err


def _adamw(w, g, m, v):
    m = ADAM_B1 * m + (1.0 - ADAM_B1) * g
    v = ADAM_B2 * v + (1.0 - ADAM_B2) * _jnp.square(g)
    m_hat = m / (1.0 - ADAM_B1 ** ADAM_STEP)
    v_hat = v / (1.0 - ADAM_B2 ** ADAM_STEP)
    delta = -ADAM_LR * (m_hat / (_jnp.sqrt(v_hat) + ADAM_EPS) + ADAM_WD * w)
    return delta, m, v


def reference(x, c, positions, w_ada, b_ada, norm_w, w_in, conv_w, conv_b, conv_ln_w, conv_ln_b, w_conv_out, q_norm_w, w_uq, kv_norm_w, w_ukv, w_attn_out, w_out, final_norm_w, loss_target, m_w_ada, m_b_ada, m_norm_w, m_w_in, m_conv_w, m_conv_b, m_conv_ln_w, m_conv_ln_b, m_w_conv_out, m_q_norm_w, m_w_uq, m_kv_norm_w, m_w_ukv, m_w_attn_out, m_w_out, m_final_norm_w, v_w_ada, v_b_ada, v_norm_w, v_w_in, v_conv_w, v_conv_b, v_conv_ln_w, v_conv_ln_b, v_w_conv_out, v_q_norm_w, v_w_uq, v_kv_norm_w, v_w_ukv, v_w_attn_out, v_w_out, v_final_norm_w):
    given = dict(x=x, c=c, positions=positions, w_ada=w_ada, b_ada=b_ada, norm_w=norm_w, w_in=w_in, conv_w=conv_w, conv_b=conv_b, conv_ln_w=conv_ln_w, conv_ln_b=conv_ln_b, w_conv_out=w_conv_out, q_norm_w=q_norm_w, w_uq=w_uq, kv_norm_w=kv_norm_w, w_ukv=w_ukv, w_attn_out=w_attn_out, w_out=w_out, final_norm_w=final_norm_w, loss_target=loss_target, m_w_ada=m_w_ada, m_b_ada=m_b_ada, m_norm_w=m_norm_w, m_w_in=m_w_in, m_conv_w=m_conv_w, m_conv_b=m_conv_b, m_conv_ln_w=m_conv_ln_w, m_conv_ln_b=m_conv_ln_b, m_w_conv_out=m_w_conv_out, m_q_norm_w=m_q_norm_w, m_w_uq=m_w_uq, m_kv_norm_w=m_kv_norm_w, m_w_ukv=m_w_ukv, m_w_attn_out=m_w_attn_out, m_w_out=m_w_out, m_final_norm_w=m_final_norm_w, v_w_ada=v_w_ada, v_b_ada=v_b_ada, v_norm_w=v_norm_w, v_w_in=v_w_in, v_conv_w=v_conv_w, v_conv_b=v_conv_b, v_conv_ln_w=v_conv_ln_w, v_conv_ln_b=v_conv_ln_b, v_w_conv_out=v_w_conv_out, v_q_norm_w=v_q_norm_w, v_w_uq=v_w_uq, v_kv_norm_w=v_kv_norm_w, v_w_ukv=v_w_ukv, v_w_attn_out=v_w_attn_out, v_w_out=v_w_out, v_final_norm_w=v_final_norm_w)
    weights = {n: given[n] for n in TWIN_WEIGHTS}
    shared = {n: given[n] for n in SHARED_INPUTS}
    per_example = {n: given[n] for n in ['x', 'c', 'positions']}
    grad_fn = _jax.value_and_grad(_loss, argnums=(0, 1))

    def one_microbatch(ex, loss_target):
        ex = dict(ex)
        diff = ex.pop(TWIN_DIFF_INPUT)
        return grad_fn(weights, diff, {**shared, **ex}, loss_target)

    if N_MICROBATCH == 1:
        loss, (grad_w, grad_x) = one_microbatch(per_example, given["loss_target"])
    else:
        def body(carry, xs):
            loss_sum, grad_sum = carry
            l_k, (gw_k, gx_k) = one_microbatch(xs[0], xs[1])
            with _jax.named_scope("update"):
                return (loss_sum + l_k, _jax.tree.map(_jnp.add, grad_sum, gw_k)), gx_k

        init = (_jnp.zeros((), _jnp.float32), _jax.tree.map(_jnp.zeros_like, weights))
        (loss, grad_w), grad_x = _jax.lax.scan(body, init, (per_example, given["loss_target"]))
    with _jax.named_scope("update"):
        delta_w, new_m, new_v = {}, {}, {}
        for n in TWIN_WEIGHTS:
            delta_w[n], new_m[n], new_v[n] = _adamw(weights[n], grad_w[n], given["m_" + n], given["v_" + n])
    return (loss, grad_x, *[grad_w[n] for n in TWIN_WEIGHTS], *[delta_w[n] for n in TWIN_WEIGHTS],
            *[new_m[n] for n in TWIN_WEIGHTS], *[new_v[n] for n in TWIN_WEIGHTS])
```

```python
import functools

import jax
import jax.numpy as jnp
from jax import lax
from jax.experimental import pallas as pl
from jax.experimental.pallas import tpu as pltpu

F32 = jnp.float32
BF16 = jnp.bfloat16

D = 1024
NH = 8
NOPE = 128
ROPE = 64
HALF = ROPE // 2
DQK = NOPE + ROPE
DV = 128
QL = 256
KVL = 256
KCONV = 31
KPAD = 32
HALO = 32
IN_COLS = 6720
MLA_COLS = QL + KVL + ROPE
PROJ_COLS = 7 * D
EPS = 1e-6
ROPE_THETA = 10000.0
N_DEV = 8

ADAM_LR = 0.001
ADAM_B1 = 0.9
ADAM_B2 = 0.999
ADAM_EPS = 1e-08
ADAM_WD = 0.01
ADAM_STEP = 10

ROWS_IN = 840
ROWS_SQ = 128
ROWS_UQ = 48
ROWS_UKV = 64
ROWS_CW = 4
ROWS_PACK = 1344
SMALL_SIZES = (3 * D, D, D, D, D, D, QL, KVL)
SMALL_COLS = 1152
SMALL_LEN = 8 * SMALL_COLS

MESH = pl.DeviceIdType.MESH
ANY = pl.BlockSpec(memory_space=pl.ANY)
V7X_VMEM_LIMIT = 56 * 1024 * 1024


def _cparams(n_axes, vmem=V7X_VMEM_LIMIT):
    return pltpu.CompilerParams(dimension_semantics=("arbitrary",) * n_axes, vmem_limit_bytes=vmem)


def _sig(x):
    return jax.nn.sigmoid(x)


def _nt(a, b):
    return lax.dot_general(a, b, (((1,), (1,)), ((), ())), preferred_element_type=F32)


def _tn(a, b):
    return lax.dot_general(a, b, (((0,), (0,)), ((), ())), preferred_element_type=F32)


def _nn(a, b):
    return jnp.dot(a, b, preferred_element_type=F32)


def _const(shape):
    return pl.BlockSpec(shape, lambda *_: (0,) * len(shape))


def _all_gather(block, name):
    r, c = block.shape

    def body(x_ref, out_ref, send_sems, recv_sems, local_sem):
        x, y, cc = lax.axis_index("x"), lax.axis_index("y"), lax.axis_index("c")
        me, sibling = (x, y, cc), (x, y, 1 - cc)
        chips = [(1 - x, y), (x, 1 - y), (1 - x, 1 - y)]

        def slot(px, py, pc):
            return out_ref.at[4 * px + 2 * py + pc]

        def copy(k, blk, to, src=None):
            return pltpu.make_async_remote_copy(
                src_ref=slot(*blk) if src is None else src, dst_ref=slot(*blk),
                send_sem=send_sems.at[k], recv_sem=recv_sems.at[k],
                device_id=to, device_id_type=MESH)

        mine = pltpu.make_async_copy(x_ref, slot(*me), local_sem)
        mine.start()
        first = [copy(0, me, sibling, src=x_ref)]
        first += [copy(1 + j, me, (*chip, cc), src=x_ref) for j, chip in enumerate(chips)]
        for cp in first:
            cp.start()
        passed = [copy(4 + j, (*chip, cc), sibling) for j, chip in enumerate(chips)]
        for j, chip in enumerate(chips):
            copy(1 + j, (*chip, cc), me).wait_recv()
            passed[j].start()
        copy(0, sibling, me).wait_recv()
        for j, chip in enumerate(chips):
            copy(4 + j, (*chip, 1 - cc), me).wait_recv()
        for cp in first + passed:
            cp.wait_send()
        mine.wait()

    return pl.pallas_call(
        body, name=name,
        out_shape=jax.ShapeDtypeStruct((N_DEV, r, c), block.dtype),
        in_specs=[ANY], out_specs=ANY,
        scratch_shapes=[pltpu.SemaphoreType.DMA((7,)), pltpu.SemaphoreType.DMA((7,)), pltpu.SemaphoreType.DMA],
    )(block)


def _grad_exchange(packed):
    _, r, c = packed.shape

    def body(src_ref, out_ref, send_sems, recv_sems, local_sem):
        x, y, cc = lax.axis_index("x"), lax.axis_index("y"), lax.axis_index("c")
        me = 4 * x + 2 * y + cc
        mine = pltpu.make_async_copy(src_ref.at[me], out_ref.at[me], local_sem)
        mine.start()
        copies = []
        for k in range(1, N_DEV):
            px = 1 - x if k & 4 else x
            py = 1 - y if k & 2 else y
            pc = 1 - cc if k & 1 else cc
            peer = 4 * px + 2 * py + pc
            copies.append(pltpu.make_async_remote_copy(
                src_ref=src_ref.at[peer], dst_ref=out_ref.at[me],
                send_sem=send_sems.at[k - 1], recv_sem=recv_sems.at[k - 1],
                device_id=(px, py, pc), device_id_type=MESH))
        for cp in copies:
            cp.start()
        for k in range(1, N_DEV):
            px = 1 - x if k & 4 else x
            py = 1 - y if k & 2 else y
            pc = 1 - cc if k & 1 else cc
            peer = 4 * px + 2 * py + pc
            pltpu.make_async_remote_copy(
                src_ref=src_ref.at[me], dst_ref=out_ref.at[peer],
                send_sem=send_sems.at[k - 1], recv_sem=recv_sems.at[k - 1],
                device_id=(px, py, pc), device_id_type=MESH).wait_recv()
        for cp in copies:
            cp.wait_send()
        mine.wait()

    return pl.pallas_call(
        body, name="grad_exchange",
        out_shape=jax.ShapeDtypeStruct(packed.shape, packed.dtype),
        in_specs=[ANY], out_specs=ANY,
        scratch_shapes=[pltpu.SemaphoreType.DMA((7,)), pltpu.SemaphoreType.DMA((7,)), pltpu.SemaphoreType.DMA],
    )(packed)


def _adam(g, w, m, v):
    m = ADAM_B1 * m + (1.0 - ADAM_B1) * g
    v = ADAM_B2 * v + (1.0 - ADAM_B2) * (g * g)
    m_hat = m / (1.0 - ADAM_B1 ** ADAM_STEP)
    v_hat = v / (1.0 - ADAM_B2 ** ADAM_STEP)
    delta = -ADAM_LR * (m_hat / (jnp.sqrt(v_hat) + ADAM_EPS) + ADAM_WD * w)
    return delta, m, v


def _reduce_adam(parts, w, m, v, name, tr):
    _, r, c = parts.shape

    def body(p_ref, w_ref, m_ref, v_ref, g_out, d_out, m_out, v_out):
        g = p_ref[0]
        for j in range(1, N_DEV):
            g = g + p_ref[j]
        delta, nm, nv = _adam(g, w_ref[...], m_ref[...], v_ref[...])
        g_out[...] = g
        d_out[...] = delta
        m_out[...] = nm
        v_out[...] = nv

    row = pl.BlockSpec((tr, c), lambda i: (i, 0))
    return pl.pallas_call(
        body, name=name, grid=(r // tr,),
        in_specs=[pl.BlockSpec((N_DEV, tr, c), lambda i: (0, i, 0)), row, row, row],
        out_specs=[row] * 4,
        out_shape=[jax.ShapeDtypeStruct((r, c), F32)] * 4,
        compiler_params=_cparams(1),
    )(parts, w, m, v)


def _ada_mod(c_all, w_ada, b_cols):
    def body(c_ref, w_ref, b_ref, o_ref):
        cv = c_ref[...]
        act = (cv * _sig(cv)).astype(BF16)
        o_ref[...] = _nn(act, w_ref[...].astype(BF16)) + b_ref[...]

    return pl.pallas_call(body, name="ada_mod", out_shape=jax.ShapeDtypeStruct((N_DEV, w_ada.shape[1]), F32))(c_all, w_ada, b_cols)


def _ada_bwd(c_all, dmod_cols, w, m, v):
    def body(c_ref, d_ref, w_ref, m_ref, v_ref, g_out, d_out, m_out, v_out):
        cv = c_ref[...]
        act = (cv * _sig(cv)).astype(BF16)
        g = _tn(act, d_ref[...].astype(BF16))
        delta, nm, nv = _adam(g, w_ref[...], m_ref[...], v_ref[...])
        g_out[...] = g
        d_out[...] = delta
        m_out[...] = nm
        v_out[...] = nv

    return pl.pallas_call(body, name="ada_bwd", out_shape=[jax.ShapeDtypeStruct(w.shape, F32)] * 4)(c_all, dmod_cols, w, m, v)


def _in_proj(x, norm_w, mod8, wp):
    s = x.shape[0]
    tm = min(1024, s)
    nk = wp.shape[1] // D

    def body(x_ref, nw_ref, mod_ref, w_ref, proj_ref, h_ref, hs_ref):
        @pl.when(pl.program_id(1) == 0)
        def _():
            xv = x_ref[...]
            rstd = lax.rsqrt(jnp.mean(xv * xv, axis=-1, keepdims=True) + EPS)
            h = (xv * rstd) * nw_ref[...] * (1.0 + mod_ref[1:2, :]) + mod_ref[0:1, :]
            hs_ref[...] = h.astype(BF16)
            h_ref[...] = hs_ref[...]

        proj_ref[...] = _nn(hs_ref[...], w_ref[...])

    return pl.pallas_call(
        body, name="in_proj", grid=(s // tm, nk),
        in_specs=[pl.BlockSpec((tm, D), lambda i, k: (i, 0)), _const((1, D)), _const((8, D)),
                  pl.BlockSpec((D, D), lambda i, k: (0, k))],
        out_specs=[pl.BlockSpec((tm, D), lambda i, k: (i, k)), pl.BlockSpec((tm, D), lambda i, k: (i, 0))],
        out_shape=[jax.ShapeDtypeStruct((s, nk * D), F32), jax.ShapeDtypeStruct((s, D), BF16)],
        scratch_shapes=[pltpu.VMEM((tm, D), BF16)],
        compiler_params=_cparams(2),
    )(x, norm_w, mod8, wp)


CONV_RC = 64
CONV_LC = 256


def _conv_fwd(proj, conv_w, conv_b, ln_w, ln_b):
    s = proj.shape[0]
    tm = min(256, s)
    hb = tm // HALO

    def body(av_ref, ag_ref, avh_ref, agh_ref, gate_ref, cw_ref, cb_ref, lw_ref, lb_ref, u1_ref, ya_ref, win_ref):
        i = pl.program_id(0)
        halo = avh_ref[...] * _sig(agh_ref[...])
        win_ref[0:HALO, :] = jnp.where(i > 0, halo, 0.0)
        win_ref[HALO:HALO + tm, :] = av_ref[...] * _sig(ag_ref[...])
        for r0 in range(0, tm, CONV_RC):
            for c0 in range(0, D, CONV_LC):
                acc = jnp.zeros((CONV_RC, CONV_LC), F32) + cb_ref[:, c0:c0 + CONV_LC]
                for j in range(KCONV):
                    acc = acc + win_ref[pl.ds(r0 + HALO - (KCONV - 1) + j, CONV_RC), pl.ds(c0, CONV_LC)] * cw_ref[j:j + 1, c0:c0 + CONV_LC]
                u1_ref[r0:r0 + CONV_RC, c0:c0 + CONV_LC] = acc
        u1 = u1_ref[...]
        mu = jnp.mean(u1, axis=-1, keepdims=True)
        xc = u1 - mu
        var = jnp.mean(xc * xc, axis=-1, keepdims=True)
        ln = xc * lax.rsqrt(var + EPS) * lw_ref[...] + lb_ref[...]
        gate = gate_ref[...]
        ya_ref[...] = ((ln * _sig(ln)) * (gate * _sig(gate))).astype(BF16)

    row = lambda k: pl.BlockSpec((tm, D), lambda i: (i, k))
    prev = lambda k: pl.BlockSpec((HALO, D), lambda i: (jnp.maximum(i * hb - 1, 0), k))
    return pl.pallas_call(
        body, name="conv_fwd", grid=(s // tm,),
        in_specs=[row(0), row(1), prev(0), prev(1), row(2), _const((KPAD, D)), _const((1, D)), _const((1, D)), _const((1, D))],
        out_specs=[pl.BlockSpec((tm, D), lambda i: (i, 0))] * 2,
        out_shape=[jax.ShapeDtypeStruct((s, D), F32), jax.ShapeDtypeStruct((s, D), BF16)],
        scratch_shapes=[pltpu.VMEM((tm + HALO, D), F32)],
        compiler_params=_cparams(1),
    )(proj, proj, proj, proj, proj, conv_w, conv_b, ln_w, ln_b)


def _rope_tables(pos_ref, if_ref):
    ang = pos_ref[...].astype(F32) * if_ref[...]
    return jnp.cos(ang), jnp.sin(ang)


def _rms_parts(x):
    rstd = lax.rsqrt(jnp.mean(x * x, axis=-1, keepdims=True) + EPS)
    return x * rstd, rstd


def _mla_prep(proj, pos, inv_freq, qnw, kvnw, wuq, wukv):
    s = proj.shape[0]
    tm = min(512, s)

    def body(p_ref, pos_ref, if_ref, qnw_ref, kvnw_ref, wuq_ref, wukv_ref, q_ref, k_ref, v_ref):
        blk = p_ref[...]
        cos, sin = _rope_tables(pos_ref, if_ref)

        def rope(r):
            x1, x2 = r[:, :HALF], r[:, HALF:]
            return jnp.concatenate([x1 * cos - x2 * sin, x1 * sin + x2 * cos], axis=-1)

        qlat = _rms_parts(blk[:, :QL])[0] * qnw_ref[...]
        kvlat = _rms_parts(blk[:, QL:QL + KVL])[0] * kvnw_ref[...]
        q = _nn(qlat.astype(BF16), wuq_ref[...])
        kv = _nn(kvlat.astype(BF16), wukv_ref[...])
        kr = rope(blk[:, QL + KVL:MLA_COLS])
        for h in range(NH):
            qh = q[:, h * DQK:(h + 1) * DQK]
            q_ref[h] = jnp.concatenate([qh[:, :NOPE], rope(qh[:, NOPE:])], axis=-1).astype(BF16)
            k_ref[h] = jnp.concatenate([kv[:, h * 256:h * 256 + NOPE], kr], axis=-1).astype(BF16)
            v_ref[h] = kv[:, h * 256 + NOPE:(h + 1) * 256].astype(BF16)

    hm = lambda d: pl.BlockSpec((NH, tm, d), lambda i: (0, i, 0))
    return pl.pallas_call(
        body, name="mla_prep", grid=(s // tm,),
        in_specs=[pl.BlockSpec((tm, D), lambda i: (i, 3)), pl.BlockSpec((tm, 1), lambda i: (i, 0)), _const((1, HALF)),
                  _const((1, QL)), _const((1, KVL)), _const((QL, NH * DQK)), _const((KVL, NH * 256))],
        out_specs=[hm(DQK), hm(DQK), hm(DV)],
        out_shape=[jax.ShapeDtypeStruct((NH, s, DQK), BF16), jax.ShapeDtypeStruct((NH, s, DQK), BF16),
                   jax.ShapeDtypeStruct((NH, s, DV), BF16)],
        compiler_params=_cparams(1),
    )(proj, pos, inv_freq, qnw, kvnw, wuq, wukv)


ATTN_SCALE = DQK ** -0.5


def _causal_mask(s, t):
    rows = lax.broadcasted_iota(jnp.int32, (t, t), 0)
    cols = lax.broadcasted_iota(jnp.int32, (t, t), 1)
    return jnp.where(cols <= rows, s, -jnp.inf)


def _attn_fwd(q, k, v):
    nh, s, _ = q.shape
    t = min(512, s // 2)
    nb = s // t

    def body(q_ref, k_ref, v_ref, o_ref, lse_ref, m_ref, l_ref, acc_ref):
        qi, kj = pl.program_id(1), pl.program_id(2)

        @pl.when(kj == 0)
        def _():
            m_ref[...] = jnp.full(m_ref.shape, -jnp.inf, F32)
            l_ref[...] = jnp.zeros(l_ref.shape, F32)
            acc_ref[...] = jnp.zeros(acc_ref.shape, F32)

        def step(diag):
            sc = _nt(q_ref[0], k_ref[0]) * ATTN_SCALE
            if diag:
                sc = _causal_mask(sc, t)
            m_prev = m_ref[...]
            m_new = jnp.maximum(m_prev, jnp.max(sc, axis=-1, keepdims=True))
            alpha = jnp.exp(m_prev - m_new)
            p = jnp.exp(sc - m_new)
            l_ref[...] = alpha * l_ref[...] + jnp.sum(p, axis=-1, keepdims=True)
            acc_ref[...] = alpha * acc_ref[...] + _nn(p.astype(BF16), v_ref[0])
            m_ref[...] = m_new

        @pl.when(kj < qi)
        def _():
            step(False)

        @pl.when(kj == qi)
        def _():
            step(True)
            l = l_ref[...]
            o_ref[...] = acc_ref[...] / l
            lse_ref[0] = jnp.broadcast_to(m_ref[...] + jnp.log(l), (t, DV))

    kv_map = lambda h, i, j: (h, jnp.minimum(j, i), 0)
    return pl.pallas_call(
        body, name="attn_fwd", grid=(nh, nb, nb),
        in_specs=[pl.BlockSpec((1, t, DQK), lambda h, i, j: (h, i, 0)), pl.BlockSpec((1, t, DQK), kv_map),
                  pl.BlockSpec((1, t, DV), kv_map)],
        out_specs=[pl.BlockSpec((t, DV), lambda h, i, j: (i, h)), pl.BlockSpec((1, t, DV), lambda h, i, j: (h, i, 0))],
        out_shape=[jax.ShapeDtypeStruct((s, nh * DV), F32), jax.ShapeDtypeStruct((nh, s, DV), F32)],
        scratch_shapes=[pltpu.VMEM((t, 1), F32), pltpu.VMEM((t, 1), F32), pltpu.VMEM((t, DV), F32)],
        compiler_params=_cparams(3),
    )(q, k, v)


def _merge_loss(x, target, ya, o, proj, mod8, fnw, wco, wao, wout):
    s = x.shape[0]
    tm = min(256, s)
    n = s // tm

    def body(x_ref, t_ref, ya_ref, o_ref, bg_ref, ga_ref, gb_ref, mod_ref, fnw_ref, wco_ref, wao_ref, wout_ref,
             dya_ref, do_ref, delta_ref, dpg_ref, dx2_ref, gw_ref, small_ref, acc_ref):
        i = pl.program_id(0)

        @pl.when(i == 0)
        def _():
            acc_ref[...] = jnp.zeros(acc_ref.shape, F32)
            small_ref[...] = jnp.zeros(small_ref.shape, F32)

        bg = bg_ref[...]
        sbg = _sig(bg)
        sb = bg * sbg
        ov = o_ref[...]
        ya = ya_ref[...]
        yb = (ov * sb).astype(BF16)
        y_a = _nn(ya, wco_ref[...])
        y_b = _nn(yb, wao_ref[...])
        sa = _sig(ga_ref[...])
        sgb = _sig(gb_ref[...])
        merged = (sa * y_a + sgb * y_b).astype(BF16)
        z = _nn(merged, wout_ref[...])
        gate = mod_ref[2:3, :]
        x2 = x_ref[...] + gate * z
        xn, rstd = _rms_parts(x2)
        fnw = fnw_ref[...]
        err = xn * fnw - t_ref[...]
        loss = jnp.sum(jnp.sum(err * err, axis=-1, keepdims=True), axis=0, keepdims=True) * (0.5 / D)
        dy = err * (1.0 / D)
        small_ref[0:1, :] += jnp.sum(dy * xn, axis=0, keepdims=True)
        dxn = dy * fnw
        dx2 = rstd * (dxn - xn * jnp.mean(dxn * xn, axis=-1, keepdims=True))
        dx2_ref[...] = dx2
        small_ref[1:2, :] += jnp.sum(dx2 * z, axis=0, keepdims=True)
        small_ref[2:3, :] += jnp.broadcast_to(loss, (1, D))
        dz = (dx2 * gate).astype(BF16)
        dmerged = _nt(dz, wout_ref[...])
        acc_ref[2] += _tn(merged, dz)
        dy_a = (dmerged * sa).astype(BF16)
        dy_b = (dmerged * sgb).astype(BF16)
        dpg_ref[:, D:2 * D] = (dmerged * y_a * (sa * (1.0 - sa))).astype(BF16)
        dpg_ref[:, 2 * D:3 * D] = (dmerged * y_b * (sgb * (1.0 - sgb))).astype(BF16)
        dya_ref[...] = _nt(dy_a, wco_ref[...])
        acc_ref[0] += _tn(ya, dy_a)
        dyb = _nt(dy_b, wao_ref[...])
        acc_ref[1] += _tn(yb, dy_b)
        do = dyb * sb
        do_ref[...] = do.astype(BF16)
        dpg_ref[:, 0:D] = (dyb * ov * (sbg * (1.0 + bg * (1.0 - sbg)))).astype(BF16)
        prod = do * ov
        for h in range(NH):
            delta_ref[h] = jnp.broadcast_to(jnp.sum(prod[:, h * DV:(h + 1) * DV], axis=-1, keepdims=True), (tm, DV))

        @pl.when(i == n - 1)
        def _():
            pltpu.sync_copy(acc_ref, gw_ref)

    row = pl.BlockSpec((tm, D), lambda i: (i, 0))
    col = lambda k: pl.BlockSpec((tm, D), lambda i: (i, k))
    wspec = pl.BlockSpec((D, D), lambda i: (0, 0), pipeline_mode=pl.Buffered(1))
    return pl.pallas_call(
        body, name="merge_loss", grid=(n,),
        in_specs=[row, row, row, row, col(4), col(5), col(6), _const((8, D)), _const((1, D)), wspec, wspec, wspec],
        out_specs=[row, row, pl.BlockSpec((NH, tm, DV), lambda i: (0, i, 0)), pl.BlockSpec((tm, 3 * D), lambda i: (i, 0)),
                   row, ANY, _const((8, D))],
        out_shape=[jax.ShapeDtypeStruct((s, D), F32), jax.ShapeDtypeStruct((s, D), BF16),
                   jax.ShapeDtypeStruct((NH, s, DV), F32), jax.ShapeDtypeStruct((s, 3 * D), BF16),
                   jax.ShapeDtypeStruct((s, D), F32), jax.ShapeDtypeStruct((3, D, D), F32),
                   jax.ShapeDtypeStruct((8, D), F32)],
        scratch_shapes=[pltpu.VMEM((3, D, D), F32)],
        compiler_params=_cparams(1),
    )(x, target, ya, o, proj, proj, proj, mod8, fnw, wco, wao, wout)


def _attn_bwd(q, k, v, do, lse, delta):
    nh, s, _ = q.shape
    t = min(512, s // 2)
    nb = s // t

    def body(q_ref, k_ref, v_ref, do_ref, lse_ref, dl_ref, dq_ref, dk_ref, dv_ref, dk_acc, dv_acc):
        kj, qi = pl.program_id(1), pl.program_id(2)

        @pl.when((kj == 0) & (qi == 0))
        def _():
            dq_ref[...] = jnp.zeros(dq_ref.shape, F32)

        @pl.when(qi == 0)
        def _():
            dk_acc[...] = jnp.zeros(dk_acc.shape, F32)
            dv_acc[...] = jnp.zeros(dv_acc.shape, F32)

        def step(diag):
            qv, kv_, dov = q_ref[0], k_ref[0], do_ref[...]
            sc = _nt(qv, kv_) * ATTN_SCALE
            if diag:
                sc = _causal_mask(sc, t)
            p = jnp.exp(sc - lse_ref[0][:, 0:1])
            pb = p.astype(BF16)
            dv_acc[...] += _tn(pb, dov)
            dp = _nt(dov, v_ref[0])
            ds = (p * (dp - dl_ref[0][:, 0:1]) * ATTN_SCALE).astype(BF16)
            dk_acc[...] += _tn(ds, qv)
            rows = pl.ds(pl.multiple_of(qi * t, t), t)
            dq_ref[0, rows, :] += _nn(ds, kv_)

        @pl.when(qi > kj)
        def _():
            step(False)

        @pl.when(qi == kj)
        def _():
            step(True)

        @pl.when(qi == nb - 1)
        def _():
            dk_ref[0] = dk_acc[...]
            dv_ref[0] = dv_acc[...]

    qmap = lambda h, j, i: (h, jnp.maximum(i, j), 0)
    kmap = lambda h, j, i: (h, j, 0)
    return pl.pallas_call(
        body, name="attn_bwd", grid=(nh, nb, nb),
        in_specs=[pl.BlockSpec((1, t, DQK), qmap), pl.BlockSpec((1, t, DQK), kmap), pl.BlockSpec((1, t, DV), kmap),
                  pl.BlockSpec((t, DV), lambda h, j, i: (jnp.maximum(i, j), h)),
                  pl.BlockSpec((1, t, DV), qmap), pl.BlockSpec((1, t, DV), qmap)],
        out_specs=[pl.BlockSpec((1, s, DQK), lambda h, j, i: (h, 0, 0)), pl.BlockSpec((1, t, DQK), kmap),
                   pl.BlockSpec((1, t, DV), kmap)],
        out_shape=[jax.ShapeDtypeStruct((nh, s, DQK), F32), jax.ShapeDtypeStruct((nh, s, DQK), F32),
                   jax.ShapeDtypeStruct((nh, s, DV), F32)],
        scratch_shapes=[pltpu.VMEM((t, DQK), F32), pltpu.VMEM((t, DV), F32)],
        compiler_params=_cparams(3),
    )(q, k, v, do, lse, delta)


def _mla_bwd(dq, dk, dv, proj, pos, inv_freq, qnw, kvnw, wuq, wukv):
    s = proj.shape[0]
    tm = min(512, s)

    def body(dq_ref, dk_ref, dv_ref, p_ref, pos_ref, if_ref, qnw_ref, kvnw_ref, wuq_ref, wukv_ref,
             dp_ref, guq_ref, gukv_ref, small_ref):
        @pl.when(pl.program_id(0) == 0)
        def _():
            guq_ref[...] = jnp.zeros(guq_ref.shape, F32)
            gukv_ref[...] = jnp.zeros(gukv_ref.shape, F32)
            small_ref[...] = jnp.zeros(small_ref.shape, F32)

        blk = p_ref[...]
        cos, sin = _rope_tables(pos_ref, if_ref)

        def unrope(g):
            g1, g2 = g[:, :HALF], g[:, HALF:]
            return jnp.concatenate([g1 * cos + g2 * sin, g2 * cos - g1 * sin], axis=-1)

        dq_cols, dkv_cols = [], []
        dkr = jnp.zeros((tm, ROPE), F32)
        for h in range(NH):
            dqh, dkh = dq_ref[h], dk_ref[h]
            dq_cols += [dqh[:, :NOPE], unrope(dqh[:, NOPE:])]
            dkv_cols += [dkh[:, :NOPE], dv_ref[h]]
            dkr = dkr + dkh[:, NOPE:]
        dq_full = jnp.concatenate(dq_cols, axis=-1).astype(BF16)
        dkv_full = jnp.concatenate(dkv_cols, axis=-1).astype(BF16)

        def latent_bwd(c, nw_ref, d_up, w_ref, g_ref, srow):
            nrm, rstd = _rms_parts(c)
            nw = nw_ref[...]
            lat = (nrm * nw).astype(BF16)
            g_ref[...] += _tn(lat, d_up)
            dlat = _nt(d_up, w_ref[...])
            small_ref[srow:srow + 1, :] += jnp.sum(dlat * nrm, axis=0, keepdims=True)
            dn = dlat * nw
            return rstd * (dn - nrm * jnp.mean(dn * nrm, axis=-1, keepdims=True))

        dcq = latent_bwd(blk[:, :QL], qnw_ref, dq_full, wuq_ref, guq_ref, 0)
        dckv = latent_bwd(blk[:, QL:QL + KVL], kvnw_ref, dkv_full, wukv_ref, gukv_ref, 1)
        dp_ref[...] = jnp.concatenate([dcq, dckv, unrope(dkr), jnp.zeros((tm, D - MLA_COLS), F32)], axis=-1).astype(BF16)

    hm = lambda d: pl.BlockSpec((NH, tm, d), lambda i: (0, i, 0))
    return pl.pallas_call(
        body, name="mla_bwd", grid=(s // tm,),
        in_specs=[hm(DQK), hm(DQK), hm(DV), pl.BlockSpec((tm, D), lambda i: (i, 3)), pl.BlockSpec((tm, 1), lambda i: (i, 0)),
                  _const((1, HALF)), _const((1, QL)), _const((1, KVL)), _const((QL, NH * DQK)), _const((KVL, NH * 256))],
        out_specs=[pl.BlockSpec((tm, D), lambda i: (i, 0)), _const((QL, NH * DQK)), _const((KVL, NH * 256)), _const((8, QL))],
        out_shape=[jax.ShapeDtypeStruct((s, D), BF16), jax.ShapeDtypeStruct((QL, NH * DQK), F32),
                   jax.ShapeDtypeStruct((KVL, NH * 256), F32), jax.ShapeDtypeStruct((8, QL), F32)],
        compiler_params=_cparams(1),
    )(dq, dk, dv, proj, pos, inv_freq, qnw, kvnw, wuq, wukv)


def _conv_rows_bwd(dya, u1, proj, ln_w, ln_b):
    s = dya.shape[0]
    tm = min(512, s)

    def body(dya_ref, u1_ref, gate_ref, lw_ref, lb_ref, du1_ref, dag_ref, small_ref):
        @pl.when(pl.program_id(0) == 0)
        def _():
            small_ref[...] = jnp.zeros(small_ref.shape, F32)

        u1 = u1_ref[...]
        mu = jnp.mean(u1, axis=-1, keepdims=True)
        xc = u1 - mu
        rstd = lax.rsqrt(jnp.mean(xc * xc, axis=-1, keepdims=True) + EPS)
        xhat = xc * rstd
        lw = lw_ref[...]
        ln = xhat * lw + lb_ref[...]
        sl = _sig(ln)
        u2 = ln * sl
        gate = gate_ref[...]
        sg = _sig(gate)
        dya = dya_ref[...]
        dag_ref[...] = (dya * u2 * (sg * (1.0 + gate * (1.0 - sg)))).astype(BF16)
        dln = dya * (gate * sg) * (sl * (1.0 + ln * (1.0 - sl)))
        small_ref[0:1, :] += jnp.sum(dln * xhat, axis=0, keepdims=True)
        small_ref[1:2, :] += jnp.sum(dln, axis=0, keepdims=True)
        dxh = dln * lw
        du1_ref[...] = rstd * (dxh - jnp.mean(dxh, axis=-1, keepdims=True) - xhat * jnp.mean(dxh * xhat, axis=-1, keepdims=True))

    row = pl.BlockSpec((tm, D), lambda i: (i, 0))
    return pl.pallas_call(
        body, name="conv_rows_bwd", grid=(s // tm,),
        in_specs=[row, row, pl.BlockSpec((tm, D), lambda i: (i, 2)), _const((1, D)), _const((1, D))],
        out_specs=[row, row, _const((8, D))],
        out_shape=[jax.ShapeDtypeStruct((s, D), F32), jax.ShapeDtypeStruct((s, D), BF16), jax.ShapeDtypeStruct((8, D), F32)],
        compiler_params=_cparams(1),
    )(dya, u1, proj, ln_w, ln_b)


def _conv_bwd(du1, proj, conv_w):
    s = du1.shape[0]
    tm = min(256, s)
    hb = tm // HALO
    n = s // tm
    last32 = s // HALO - 1

    def body(d_ref, dn_ref, av_ref, ag_ref, avh_ref, agh_ref, cw_ref, dp_ref, gcw_ref, small_ref, dwin_ref, uwin_ref, acc_ref):
        i = pl.program_id(0)

        @pl.when(i == 0)
        def _():
            acc_ref[...] = jnp.zeros(acc_ref.shape, F32)
            small_ref[...] = jnp.zeros(small_ref.shape, F32)

        dwin_ref[0:tm, :] = d_ref[...]
        dwin_ref[tm:tm + HALO, :] = jnp.where(i < n - 1, dn_ref[...], 0.0)
        halo = avh_ref[...] * _sig(agh_ref[...])
        uwin_ref[0:HALO, :] = jnp.where(i > 0, halo, 0.0)
        av = av_ref[...]
        sg = _sig(ag_ref[...])
        uwin_ref[HALO:HALO + tm, :] = av * sg
        small_ref[0:1, :] += jnp.sum(d_ref[...], axis=0, keepdims=True)

        for c0 in range(0, D, CONV_LC):
            lanes = pl.ds(c0, CONV_LC)
            for r0 in range(0, tm, CONV_RC):
                acc = jnp.zeros((CONV_RC, CONV_LC), F32)
                for j in range(KCONV):
                    acc = acc + dwin_ref[pl.ds(r0 + KCONV - 1 - j, CONV_RC), lanes] * cw_ref[j:j + 1, c0:c0 + CONV_LC]
                a = av[r0:r0 + CONV_RC, c0:c0 + CONV_LC]
                g = sg[r0:r0 + CONV_RC, c0:c0 + CONV_LC]
                dp_ref[r0:r0 + CONV_RC, c0:c0 + CONV_LC] = (acc * g).astype(BF16)
                dp_ref[r0:r0 + CONV_RC, D + c0:D + c0 + CONV_LC] = (acc * a * (g * (1.0 - g))).astype(BF16)
            for j in range(KCONV):
                part = jnp.zeros((8, CONV_LC), F32)
                for r0 in range(0, tm, CONV_RC):
                    prod = dwin_ref[pl.ds(r0, CONV_RC), lanes] * uwin_ref[pl.ds(r0 + HALO - (KCONV - 1) + j, CONV_RC), lanes]
                    part = part + jnp.sum(prod.reshape(CONV_RC // 8, 8, CONV_LC), axis=0)
                acc_ref[j, :, c0:c0 + CONV_LC] += part

        @pl.when(i == n - 1)
        def _():
            gcw_ref[...] = jnp.sum(acc_ref[...], axis=1)

    row = lambda k: pl.BlockSpec((tm, D), lambda i: (i, k))
    prev = lambda k: pl.BlockSpec((HALO, D), lambda i: (jnp.maximum(i * hb - 1, 0), k))
    return pl.pallas_call(
        body, name="conv_bwd", grid=(n,),
        in_specs=[row(0), pl.BlockSpec((HALO, D), lambda i: (jnp.minimum((i + 1) * hb, last32), 0)),
                  row(0), row(1), prev(0), prev(1), _const((KPAD, D))],
        out_specs=[pl.BlockSpec((tm, 2 * D), lambda i: (i, 0)), _const((KPAD, D)), _const((8, D))],
        out_shape=[jax.ShapeDtypeStruct((s, 2 * D), BF16), jax.ShapeDtypeStruct((KPAD, D), F32), jax.ShapeDtypeStruct((8, D), F32)],
        scratch_shapes=[pltpu.VMEM((tm + HALO, D), F32), pltpu.VMEM((tm + HALO, D), F32), pltpu.VMEM((KPAD, 8, D), F32)],
        compiler_params=_cparams(1),
    )(du1, du1, proj, proj, proj, proj, conv_w)


def _dproj_specs(tm, rows_first):
    def spec(lo, hi):
        def idx(a, b):
            i, k = (a, b) if rows_first else (b, a)
            col = jnp.clip(k - lo, 0, hi - lo - 1)
            if rows_first:
                return (i, col)
            return (jnp.where((k >= lo) & (k < hi), i, 0), col)
        return pl.BlockSpec((tm, D), idx)
    return [spec(0, 2), spec(2, 3), spec(3, 4), spec(4, 7)]


def _pick_dproj(k, refs, fn):
    vg, ag, mla, gates = refs

    @pl.when(k < 2)
    def _():
        fn(vg)

    @pl.when(k == 2)
    def _():
        fn(ag)

    @pl.when(k == 3)
    def _():
        fn(mla)

    @pl.when(k > 3)
    def _():
        fn(gates)


def _in_proj_bwd_x(dps, wp, x, dx2, norm_w, mod8):
    s = x.shape[0]
    tm = min(512, s)
    nk = wp.shape[1] // D

    def body(vg_ref, ag_ref, mla_ref, g_ref, w_ref, x_ref, dx2_ref, nw_ref, mod_ref, gx_ref, small_ref, acc_ref):
        i, k = pl.program_id(0), pl.program_id(1)

        @pl.when((i == 0) & (k == 0))
        def _():
            small_ref[...] = jnp.zeros(small_ref.shape, F32)

        @pl.when(k == 0)
        def _():
            acc_ref[...] = jnp.zeros(acc_ref.shape, F32)

        def add(ref):
            acc_ref[...] += _nt(ref[...], w_ref[...])

        _pick_dproj(k, (vg_ref, ag_ref, mla_ref, g_ref), add)

        @pl.when(k == nk - 1)
        def _():
            dh = acc_ref[...]
            xn, rstd = _rms_parts(x_ref[...])
            nw = nw_ref[...]
            hn = xn * nw
            small_ref[0:1, :] += jnp.sum(dh, axis=0, keepdims=True)
            small_ref[1:2, :] += jnp.sum(dh * hn, axis=0, keepdims=True)
            dhn = dh * (1.0 + mod_ref[1:2, :])
            small_ref[2:3, :] += jnp.sum(dhn * xn, axis=0, keepdims=True)
            dxn = dhn * nw
            gx_ref[...] = rstd * (dxn - xn * jnp.mean(dxn * xn, axis=-1, keepdims=True)) + dx2_ref[...]

    row = pl.BlockSpec((tm, D), lambda i, k: (i, 0))
    return pl.pallas_call(
        body, name="in_proj_bwd_x", grid=(s // tm, nk),
        in_specs=_dproj_specs(tm, True) + [pl.BlockSpec((D, D), lambda i, k: (0, k)), row, row, _const((1, D)), _const((8, D))],
        out_specs=[row, _const((8, D))],
        out_shape=[jax.ShapeDtypeStruct((s, D), F32), jax.ShapeDtypeStruct((8, D), F32)],
        scratch_shapes=[pltpu.VMEM((tm, D), F32)],
        compiler_params=_cparams(2),
    )(*dps, wp, x, dx2, norm_w, mod8)


def _in_proj_bwd_w(dps, h, nk):
    s = h.shape[0]
    tm = min(512, s)

    def body(vg_ref, ag_ref, mla_ref, g_ref, h_ref, gw_ref):
        k, i = pl.program_id(0), pl.program_id(1)

        @pl.when(i == 0)
        def _():
            gw_ref[...] = jnp.zeros(gw_ref.shape, F32)

        def add(ref):
            gw_ref[...] += _tn(h_ref[...], ref[...])

        _pick_dproj(k, (vg_ref, ag_ref, mla_ref, g_ref), add)

    return pl.pallas_call(
        body, name="in_proj_bwd_w", grid=(nk, s // tm),
        in_specs=_dproj_specs(tm, False) + [pl.BlockSpec((tm, D), lambda k, i: (i, 0))],
        out_specs=pl.BlockSpec((D, D), lambda k, i: (0, k)),
        out_shape=jax.ShapeDtypeStruct((D, nk * D), F32),
        compiler_params=_cparams(2),
    )(*dps, h)


def _pack_shard(w_in, wco, wao, wout, wuq, wukv, conv_w):
    cw = jnp.pad(conv_w.reshape(-1), (0, ROWS_CW * D - KCONV * 128)).reshape(ROWS_CW, D)
    parts = [w_in.reshape(ROWS_IN, D), wco, wao, wout, wuq.reshape(ROWS_UQ, D), wukv.reshape(ROWS_UKV, D), cw,
             jnp.zeros((ROWS_PACK - ROWS_IN - 3 * ROWS_SQ - ROWS_UQ - ROWS_UKV - ROWS_CW, D), w_in.dtype)]
    return jnp.concatenate(parts, axis=0)


def _split_rows(slab):
    out, r = [], 0
    for n in (ROWS_IN, ROWS_SQ, ROWS_SQ, ROWS_SQ, ROWS_UQ, ROWS_UKV, ROWS_CW):
        out.append(slab[..., r:r + n, :])
        r += n
    return out


def _unpack_shard(slab):
    w_in, wco, wao, wout, wuq, wukv, cw = _split_rows(slab)
    return (w_in.reshape(D, ROWS_IN), wco, wao, wout, wuq.reshape(QL, NH * DQK // N_DEV), wukv.reshape(KVL, NH * 256 // N_DEV),
            cw.reshape(-1)[:KCONV * 128].reshape(KCONV, 128))


def _unpack_gathered(g):
    w_in, wco, wao, wout, wuq, wukv, cw = _split_rows(g)
    w_in = w_in.reshape(N_DEV, D, ROWS_IN).transpose(1, 0, 2).reshape(D, IN_COLS)
    wuq = wuq.reshape(N_DEV, QL, NH * DQK // N_DEV).transpose(1, 0, 2).reshape(QL, NH * DQK)
    wukv = wukv.reshape(N_DEV, KVL, NH * 256 // N_DEV).transpose(1, 0, 2).reshape(KVL, NH * 256)
    cw = cw.reshape(N_DEV, ROWS_CW * D)[:, :KCONV * 128].reshape(N_DEV, KCONV, 128).transpose(1, 0, 2).reshape(KCONV, D)
    return w_in, wco.reshape(D, D), wao.reshape(D, D), wout.reshape(D, D), wuq, wukv, cw


def _pack_grads(g_in, gco, gao, gout, guq, gukv, gcw):
    g_in = g_in.reshape(D, N_DEV, ROWS_IN).transpose(1, 0, 2).reshape(N_DEV, ROWS_IN, D)
    guq = guq.reshape(QL, N_DEV, -1).transpose(1, 0, 2).reshape(N_DEV, ROWS_UQ, D)
    gukv = gukv.reshape(KVL, N_DEV, -1).transpose(1, 0, 2).reshape(N_DEV, ROWS_UKV, D)
    gcw = gcw.reshape(KCONV, N_DEV, 128).transpose(1, 0, 2).reshape(N_DEV, KCONV * 128)
    gcw = jnp.pad(gcw, ((0, 0), (0, ROWS_CW * D - KCONV * 128))).reshape(N_DEV, ROWS_CW, D)
    sq = lambda a: a.reshape(N_DEV, ROWS_SQ, D)
    pad = jnp.zeros((N_DEV, ROWS_PACK - ROWS_IN - 3 * ROWS_SQ - ROWS_UQ - ROWS_UKV - ROWS_CW, D), F32)
    return jnp.concatenate([g_in, sq(gco), sq(gao), sq(gout), guq, gukv, gcw, pad], axis=1)


def _pack_small(vecs):
    flat = jnp.concatenate([v.reshape(-1) for v in vecs])
    return jnp.pad(flat, (0, SMALL_LEN - flat.shape[0])).reshape(8, SMALL_COLS)


def _unpack_small(a, shapes):
    flat = a.reshape(-1)
    out, off = [], 0
    for shp, n in zip(shapes, SMALL_SIZES):
        out.append(flat[off:off + n].reshape(shp))
        off += n
    return out


def kernel(x, c, positions, w_ada, b_ada, norm_w, w_in, conv_w, conv_b, conv_ln_w, conv_ln_b, w_conv_out, q_norm_w, w_uq, kv_norm_w, w_ukv, w_attn_out, w_out, final_norm_w, loss_target, m_w_ada, m_b_ada, m_norm_w, m_w_in, m_conv_w, m_conv_b, m_conv_ln_w, m_conv_ln_b, m_w_conv_out, m_q_norm_w, m_w_uq, m_kv_norm_w, m_w_ukv, m_w_attn_out, m_w_out, m_final_norm_w, v_w_ada, v_b_ada, v_norm_w, v_w_in, v_conv_w, v_conv_b, v_conv_ln_w, v_conv_ln_b, v_w_conv_out, v_q_norm_w, v_w_uq, v_kv_norm_w, v_w_ukv, v_w_attn_out, v_w_out, v_final_norm_w):
    me = 4 * lax.axis_index("x") + 2 * lax.axis_index("y") + lax.axis_index("c")
    xs, tgt = x[0], loss_target[0]
    s = xs.shape[0]
    ada_cols = w_ada.shape[2]

    sharded = lambda t: tuple(a[0] for a in t)
    w_sh = sharded((w_in, w_conv_out, w_attn_out, w_out, w_uq, w_ukv, conv_w))
    gathered = _all_gather(_pack_shard(*w_sh).astype(BF16), "gather_weights")
    win_f, wco, wao, wout, wuq, wukv, cw = _unpack_gathered(gathered)
    split = 3 * D + MLA_COLS
    wp = jnp.concatenate([win_f[:, :split], jnp.zeros((D, D - MLA_COLS), BF16), win_f[:, split:]], axis=1)
    cw32 = jnp.pad(cw.astype(F32), ((0, KPAD - KCONV), (0, 0)))

    c_all = _all_gather(jnp.broadcast_to(c, (8, D)), "gather_c")[:, 0, :]
    b_cols = lax.dynamic_slice(b_ada, (0, me * ada_cols), (1, ada_cols))
    mod_cols = _all_gather(_ada_mod(c_all, w_ada[0], b_cols), "gather_mod")
    mod = lax.dynamic_index_in_dim(mod_cols, me, axis=1, keepdims=False).reshape(3, D)
    mod8 = jnp.pad(mod, ((0, 5), (0, 0)))

    pos = positions.reshape(s, 1)
    inv_freq = (ROPE_THETA ** (-jnp.arange(0, ROPE, 2, dtype=F32) / ROPE)).reshape(1, HALF)
    proj, h = _in_proj(xs, norm_w, mod8, wp)
    u1, ya = _conv_fwd(proj, cw32, conv_b, conv_ln_w, conv_ln_b)
    q, k, v = _mla_prep(proj, pos, inv_freq, q_norm_w, kv_norm_w, wuq, wukv)
    o, lse = _attn_fwd(q, k, v)

    dya, do, delta, dp_gates, dx2, gw3, small_a = _merge_loss(xs, tgt, ya, o, proj, mod8, final_norm_w.reshape(1, D), wco, wao, wout)
    dq, dk, dv = _attn_bwd(q, k, v, do, lse, delta)
    dp_mla, guq, gukv, small_b = _mla_bwd(dq, dk, dv, proj, pos, inv_freq, q_norm_w, kv_norm_w, wuq, wukv)
    du1, dp_ag, small_c = _conv_rows_bwd(dya, u1, proj, conv_ln_w, conv_ln_b)
    dp_vg, gcw, small_d = _conv_bwd(du1, proj, cw32)
    dps = [dp_vg, dp_ag, dp_mla, dp_gates]
    grad_x, small_e = _in_proj_bwd_x(dps, wp, xs, dx2, norm_w, mod8)
    gwp = _in_proj_bwd_w(dps, h, wp.shape[1] // D)
    g_in = jnp.concatenate([gwp[:, :split], gwp[:, 3 * D + D:]], axis=1)

    recv = _grad_exchange(_pack_grads(g_in, gw3[0], gw3[1], gw3[2], guq, gukv, gcw[:KCONV]))
    big = _reduce_adam(recv, _pack_shard(*w_sh), _pack_shard(*sharded((m_w_in, m_w_conv_out, m_w_attn_out, m_w_out, m_w_uq, m_w_ukv, m_conv_w))),
                       _pack_shard(*sharded((v_w_in, v_w_conv_out, v_w_attn_out, v_w_out, v_w_uq, v_w_ukv, v_conv_w))), "adam_sharded", ROWS_PACK // 8)
    big = [_unpack_shard(a) for a in big]

    dmod = jnp.concatenate([small_e[0], small_e[1], small_a[1]])
    payload = _pack_small([dmod, small_e[2], small_d[0], small_c[0], small_c[1], small_a[0], small_b[0], small_b[1], small_a[2, 0:1]])
    pay_all = _all_gather(payload, "gather_small")
    small_w = (b_ada, norm_w, conv_b, conv_ln_w, conv_ln_b, final_norm_w, q_norm_w, kv_norm_w)
    small_m = (m_b_ada, m_norm_w, m_conv_b, m_conv_ln_w, m_conv_ln_b, m_final_norm_w, m_q_norm_w, m_kv_norm_w)
    small_v = (v_b_ada, v_norm_w, v_conv_b, v_conv_ln_w, v_conv_ln_b, v_final_norm_w, v_q_norm_w, v_kv_norm_w)
    sm = _reduce_adam(pay_all, _pack_small(small_w), _pack_small(small_m), _pack_small(small_v), "adam_replicated", 8)
    loss = sm[0].reshape(-1)[sum(SMALL_SIZES)]
    shapes = [t.shape for t in small_w]
    sm = [_unpack_small(a, shapes) for a in sm]

    dmod_all = pay_all.reshape(N_DEV, SMALL_LEN)[:, :3 * D]
    dmod_cols = lax.dynamic_slice(dmod_all, (0, me * ada_cols), (N_DEV, ada_cols))
    ada = _ada_bwd(c_all, dmod_cols, w_ada[0], m_w_ada[0], v_w_ada[0])

    def group(i):
        b_in, b_co, b_ao, b_out, b_uq, b_ukv, b_cw = big[i]
        s_bada, s_nw, s_cb, s_clw, s_clb, s_fnw, s_qnw, s_kvnw = sm[i]
        return (ada[i][None], s_bada, s_nw, b_in[None], b_cw[None], s_cb, s_clw, s_clb, b_co[None], s_qnw, b_uq[None], s_kvnw,
                b_ukv[None], b_ao[None], b_out[None], s_fnw)

    return (loss, grad_x[None], *group(0), *group(1), *group(2), *group(3))
```

```python
import functools

import jax
import jax.numpy as jnp
from jax import lax
from jax.experimental import pallas as pl
from jax.experimental.pallas import tpu as pltpu

F32 = jnp.float32
BF16 = jnp.bfloat16

D = 1024
NH = 8
NOPE = 128
ROPE = 64
HALF = ROPE // 2
DQK = NOPE + ROPE
DV = 128
QL = 256
KVL = 256
KCONV = 31
KPAD = 32
HALO = 32
IN_COLS = 6720
MLA_COLS = QL + KVL + ROPE
PROJ_COLS = 7 * D
EPS = 1e-6
ROPE_THETA = 10000.0
N_DEV = 8

ADAM_LR = 0.001
ADAM_B1 = 0.9
ADAM_B2 = 0.999
ADAM_EPS = 1e-08
ADAM_WD = 0.01
ADAM_STEP = 10

ROWS_IN = 840
ROWS_SQ = 128
ROWS_UQ = 48
ROWS_UKV = 64
ROWS_CW = 4
ROWS_PACK = 1344
SMALL_SIZES = (3 * D, D, D, D, D, D, QL, KVL)
SMALL_COLS = 1152
SMALL_LEN = 8 * SMALL_COLS

MESH = pl.DeviceIdType.MESH
ANY = pl.BlockSpec(memory_space=pl.ANY)
V7X_VMEM_LIMIT = 56 * 1024 * 1024


def _cparams(n_axes, vmem=V7X_VMEM_LIMIT):
    return pltpu.CompilerParams(dimension_semantics=("arbitrary",) * n_axes, vmem_limit_bytes=vmem)


def _sig(x):
    return jax.nn.sigmoid(x)


def _nt(a, b):
    return lax.dot_general(a, b, (((1,), (1,)), ((), ())), preferred_element_type=F32)


def _tn(a, b):
    return lax.dot_general(a, b, (((0,), (0,)), ((), ())), preferred_element_type=F32)


def _nn(a, b):
    return jnp.dot(a, b, preferred_element_type=F32)


def _const(shape):
    return pl.BlockSpec(shape, lambda *_: (0,) * len(shape))


def _all_gather(block, name):
    r, c = block.shape

    def body(x_ref, out_ref, send_sems, recv_sems, local_sem):
        x, y, cc = lax.axis_index("x"), lax.axis_index("y"), lax.axis_index("c")
        me, sibling = (x, y, cc), (x, y, 1 - cc)
        chips = [(1 - x, y), (x, 1 - y), (1 - x, 1 - y)]

        def slot(px, py, pc):
            return out_ref.at[4 * px + 2 * py + pc]

        def copy(k, blk, to, src=None):
            return pltpu.make_async_remote_copy(
                src_ref=slot(*blk) if src is None else src, dst_ref=slot(*blk),
                send_sem=send_sems.at[k], recv_sem=recv_sems.at[k],
                device_id=to, device_id_type=MESH)

        mine = pltpu.make_async_copy(x_ref, slot(*me), local_sem)
        mine.start()
        first = [copy(0, me, sibling, src=x_ref)]
        first += [copy(1 + j, me, (*chip, cc), src=x_ref) for j, chip in enumerate(chips)]
        for cp in first:
            cp.start()
        passed = [copy(4 + j, (*chip, cc), sibling) for j, chip in enumerate(chips)]
        for j, chip in enumerate(chips):
            copy(1 + j, (*chip, cc), me).wait_recv()
            passed[j].start()
        copy(0, sibling, me).wait_recv()
        for j, chip in enumerate(chips):
            copy(4 + j, (*chip, 1 - cc), me).wait_recv()
        for cp in first + passed:
            cp.wait_send()
        mine.wait()

    return pl.pallas_call(
        body, name=name,
        out_shape=jax.ShapeDtypeStruct((N_DEV, r, c), block.dtype),
        in_specs=[ANY], out_specs=ANY,
        scratch_shapes=[pltpu.SemaphoreType.DMA((7,)), pltpu.SemaphoreType.DMA((7,)), pltpu.SemaphoreType.DMA],
    )(block)


def _grad_exchange(packed):
    _, r, c = packed.shape

    def body(src_ref, out_ref, send_sems, recv_sems, local_sem):
        x, y, cc = lax.axis_index("x"), lax.axis_index("y"), lax.axis_index("c")
        me = 4 * x + 2 * y + cc
        mine = pltpu.make_async_copy(src_ref.at[me], out_ref.at[me], local_sem)
        mine.start()
        copies = []
        for k in range(1, N_DEV):
            px = 1 - x if k & 4 else x
            py = 1 - y if k & 2 else y
            pc = 1 - cc if k & 1 else cc
            peer = 4 * px + 2 * py + pc
            copies.append(pltpu.make_async_remote_copy(
                src_ref=src_ref.at[peer], dst_ref=out_ref.at[me],
                send_sem=send_sems.at[k - 1], recv_sem=recv_sems.at[k - 1],
                device_id=(px, py, pc), device_id_type=MESH))
        for cp in copies:
            cp.start()
        for k in range(1, N_DEV):
            px = 1 - x if k & 4 else x
            py = 1 - y if k & 2 else y
            pc = 1 - cc if k & 1 else cc
            peer = 4 * px + 2 * py + pc
            pltpu.make_async_remote_copy(
                src_ref=src_ref.at[me], dst_ref=out_ref.at[peer],
                send_sem=send_sems.at[k - 1], recv_sem=recv_sems.at[k - 1],
                device_id=(px, py, pc), device_id_type=MESH).wait_recv()
        for cp in copies:
            cp.wait_send()
        mine.wait()

    return pl.pallas_call(
        body, name="grad_exchange",
        out_shape=jax.ShapeDtypeStruct(packed.shape, packed.dtype),
        in_specs=[ANY], out_specs=ANY,
        scratch_shapes=[pltpu.SemaphoreType.DMA((7,)), pltpu.SemaphoreType.DMA((7,)), pltpu.SemaphoreType.DMA],
    )(packed)


def _adam(g, w, m, v):
    m = ADAM_B1 * m + (1.0 - ADAM_B1) * g
    v = ADAM_B2 * v + (1.0 - ADAM_B2) * (g * g)
    m_hat = m / (1.0 - ADAM_B1 ** ADAM_STEP)
    v_hat = v / (1.0 - ADAM_B2 ** ADAM_STEP)
    delta = -ADAM_LR * (m_hat / (jnp.sqrt(v_hat) + ADAM_EPS) + ADAM_WD * w)
    return delta, m, v


def _reduce_adam(parts, w, m, v, name, tr):
    _, r, c = parts.shape

    def body(p_ref, w_ref, m_ref, v_ref, g_out, d_out, m_out, v_out):
        g = p_ref[0]
        for j in range(1, N_DEV):
            g = g + p_ref[j]
        delta, nm, nv = _adam(g, w_ref[...], m_ref[...], v_ref[...])
        g_out[...] = g
        d_out[...] = delta
        m_out[...] = nm
        v_out[...] = nv

    row = pl.BlockSpec((tr, c), lambda i: (i, 0))
    return pl.pallas_call(
        body, name=name, grid=(r // tr,),
        in_specs=[pl.BlockSpec((N_DEV, tr, c), lambda i: (0, i, 0)), row, row, row],
        out_specs=[row] * 4,
        out_shape=[jax.ShapeDtypeStruct((r, c), F32)] * 4,
        compiler_params=_cparams(1),
    )(parts, w, m, v)


def _ada_mod(c_all, w_ada, b_cols):
    def body(c_ref, w_ref, b_ref, o_ref):
        cv = c_ref[...]
        act = (cv * _sig(cv)).astype(BF16)
        o_ref[...] = _nn(act, w_ref[...].astype(BF16)) + b_ref[...]

    return pl.pallas_call(body, name="ada_mod", out_shape=jax.ShapeDtypeStruct((N_DEV, w_ada.shape[1]), F32))(c_all, w_ada, b_cols)


def _ada_bwd(c_all, dmod_cols, w, m, v):
    def body(c_ref, d_ref, w_ref, m_ref, v_ref, g_out, d_out, m_out, v_out):
        cv = c_ref[...]
        act = (cv * _sig(cv)).astype(BF16)
        g = _tn(act, d_ref[...].astype(BF16))
        delta, nm, nv = _adam(g, w_ref[...], m_ref[...], v_ref[...])
        g_out[...] = g
        d_out[...] = delta
        m_out[...] = nm
        v_out[...] = nv

    return pl.pallas_call(body, name="ada_bwd", out_shape=[jax.ShapeDtypeStruct(w.shape, F32)] * 4)(c_all, dmod_cols, w, m, v)


def _in_proj(x, norm_w, mod8, wp):
    s = x.shape[0]
    tm = min(1024, s)
    nk = wp.shape[1] // D

    def body(x_ref, nw_ref, mod_ref, w_ref, proj_ref, h_ref, hs_ref):
        @pl.when(pl.program_id(1) == 0)
        def _():
            xv = x_ref[...]
            rstd = lax.rsqrt(jnp.mean(xv * xv, axis=-1, keepdims=True) + EPS)
            h = (xv * rstd) * nw_ref[...] * (1.0 + mod_ref[1:2, :]) + mod_ref[0:1, :]
            hs_ref[...] = h.astype(BF16)
            h_ref[...] = hs_ref[...]

        proj_ref[...] = _nn(hs_ref[...], w_ref[...])

    return pl.pallas_call(
        body, name="in_proj", grid=(s // tm, nk),
        in_specs=[pl.BlockSpec((tm, D), lambda i, k: (i, 0)), _const((1, D)), _const((8, D)),
                  pl.BlockSpec((D, D), lambda i, k: (0, k))],
        out_specs=[pl.BlockSpec((tm, D), lambda i, k: (i, k)), pl.BlockSpec((tm, D), lambda i, k: (i, 0))],
        out_shape=[jax.ShapeDtypeStruct((s, nk * D), F32), jax.ShapeDtypeStruct((s, D), BF16)],
        scratch_shapes=[pltpu.VMEM((tm, D), BF16)],
        compiler_params=_cparams(2),
    )(x, norm_w, mod8, wp)


CONV_RC = 64
CONV_LC = 256


def _conv_fwd(proj, conv_w, conv_b, ln_w, ln_b):
    s = proj.shape[0]
    tm = min(256, s)
    hb = tm // HALO

    def body(av_ref, ag_ref, avh_ref, agh_ref, gate_ref, cw_ref, cb_ref, lw_ref, lb_ref, u1_ref, ya_ref, win_ref):
        i = pl.program_id(0)
        halo = avh_ref[...] * _sig(agh_ref[...])
        win_ref[0:HALO, :] = jnp.where(i > 0, halo, 0.0)
        win_ref[HALO:HALO + tm, :] = av_ref[...] * _sig(ag_ref[...])
        for r0 in range(0, tm, CONV_RC):
            for c0 in range(0, D, CONV_LC):
                acc = jnp.zeros((CONV_RC, CONV_LC), F32) + cb_ref[:, c0:c0 + CONV_LC]
                for j in range(KCONV):
                    acc = acc + win_ref[pl.ds(r0 + HALO - (KCONV - 1) + j, CONV_RC), pl.ds(c0, CONV_LC)] * cw_ref[j:j + 1, c0:c0 + CONV_LC]
                u1_ref[r0:r0 + CONV_RC, c0:c0 + CONV_LC] = acc
        u1 = u1_ref[...]
        mu = jnp.mean(u1, axis=-1, keepdims=True)
        xc = u1 - mu
        var = jnp.mean(xc * xc, axis=-1, keepdims=True)
        ln = xc * lax.rsqrt(var + EPS) * lw_ref[...] + lb_ref[...]
        gate = gate_ref[...]
        ya_ref[...] = ((ln * _sig(ln)) * (gate * _sig(gate))).astype(BF16)

    row = lambda k: pl.BlockSpec((tm, D), lambda i: (i, k))
    prev = lambda k: pl.BlockSpec((HALO, D), lambda i: (jnp.maximum(i * hb - 1, 0), k))
    return pl.pallas_call(
        body, name="conv_fwd", grid=(s // tm,),
        in_specs=[row(0), row(1), prev(0), prev(1), row(2), _const((KPAD, D)), _const((1, D)), _const((1, D)), _const((1, D))],
        out_specs=[pl.BlockSpec((tm, D), lambda i: (i, 0))] * 2,
        out_shape=[jax.ShapeDtypeStruct((s, D), F32), jax.ShapeDtypeStruct((s, D), BF16)],
        scratch_shapes=[pltpu.VMEM((tm + HALO, D), F32)],
        compiler_params=_cparams(1),
    )(proj, proj, proj, proj, proj, conv_w, conv_b, ln_w, ln_b)


def _rope_tables(pos_ref, if_ref):
    ang = pos_ref[...].astype(F32) * if_ref[...]
    return jnp.cos(ang), jnp.sin(ang)


def _rms_parts(x):
    rstd = lax.rsqrt(jnp.mean(x * x, axis=-1, keepdims=True) + EPS)
    return x * rstd, rstd


def _mla_prep(proj, pos, inv_freq, qnw, kvnw, wuq, wukv):
    s = proj.shape[0]
    tm = min(512, s)

    def body(p_ref, pos_ref, if_ref, qnw_ref, kvnw_ref, wuq_ref, wukv_ref, q_ref, k_ref, v_ref):
        blk = p_ref[...]
        cos, sin = _rope_tables(pos_ref, if_ref)

        def rope(r):
            x1, x2 = r[:, :HALF], r[:, HALF:]
            return jnp.concatenate([x1 * cos - x2 * sin, x1 * sin + x2 * cos], axis=-1)

        qlat = _rms_parts(blk[:, :QL])[0] * qnw_ref[...]
        kvlat = _rms_parts(blk[:, QL:QL + KVL])[0] * kvnw_ref[...]
        q = _nn(qlat.astype(BF16), wuq_ref[...])
        kv = _nn(kvlat.astype(BF16), wukv_ref[...])
        kr = rope(blk[:, QL + KVL:MLA_COLS])
        for h in range(NH):
            qh = q[:, h * DQK:(h + 1) * DQK]
            q_ref[h] = (jnp.concatenate([qh[:, :NOPE], rope(qh[:, NOPE:])], axis=-1) * ATTN_SCALE).astype(BF16)
            k_ref[h] = jnp.concatenate([kv[:, h * 256:h * 256 + NOPE], kr], axis=-1).astype(BF16)
            v_ref[h] = kv[:, h * 256 + NOPE:(h + 1) * 256].astype(BF16)

    hm = lambda d: pl.BlockSpec((NH, tm, d), lambda i: (0, i, 0))
    return pl.pallas_call(
        body, name="mla_prep", grid=(s // tm,),
        in_specs=[pl.BlockSpec((tm, D), lambda i: (i, 3)), pl.BlockSpec((tm, 1), lambda i: (i, 0)), _const((1, HALF)),
                  _const((1, QL)), _const((1, KVL)), _const((QL, NH * DQK)), _const((KVL, NH * 256))],
        out_specs=[hm(DQK), hm(DQK), hm(DV)],
        out_shape=[jax.ShapeDtypeStruct((NH, s, DQK), BF16), jax.ShapeDtypeStruct((NH, s, DQK), BF16),
                   jax.ShapeDtypeStruct((NH, s, DV), BF16)],
        compiler_params=_cparams(1),
    )(proj, pos, inv_freq, qnw, kvnw, wuq, wukv)


ATTN_SCALE = DQK ** -0.5


def _causal_mask(s, t):
    rows = lax.broadcasted_iota(jnp.int32, (t, t), 0)
    cols = lax.broadcasted_iota(jnp.int32, (t, t), 1)
    return jnp.where(cols <= rows, s, -jnp.inf)


def _attn_tile(s):
    return min(1024, s // 2)


def _attn_fwd(q, k, v):
    nh, s, _ = q.shape
    t = _attn_tile(s)

    def body(q_ref, k_ref, v_ref, o_ref, lse_ref):
        qi = pl.program_id(1)
        qv = q_ref[0]

        def chunk(c, carry, diag):
            m, l, acc = carry
            rows = pl.ds(pl.multiple_of(c * t, t), t)
            sc = _nt(qv, k_ref[0, rows, :])
            if diag:
                sc = _causal_mask(sc, t)
            m_new = jnp.maximum(m, jnp.max(sc, axis=-1, keepdims=True))
            alpha = jnp.exp(m - m_new)
            p = jnp.exp(sc - m_new)
            l = alpha * l + jnp.sum(p, axis=-1, keepdims=True)
            acc = alpha * acc + _nn(p.astype(BF16), v_ref[0, rows, :])
            return m_new, l, acc

        init = (jnp.full((t, 1), -jnp.inf, F32), jnp.zeros((t, 1), F32), jnp.zeros((t, DV), F32))
        carry = lax.fori_loop(0, qi, lambda c, cr: chunk(c, cr, False), init)
        m, l, acc = chunk(qi, carry, True)
        o_ref[...] = acc / l
        lse_ref[0] = jnp.broadcast_to(m + jnp.log(l), (t, DV))

    head = lambda d: pl.BlockSpec((1, s, d), lambda h, i: (h, 0, 0))
    return pl.pallas_call(
        body, name="attn_fwd", grid=(nh, s // t),
        in_specs=[pl.BlockSpec((1, t, DQK), lambda h, i: (h, i, 0)), head(DQK), head(DV)],
        out_specs=[pl.BlockSpec((t, DV), lambda h, i: (i, h)), pl.BlockSpec((1, t, DV), lambda h, i: (h, i, 0))],
        out_shape=[jax.ShapeDtypeStruct((s, nh * DV), F32), jax.ShapeDtypeStruct((nh, s, DV), F32)],
        compiler_params=_cparams(2),
    )(q, k, v)


def _merge_loss(x, target, ya, o, proj, mod8, fnw, wco, wao, wout):
    s = x.shape[0]
    tm = min(256, s)
    n = s // tm

    def body(x_ref, t_ref, ya_ref, o_ref, bg_ref, ga_ref, gb_ref, mod_ref, fnw_ref, wco_ref, wao_ref, wout_ref,
             dya_ref, do_ref, delta_ref, dpg_ref, dx2_ref, gw_ref, small_ref, acc_ref):
        i = pl.program_id(0)

        @pl.when(i == 0)
        def _():
            acc_ref[...] = jnp.zeros(acc_ref.shape, F32)
            small_ref[...] = jnp.zeros(small_ref.shape, F32)

        bg = bg_ref[...]
        sbg = _sig(bg)
        sb = bg * sbg
        ov = o_ref[...]
        ya = ya_ref[...]
        yb = (ov * sb).astype(BF16)
        y_a = _nn(ya, wco_ref[...])
        y_b = _nn(yb, wao_ref[...])
        sa = _sig(ga_ref[...])
        sgb = _sig(gb_ref[...])
        merged = (sa * y_a + sgb * y_b).astype(BF16)
        z = _nn(merged, wout_ref[...])
        gate = mod_ref[2:3, :]
        x2 = x_ref[...] + gate * z
        xn, rstd = _rms_parts(x2)
        fnw = fnw_ref[...]
        err = xn * fnw - t_ref[...]
        loss = jnp.sum(jnp.sum(err * err, axis=-1, keepdims=True), axis=0, keepdims=True) * (0.5 / D)
        dy = err * (1.0 / D)
        small_ref[0:1, :] += jnp.sum(dy * xn, axis=0, keepdims=True)
        dxn = dy * fnw
        dx2 = rstd * (dxn - xn * jnp.mean(dxn * xn, axis=-1, keepdims=True))
        dx2_ref[...] = dx2
        small_ref[1:2, :] += jnp.sum(dx2 * z, axis=0, keepdims=True)
        small_ref[2:3, :] += jnp.broadcast_to(loss, (1, D))
        dz = (dx2 * gate).astype(BF16)
        dmerged = _nt(dz, wout_ref[...])
        acc_ref[2] += _tn(merged, dz)
        dy_a = (dmerged * sa).astype(BF16)
        dy_b = (dmerged * sgb).astype(BF16)
        dpg_ref[:, D:2 * D] = (dmerged * y_a * (sa * (1.0 - sa))).astype(BF16)
        dpg_ref[:, 2 * D:3 * D] = (dmerged * y_b * (sgb * (1.0 - sgb))).astype(BF16)
        dya_ref[...] = _nt(dy_a, wco_ref[...])
        acc_ref[0] += _tn(ya, dy_a)
        dyb = _nt(dy_b, wao_ref[...])
        acc_ref[1] += _tn(yb, dy_b)
        do = dyb * sb
        do_ref[...] = do.astype(BF16)
        dpg_ref[:, 0:D] = (dyb * ov * (sbg * (1.0 + bg * (1.0 - sbg)))).astype(BF16)
        prod = do * ov
        for h in range(NH):
            delta_ref[h] = jnp.broadcast_to(jnp.sum(prod[:, h * DV:(h + 1) * DV], axis=-1, keepdims=True), (tm, DV))

        @pl.when(i == n - 1)
        def _():
            pltpu.sync_copy(acc_ref, gw_ref)

    row = pl.BlockSpec((tm, D), lambda i: (i, 0))
    col = lambda k: pl.BlockSpec((tm, D), lambda i: (i, k))
    wspec = pl.BlockSpec((D, D), lambda i: (0, 0), pipeline_mode=pl.Buffered(1))
    return pl.pallas_call(
        body, name="merge_loss", grid=(n,),
        in_specs=[row, row, row, row, col(4), col(5), col(6), _const((8, D)), _const((1, D)), wspec, wspec, wspec],
        out_specs=[row, row, pl.BlockSpec((NH, tm, DV), lambda i: (0, i, 0)), pl.BlockSpec((tm, 3 * D), lambda i: (i, 0)),
                   row, ANY, _const((8, D))],
        out_shape=[jax.ShapeDtypeStruct((s, D), F32), jax.ShapeDtypeStruct((s, D), BF16),
                   jax.ShapeDtypeStruct((NH, s, DV), F32), jax.ShapeDtypeStruct((s, 3 * D), BF16),
                   jax.ShapeDtypeStruct((s, D), F32), jax.ShapeDtypeStruct((3, D, D), F32),
                   jax.ShapeDtypeStruct((8, D), F32)],
        scratch_shapes=[pltpu.VMEM((3, D, D), F32)],
        compiler_params=_cparams(1),
    )(x, target, ya, o, proj, proj, proj, mod8, fnw, wco, wao, wout)


def _attn_bwd(q, k, v, do, lse, delta):
    nh, s, _ = q.shape
    t = _attn_tile(s)
    nb = s // t

    def body(q_ref, k_ref, v_ref, do_ref, lse_ref, dl_ref, dq_ref, dk_ref, dv_ref):
        kj = pl.program_id(1)

        @pl.when(kj == 0)
        def _():
            dq_ref[...] = jnp.zeros(dq_ref.shape, F32)

        kv_, vv = k_ref[0], v_ref[0]

        def chunk(c, carry, diag):
            dk, dv = carry
            rows = pl.ds(pl.multiple_of(c * t, t), t)
            qv = q_ref[0, rows, :]
            dov = do_ref[rows, :]
            sc = _nt(qv, kv_)
            if diag:
                sc = _causal_mask(sc, t)
            p = jnp.exp(sc - lse_ref[0, rows, 0:1])
            dv = dv + _tn(p.astype(BF16), dov)
            dp = _nt(dov, vv)
            ds = (p * (dp - dl_ref[0, rows, 0:1])).astype(BF16)
            dk = dk + _tn(ds, qv)
            dq_ref[0, rows, :] += _nn(ds, kv_)
            return dk, dv

        carry = chunk(kj, (jnp.zeros((t, DQK), F32), jnp.zeros((t, DV), F32)), True)
        dk, dv = lax.fori_loop(kj + 1, nb, lambda c, cr: chunk(c, cr, False), carry)
        dk_ref[0] = dk
        dv_ref[0] = dv

    head = lambda d: pl.BlockSpec((1, s, d), lambda h, j: (h, 0, 0))
    blk = lambda d: pl.BlockSpec((1, t, d), lambda h, j: (h, j, 0))
    return pl.pallas_call(
        body, name="attn_bwd", grid=(nh, nb),
        in_specs=[head(DQK), blk(DQK), blk(DV), pl.BlockSpec((s, DV), lambda h, j: (0, h)), head(DV), head(DV)],
        out_specs=[head(DQK), blk(DQK), blk(DV)],
        out_shape=[jax.ShapeDtypeStruct((nh, s, DQK), F32), jax.ShapeDtypeStruct((nh, s, DQK), F32),
                   jax.ShapeDtypeStruct((nh, s, DV), F32)],
        compiler_params=_cparams(2),
    )(q, k, v, do, lse, delta)


def _mla_bwd(dq, dk, dv, proj, pos, inv_freq, qnw, kvnw, wuq, wukv):
    s = proj.shape[0]
    tm = min(512, s)

    def body(dq_ref, dk_ref, dv_ref, p_ref, pos_ref, if_ref, qnw_ref, kvnw_ref, wuq_ref, wukv_ref,
             dp_ref, guq_ref, gukv_ref, small_ref):
        @pl.when(pl.program_id(0) == 0)
        def _():
            guq_ref[...] = jnp.zeros(guq_ref.shape, F32)
            gukv_ref[...] = jnp.zeros(gukv_ref.shape, F32)
            small_ref[...] = jnp.zeros(small_ref.shape, F32)

        blk = p_ref[...]
        cos, sin = _rope_tables(pos_ref, if_ref)

        def unrope(g):
            g1, g2 = g[:, :HALF], g[:, HALF:]
            return jnp.concatenate([g1 * cos + g2 * sin, g2 * cos - g1 * sin], axis=-1)

        dq_cols, dkv_cols = [], []
        dkr = jnp.zeros((tm, ROPE), F32)
        for h in range(NH):
            dqh, dkh = dq_ref[h] * ATTN_SCALE, dk_ref[h]
            dq_cols += [dqh[:, :NOPE], unrope(dqh[:, NOPE:])]
            dkv_cols += [dkh[:, :NOPE], dv_ref[h]]
            dkr = dkr + dkh[:, NOPE:]
        dq_full = jnp.concatenate(dq_cols, axis=-1).astype(BF16)
        dkv_full = jnp.concatenate(dkv_cols, axis=-1).astype(BF16)

        def latent_bwd(c, nw_ref, d_up, w_ref, g_ref, srow):
            nrm, rstd = _rms_parts(c)
            nw = nw_ref[...]
            lat = (nrm * nw).astype(BF16)
            g_ref[...] += _tn(lat, d_up)
            dlat = _nt(d_up, w_ref[...])
            small_ref[srow:srow + 1, :] += jnp.sum(dlat * nrm, axis=0, keepdims=True)
            dn = dlat * nw
            return rstd * (dn - nrm * jnp.mean(dn * nrm, axis=-1, keepdims=True))

        dcq = latent_bwd(blk[:, :QL], qnw_ref, dq_full, wuq_ref, guq_ref, 0)
        dckv = latent_bwd(blk[:, QL:QL + KVL], kvnw_ref, dkv_full, wukv_ref, gukv_ref, 1)
        dp_ref[...] = jnp.concatenate([dcq, dckv, unrope(dkr), jnp.zeros((tm, D - MLA_COLS), F32)], axis=-1).astype(BF16)

    hm = lambda d: pl.BlockSpec((NH, tm, d), lambda i: (0, i, 0))
    return pl.pallas_call(
        body, name="mla_bwd", grid=(s // tm,),
        in_specs=[hm(DQK), hm(DQK), hm(DV), pl.BlockSpec((tm, D), lambda i: (i, 3)), pl.BlockSpec((tm, 1), lambda i: (i, 0)),
                  _const((1, HALF)), _const((1, QL)), _const((1, KVL)), _const((QL, NH * DQK)), _const((KVL, NH * 256))],
        out_specs=[pl.BlockSpec((tm, D), lambda i: (i, 0)), _const((QL, NH * DQK)), _const((KVL, NH * 256)), _const((8, QL))],
        out_shape=[jax.ShapeDtypeStruct((s, D), BF16), jax.ShapeDtypeStruct((QL, NH * DQK), F32),
                   jax.ShapeDtypeStruct((KVL, NH * 256), F32), jax.ShapeDtypeStruct((8, QL), F32)],
        compiler_params=_cparams(1),
    )(dq, dk, dv, proj, pos, inv_freq, qnw, kvnw, wuq, wukv)


def _conv_rows_bwd(dya, u1, proj, ln_w, ln_b):
    s = dya.shape[0]
    tm = min(512, s)

    def body(dya_ref, u1_ref, gate_ref, lw_ref, lb_ref, du1_ref, dag_ref, small_ref):
        @pl.when(pl.program_id(0) == 0)
        def _():
            small_ref[...] = jnp.zeros(small_ref.shape, F32)

        u1 = u1_ref[...]
        mu = jnp.mean(u1, axis=-1, keepdims=True)
        xc = u1 - mu
        rstd = lax.rsqrt(jnp.mean(xc * xc, axis=-1, keepdims=True) + EPS)
        xhat = xc * rstd
        lw = lw_ref[...]
        ln = xhat * lw + lb_ref[...]
        sl = _sig(ln)
        u2 = ln * sl
        gate = gate_ref[...]
        sg = _sig(gate)
        dya = dya_ref[...]
        dag_ref[...] = (dya * u2 * (sg * (1.0 + gate * (1.0 - sg)))).astype(BF16)
        dln = dya * (gate * sg) * (sl * (1.0 + ln * (1.0 - sl)))
        small_ref[0:1, :] += jnp.sum(dln * xhat, axis=0, keepdims=True)
        small_ref[1:2, :] += jnp.sum(dln, axis=0, keepdims=True)
        dxh = dln * lw
        du1_ref[...] = rstd * (dxh - jnp.mean(dxh, axis=-1, keepdims=True) - xhat * jnp.mean(dxh * xhat, axis=-1, keepdims=True))

    row = pl.BlockSpec((tm, D), lambda i: (i, 0))
    return pl.pallas_call(
        body, name="conv_rows_bwd", grid=(s // tm,),
        in_specs=[row, row, pl.BlockSpec((tm, D), lambda i: (i, 2)), _const((1, D)), _const((1, D))],
        out_specs=[row, row, _const((8, D))],
        out_shape=[jax.ShapeDtypeStruct((s, D), F32), jax.ShapeDtypeStruct((s, D), BF16), jax.ShapeDtypeStruct((8, D), F32)],
        compiler_params=_cparams(1),
    )(dya, u1, proj, ln_w, ln_b)


def _conv_bwd(du1, proj, conv_w):
    s = du1.shape[0]
    tm = min(256, s)
    hb = tm // HALO
    n = s // tm
    last32 = s // HALO - 1

    def body(d_ref, dn_ref, av_ref, ag_ref, avh_ref, agh_ref, cw_ref, dp_ref, gcw_ref, small_ref, dwin_ref, uwin_ref, acc_ref):
        i = pl.program_id(0)

        @pl.when(i == 0)
        def _():
            acc_ref[...] = jnp.zeros(acc_ref.shape, F32)
            small_ref[...] = jnp.zeros(small_ref.shape, F32)

        dwin_ref[0:tm, :] = d_ref[...]
        dwin_ref[tm:tm + HALO, :] = jnp.where(i < n - 1, dn_ref[...], 0.0)
        halo = avh_ref[...] * _sig(agh_ref[...])
        uwin_ref[0:HALO, :] = jnp.where(i > 0, halo, 0.0)
        av = av_ref[...]
        sg = _sig(ag_ref[...])
        uwin_ref[HALO:HALO + tm, :] = av * sg
        small_ref[0:1, :] += jnp.sum(d_ref[...], axis=0, keepdims=True)

        for c0 in range(0, D, CONV_LC):
            lanes = pl.ds(c0, CONV_LC)
            for r0 in range(0, tm, CONV_RC):
                acc = jnp.zeros((CONV_RC, CONV_LC), F32)
                for j in range(KCONV):
                    acc = acc + dwin_ref[pl.ds(r0 + KCONV - 1 - j, CONV_RC), lanes] * cw_ref[j:j + 1, c0:c0 + CONV_LC]
                a = av[r0:r0 + CONV_RC, c0:c0 + CONV_LC]
                g = sg[r0:r0 + CONV_RC, c0:c0 + CONV_LC]
                dp_ref[r0:r0 + CONV_RC, c0:c0 + CONV_LC] = (acc * g).astype(BF16)
                dp_ref[r0:r0 + CONV_RC, D + c0:D + c0 + CONV_LC] = (acc * a * (g * (1.0 - g))).astype(BF16)
            for j in range(KCONV):
                part = jnp.zeros((8, CONV_LC), F32)
                for r0 in range(0, tm, CONV_RC):
                    prod = dwin_ref[pl.ds(r0, CONV_RC), lanes] * uwin_ref[pl.ds(r0 + HALO - (KCONV - 1) + j, CONV_RC), lanes]
                    part = part + jnp.sum(prod.reshape(CONV_RC // 8, 8, CONV_LC), axis=0)
                acc_ref[j, :, c0:c0 + CONV_LC] += part

        @pl.when(i == n - 1)
        def _():
            gcw_ref[...] = jnp.sum(acc_ref[...], axis=1)

    row = lambda k: pl.BlockSpec((tm, D), lambda i: (i, k))
    prev = lambda k: pl.BlockSpec((HALO, D), lambda i: (jnp.maximum(i * hb - 1, 0), k))
    return pl.pallas_call(
        body, name="conv_bwd", grid=(n,),
        in_specs=[row(0), pl.BlockSpec((HALO, D), lambda i: (jnp.minimum((i + 1) * hb, last32), 0)),
                  row(0), row(1), prev(0), prev(1), _const((KPAD, D))],
        out_specs=[pl.BlockSpec((tm, 2 * D), lambda i: (i, 0)), _const((KPAD, D)), _const((8, D))],
        out_shape=[jax.ShapeDtypeStruct((s, 2 * D), BF16), jax.ShapeDtypeStruct((KPAD, D), F32), jax.ShapeDtypeStruct((8, D), F32)],
        scratch_shapes=[pltpu.VMEM((tm + HALO, D), F32), pltpu.VMEM((tm + HALO, D), F32), pltpu.VMEM((KPAD, 8, D), F32)],
        compiler_params=_cparams(1),
    )(du1, du1, proj, proj, proj, proj, conv_w)


def _dproj_specs(tm, rows_first):
    def spec(lo, hi):
        def idx(a, b):
            i, k = (a, b) if rows_first else (b, a)
            col = jnp.clip(k - lo, 0, hi - lo - 1)
            if rows_first:
                return (i, col)
            return (jnp.where((k >= lo) & (k < hi), i, 0), col)
        return pl.BlockSpec((tm, D), idx)
    return [spec(0, 2), spec(2, 3), spec(3, 4), spec(4, 7)]


def _pick_dproj(k, refs, fn):
    vg, ag, mla, gates = refs

    @pl.when(k < 2)
    def _():
        fn(vg)

    @pl.when(k == 2)
    def _():
        fn(ag)

    @pl.when(k == 3)
    def _():
        fn(mla)

    @pl.when(k > 3)
    def _():
        fn(gates)


def _in_proj_bwd_x(dps, wp, x, dx2, norm_w, mod8):
    s = x.shape[0]
    tm = min(512, s)
    nk = wp.shape[1] // D

    def body(vg_ref, ag_ref, mla_ref, g_ref, w_ref, x_ref, dx2_ref, nw_ref, mod_ref, gx_ref, small_ref, acc_ref):
        i, k = pl.program_id(0), pl.program_id(1)

        @pl.when((i == 0) & (k == 0))
        def _():
            small_ref[...] = jnp.zeros(small_ref.shape, F32)

        @pl.when(k == 0)
        def _():
            acc_ref[...] = jnp.zeros(acc_ref.shape, F32)

        def add(ref):
            acc_ref[...] += _nt(ref[...], w_ref[...])

        _pick_dproj(k, (vg_ref, ag_ref, mla_ref, g_ref), add)

        @pl.when(k == nk - 1)
        def _():
            dh = acc_ref[...]
            xn, rstd = _rms_parts(x_ref[...])
            nw = nw_ref[...]
            hn = xn * nw
            small_ref[0:1, :] += jnp.sum(dh, axis=0, keepdims=True)
            small_ref[1:2, :] += jnp.sum(dh * hn, axis=0, keepdims=True)
            dhn = dh * (1.0 + mod_ref[1:2, :])
            small_ref[2:3, :] += jnp.sum(dhn * xn, axis=0, keepdims=True)
            dxn = dhn * nw
            gx_ref[...] = rstd * (dxn - xn * jnp.mean(dxn * xn, axis=-1, keepdims=True)) + dx2_ref[...]

    row = pl.BlockSpec((tm, D), lambda i, k: (i, 0))
    return pl.pallas_call(
        body, name="in_proj_bwd_x", grid=(s // tm, nk),
        in_specs=_dproj_specs(tm, True) + [pl.BlockSpec((D, D), lambda i, k: (0, k)), row, row, _const((1, D)), _const((8, D))],
        out_specs=[row, _const((8, D))],
        out_shape=[jax.ShapeDtypeStruct((s, D), F32), jax.ShapeDtypeStruct((8, D), F32)],
        scratch_shapes=[pltpu.VMEM((tm, D), F32)],
        compiler_params=_cparams(2),
    )(*dps, wp, x, dx2, norm_w, mod8)


def _in_proj_bwd_w(dps, h, nk):
    s = h.shape[0]
    tm = min(512, s)

    def body(vg_ref, ag_ref, mla_ref, g_ref, h_ref, gw_ref):
        k, i = pl.program_id(0), pl.program_id(1)

        @pl.when(i == 0)
        def _():
            gw_ref[...] = jnp.zeros(gw_ref.shape, F32)

        def add(ref):
            gw_ref[...] += _tn(h_ref[...], ref[...])

        _pick_dproj(k, (vg_ref, ag_ref, mla_ref, g_ref), add)

    return pl.pallas_call(
        body, name="in_proj_bwd_w", grid=(nk, s // tm),
        in_specs=_dproj_specs(tm, False) + [pl.BlockSpec((tm, D), lambda k, i: (i, 0))],
        out_specs=pl.BlockSpec((D, D), lambda k, i: (0, k)),
        out_shape=jax.ShapeDtypeStruct((D, nk * D), F32),
        compiler_params=_cparams(2),
    )(*dps, h)


def _pack_shard(w_in, wco, wao, wout, wuq, wukv, conv_w):
    cw = jnp.pad(conv_w.reshape(-1), (0, ROWS_CW * D - KCONV * 128)).reshape(ROWS_CW, D)
    parts = [w_in.reshape(ROWS_IN, D), wco, wao, wout, wuq.reshape(ROWS_UQ, D), wukv.reshape(ROWS_UKV, D), cw,
             jnp.zeros((ROWS_PACK - ROWS_IN - 3 * ROWS_SQ - ROWS_UQ - ROWS_UKV - ROWS_CW, D), w_in.dtype)]
    return jnp.concatenate(parts, axis=0)


def _split_rows(slab):
    out, r = [], 0
    for n in (ROWS_IN, ROWS_SQ, ROWS_SQ, ROWS_SQ, ROWS_UQ, ROWS_UKV, ROWS_CW):
        out.append(slab[..., r:r + n, :])
        r += n
    return out


def _unpack_shard(slab):
    w_in, wco, wao, wout, wuq, wukv, cw = _split_rows(slab)
    return (w_in.reshape(D, ROWS_IN), wco, wao, wout, wuq.reshape(QL, NH * DQK // N_DEV), wukv.reshape(KVL, NH * 256 // N_DEV),
            cw.reshape(-1)[:KCONV * 128].reshape(KCONV, 128))


def _unpack_gathered(g):
    w_in, wco, wao, wout, wuq, wukv, cw = _split_rows(g)
    w_in = w_in.reshape(N_DEV, D, ROWS_IN).transpose(1, 0, 2).reshape(D, IN_COLS)
    wuq = wuq.reshape(N_DEV, QL, NH * DQK // N_DEV).transpose(1, 0, 2).reshape(QL, NH * DQK)
    wukv = wukv.reshape(N_DEV, KVL, NH * 256 // N_DEV).transpose(1, 0, 2).reshape(KVL, NH * 256)
    cw = cw.reshape(N_DEV, ROWS_CW * D)[:, :KCONV * 128].reshape(N_DEV, KCONV, 128).transpose(1, 0, 2).reshape(KCONV, D)
    return w_in, wco.reshape(D, D), wao.reshape(D, D), wout.reshape(D, D), wuq, wukv, cw


def _pack_grads(g_in, gco, gao, gout, guq, gukv, gcw):
    g_in = g_in.reshape(D, N_DEV, ROWS_IN).transpose(1, 0, 2).reshape(N_DEV, ROWS_IN, D)
    guq = guq.reshape(QL, N_DEV, -1).transpose(1, 0, 2).reshape(N_DEV, ROWS_UQ, D)
    gukv = gukv.reshape(KVL, N_DEV, -1).transpose(1, 0, 2).reshape(N_DEV, ROWS_UKV, D)
    gcw = gcw.reshape(KCONV, N_DEV, 128).transpose(1, 0, 2).reshape(N_DEV, KCONV * 128)
    gcw = jnp.pad(gcw, ((0, 0), (0, ROWS_CW * D - KCONV * 128))).reshape(N_DEV, ROWS_CW, D)
    sq = lambda a: a.reshape(N_DEV, ROWS_SQ, D)
    pad = jnp.zeros((N_DEV, ROWS_PACK - ROWS_IN - 3 * ROWS_SQ - ROWS_UQ - ROWS_UKV - ROWS_CW, D), F32)
    return jnp.concatenate([g_in, sq(gco), sq(gao), sq(gout), guq, gukv, gcw, pad], axis=1)


def _pack_small(vecs):
    flat = jnp.concatenate([v.reshape(-1) for v in vecs])
    return jnp.pad(flat, (0, SMALL_LEN - flat.shape[0])).reshape(8, SMALL_COLS)


def _unpack_small(a, shapes):
    flat = a.reshape(-1)
    out, off = [], 0
    for shp, n in zip(shapes, SMALL_SIZES):
        out.append(flat[off:off + n].reshape(shp))
        off += n
    return out


def kernel(x, c, positions, w_ada, b_ada, norm_w, w_in, conv_w, conv_b, conv_ln_w, conv_ln_b, w_conv_out, q_norm_w, w_uq, kv_norm_w, w_ukv, w_attn_out, w_out, final_norm_w, loss_target, m_w_ada, m_b_ada, m_norm_w, m_w_in, m_conv_w, m_conv_b, m_conv_ln_w, m_conv_ln_b, m_w_conv_out, m_q_norm_w, m_w_uq, m_kv_norm_w, m_w_ukv, m_w_attn_out, m_w_out, m_final_norm_w, v_w_ada, v_b_ada, v_norm_w, v_w_in, v_conv_w, v_conv_b, v_conv_ln_w, v_conv_ln_b, v_w_conv_out, v_q_norm_w, v_w_uq, v_kv_norm_w, v_w_ukv, v_w_attn_out, v_w_out, v_final_norm_w):
    me = 4 * lax.axis_index("x") + 2 * lax.axis_index("y") + lax.axis_index("c")
    xs, tgt = x[0], loss_target[0]
    s = xs.shape[0]
    ada_cols = w_ada.shape[2]

    sharded = lambda t: tuple(a[0] for a in t)
    w_sh = sharded((w_in, w_conv_out, w_attn_out, w_out, w_uq, w_ukv, conv_w))
    gathered = _all_gather(_pack_shard(*w_sh).astype(BF16), "gather_weights")
    win_f, wco, wao, wout, wuq, wukv, cw = _unpack_gathered(gathered)
    split = 3 * D + MLA_COLS
    wp = jnp.concatenate([win_f[:, :split], jnp.zeros((D, D - MLA_COLS), BF16), win_f[:, split:]], axis=1)
    cw32 = jnp.pad(cw.astype(F32), ((0, KPAD - KCONV), (0, 0)))

    c_all = _all_gather(jnp.broadcast_to(c, (8, D)), "gather_c")[:, 0, :]
    b_cols = lax.dynamic_slice(b_ada, (0, me * ada_cols), (1, ada_cols))
    mod_cols = _all_gather(_ada_mod(c_all, w_ada[0], b_cols), "gather_mod")
    mod = lax.dynamic_index_in_dim(mod_cols, me, axis=1, keepdims=False).reshape(3, D)
    mod8 = jnp.pad(mod, ((0, 5), (0, 0)))

    pos = positions.reshape(s, 1)
    inv_freq = (ROPE_THETA ** (-jnp.arange(0, ROPE, 2, dtype=F32) / ROPE)).reshape(1, HALF)
    proj, h = _in_proj(xs, norm_w, mod8, wp)
    u1, ya = _conv_fwd(proj, cw32, conv_b, conv_ln_w, conv_ln_b)
    q, k, v = _mla_prep(proj, pos, inv_freq, q_norm_w, kv_norm_w, wuq, wukv)
    o, lse = _attn_fwd(q, k, v)

    dya, do, delta, dp_gates, dx2, gw3, small_a = _merge_loss(xs, tgt, ya, o, proj, mod8, final_norm_w.reshape(1, D), wco, wao, wout)
    dq, dk, dv = _attn_bwd(q, k, v, do, lse, delta)
    dp_mla, guq, gukv, small_b = _mla_bwd(dq, dk, dv, proj, pos, inv_freq, q_norm_w, kv_norm_w, wuq, wukv)
    du1, dp_ag, small_c = _conv_rows_bwd(dya, u1, proj, conv_ln_w, conv_ln_b)
    dp_vg, gcw, small_d = _conv_bwd(du1, proj, cw32)
    dps = [dp_vg, dp_ag, dp_mla, dp_gates]
    grad_x, small_e = _in_proj_bwd_x(dps, wp, xs, dx2, norm_w, mod8)
    gwp = _in_proj_bwd_w(dps, h, wp.shape[1] // D)
    g_in = jnp.concatenate([gwp[:, :split], gwp[:, 3 * D + D:]], axis=1)

    recv = _grad_exchange(_pack_grads(g_in, gw3[0], gw3[1], gw3[2], guq, gukv, gcw[:KCONV]))
    big = _reduce_adam(recv, _pack_shard(*w_sh), _pack_shard(*sharded((m_w_in, m_w_conv_out, m_w_attn_out, m_w_out, m_w_uq, m_w_ukv, m_conv_w))),
                       _pack_shard(*sharded((v_w_in, v_w_conv_out, v_w_attn_out, v_w_out, v_w_uq, v_w_ukv, v_conv_w))), "adam_sharded", ROWS_PACK // 8)
    big = [_unpack_shard(a) for a in big]

    dmod = jnp.concatenate([small_e[0], small_e[1], small_a[1]])
    payload = _pack_small([dmod, small_e[2], small_d[0], small_c[0], small_c[1], small_a[0], small_b[0], small_b[1], small_a[2, 0:1]])
    pay_all = _all_gather(payload, "gather_small")
    small_w = (b_ada, norm_w, conv_b, conv_ln_w, conv_ln_b, final_norm_w, q_norm_w, kv_norm_w)
    small_m = (m_b_ada, m_norm_w, m_conv_b, m_conv_ln_w, m_conv_ln_b, m_final_norm_w, m_q_norm_w, m_kv_norm_w)
    small_v = (v_b_ada, v_norm_w, v_conv_b, v_conv_ln_w, v_conv_ln_b, v_final_norm_w, v_q_norm_w, v_kv_norm_w)
    sm = _reduce_adam(pay_all, _pack_small(small_w), _pack_small(small_m), _pack_small(small_v), "adam_replicated", 8)
    loss = sm[0].reshape(-1)[sum(SMALL_SIZES)]
    shapes = [t.shape for t in small_w]
    sm = [_unpack_small(a, shapes) for a in sm]

    dmod_all = pay_all.reshape(N_DEV, SMALL_LEN)[:, :3 * D]
    dmod_cols = lax.dynamic_slice(dmod_all, (0, me * ada_cols), (N_DEV, ada_cols))
    ada = _ada_bwd(c_all, dmod_cols, w_ada[0], m_w_ada[0], v_w_ada[0])

    def group(i):
        b_in, b_co, b_ao, b_out, b_uq, b_ukv, b_cw = big[i]
        s_bada, s_nw, s_cb, s_clw, s_clb, s_fnw, s_qnw, s_kvnw = sm[i]
        return (ada[i][None], s_bada, s_nw, b_in[None], b_cw[None], s_cb, s_clw, s_clb, b_co[None], s_qnw, b_uq[None], s_kvnw,
                b_ukv[None], b_ao[None], b_out[None], s_fnw)

    return (loss, grad_x[None], *group(0), *group(1), *group(2), *group(3))
```

```python
import functools

import jax
import jax.numpy as jnp
from jax import lax
from jax.experimental import pallas as pl
from jax.experimental.pallas import tpu as pltpu

F32 = jnp.float32
BF16 = jnp.bfloat16

D = 1024
NH = 8
NOPE = 128
ROPE = 64
HALF = ROPE // 2
DQK = NOPE + ROPE
DV = 128
QL = 256
KVL = 256
KCONV = 31
KPAD = 32
HALO = 32
IN_COLS = 6720
MLA_COLS = QL + KVL + ROPE
PROJ_COLS = 7 * D
EPS = 1e-6
ROPE_THETA = 10000.0
N_DEV = 8

ADAM_LR = 0.001
ADAM_B1 = 0.9
ADAM_B2 = 0.999
ADAM_EPS = 1e-08
ADAM_WD = 0.01
ADAM_STEP = 10

ROWS_IN = 840
ROWS_SQ = 128
ROWS_UQ = 48
ROWS_UKV = 64
ROWS_CW = 4
ROWS_PACK = 1344
SMALL_SIZES = (3 * D, D, D, D, D, D, QL, KVL)
SMALL_COLS = 1152
SMALL_LEN = 8 * SMALL_COLS

MESH = pl.DeviceIdType.MESH
ANY = pl.BlockSpec(memory_space=pl.ANY)
V7X_VMEM_LIMIT = 56 * 1024 * 1024


def _cparams(n_axes, vmem=V7X_VMEM_LIMIT):
    return pltpu.CompilerParams(dimension_semantics=("arbitrary",) * n_axes, vmem_limit_bytes=vmem)


def _sig(x):
    return jax.nn.sigmoid(x)


def _nt(a, b):
    return lax.dot_general(a, b, (((1,), (1,)), ((), ())), preferred_element_type=F32)


def _tn(a, b):
    return lax.dot_general(a, b, (((0,), (0,)), ((), ())), preferred_element_type=F32)


def _nn(a, b):
    return jnp.dot(a, b, preferred_element_type=F32)


def _const(shape):
    return pl.BlockSpec(shape, lambda *_: (0,) * len(shape))


def _all_gather(block, name):
    r, c = block.shape

    def body(x_ref, out_ref, send_sems, recv_sems, local_sem):
        x, y, cc = lax.axis_index("x"), lax.axis_index("y"), lax.axis_index("c")
        me, sibling = (x, y, cc), (x, y, 1 - cc)
        chips = [(1 - x, y), (x, 1 - y), (1 - x, 1 - y)]

        def slot(px, py, pc):
            return out_ref.at[4 * px + 2 * py + pc]

        def copy(k, blk, to, src=None):
            return pltpu.make_async_remote_copy(
                src_ref=slot(*blk) if src is None else src, dst_ref=slot(*blk),
                send_sem=send_sems.at[k], recv_sem=recv_sems.at[k],
                device_id=to, device_id_type=MESH)

        mine = pltpu.make_async_copy(x_ref, slot(*me), local_sem)
        mine.start()
        first = [copy(0, me, sibling, src=x_ref)]
        first += [copy(1 + j, me, (*chip, cc), src=x_ref) for j, chip in enumerate(chips)]
        for cp in first:
            cp.start()
        passed = [copy(4 + j, (*chip, cc), sibling) for j, chip in enumerate(chips)]
        for j, chip in enumerate(chips):
            copy(1 + j, (*chip, cc), me).wait_recv()
            passed[j].start()
        copy(0, sibling, me).wait_recv()
        for j, chip in enumerate(chips):
            copy(4 + j, (*chip, 1 - cc), me).wait_recv()
        for cp in first + passed:
            cp.wait_send()
        mine.wait()

    return pl.pallas_call(
        body, name=name,
        out_shape=jax.ShapeDtypeStruct((N_DEV, r, c), block.dtype),
        in_specs=[ANY], out_specs=ANY,
        scratch_shapes=[pltpu.SemaphoreType.DMA((7,)), pltpu.SemaphoreType.DMA((7,)), pltpu.SemaphoreType.DMA],
    )(block)


N_CHIP = 4


def _pair_exchange(packed):
    _, _, r, c = packed.shape

    def body(src_ref, out_ref, send_sem, recv_sem):
        x, y, cc = lax.axis_index("x"), lax.axis_index("y"), lax.axis_index("c")
        cp = pltpu.make_async_remote_copy(
            src_ref=src_ref.at[1 - cc], dst_ref=out_ref, send_sem=send_sem, recv_sem=recv_sem,
            device_id=(x, y, 1 - cc), device_id_type=MESH)
        cp.start()
        cp.wait()

    return pl.pallas_call(
        body, name="pair_exchange",
        out_shape=jax.ShapeDtypeStruct((N_CHIP, r, c), packed.dtype),
        in_specs=[ANY], out_specs=ANY,
        scratch_shapes=[pltpu.SemaphoreType.DMA, pltpu.SemaphoreType.DMA],
    )(packed)


def _pair_add(core, packed, got, tr):
    _, _, r, c = packed.shape

    def body(core_ref, own_ref, got_ref, out_ref):
        out_ref[...] = (own_ref[0].astype(F32) + got_ref[...].astype(F32)).astype(out_ref.dtype)

    blk = pl.BlockSpec((1, tr, c), lambda j, i, core_ref: (j, i, 0))
    return pl.pallas_call(
        body, name="pair_add",
        grid_spec=pltpu.PrefetchScalarGridSpec(
            num_scalar_prefetch=1, grid=(N_CHIP, r // tr),
            in_specs=[pl.BlockSpec((1, 1, tr, c), lambda j, i, core_ref: (core_ref[0], j, i, 0)), blk],
            out_specs=blk),
        out_shape=jax.ShapeDtypeStruct((N_CHIP, r, c), packed.dtype),
        compiler_params=_cparams(2),
    )(core, packed, got)


def _chip_exchange(half):
    _, r, c = half.shape

    def body(src_ref, out_ref, send_sems, recv_sems, local_sem):
        x, y, cc = lax.axis_index("x"), lax.axis_index("y"), lax.axis_index("c")
        me = 2 * x + y
        mine = pltpu.make_async_copy(src_ref.at[me], out_ref.at[me], local_sem)
        mine.start()

        def peer(k):
            return (1 - x if k & 2 else x), (1 - y if k & 1 else y)

        copies = []
        for k in range(1, N_CHIP):
            px, py = peer(k)
            copies.append(pltpu.make_async_remote_copy(
                src_ref=src_ref.at[2 * px + py], dst_ref=out_ref.at[me],
                send_sem=send_sems.at[k - 1], recv_sem=recv_sems.at[k - 1],
                device_id=(px, py, cc), device_id_type=MESH))
        for cp in copies:
            cp.start()
        for k in range(1, N_CHIP):
            px, py = peer(k)
            pltpu.make_async_remote_copy(
                src_ref=src_ref.at[me], dst_ref=out_ref.at[2 * px + py],
                send_sem=send_sems.at[k - 1], recv_sem=recv_sems.at[k - 1],
                device_id=(px, py, cc), device_id_type=MESH).wait_recv()
        for cp in copies:
            cp.wait_send()
        mine.wait()

    return pl.pallas_call(
        body, name="chip_exchange",
        out_shape=jax.ShapeDtypeStruct(half.shape, half.dtype),
        in_specs=[ANY], out_specs=ANY,
        scratch_shapes=[pltpu.SemaphoreType.DMA((3,)), pltpu.SemaphoreType.DMA((3,)), pltpu.SemaphoreType.DMA],
    )(half)


def _adam(g, w, m, v):
    m = ADAM_B1 * m + (1.0 - ADAM_B1) * g
    v = ADAM_B2 * v + (1.0 - ADAM_B2) * (g * g)
    m_hat = m / (1.0 - ADAM_B1 ** ADAM_STEP)
    v_hat = v / (1.0 - ADAM_B2 ** ADAM_STEP)
    delta = -ADAM_LR * (m_hat / (jnp.sqrt(v_hat) + ADAM_EPS) + ADAM_WD * w)
    return delta, m, v


def _reduce_adam(parts, w, m, v, name, tr):
    n, r, c = parts.shape

    def body(p_ref, w_ref, m_ref, v_ref, g_out, d_out, m_out, v_out):
        g = p_ref[0].astype(F32)
        for j in range(1, n):
            g = g + p_ref[j].astype(F32)
        delta, nm, nv = _adam(g, w_ref[...], m_ref[...], v_ref[...])
        g_out[...] = g
        d_out[...] = delta
        m_out[...] = nm
        v_out[...] = nv

    row = pl.BlockSpec((tr, c), lambda i: (i, 0))
    return pl.pallas_call(
        body, name=name, grid=(r // tr,),
        in_specs=[pl.BlockSpec((n, tr, c), lambda i: (0, i, 0)), row, row, row],
        out_specs=[row] * 4,
        out_shape=[jax.ShapeDtypeStruct((r, c), F32)] * 4,
        compiler_params=_cparams(1),
    )(parts, w, m, v)


def _ada_mod(c_all, w_ada, b_cols):
    def body(c_ref, w_ref, b_ref, o_ref):
        cv = c_ref[...]
        act = (cv * _sig(cv)).astype(BF16)
        o_ref[...] = _nn(act, w_ref[...].astype(BF16)) + b_ref[...]

    return pl.pallas_call(body, name="ada_mod", out_shape=jax.ShapeDtypeStruct((N_DEV, w_ada.shape[1]), F32))(c_all, w_ada, b_cols)


def _ada_bwd(c_all, dmod_cols, w, m, v):
    def body(c_ref, d_ref, w_ref, m_ref, v_ref, g_out, d_out, m_out, v_out):
        cv = c_ref[...]
        act = (cv * _sig(cv)).astype(BF16)
        g = _tn(act, d_ref[...].astype(BF16))
        delta, nm, nv = _adam(g, w_ref[...], m_ref[...], v_ref[...])
        g_out[...] = g
        d_out[...] = delta
        m_out[...] = nm
        v_out[...] = nv

    return pl.pallas_call(body, name="ada_bwd", out_shape=[jax.ShapeDtypeStruct(w.shape, F32)] * 4)(c_all, dmod_cols, w, m, v)


def _in_proj(x, norm_w, mod8, wp):
    s = x.shape[0]
    tm = min(1024, s)
    nk = wp.shape[1] // D

    def body(x_ref, nw_ref, mod_ref, w_ref, proj_ref, h_ref, hs_ref):
        @pl.when(pl.program_id(1) == 0)
        def _():
            xv = x_ref[...]
            rstd = lax.rsqrt(jnp.mean(xv * xv, axis=-1, keepdims=True) + EPS)
            h = (xv * rstd) * nw_ref[...] * (1.0 + mod_ref[1:2, :]) + mod_ref[0:1, :]
            hs_ref[...] = h.astype(BF16)
            h_ref[...] = hs_ref[...]

        proj_ref[...] = _nn(hs_ref[...], w_ref[...])

    return pl.pallas_call(
        body, name="in_proj", grid=(s // tm, nk),
        in_specs=[pl.BlockSpec((tm, D), lambda i, k: (i, 0)), _const((1, D)), _const((8, D)),
                  pl.BlockSpec((D, D), lambda i, k: (0, k))],
        out_specs=[pl.BlockSpec((tm, D), lambda i, k: (i, k)), pl.BlockSpec((tm, D), lambda i, k: (i, 0))],
        out_shape=[jax.ShapeDtypeStruct((s, nk * D), F32), jax.ShapeDtypeStruct((s, D), BF16)],
        scratch_shapes=[pltpu.VMEM((tm, D), BF16)],
        compiler_params=_cparams(2),
    )(x, norm_w, mod8, wp)


CONV_RC = 64
CONV_LC = 256


def _conv_fwd(proj, conv_w, conv_b, ln_w, ln_b):
    s = proj.shape[0]
    tm = min(256, s)
    hb = tm // HALO

    def body(av_ref, ag_ref, avh_ref, agh_ref, gate_ref, cw_ref, cb_ref, lw_ref, lb_ref, u1_ref, ya_ref, win_ref):
        i = pl.program_id(0)
        halo = avh_ref[...] * _sig(agh_ref[...])
        win_ref[0:HALO, :] = jnp.where(i > 0, halo, 0.0)
        win_ref[HALO:HALO + tm, :] = av_ref[...] * _sig(ag_ref[...])
        for r0 in range(0, tm, CONV_RC):
            for c0 in range(0, D, CONV_LC):
                acc = jnp.zeros((CONV_RC, CONV_LC), F32) + cb_ref[:, c0:c0 + CONV_LC]
                for j in range(KCONV):
                    acc = acc + win_ref[pl.ds(r0 + HALO - (KCONV - 1) + j, CONV_RC), pl.ds(c0, CONV_LC)] * cw_ref[j:j + 1, c0:c0 + CONV_LC]
                u1_ref[r0:r0 + CONV_RC, c0:c0 + CONV_LC] = acc
        u1 = u1_ref[...]
        mu = jnp.mean(u1, axis=-1, keepdims=True)
        xc = u1 - mu
        var = jnp.mean(xc * xc, axis=-1, keepdims=True)
        ln = xc * lax.rsqrt(var + EPS) * lw_ref[...] + lb_ref[...]
        gate = gate_ref[...]
        ya_ref[...] = ((ln * _sig(ln)) * (gate * _sig(gate))).astype(BF16)

    row = lambda k: pl.BlockSpec((tm, D), lambda i: (i, k))
    prev = lambda k: pl.BlockSpec((HALO, D), lambda i: (jnp.maximum(i * hb - 1, 0), k))
    return pl.pallas_call(
        body, name="conv_fwd", grid=(s // tm,),
        in_specs=[row(0), row(1), prev(0), prev(1), row(2), _const((KPAD, D)), _const((1, D)), _const((1, D)), _const((1, D))],
        out_specs=[pl.BlockSpec((tm, D), lambda i: (i, 0))] * 2,
        out_shape=[jax.ShapeDtypeStruct((s, D), F32), jax.ShapeDtypeStruct((s, D), BF16)],
        scratch_shapes=[pltpu.VMEM((tm + HALO, D), F32)],
        compiler_params=_cparams(1),
    )(proj, proj, proj, proj, proj, conv_w, conv_b, ln_w, ln_b)


def _rope_tables(pos_ref, if_ref):
    ang = pos_ref[...].astype(F32) * if_ref[...]
    return jnp.cos(ang), jnp.sin(ang)


def _rms_parts(x):
    rstd = lax.rsqrt(jnp.mean(x * x, axis=-1, keepdims=True) + EPS)
    return x * rstd, rstd


def _mla_prep(proj, pos, inv_freq, qnw, kvnw, wuq, wukv):
    s = proj.shape[0]
    tm = min(512, s)

    def body(p_ref, pos_ref, if_ref, qnw_ref, kvnw_ref, wuq_ref, wukv_ref, q_ref, k_ref, v_ref):
        blk = p_ref[...]
        cos, sin = _rope_tables(pos_ref, if_ref)

        def rope(r):
            x1, x2 = r[:, :HALF], r[:, HALF:]
            return jnp.concatenate([x1 * cos - x2 * sin, x1 * sin + x2 * cos], axis=-1)

        qlat = _rms_parts(blk[:, :QL])[0] * qnw_ref[...]
        kvlat = _rms_parts(blk[:, QL:QL + KVL])[0] * kvnw_ref[...]
        q = _nn(qlat.astype(BF16), wuq_ref[...])
        kv = _nn(kvlat.astype(BF16), wukv_ref[...])
        kr = rope(blk[:, QL + KVL:MLA_COLS])
        for h in range(NH):
            qh = q[:, h * DQK:(h + 1) * DQK]
            q_ref[h] = (jnp.concatenate([qh[:, :NOPE], rope(qh[:, NOPE:])], axis=-1) * ATTN_SCALE).astype(BF16)
            k_ref[h] = jnp.concatenate([kv[:, h * 256:h * 256 + NOPE], kr], axis=-1).astype(BF16)
            v_ref[h] = kv[:, h * 256 + NOPE:(h + 1) * 256].astype(BF16)

    hm = lambda d: pl.BlockSpec((NH, tm, d), lambda i: (0, i, 0))
    return pl.pallas_call(
        body, name="mla_prep", grid=(s // tm,),
        in_specs=[pl.BlockSpec((tm, D), lambda i: (i, 3)), pl.BlockSpec((tm, 1), lambda i: (i, 0)), _const((1, HALF)),
                  _const((1, QL)), _const((1, KVL)), _const((QL, NH * DQK)), _const((KVL, NH * 256))],
        out_specs=[hm(DQK), hm(DQK), hm(DV)],
        out_shape=[jax.ShapeDtypeStruct((NH, s, DQK), BF16), jax.ShapeDtypeStruct((NH, s, DQK), BF16),
                   jax.ShapeDtypeStruct((NH, s, DV), BF16)],
        compiler_params=_cparams(1),
    )(proj, pos, inv_freq, qnw, kvnw, wuq, wukv)


ATTN_SCALE = DQK ** -0.5


def _causal_mask(s, t):
    rows = lax.broadcasted_iota(jnp.int32, (t, t), 0)
    cols = lax.broadcasted_iota(jnp.int32, (t, t), 1)
    return jnp.where(cols <= rows, s, -jnp.inf)


def _attn_tile(s):
    return min(1024, s // 2)


def _attn_fwd(q, k, v):
    nh, s, _ = q.shape
    t = _attn_tile(s)

    def body(q_ref, k_ref, v_ref, o_ref, lse_ref):
        qi = pl.program_id(1)
        qv = q_ref[0]

        def chunk(c, carry, diag):
            m, l, acc = carry
            rows = pl.ds(pl.multiple_of(c * t, t), t)
            sc = _nt(qv, k_ref[0, rows, :])
            if diag:
                sc = _causal_mask(sc, t)
            m_new = jnp.maximum(m, jnp.max(sc, axis=-1, keepdims=True))
            alpha = jnp.exp(m - m_new)
            p = jnp.exp(sc - m_new)
            l = alpha * l + jnp.sum(p, axis=-1, keepdims=True)
            acc = alpha * acc + _nn(p.astype(BF16), v_ref[0, rows, :])
            return m_new, l, acc

        init = (jnp.full((t, 1), -jnp.inf, F32), jnp.zeros((t, 1), F32), jnp.zeros((t, DV), F32))
        carry = lax.fori_loop(0, qi, lambda c, cr: chunk(c, cr, False), init)
        m, l, acc = chunk(qi, carry, True)
        o_ref[...] = acc / l
        lse_ref[0] = jnp.broadcast_to(m + jnp.log(l), (t, DV))

    head = lambda d: pl.BlockSpec((1, s, d), lambda h, i: (h, 0, 0))
    return pl.pallas_call(
        body, name="attn_fwd", grid=(nh, s // t),
        in_specs=[pl.BlockSpec((1, t, DQK), lambda h, i: (h, i, 0)), head(DQK), head(DV)],
        out_specs=[pl.BlockSpec((t, DV), lambda h, i: (i, h)), pl.BlockSpec((1, t, DV), lambda h, i: (h, i, 0))],
        out_shape=[jax.ShapeDtypeStruct((s, nh * DV), F32), jax.ShapeDtypeStruct((nh, s, DV), F32)],
        compiler_params=_cparams(2),
    )(q, k, v)


def _merge_loss(x, target, ya, o, proj, mod8, fnw, wco, wao, wout):
    s = x.shape[0]
    tm = min(256, s)
    n = s // tm

    def body(x_ref, t_ref, ya_ref, o_ref, bg_ref, ga_ref, gb_ref, mod_ref, fnw_ref, wco_ref, wao_ref, wout_ref,
             dya_ref, do_ref, delta_ref, dpg_ref, dx2_ref, gw_ref, small_ref, acc_ref):
        i = pl.program_id(0)

        @pl.when(i == 0)
        def _():
            acc_ref[...] = jnp.zeros(acc_ref.shape, F32)
            small_ref[...] = jnp.zeros(small_ref.shape, F32)

        bg = bg_ref[...]
        sbg = _sig(bg)
        sb = bg * sbg
        ov = o_ref[...]
        ya = ya_ref[...]
        yb = (ov * sb).astype(BF16)
        y_a = _nn(ya, wco_ref[...])
        y_b = _nn(yb, wao_ref[...])
        sa = _sig(ga_ref[...])
        sgb = _sig(gb_ref[...])
        merged = (sa * y_a + sgb * y_b).astype(BF16)
        z = _nn(merged, wout_ref[...])
        gate = mod_ref[2:3, :]
        x2 = x_ref[...] + gate * z
        xn, rstd = _rms_parts(x2)
        fnw = fnw_ref[...]
        err = xn * fnw - t_ref[...]
        loss = jnp.sum(jnp.sum(err * err, axis=-1, keepdims=True), axis=0, keepdims=True) * (0.5 / D)
        dy = err * (1.0 / D)
        small_ref[0:1, :] += jnp.sum(dy * xn, axis=0, keepdims=True)
        dxn = dy * fnw
        dx2 = rstd * (dxn - xn * jnp.mean(dxn * xn, axis=-1, keepdims=True))
        dx2_ref[...] = dx2
        small_ref[1:2, :] += jnp.sum(dx2 * z, axis=0, keepdims=True)
        small_ref[2:3, :] += jnp.broadcast_to(loss, (1, D))
        dz = (dx2 * gate).astype(BF16)
        dmerged = _nt(dz, wout_ref[...])
        acc_ref[2] += _tn(merged, dz)
        dy_a = (dmerged * sa).astype(BF16)
        dy_b = (dmerged * sgb).astype(BF16)
        dpg_ref[:, D:2 * D] = (dmerged * y_a * (sa * (1.0 - sa))).astype(BF16)
        dpg_ref[:, 2 * D:3 * D] = (dmerged * y_b * (sgb * (1.0 - sgb))).astype(BF16)
        dya_ref[...] = _nt(dy_a, wco_ref[...])
        acc_ref[0] += _tn(ya, dy_a)
        dyb = _nt(dy_b, wao_ref[...])
        acc_ref[1] += _tn(yb, dy_b)
        do = dyb * sb
        do_ref[...] = do.astype(BF16)
        dpg_ref[:, 0:D] = (dyb * ov * (sbg * (1.0 + bg * (1.0 - sbg)))).astype(BF16)
        prod = do * ov
        for h in range(NH):
            delta_ref[h] = jnp.broadcast_to(jnp.sum(prod[:, h * DV:(h + 1) * DV], axis=-1, keepdims=True), (tm, DV))

        @pl.when(i == n - 1)
        def _():
            pltpu.sync_copy(acc_ref, gw_ref)

    row = pl.BlockSpec((tm, D), lambda i: (i, 0))
    col = lambda k: pl.BlockSpec((tm, D), lambda i: (i, k))
    wspec = pl.BlockSpec((D, D), lambda i: (0, 0), pipeline_mode=pl.Buffered(1))
    return pl.pallas_call(
        body, name="merge_loss", grid=(n,),
        in_specs=[row, row, row, row, col(4), col(5), col(6), _const((8, D)), _const((1, D)), wspec, wspec, wspec],
        out_specs=[row, row, pl.BlockSpec((NH, tm, DV), lambda i: (0, i, 0)), pl.BlockSpec((tm, 3 * D), lambda i: (i, 0)),
                   row, ANY, _const((8, D))],
        out_shape=[jax.ShapeDtypeStruct((s, D), F32), jax.ShapeDtypeStruct((s, D), BF16),
                   jax.ShapeDtypeStruct((NH, s, DV), F32), jax.ShapeDtypeStruct((s, 3 * D), BF16),
                   jax.ShapeDtypeStruct((s, D), F32), jax.ShapeDtypeStruct((3, D, D), F32),
                   jax.ShapeDtypeStruct((8, D), F32)],
        scratch_shapes=[pltpu.VMEM((3, D, D), F32)],
        compiler_params=_cparams(1),
    )(x, target, ya, o, proj, proj, proj, mod8, fnw, wco, wao, wout)


def _attn_bwd(q, k, v, do, lse, delta):
    nh, s, _ = q.shape
    t = _attn_tile(s)
    nb = s // t

    def body(q_ref, k_ref, v_ref, do_ref, lse_ref, dl_ref, dq_ref, dk_ref, dv_ref):
        kj = pl.program_id(1)

        @pl.when(kj == 0)
        def _():
            dq_ref[...] = jnp.zeros(dq_ref.shape, F32)

        kv_, vv = k_ref[0], v_ref[0]

        def chunk(c, carry, diag):
            dk, dv = carry
            rows = pl.ds(pl.multiple_of(c * t, t), t)
            qv = q_ref[0, rows, :]
            dov = do_ref[rows, :]
            sc = _nt(qv, kv_)
            if diag:
                sc = _causal_mask(sc, t)
            p = jnp.exp(sc - lse_ref[0, rows, 0:1])
            dv = dv + _tn(p.astype(BF16), dov)
            dp = _nt(dov, vv)
            ds = (p * (dp - dl_ref[0, rows, 0:1])).astype(BF16)
            dk = dk + _tn(ds, qv)
            dq_ref[0, rows, :] += _nn(ds, kv_)
            return dk, dv

        carry = chunk(kj, (jnp.zeros((t, DQK), F32), jnp.zeros((t, DV), F32)), True)
        dk, dv = lax.fori_loop(kj + 1, nb, lambda c, cr: chunk(c, cr, False), carry)
        dk_ref[0] = dk
        dv_ref[0] = dv

    head = lambda d: pl.BlockSpec((1, s, d), lambda h, j: (h, 0, 0))
    blk = lambda d: pl.BlockSpec((1, t, d), lambda h, j: (h, j, 0))
    return pl.pallas_call(
        body, name="attn_bwd", grid=(nh, nb),
        in_specs=[head(DQK), blk(DQK), blk(DV), pl.BlockSpec((s, DV), lambda h, j: (0, h)), head(DV), head(DV)],
        out_specs=[head(DQK), blk(DQK), blk(DV)],
        out_shape=[jax.ShapeDtypeStruct((nh, s, DQK), F32), jax.ShapeDtypeStruct((nh, s, DQK), F32),
                   jax.ShapeDtypeStruct((nh, s, DV), F32)],
        compiler_params=_cparams(2),
    )(q, k, v, do, lse, delta)


def _mla_bwd(dq, dk, dv, proj, pos, inv_freq, qnw, kvnw, wuq, wukv):
    s = proj.shape[0]
    tm = min(512, s)

    def body(dq_ref, dk_ref, dv_ref, p_ref, pos_ref, if_ref, qnw_ref, kvnw_ref, wuq_ref, wukv_ref,
             dp_ref, guq_ref, gukv_ref, small_ref):
        @pl.when(pl.program_id(0) == 0)
        def _():
            guq_ref[...] = jnp.zeros(guq_ref.shape, F32)
            gukv_ref[...] = jnp.zeros(gukv_ref.shape, F32)
            small_ref[...] = jnp.zeros(small_ref.shape, F32)

        blk = p_ref[...]
        cos, sin = _rope_tables(pos_ref, if_ref)

        def unrope(g):
            g1, g2 = g[:, :HALF], g[:, HALF:]
            return jnp.concatenate([g1 * cos + g2 * sin, g2 * cos - g1 * sin], axis=-1)

        dq_cols, dkv_cols = [], []
        dkr = jnp.zeros((tm, ROPE), F32)
        for h in range(NH):
            dqh, dkh = dq_ref[h] * ATTN_SCALE, dk_ref[h]
            dq_cols += [dqh[:, :NOPE], unrope(dqh[:, NOPE:])]
            dkv_cols += [dkh[:, :NOPE], dv_ref[h]]
            dkr = dkr + dkh[:, NOPE:]
        dq_full = jnp.concatenate(dq_cols, axis=-1).astype(BF16)
        dkv_full = jnp.concatenate(dkv_cols, axis=-1).astype(BF16)

        def latent_bwd(c, nw_ref, d_up, w_ref, g_ref, srow):
            nrm, rstd = _rms_parts(c)
            nw = nw_ref[...]
            lat = (nrm * nw).astype(BF16)
            g_ref[...] += _tn(lat, d_up)
            dlat = _nt(d_up, w_ref[...])
            small_ref[srow:srow + 1, :] += jnp.sum(dlat * nrm, axis=0, keepdims=True)
            dn = dlat * nw
            return rstd * (dn - nrm * jnp.mean(dn * nrm, axis=-1, keepdims=True))

        dcq = latent_bwd(blk[:, :QL], qnw_ref, dq_full, wuq_ref, guq_ref, 0)
        dckv = latent_bwd(blk[:, QL:QL + KVL], kvnw_ref, dkv_full, wukv_ref, gukv_ref, 1)
        dp_ref[...] = jnp.concatenate([dcq, dckv, unrope(dkr), jnp.zeros((tm, D - MLA_COLS), F32)], axis=-1).astype(BF16)

    hm = lambda d: pl.BlockSpec((NH, tm, d), lambda i: (0, i, 0))
    return pl.pallas_call(
        body, name="mla_bwd", grid=(s // tm,),
        in_specs=[hm(DQK), hm(DQK), hm(DV), pl.BlockSpec((tm, D), lambda i: (i, 3)), pl.BlockSpec((tm, 1), lambda i: (i, 0)),
                  _const((1, HALF)), _const((1, QL)), _const((1, KVL)), _const((QL, NH * DQK)), _const((KVL, NH * 256))],
        out_specs=[pl.BlockSpec((tm, D), lambda i: (i, 0)), _const((QL, NH * DQK)), _const((KVL, NH * 256)), _const((8, QL))],
        out_shape=[jax.ShapeDtypeStruct((s, D), BF16), jax.ShapeDtypeStruct((QL, NH * DQK), F32),
                   jax.ShapeDtypeStruct((KVL, NH * 256), F32), jax.ShapeDtypeStruct((8, QL), F32)],
        compiler_params=_cparams(1),
    )(dq, dk, dv, proj, pos, inv_freq, qnw, kvnw, wuq, wukv)


def _conv_rows_bwd(dya, u1, proj, ln_w, ln_b):
    s = dya.shape[0]
    tm = min(512, s)

    def body(dya_ref, u1_ref, gate_ref, lw_ref, lb_ref, du1_ref, dag_ref, small_ref):
        @pl.when(pl.program_id(0) == 0)
        def _():
            small_ref[...] = jnp.zeros(small_ref.shape, F32)

        u1 = u1_ref[...]
        mu = jnp.mean(u1, axis=-1, keepdims=True)
        xc = u1 - mu
        rstd = lax.rsqrt(jnp.mean(xc * xc, axis=-1, keepdims=True) + EPS)
        xhat = xc * rstd
        lw = lw_ref[...]
        ln = xhat * lw + lb_ref[...]
        sl = _sig(ln)
        u2 = ln * sl
        gate = gate_ref[...]
        sg = _sig(gate)
        dya = dya_ref[...]
        dag_ref[...] = (dya * u2 * (sg * (1.0 + gate * (1.0 - sg)))).astype(BF16)
        dln = dya * (gate * sg) * (sl * (1.0 + ln * (1.0 - sl)))
        small_ref[0:1, :] += jnp.sum(dln * xhat, axis=0, keepdims=True)
        small_ref[1:2, :] += jnp.sum(dln, axis=0, keepdims=True)
        dxh = dln * lw
        du1_ref[...] = rstd * (dxh - jnp.mean(dxh, axis=-1, keepdims=True) - xhat * jnp.mean(dxh * xhat, axis=-1, keepdims=True))

    row = pl.BlockSpec((tm, D), lambda i: (i, 0))
    return pl.pallas_call(
        body, name="conv_rows_bwd", grid=(s // tm,),
        in_specs=[row, row, pl.BlockSpec((tm, D), lambda i: (i, 2)), _const((1, D)), _const((1, D))],
        out_specs=[row, row, _const((8, D))],
        out_shape=[jax.ShapeDtypeStruct((s, D), F32), jax.ShapeDtypeStruct((s, D), BF16), jax.ShapeDtypeStruct((8, D), F32)],
        compiler_params=_cparams(1),
    )(dya, u1, proj, ln_w, ln_b)


def _conv_bwd(du1, proj, conv_w):
    s = du1.shape[0]
    tm = min(256, s)
    hb = tm // HALO
    n = s // tm
    last32 = s // HALO - 1

    def body(d_ref, dn_ref, av_ref, ag_ref, avh_ref, agh_ref, cw_ref, dp_ref, gcw_ref, small_ref, dwin_ref, uwin_ref, acc_ref):
        i = pl.program_id(0)

        @pl.when(i == 0)
        def _():
            acc_ref[...] = jnp.zeros(acc_ref.shape, F32)
            small_ref[...] = jnp.zeros(small_ref.shape, F32)

        dwin_ref[0:tm, :] = d_ref[...]
        dwin_ref[tm:tm + HALO, :] = jnp.where(i < n - 1, dn_ref[...], 0.0)
        halo = avh_ref[...] * _sig(agh_ref[...])
        uwin_ref[0:HALO, :] = jnp.where(i > 0, halo, 0.0)
        av = av_ref[...]
        sg = _sig(ag_ref[...])
        uwin_ref[HALO:HALO + tm, :] = av * sg
        small_ref[0:1, :] += jnp.sum(d_ref[...], axis=0, keepdims=True)

        for c0 in range(0, D, CONV_LC):
            lanes = pl.ds(c0, CONV_LC)
            for r0 in range(0, tm, CONV_RC):
                acc = jnp.zeros((CONV_RC, CONV_LC), F32)
                for j in range(KCONV):
                    acc = acc + dwin_ref[pl.ds(r0 + KCONV - 1 - j, CONV_RC), lanes] * cw_ref[j:j + 1, c0:c0 + CONV_LC]
                a = av[r0:r0 + CONV_RC, c0:c0 + CONV_LC]
                g = sg[r0:r0 + CONV_RC, c0:c0 + CONV_LC]
                dp_ref[r0:r0 + CONV_RC, c0:c0 + CONV_LC] = (acc * g).astype(BF16)
                dp_ref[r0:r0 + CONV_RC, D + c0:D + c0 + CONV_LC] = (acc * a * (g * (1.0 - g))).astype(BF16)
            for j in range(KCONV):
                part = jnp.zeros((8, CONV_LC), F32)
                for r0 in range(0, tm, CONV_RC):
                    prod = dwin_ref[pl.ds(r0, CONV_RC), lanes] * uwin_ref[pl.ds(r0 + HALO - (KCONV - 1) + j, CONV_RC), lanes]
                    part = part + jnp.sum(prod.reshape(CONV_RC // 8, 8, CONV_LC), axis=0)
                acc_ref[j, :, c0:c0 + CONV_LC] += part

        @pl.when(i == n - 1)
        def _():
            gcw_ref[...] = jnp.sum(acc_ref[...], axis=1)

    row = lambda k: pl.BlockSpec((tm, D), lambda i: (i, k))
    prev = lambda k: pl.BlockSpec((HALO, D), lambda i: (jnp.maximum(i * hb - 1, 0), k))
    return pl.pallas_call(
        body, name="conv_bwd", grid=(n,),
        in_specs=[row(0), pl.BlockSpec((HALO, D), lambda i: (jnp.minimum((i + 1) * hb, last32), 0)),
                  row(0), row(1), prev(0), prev(1), _const((KPAD, D))],
        out_specs=[pl.BlockSpec((tm, 2 * D), lambda i: (i, 0)), _const((KPAD, D)), _const((8, D))],
        out_shape=[jax.ShapeDtypeStruct((s, 2 * D), BF16), jax.ShapeDtypeStruct((KPAD, D), F32), jax.ShapeDtypeStruct((8, D), F32)],
        scratch_shapes=[pltpu.VMEM((tm + HALO, D), F32), pltpu.VMEM((tm + HALO, D), F32), pltpu.VMEM((KPAD, 8, D), F32)],
        compiler_params=_cparams(1),
    )(du1, du1, proj, proj, proj, proj, conv_w)


def _dproj_specs(tm, rows_first):
    def spec(lo, hi):
        def idx(a, b):
            i, k = (a, b) if rows_first else (b, a)
            col = jnp.clip(k - lo, 0, hi - lo - 1)
            if rows_first:
                return (i, col)
            return (jnp.where((k >= lo) & (k < hi), i, 0), col)
        return pl.BlockSpec((tm, D), idx)
    return [spec(0, 2), spec(2, 3), spec(3, 4), spec(4, 7)]


def _pick_dproj(k, refs, fn):
    vg, ag, mla, gates = refs

    @pl.when(k < 2)
    def _():
        fn(vg)

    @pl.when(k == 2)
    def _():
        fn(ag)

    @pl.when(k == 3)
    def _():
        fn(mla)

    @pl.when(k > 3)
    def _():
        fn(gates)


def _in_proj_bwd_x(dps, wp, x, dx2, norm_w, mod8):
    s = x.shape[0]
    tm = min(512, s)
    nk = wp.shape[1] // D

    def body(vg_ref, ag_ref, mla_ref, g_ref, w_ref, x_ref, dx2_ref, nw_ref, mod_ref, gx_ref, small_ref, acc_ref):
        i, k = pl.program_id(0), pl.program_id(1)

        @pl.when((i == 0) & (k == 0))
        def _():
            small_ref[...] = jnp.zeros(small_ref.shape, F32)

        @pl.when(k == 0)
        def _():
            acc_ref[...] = jnp.zeros(acc_ref.shape, F32)

        def add(ref):
            acc_ref[...] += _nt(ref[...], w_ref[...])

        _pick_dproj(k, (vg_ref, ag_ref, mla_ref, g_ref), add)

        @pl.when(k == nk - 1)
        def _():
            dh = acc_ref[...]
            xn, rstd = _rms_parts(x_ref[...])
            nw = nw_ref[...]
            hn = xn * nw
            small_ref[0:1, :] += jnp.sum(dh, axis=0, keepdims=True)
            small_ref[1:2, :] += jnp.sum(dh * hn, axis=0, keepdims=True)
            dhn = dh * (1.0 + mod_ref[1:2, :])
            small_ref[2:3, :] += jnp.sum(dhn * xn, axis=0, keepdims=True)
            dxn = dhn * nw
            gx_ref[...] = rstd * (dxn - xn * jnp.mean(dxn * xn, axis=-1, keepdims=True)) + dx2_ref[...]

    row = pl.BlockSpec((tm, D), lambda i, k: (i, 0))
    return pl.pallas_call(
        body, name="in_proj_bwd_x", grid=(s // tm, nk),
        in_specs=_dproj_specs(tm, True) + [pl.BlockSpec((D, D), lambda i, k: (0, k)), row, row, _const((1, D)), _const((8, D))],
        out_specs=[row, _const((8, D))],
        out_shape=[jax.ShapeDtypeStruct((s, D), F32), jax.ShapeDtypeStruct((8, D), F32)],
        scratch_shapes=[pltpu.VMEM((tm, D), F32)],
        compiler_params=_cparams(2),
    )(*dps, wp, x, dx2, norm_w, mod8)


def _in_proj_bwd_w(dps, h, nk):
    s = h.shape[0]
    tm = min(512, s)

    def body(vg_ref, ag_ref, mla_ref, g_ref, h_ref, gw_ref):
        k, i = pl.program_id(0), pl.program_id(1)

        @pl.when(i == 0)
        def _():
            gw_ref[...] = jnp.zeros(gw_ref.shape, F32)

        def add(ref):
            gw_ref[...] += _tn(h_ref[...], ref[...])

        _pick_dproj(k, (vg_ref, ag_ref, mla_ref, g_ref), add)

    return pl.pallas_call(
        body, name="in_proj_bwd_w", grid=(nk, s // tm),
        in_specs=_dproj_specs(tm, False) + [pl.BlockSpec((tm, D), lambda k, i: (i, 0))],
        out_specs=pl.BlockSpec((D, D), lambda k, i: (0, k)),
        out_shape=jax.ShapeDtypeStruct((D, nk * D), F32),
        compiler_params=_cparams(2),
    )(*dps, h)


def _pack_shard(w_in, wco, wao, wout, wuq, wukv, conv_w):
    cw = jnp.pad(conv_w.reshape(-1), (0, ROWS_CW * D - KCONV * 128)).reshape(ROWS_CW, D)
    parts = [w_in.reshape(ROWS_IN, D), wco, wao, wout, wuq.reshape(ROWS_UQ, D), wukv.reshape(ROWS_UKV, D), cw,
             jnp.zeros((ROWS_PACK - ROWS_IN - 3 * ROWS_SQ - ROWS_UQ - ROWS_UKV - ROWS_CW, D), w_in.dtype)]
    return jnp.concatenate(parts, axis=0)


def _split_rows(slab):
    out, r = [], 0
    for n in (ROWS_IN, ROWS_SQ, ROWS_SQ, ROWS_SQ, ROWS_UQ, ROWS_UKV, ROWS_CW):
        out.append(slab[..., r:r + n, :])
        r += n
    return out


def _unpack_shard(slab):
    w_in, wco, wao, wout, wuq, wukv, cw = _split_rows(slab)
    return (w_in.reshape(D, ROWS_IN), wco, wao, wout, wuq.reshape(QL, NH * DQK // N_DEV), wukv.reshape(KVL, NH * 256 // N_DEV),
            cw.reshape(-1)[:KCONV * 128].reshape(KCONV, 128))


def _unpack_gathered(g):
    w_in, wco, wao, wout, wuq, wukv, cw = _split_rows(g)
    w_in = w_in.reshape(N_DEV, D, ROWS_IN).transpose(1, 0, 2).reshape(D, IN_COLS)
    wuq = wuq.reshape(N_DEV, QL, NH * DQK // N_DEV).transpose(1, 0, 2).reshape(QL, NH * DQK)
    wukv = wukv.reshape(N_DEV, KVL, NH * 256 // N_DEV).transpose(1, 0, 2).reshape(KVL, NH * 256)
    cw = cw.reshape(N_DEV, ROWS_CW * D)[:, :KCONV * 128].reshape(N_DEV, KCONV, 128).transpose(1, 0, 2).reshape(KCONV, D)
    return w_in, wco.reshape(D, D), wao.reshape(D, D), wout.reshape(D, D), wuq, wukv, cw


def _pack_grads(g_in, gco, gao, gout, guq, gukv, gcw):
    g_in = g_in.reshape(D, N_DEV, ROWS_IN).transpose(1, 0, 2).reshape(N_DEV, ROWS_IN, D)
    guq = guq.reshape(QL, N_DEV, -1).transpose(1, 0, 2).reshape(N_DEV, ROWS_UQ, D)
    gukv = gukv.reshape(KVL, N_DEV, -1).transpose(1, 0, 2).reshape(N_DEV, ROWS_UKV, D)
    gcw = gcw.reshape(KCONV, N_DEV, 128).transpose(1, 0, 2).reshape(N_DEV, KCONV * 128)
    gcw = jnp.pad(gcw, ((0, 0), (0, ROWS_CW * D - KCONV * 128))).reshape(N_DEV, ROWS_CW, D)
    sq = lambda a: a.reshape(N_DEV, ROWS_SQ, D)
    pad = jnp.zeros((N_DEV, ROWS_PACK - ROWS_IN - 3 * ROWS_SQ - ROWS_UQ - ROWS_UKV - ROWS_CW, D), F32)
    slabs = jnp.concatenate([g_in, sq(gco), sq(gao), sq(gout), guq, gukv, gcw, pad], axis=1).astype(BF16)
    return slabs.reshape(N_CHIP, 2, ROWS_PACK, D).transpose(1, 0, 2, 3)


def _pack_small(vecs):
    flat = jnp.concatenate([v.reshape(-1) for v in vecs])
    return jnp.pad(flat, (0, SMALL_LEN - flat.shape[0])).reshape(8, SMALL_COLS)


def _unpack_small(a, shapes):
    flat = a.reshape(-1)
    out, off = [], 0
    for shp, n in zip(shapes, SMALL_SIZES):
        out.append(flat[off:off + n].reshape(shp))
        off += n
    return out


def kernel(x, c, positions, w_ada, b_ada, norm_w, w_in, conv_w, conv_b, conv_ln_w, conv_ln_b, w_conv_out, q_norm_w, w_uq, kv_norm_w, w_ukv, w_attn_out, w_out, final_norm_w, loss_target, m_w_ada, m_b_ada, m_norm_w, m_w_in, m_conv_w, m_conv_b, m_conv_ln_w, m_conv_ln_b, m_w_conv_out, m_q_norm_w, m_w_uq, m_kv_norm_w, m_w_ukv, m_w_attn_out, m_w_out, m_final_norm_w, v_w_ada, v_b_ada, v_norm_w, v_w_in, v_conv_w, v_conv_b, v_conv_ln_w, v_conv_ln_b, v_w_conv_out, v_q_norm_w, v_w_uq, v_kv_norm_w, v_w_ukv, v_w_attn_out, v_w_out, v_final_norm_w):
    me = 4 * lax.axis_index("x") + 2 * lax.axis_index("y") + lax.axis_index("c")
    xs, tgt = x[0], loss_target[0]
    s = xs.shape[0]
    ada_cols = w_ada.shape[2]

    sharded = lambda t: tuple(a[0] for a in t)
    w_sh = sharded((w_in, w_conv_out, w_attn_out, w_out, w_uq, w_ukv, conv_w))
    gathered = _all_gather(_pack_shard(*w_sh).astype(BF16), "gather_weights")
    win_f, wco, wao, wout, wuq, wukv, cw = _unpack_gathered(gathered)
    split = 3 * D + MLA_COLS
    wp = jnp.concatenate([win_f[:, :split], jnp.zeros((D, D - MLA_COLS), BF16), win_f[:, split:]], axis=1)
    cw32 = jnp.pad(cw.astype(F32), ((0, KPAD - KCONV), (0, 0)))

    c_all = _all_gather(jnp.broadcast_to(c, (8, D)), "gather_c")[:, 0, :]
    b_cols = lax.dynamic_slice(b_ada, (0, me * ada_cols), (1, ada_cols))
    mod_cols = _all_gather(_ada_mod(c_all, w_ada[0], b_cols), "gather_mod")
    mod = lax.dynamic_index_in_dim(mod_cols, me, axis=1, keepdims=False).reshape(3, D)
    mod8 = jnp.pad(mod, ((0, 5), (0, 0)))

    pos = positions.reshape(s, 1)
    inv_freq = (ROPE_THETA ** (-jnp.arange(0, ROPE, 2, dtype=F32) / ROPE)).reshape(1, HALF)
    proj, h = _in_proj(xs, norm_w, mod8, wp)
    u1, ya = _conv_fwd(proj, cw32, conv_b, conv_ln_w, conv_ln_b)
    q, k, v = _mla_prep(proj, pos, inv_freq, q_norm_w, kv_norm_w, wuq, wukv)
    o, lse = _attn_fwd(q, k, v)

    dya, do, delta, dp_gates, dx2, gw3, small_a = _merge_loss(xs, tgt, ya, o, proj, mod8, final_norm_w.reshape(1, D), wco, wao, wout)
    dq, dk, dv = _attn_bwd(q, k, v, do, lse, delta)
    dp_mla, guq, gukv, small_b = _mla_bwd(dq, dk, dv, proj, pos, inv_freq, q_norm_w, kv_norm_w, wuq, wukv)
    du1, dp_ag, small_c = _conv_rows_bwd(dya, u1, proj, conv_ln_w, conv_ln_b)
    dp_vg, gcw, small_d = _conv_bwd(du1, proj, cw32)
    dps = [dp_vg, dp_ag, dp_mla, dp_gates]
    grad_x, small_e = _in_proj_bwd_x(dps, wp, xs, dx2, norm_w, mod8)
    gwp = _in_proj_bwd_w(dps, h, wp.shape[1] // D)
    g_in = jnp.concatenate([gwp[:, :split], gwp[:, 3 * D + D:]], axis=1)

    packed = _pack_grads(g_in, gw3[0], gw3[1], gw3[2], guq, gukv, gcw[:KCONV])
    core = lax.axis_index("c").astype(jnp.int32).reshape(1)
    half = _pair_add(core, packed, _pair_exchange(packed), ROWS_PACK // 4)
    recv = _chip_exchange(half)
    big = _reduce_adam(recv, _pack_shard(*w_sh), _pack_shard(*sharded((m_w_in, m_w_conv_out, m_w_attn_out, m_w_out, m_w_uq, m_w_ukv, m_conv_w))),
                       _pack_shard(*sharded((v_w_in, v_w_conv_out, v_w_attn_out, v_w_out, v_w_uq, v_w_ukv, v_conv_w))), "adam_sharded", ROWS_PACK // 4)
    big = [_unpack_shard(a) for a in big]

    dmod = jnp.concatenate([small_e[0], small_e[1], small_a[1]])
    payload = _pack_small([dmod, small_e[2], small_d[0], small_c[0], small_c[1], small_a[0], small_b[0], small_b[1], small_a[2, 0:1]])
    pay_all = _all_gather(payload, "gather_small")
    small_w = (b_ada, norm_w, conv_b, conv_ln_w, conv_ln_b, final_norm_w, q_norm_w, kv_norm_w)
    small_m = (m_b_ada, m_norm_w, m_conv_b, m_conv_ln_w, m_conv_ln_b, m_final_norm_w, m_q_norm_w, m_kv_norm_w)
    small_v = (v_b_ada, v_norm_w, v_conv_b, v_conv_ln_w, v_conv_ln_b, v_final_norm_w, v_q_norm_w, v_kv_norm_w)
    sm = _reduce_adam(pay_all, _pack_small(small_w), _pack_small(small_m), _pack_small(small_v), "adam_replicated", 8)
    loss = sm[0].reshape(-1)[sum(SMALL_SIZES)]
    shapes = [t.shape for t in small_w]
    sm = [_unpack_small(a, shapes) for a in sm]

    dmod_all = pay_all.reshape(N_DEV, SMALL_LEN)[:, :3 * D]
    dmod_cols = lax.dynamic_slice(dmod_all, (0, me * ada_cols), (N_DEV, ada_cols))
    ada = _ada_bwd(c_all, dmod_cols, w_ada[0], m_w_ada[0], v_w_ada[0])

    def group(i):
        b_in, b_co, b_ao, b_out, b_uq, b_ukv, b_cw = big[i]
        s_bada, s_nw, s_cb, s_clw, s_clb, s_fnw, s_qnw, s_kvnw = sm[i]
        return (ada[i][None], s_bada, s_nw, b_in[None], b_cw[None], s_cb, s_clw, s_clb, b_co[None], s_qnw, b_uq[None], s_kvnw,
                b_ukv[None], b_ao[None], b_out[None], s_fnw)

    return (loss, grad_x[None], *group(0), *group(1), *group(2), *group(3))
```

```python
import functools

import jax
import jax.numpy as jnp
from jax import lax
from jax.experimental import pallas as pl
from jax.experimental.pallas import tpu as pltpu

F32 = jnp.float32
BF16 = jnp.bfloat16

D = 1024
NH = 8
NOPE = 128
ROPE = 64
HALF = ROPE // 2
DQK = NOPE + ROPE
DV = 128
QL = 256
KVL = 256
KCONV = 31
KPAD = 32
HALO = 32
IN_COLS = 6720
MLA_COLS = QL + KVL + ROPE
PROJ_COLS = 7 * D
EPS = 1e-6
ROPE_THETA = 10000.0
N_DEV = 8

ADAM_LR = 0.001
ADAM_B1 = 0.9
ADAM_B2 = 0.999
ADAM_EPS = 1e-08
ADAM_WD = 0.01
ADAM_STEP = 10

ROWS_IN = 840
ROWS_SQ = 128
ROWS_UQ = 48
ROWS_UKV = 64
ROWS_CW = 4
OFF_SQ = 896
OFF_TAIL = OFF_SQ + 3 * ROWS_SQ
ROWS_TAIL = 128
ROWS_PACK = OFF_TAIL + ROWS_TAIL
SMALL_SIZES = (3 * D, D, D, D, D, D, QL, KVL)
SMALL_COLS = 1152
SMALL_LEN = 8 * SMALL_COLS

MESH = pl.DeviceIdType.MESH
ANY = pl.BlockSpec(memory_space=pl.ANY)
V7X_VMEM_LIMIT = 56 * 1024 * 1024


def _cparams(n_axes, vmem=V7X_VMEM_LIMIT):
    return pltpu.CompilerParams(dimension_semantics=("arbitrary",) * n_axes, vmem_limit_bytes=vmem)


def _sig(x):
    return jax.nn.sigmoid(x)


def _nt(a, b):
    return lax.dot_general(a, b, (((1,), (1,)), ((), ())), preferred_element_type=F32)


def _tn(a, b):
    return lax.dot_general(a, b, (((0,), (0,)), ((), ())), preferred_element_type=F32)


def _nn(a, b):
    return jnp.dot(a, b, preferred_element_type=F32)


def _const(shape):
    return pl.BlockSpec(shape, lambda *_: (0,) * len(shape))


def _all_gather(block, name):
    r, c = block.shape

    def body(x_ref, out_ref, send_sems, recv_sems, local_sem):
        x, y, cc = lax.axis_index("x"), lax.axis_index("y"), lax.axis_index("c")
        me, sibling = (x, y, cc), (x, y, 1 - cc)
        chips = [(1 - x, y), (x, 1 - y), (1 - x, 1 - y)]

        def slot(px, py, pc):
            return out_ref.at[4 * px + 2 * py + pc]

        def copy(k, blk, to, src=None):
            return pltpu.make_async_remote_copy(
                src_ref=slot(*blk) if src is None else src, dst_ref=slot(*blk),
                send_sem=send_sems.at[k], recv_sem=recv_sems.at[k],
                device_id=to, device_id_type=MESH)

        mine = pltpu.make_async_copy(x_ref, slot(*me), local_sem)
        mine.start()
        first = [copy(0, me, sibling, src=x_ref)]
        first += [copy(1 + j, me, (*chip, cc), src=x_ref) for j, chip in enumerate(chips)]
        for cp in first:
            cp.start()
        passed = [copy(4 + j, (*chip, cc), sibling) for j, chip in enumerate(chips)]
        for j, chip in enumerate(chips):
            copy(1 + j, (*chip, cc), me).wait_recv()
            passed[j].start()
        copy(0, sibling, me).wait_recv()
        for j, chip in enumerate(chips):
            copy(4 + j, (*chip, 1 - cc), me).wait_recv()
        for cp in first + passed:
            cp.wait_send()
        mine.wait()

    return pl.pallas_call(
        body, name=name,
        out_shape=jax.ShapeDtypeStruct((N_DEV, r, c), block.dtype),
        in_specs=[ANY], out_specs=ANY,
        scratch_shapes=[pltpu.SemaphoreType.DMA((7,)), pltpu.SemaphoreType.DMA((7,)), pltpu.SemaphoreType.DMA],
    )(block)


N_CHIP = 4


def _pair_exchange(packed):
    _, _, r, c = packed.shape

    def body(src_ref, out_ref, send_sem, recv_sem):
        x, y, cc = lax.axis_index("x"), lax.axis_index("y"), lax.axis_index("c")
        cp = pltpu.make_async_remote_copy(
            src_ref=src_ref.at[1 - cc], dst_ref=out_ref, send_sem=send_sem, recv_sem=recv_sem,
            device_id=(x, y, 1 - cc), device_id_type=MESH)
        cp.start()
        cp.wait()

    return pl.pallas_call(
        body, name="pair_exchange",
        out_shape=jax.ShapeDtypeStruct((N_CHIP, r, c), packed.dtype),
        in_specs=[ANY], out_specs=ANY,
        scratch_shapes=[pltpu.SemaphoreType.DMA, pltpu.SemaphoreType.DMA],
    )(packed)


def _pair_add(core, packed, got, tr):
    _, _, r, c = packed.shape

    def body(core_ref, own_ref, got_ref, out_ref):
        out_ref[...] = (own_ref[0].astype(F32) + got_ref[...].astype(F32)).astype(out_ref.dtype)

    blk = pl.BlockSpec((1, tr, c), lambda j, i, core_ref: (j, i, 0))
    return pl.pallas_call(
        body, name="pair_add",
        grid_spec=pltpu.PrefetchScalarGridSpec(
            num_scalar_prefetch=1, grid=(N_CHIP, r // tr),
            in_specs=[pl.BlockSpec((1, 1, tr, c), lambda j, i, core_ref: (core_ref[0], j, i, 0)), blk],
            out_specs=blk),
        out_shape=jax.ShapeDtypeStruct((N_CHIP, r, c), packed.dtype),
        compiler_params=_cparams(2),
    )(core, packed, got)


def _chip_exchange(half):
    _, r, c = half.shape

    def body(src_ref, out_ref, send_sems, recv_sems, local_sem):
        x, y, cc = lax.axis_index("x"), lax.axis_index("y"), lax.axis_index("c")
        me = 2 * x + y
        mine = pltpu.make_async_copy(src_ref.at[me], out_ref.at[me], local_sem)
        mine.start()

        def peer(k):
            return (1 - x if k & 2 else x), (1 - y if k & 1 else y)

        copies = []
        for k in range(1, N_CHIP):
            px, py = peer(k)
            copies.append(pltpu.make_async_remote_copy(
                src_ref=src_ref.at[2 * px + py], dst_ref=out_ref.at[me],
                send_sem=send_sems.at[k - 1], recv_sem=recv_sems.at[k - 1],
                device_id=(px, py, cc), device_id_type=MESH))
        for cp in copies:
            cp.start()
        for k in range(1, N_CHIP):
            px, py = peer(k)
            pltpu.make_async_remote_copy(
                src_ref=src_ref.at[me], dst_ref=out_ref.at[2 * px + py],
                send_sem=send_sems.at[k - 1], recv_sem=recv_sems.at[k - 1],
                device_id=(px, py, cc), device_id_type=MESH).wait_recv()
        for cp in copies:
            cp.wait_send()
        mine.wait()

    return pl.pallas_call(
        body, name="chip_exchange",
        out_shape=jax.ShapeDtypeStruct(half.shape, half.dtype),
        in_specs=[ANY], out_specs=ANY,
        scratch_shapes=[pltpu.SemaphoreType.DMA((3,)), pltpu.SemaphoreType.DMA((3,)), pltpu.SemaphoreType.DMA],
    )(half)


def _adam(g, w, m, v):
    m = ADAM_B1 * m + (1.0 - ADAM_B1) * g
    v = ADAM_B2 * v + (1.0 - ADAM_B2) * (g * g)
    m_hat = m / (1.0 - ADAM_B1 ** ADAM_STEP)
    v_hat = v / (1.0 - ADAM_B2 ** ADAM_STEP)
    delta = -ADAM_LR * (m_hat / (jnp.sqrt(v_hat) + ADAM_EPS) + ADAM_WD * w)
    return delta, m, v


def _adam_w_in(parts, w, m, v):
    n = parts.shape[0]
    tc = 128

    def body(p_ref, w_ref, m_ref, v_ref, g_out, d_out, m_out, v_out):
        gt = p_ref[0].astype(F32)
        for j in range(1, n):
            gt = gt + p_ref[j].astype(F32)
        g = gt.T[:, :ROWS_IN]
        delta, nm, nv = _adam(g, w_ref[...], m_ref[...], v_ref[...])
        g_out[...] = g
        d_out[...] = delta
        m_out[...] = nm
        v_out[...] = nv

    row = pl.BlockSpec((tc, ROWS_IN), lambda i: (i, 0))
    return pl.pallas_call(
        body, name="adam_w_in", grid=(D // tc,),
        in_specs=[pl.BlockSpec((n, OFF_SQ, tc), lambda i: (0, 0, i)), row, row, row],
        out_specs=[row] * 4,
        out_shape=[jax.ShapeDtypeStruct((D, ROWS_IN), F32)] * 4,
        compiler_params=_cparams(1),
    )(parts, w, m, v)


def _reduce_adam(parts, w, m, v, name, tr, first_block):
    n, _, c = parts.shape
    r = w.shape[0]

    def body(p_ref, w_ref, m_ref, v_ref, g_out, d_out, m_out, v_out):
        g = p_ref[0].astype(F32)
        for j in range(1, n):
            g = g + p_ref[j].astype(F32)
        delta, nm, nv = _adam(g, w_ref[...], m_ref[...], v_ref[...])
        g_out[...] = g
        d_out[...] = delta
        m_out[...] = nm
        v_out[...] = nv

    row = pl.BlockSpec((tr, c), lambda i: (i, 0))
    return pl.pallas_call(
        body, name=name, grid=(r // tr,),
        in_specs=[pl.BlockSpec((n, tr, c), lambda i: (0, first_block + i, 0)), row, row, row],
        out_specs=[row] * 4,
        out_shape=[jax.ShapeDtypeStruct((r, c), F32)] * 4,
        compiler_params=_cparams(1),
    )(parts, w, m, v)


def _ada_mod(c_all, w_ada, b_cols):
    def body(c_ref, w_ref, b_ref, o_ref):
        cv = c_ref[...]
        act = (cv * _sig(cv)).astype(BF16)
        o_ref[...] = _nn(act, w_ref[...].astype(BF16)) + b_ref[...]

    return pl.pallas_call(body, name="ada_mod", out_shape=jax.ShapeDtypeStruct((N_DEV, w_ada.shape[1]), F32))(c_all, w_ada, b_cols)


def _ada_bwd(c_all, dmod_cols, w, m, v):
    def body(c_ref, d_ref, w_ref, m_ref, v_ref, g_out, d_out, m_out, v_out):
        cv = c_ref[...]
        act = (cv * _sig(cv)).astype(BF16)
        g = _tn(act, d_ref[...].astype(BF16))
        delta, nm, nv = _adam(g, w_ref[...], m_ref[...], v_ref[...])
        g_out[...] = g
        d_out[...] = delta
        m_out[...] = nm
        v_out[...] = nv

    return pl.pallas_call(body, name="ada_bwd", out_shape=[jax.ShapeDtypeStruct(w.shape, F32)] * 4)(c_all, dmod_cols, w, m, v)


def _in_proj(x, norm_w, mod8, wt):
    s = x.shape[0]
    tm = min(1024, s)
    nk = wt.shape[0] // D

    def body(x_ref, nw_ref, mod_ref, w_ref, proj_ref, h_ref, hs_ref):
        @pl.when(pl.program_id(1) == 0)
        def _():
            xv = x_ref[...]
            rstd = lax.rsqrt(jnp.mean(xv * xv, axis=-1, keepdims=True) + EPS)
            h = (xv * rstd) * nw_ref[...] * (1.0 + mod_ref[1:2, :]) + mod_ref[0:1, :]
            hs_ref[...] = h.astype(BF16)
            h_ref[...] = hs_ref[...]

        proj_ref[...] = _nt(hs_ref[...], w_ref[...])

    return pl.pallas_call(
        body, name="in_proj", grid=(s // tm, nk),
        in_specs=[pl.BlockSpec((tm, D), lambda i, k: (i, 0)), _const((1, D)), _const((8, D)),
                  pl.BlockSpec((D, D), lambda i, k: (k, 0))],
        out_specs=[pl.BlockSpec((tm, D), lambda i, k: (i, k)), pl.BlockSpec((tm, D), lambda i, k: (i, 0))],
        out_shape=[jax.ShapeDtypeStruct((s, nk * D), F32), jax.ShapeDtypeStruct((s, D), BF16)],
        scratch_shapes=[pltpu.VMEM((tm, D), BF16)],
        compiler_params=_cparams(2),
    )(x, norm_w, mod8, wt)


CONV_RC = 64
CONV_LC = 256


def _conv_fwd(proj, conv_w, conv_b, ln_w, ln_b):
    s = proj.shape[0]
    tm = min(256, s)
    hb = tm // HALO

    def body(av_ref, ag_ref, avh_ref, agh_ref, gate_ref, cw_ref, cb_ref, lw_ref, lb_ref, u1_ref, ya_ref, win_ref):
        i = pl.program_id(0)
        halo = avh_ref[...] * _sig(agh_ref[...])
        win_ref[0:HALO, :] = jnp.where(i > 0, halo, 0.0)
        win_ref[HALO:HALO + tm, :] = av_ref[...] * _sig(ag_ref[...])
        for r0 in range(0, tm, CONV_RC):
            for c0 in range(0, D, CONV_LC):
                acc = jnp.zeros((CONV_RC, CONV_LC), F32) + cb_ref[:, c0:c0 + CONV_LC]
                for j in range(KCONV):
                    acc = acc + win_ref[pl.ds(r0 + HALO - (KCONV - 1) + j, CONV_RC), pl.ds(c0, CONV_LC)] * cw_ref[j:j + 1, c0:c0 + CONV_LC]
                u1_ref[r0:r0 + CONV_RC, c0:c0 + CONV_LC] = acc
        u1 = u1_ref[...]
        mu = jnp.mean(u1, axis=-1, keepdims=True)
        xc = u1 - mu
        var = jnp.mean(xc * xc, axis=-1, keepdims=True)
        ln = xc * lax.rsqrt(var + EPS) * lw_ref[...] + lb_ref[...]
        gate = gate_ref[...]
        ya_ref[...] = ((ln * _sig(ln)) * (gate * _sig(gate))).astype(BF16)

    row = lambda k: pl.BlockSpec((tm, D), lambda i: (i, k))
    prev = lambda k: pl.BlockSpec((HALO, D), lambda i: (jnp.maximum(i * hb - 1, 0), k))
    return pl.pallas_call(
        body, name="conv_fwd", grid=(s // tm,),
        in_specs=[row(0), row(1), prev(0), prev(1), row(2), _const((KPAD, D)), _const((1, D)), _const((1, D)), _const((1, D))],
        out_specs=[pl.BlockSpec((tm, D), lambda i: (i, 0))] * 2,
        out_shape=[jax.ShapeDtypeStruct((s, D), F32), jax.ShapeDtypeStruct((s, D), BF16)],
        scratch_shapes=[pltpu.VMEM((tm + HALO, D), F32)],
        compiler_params=_cparams(1),
    )(proj, proj, proj, proj, proj, conv_w, conv_b, ln_w, ln_b)


def _rope_tables(pos_ref, if_ref):
    ang = pos_ref[...].astype(F32) * if_ref[...]
    return jnp.cos(ang), jnp.sin(ang)


def _rms_parts(x):
    rstd = lax.rsqrt(jnp.mean(x * x, axis=-1, keepdims=True) + EPS)
    return x * rstd, rstd


def _mla_prep(proj, pos, inv_freq, qnw, kvnw, wuq, wukv):
    s = proj.shape[0]
    tm = min(512, s)

    def body(p_ref, pos_ref, if_ref, qnw_ref, kvnw_ref, wuq_ref, wukv_ref, q_ref, k_ref, v_ref):
        blk = p_ref[...]
        cos, sin = _rope_tables(pos_ref, if_ref)

        def rope(r):
            x1, x2 = r[:, :HALF], r[:, HALF:]
            return jnp.concatenate([x1 * cos - x2 * sin, x1 * sin + x2 * cos], axis=-1)

        qlat = _rms_parts(blk[:, :QL])[0] * qnw_ref[...]
        kvlat = _rms_parts(blk[:, QL:QL + KVL])[0] * kvnw_ref[...]
        q = _nn(qlat.astype(BF16), wuq_ref[...])
        kv = _nn(kvlat.astype(BF16), wukv_ref[...])
        kr = rope(blk[:, QL + KVL:MLA_COLS])
        for h in range(NH):
            qh = q[:, h * DQK:(h + 1) * DQK]
            q_ref[h] = (jnp.concatenate([qh[:, :NOPE], rope(qh[:, NOPE:])], axis=-1) * ATTN_SCALE).astype(BF16)
            k_ref[h] = jnp.concatenate([kv[:, h * 256:h * 256 + NOPE], kr], axis=-1).astype(BF16)
            v_ref[h] = kv[:, h * 256 + NOPE:(h + 1) * 256].astype(BF16)

    hm = lambda d: pl.BlockSpec((NH, tm, d), lambda i: (0, i, 0))
    return pl.pallas_call(
        body, name="mla_prep", grid=(s // tm,),
        in_specs=[pl.BlockSpec((tm, D), lambda i: (i, 3)), pl.BlockSpec((tm, 1), lambda i: (i, 0)), _const((1, HALF)),
                  _const((1, QL)), _const((1, KVL)), _const((QL, NH * DQK)), _const((KVL, NH * 256))],
        out_specs=[hm(DQK), hm(DQK), hm(DV)],
        out_shape=[jax.ShapeDtypeStruct((NH, s, DQK), BF16), jax.ShapeDtypeStruct((NH, s, DQK), BF16),
                   jax.ShapeDtypeStruct((NH, s, DV), BF16)],
        compiler_params=_cparams(1),
    )(proj, pos, inv_freq, qnw, kvnw, wuq, wukv)


ATTN_SCALE = DQK ** -0.5


def _causal_mask(s, t):
    rows = lax.broadcasted_iota(jnp.int32, (t, t), 0)
    cols = lax.broadcasted_iota(jnp.int32, (t, t), 1)
    return jnp.where(cols <= rows, s, -jnp.inf)


def _attn_tile(s):
    return min(1024, s // 2)


def _attn_fwd(q, k, v):
    nh, s, _ = q.shape
    t = _attn_tile(s)

    def body(q_ref, k_ref, v_ref, o_ref, lse_ref):
        qi = pl.program_id(1)
        qv = q_ref[0]

        def chunk(c, carry, diag):
            m, l, acc = carry
            rows = pl.ds(pl.multiple_of(c * t, t), t)
            sc = _nt(qv, k_ref[0, rows, :])
            if diag:
                sc = _causal_mask(sc, t)
            m_new = jnp.maximum(m, jnp.max(sc, axis=-1, keepdims=True))
            alpha = jnp.exp(m - m_new)
            p = jnp.exp(sc - m_new)
            l = alpha * l + jnp.sum(p, axis=-1, keepdims=True)
            acc = alpha * acc + _nn(p.astype(BF16), v_ref[0, rows, :])
            return m_new, l, acc

        init = (jnp.full((t, 1), -jnp.inf, F32), jnp.zeros((t, 1), F32), jnp.zeros((t, DV), F32))
        carry = lax.fori_loop(0, qi, lambda c, cr: chunk(c, cr, False), init)
        m, l, acc = chunk(qi, carry, True)
        o_ref[...] = acc / l
        lse_ref[0] = jnp.broadcast_to(m + jnp.log(l), (t, DV))

    head = lambda d: pl.BlockSpec((1, s, d), lambda h, i: (h, 0, 0))
    return pl.pallas_call(
        body, name="attn_fwd", grid=(nh, s // t),
        in_specs=[pl.BlockSpec((1, t, DQK), lambda h, i: (h, i, 0)), head(DQK), head(DV)],
        out_specs=[pl.BlockSpec((t, DV), lambda h, i: (i, h)), pl.BlockSpec((1, t, DV), lambda h, i: (h, i, 0))],
        out_shape=[jax.ShapeDtypeStruct((s, nh * DV), F32), jax.ShapeDtypeStruct((nh, s, DV), F32)],
        compiler_params=_cparams(2),
    )(q, k, v)


def _merge_loss(x, target, ya, o, proj, mod8, fnw, gathered):
    s = x.shape[0]
    tm = min(256, s)
    n = s // tm

    def body(x_ref, t_ref, ya_ref, o_ref, bg_ref, ga_ref, gb_ref, mod_ref, fnw_ref, wco_ref, wao_ref, wout_ref,
             dya_ref, do_ref, delta_ref, dpg_ref, dx2_ref, gw_ref, small_ref, acc_ref):
        i = pl.program_id(0)

        @pl.when(i == 0)
        def _():
            acc_ref[...] = jnp.zeros(acc_ref.shape, F32)
            small_ref[...] = jnp.zeros(small_ref.shape, F32)

        bg = bg_ref[...]
        sbg = _sig(bg)
        sb = bg * sbg
        ov = o_ref[...]
        ya = ya_ref[...]
        yb = (ov * sb).astype(BF16)
        square = lambda ref: ref[...].reshape(D, D)
        y_a = _nn(ya, square(wco_ref))
        y_b = _nn(yb, square(wao_ref))
        sa = _sig(ga_ref[...])
        sgb = _sig(gb_ref[...])
        merged = (sa * y_a + sgb * y_b).astype(BF16)
        z = _nn(merged, square(wout_ref))
        gate = mod_ref[2:3, :]
        x2 = x_ref[...] + gate * z
        xn, rstd = _rms_parts(x2)
        fnw = fnw_ref[...]
        err = xn * fnw - t_ref[...]
        loss = jnp.sum(jnp.sum(err * err, axis=-1, keepdims=True), axis=0, keepdims=True) * (0.5 / D)
        dy = err * (1.0 / D)
        small_ref[0:1, :] += jnp.sum(dy * xn, axis=0, keepdims=True)
        dxn = dy * fnw
        dx2 = rstd * (dxn - xn * jnp.mean(dxn * xn, axis=-1, keepdims=True))
        dx2_ref[...] = dx2
        small_ref[1:2, :] += jnp.sum(dx2 * z, axis=0, keepdims=True)
        small_ref[2:3, :] += jnp.broadcast_to(loss, (1, D))
        dz = (dx2 * gate).astype(BF16)
        dmerged = _nt(dz, square(wout_ref))
        acc_ref[2] += _tn(merged, dz)
        dy_a = (dmerged * sa).astype(BF16)
        dy_b = (dmerged * sgb).astype(BF16)
        dpg_ref[:, D:2 * D] = (dmerged * y_a * (sa * (1.0 - sa))).astype(BF16)
        dpg_ref[:, 2 * D:3 * D] = (dmerged * y_b * (sgb * (1.0 - sgb))).astype(BF16)
        dya_ref[...] = _nt(dy_a, square(wco_ref))
        acc_ref[0] += _tn(ya, dy_a)
        dyb = _nt(dy_b, square(wao_ref))
        acc_ref[1] += _tn(yb, dy_b)
        do = dyb * sb
        do_ref[...] = do.astype(BF16)
        dpg_ref[:, 0:D] = (dyb * ov * (sbg * (1.0 + bg * (1.0 - sbg)))).astype(BF16)
        prod = do * ov
        for h in range(NH):
            delta_ref[h] = jnp.broadcast_to(jnp.sum(prod[:, h * DV:(h + 1) * DV], axis=-1, keepdims=True), (tm, DV))

        @pl.when(i == n - 1)
        def _():
            pltpu.sync_copy(acc_ref, gw_ref)

    row = pl.BlockSpec((tm, D), lambda i: (i, 0))
    col = lambda k: pl.BlockSpec((tm, D), lambda i: (i, k))
    wspec = lambda j: pl.BlockSpec((N_DEV, ROWS_SQ, D), lambda i: (0, OFF_SQ // ROWS_SQ + j, 0), pipeline_mode=pl.Buffered(1))
    return pl.pallas_call(
        body, name="merge_loss", grid=(n,),
        in_specs=[row, row, row, row, col(4), col(5), col(6), _const((8, D)), _const((1, D)), wspec(0), wspec(1), wspec(2)],
        out_specs=[row, row, pl.BlockSpec((NH, tm, DV), lambda i: (0, i, 0)), pl.BlockSpec((tm, 3 * D), lambda i: (i, 0)),
                   row, ANY, _const((8, D))],
        out_shape=[jax.ShapeDtypeStruct((s, D), F32), jax.ShapeDtypeStruct((s, D), BF16),
                   jax.ShapeDtypeStruct((NH, s, DV), F32), jax.ShapeDtypeStruct((s, 3 * D), BF16),
                   jax.ShapeDtypeStruct((s, D), F32), jax.ShapeDtypeStruct((3, D, D), F32),
                   jax.ShapeDtypeStruct((8, D), F32)],
        scratch_shapes=[pltpu.VMEM((3, D, D), F32)],
        compiler_params=_cparams(1),
    )(x, target, ya, o, proj, proj, proj, mod8, fnw, gathered, gathered, gathered)


def _attn_bwd(q, k, v, do, lse, delta):
    nh, s, _ = q.shape
    t = _attn_tile(s)
    nb = s // t

    def body(q_ref, k_ref, v_ref, do_ref, lse_ref, dl_ref, dq_ref, dk_ref, dv_ref):
        kj = pl.program_id(1)

        @pl.when(kj == 0)
        def _():
            dq_ref[...] = jnp.zeros(dq_ref.shape, F32)

        kv_, vv = k_ref[0], v_ref[0]

        def chunk(c, carry, diag):
            dk, dv = carry
            rows = pl.ds(pl.multiple_of(c * t, t), t)
            qv = q_ref[0, rows, :]
            dov = do_ref[rows, :]
            sc = _nt(qv, kv_)
            if diag:
                sc = _causal_mask(sc, t)
            p = jnp.exp(sc - lse_ref[0, rows, 0:1])
            dv = dv + _tn(p.astype(BF16), dov)
            dp = _nt(dov, vv)
            ds = (p * (dp - dl_ref[0, rows, 0:1])).astype(BF16)
            dk = dk + _tn(ds, qv)
            dq_ref[0, rows, :] += _nn(ds, kv_)
            return dk, dv

        carry = chunk(kj, (jnp.zeros((t, DQK), F32), jnp.zeros((t, DV), F32)), True)
        dk, dv = lax.fori_loop(kj + 1, nb, lambda c, cr: chunk(c, cr, False), carry)
        dk_ref[0] = dk
        dv_ref[0] = dv

    head = lambda d: pl.BlockSpec((1, s, d), lambda h, j: (h, 0, 0))
    blk = lambda d: pl.BlockSpec((1, t, d), lambda h, j: (h, j, 0))
    return pl.pallas_call(
        body, name="attn_bwd", grid=(nh, nb),
        in_specs=[head(DQK), blk(DQK), blk(DV), pl.BlockSpec((s, DV), lambda h, j: (0, h)), head(DV), head(DV)],
        out_specs=[head(DQK), blk(DQK), blk(DV)],
        out_shape=[jax.ShapeDtypeStruct((nh, s, DQK), F32), jax.ShapeDtypeStruct((nh, s, DQK), F32),
                   jax.ShapeDtypeStruct((nh, s, DV), F32)],
        compiler_params=_cparams(2),
    )(q, k, v, do, lse, delta)


def _mla_bwd(dq, dk, dv, proj, pos, inv_freq, qnw, kvnw, wuq, wukv):
    s = proj.shape[0]
    tm = min(512, s)

    def body(dq_ref, dk_ref, dv_ref, p_ref, pos_ref, if_ref, qnw_ref, kvnw_ref, wuq_ref, wukv_ref,
             dp_ref, guq_ref, gukv_ref, small_ref):
        @pl.when(pl.program_id(0) == 0)
        def _():
            guq_ref[...] = jnp.zeros(guq_ref.shape, F32)
            gukv_ref[...] = jnp.zeros(gukv_ref.shape, F32)
            small_ref[...] = jnp.zeros(small_ref.shape, F32)

        blk = p_ref[...]
        cos, sin = _rope_tables(pos_ref, if_ref)

        def unrope(g):
            g1, g2 = g[:, :HALF], g[:, HALF:]
            return jnp.concatenate([g1 * cos + g2 * sin, g2 * cos - g1 * sin], axis=-1)

        dq_cols, dkv_cols = [], []
        dkr = jnp.zeros((tm, ROPE), F32)
        for h in range(NH):
            dqh, dkh = dq_ref[h] * ATTN_SCALE, dk_ref[h]
            dq_cols += [dqh[:, :NOPE], unrope(dqh[:, NOPE:])]
            dkv_cols += [dkh[:, :NOPE], dv_ref[h]]
            dkr = dkr + dkh[:, NOPE:]
        dq_full = jnp.concatenate(dq_cols, axis=-1).astype(BF16)
        dkv_full = jnp.concatenate(dkv_cols, axis=-1).astype(BF16)

        def latent_bwd(c, nw_ref, d_up, w_ref, g_ref, srow):
            nrm, rstd = _rms_parts(c)
            nw = nw_ref[...]
            lat = (nrm * nw).astype(BF16)
            g_ref[...] += _tn(lat, d_up)
            dlat = _nt(d_up, w_ref[...])
            small_ref[srow:srow + 1, :] += jnp.sum(dlat * nrm, axis=0, keepdims=True)
            dn = dlat * nw
            return rstd * (dn - nrm * jnp.mean(dn * nrm, axis=-1, keepdims=True))

        dcq = latent_bwd(blk[:, :QL], qnw_ref, dq_full, wuq_ref, guq_ref, 0)
        dckv = latent_bwd(blk[:, QL:QL + KVL], kvnw_ref, dkv_full, wukv_ref, gukv_ref, 1)
        dp_ref[...] = jnp.concatenate([dcq, dckv, unrope(dkr), jnp.zeros((tm, D - MLA_COLS), F32)], axis=-1).astype(BF16)

    hm = lambda d: pl.BlockSpec((NH, tm, d), lambda i: (0, i, 0))
    return pl.pallas_call(
        body, name="mla_bwd", grid=(s // tm,),
        in_specs=[hm(DQK), hm(DQK), hm(DV), pl.BlockSpec((tm, D), lambda i: (i, 3)), pl.BlockSpec((tm, 1), lambda i: (i, 0)),
                  _const((1, HALF)), _const((1, QL)), _const((1, KVL)), _const((QL, NH * DQK)), _const((KVL, NH * 256))],
        out_specs=[pl.BlockSpec((tm, D), lambda i: (i, 0)), _const((QL, NH * DQK)), _const((KVL, NH * 256)), _const((8, QL))],
        out_shape=[jax.ShapeDtypeStruct((s, D), BF16), jax.ShapeDtypeStruct((QL, NH * DQK), F32),
                   jax.ShapeDtypeStruct((KVL, NH * 256), F32), jax.ShapeDtypeStruct((8, QL), F32)],
        compiler_params=_cparams(1),
    )(dq, dk, dv, proj, pos, inv_freq, qnw, kvnw, wuq, wukv)


def _conv_rows_bwd(dya, u1, proj, ln_w, ln_b):
    s = dya.shape[0]
    tm = min(512, s)

    def body(dya_ref, u1_ref, gate_ref, lw_ref, lb_ref, du1_ref, dag_ref, small_ref):
        @pl.when(pl.program_id(0) == 0)
        def _():
            small_ref[...] = jnp.zeros(small_ref.shape, F32)

        u1 = u1_ref[...]
        mu = jnp.mean(u1, axis=-1, keepdims=True)
        xc = u1 - mu
        rstd = lax.rsqrt(jnp.mean(xc * xc, axis=-1, keepdims=True) + EPS)
        xhat = xc * rstd
        lw = lw_ref[...]
        ln = xhat * lw + lb_ref[...]
        sl = _sig(ln)
        u2 = ln * sl
        gate = gate_ref[...]
        sg = _sig(gate)
        dya = dya_ref[...]
        dag_ref[...] = (dya * u2 * (sg * (1.0 + gate * (1.0 - sg)))).astype(BF16)
        dln = dya * (gate * sg) * (sl * (1.0 + ln * (1.0 - sl)))
        small_ref[0:1, :] += jnp.sum(dln * xhat, axis=0, keepdims=True)
        small_ref[1:2, :] += jnp.sum(dln, axis=0, keepdims=True)
        dxh = dln * lw
        du1_ref[...] = rstd * (dxh - jnp.mean(dxh, axis=-1, keepdims=True) - xhat * jnp.mean(dxh * xhat, axis=-1, keepdims=True))

    row = pl.BlockSpec((tm, D), lambda i: (i, 0))
    return pl.pallas_call(
        body, name="conv_rows_bwd", grid=(s // tm,),
        in_specs=[row, row, pl.BlockSpec((tm, D), lambda i: (i, 2)), _const((1, D)), _const((1, D))],
        out_specs=[row, row, _const((8, D))],
        out_shape=[jax.ShapeDtypeStruct((s, D), F32), jax.ShapeDtypeStruct((s, D), BF16), jax.ShapeDtypeStruct((8, D), F32)],
        compiler_params=_cparams(1),
    )(dya, u1, proj, ln_w, ln_b)


def _conv_bwd(du1, proj, conv_w):
    s = du1.shape[0]
    tm = min(256, s)
    hb = tm // HALO
    n = s // tm
    last32 = s // HALO - 1

    def body(d_ref, dn_ref, av_ref, ag_ref, avh_ref, agh_ref, cw_ref, dp_ref, gcw_ref, small_ref, dwin_ref, uwin_ref, acc_ref):
        i = pl.program_id(0)

        @pl.when(i == 0)
        def _():
            acc_ref[...] = jnp.zeros(acc_ref.shape, F32)
            small_ref[...] = jnp.zeros(small_ref.shape, F32)

        dwin_ref[0:tm, :] = d_ref[...]
        dwin_ref[tm:tm + HALO, :] = jnp.where(i < n - 1, dn_ref[...], 0.0)
        halo = avh_ref[...] * _sig(agh_ref[...])
        uwin_ref[0:HALO, :] = jnp.where(i > 0, halo, 0.0)
        av = av_ref[...]
        sg = _sig(ag_ref[...])
        uwin_ref[HALO:HALO + tm, :] = av * sg
        small_ref[0:1, :] += jnp.sum(d_ref[...], axis=0, keepdims=True)

        for c0 in range(0, D, CONV_LC):
            lanes = pl.ds(c0, CONV_LC)
            for r0 in range(0, tm, CONV_RC):
                acc = jnp.zeros((CONV_RC, CONV_LC), F32)
                for j in range(KCONV):
                    acc = acc + dwin_ref[pl.ds(r0 + KCONV - 1 - j, CONV_RC), lanes] * cw_ref[j:j + 1, c0:c0 + CONV_LC]
                a = av[r0:r0 + CONV_RC, c0:c0 + CONV_LC]
                g = sg[r0:r0 + CONV_RC, c0:c0 + CONV_LC]
                dp_ref[r0:r0 + CONV_RC, c0:c0 + CONV_LC] = (acc * g).astype(BF16)
                dp_ref[r0:r0 + CONV_RC, D + c0:D + c0 + CONV_LC] = (acc * a * (g * (1.0 - g))).astype(BF16)
            for j in range(KCONV):
                part = jnp.zeros((8, CONV_LC), F32)
                for r0 in range(0, tm, CONV_RC):
                    prod = dwin_ref[pl.ds(r0, CONV_RC), lanes] * uwin_ref[pl.ds(r0 + HALO - (KCONV - 1) + j, CONV_RC), lanes]
                    part = part + jnp.sum(prod.reshape(CONV_RC // 8, 8, CONV_LC), axis=0)
                acc_ref[j, :, c0:c0 + CONV_LC] += part

        @pl.when(i == n - 1)
        def _():
            gcw_ref[...] = jnp.sum(acc_ref[...], axis=1)

    row = lambda k: pl.BlockSpec((tm, D), lambda i: (i, k))
    prev = lambda k: pl.BlockSpec((HALO, D), lambda i: (jnp.maximum(i * hb - 1, 0), k))
    return pl.pallas_call(
        body, name="conv_bwd", grid=(n,),
        in_specs=[row(0), pl.BlockSpec((HALO, D), lambda i: (jnp.minimum((i + 1) * hb, last32), 0)),
                  row(0), row(1), prev(0), prev(1), _const((KPAD, D))],
        out_specs=[pl.BlockSpec((tm, 2 * D), lambda i: (i, 0)), _const((KPAD, D)), _const((8, D))],
        out_shape=[jax.ShapeDtypeStruct((s, 2 * D), BF16), jax.ShapeDtypeStruct((KPAD, D), F32), jax.ShapeDtypeStruct((8, D), F32)],
        scratch_shapes=[pltpu.VMEM((tm + HALO, D), F32), pltpu.VMEM((tm + HALO, D), F32), pltpu.VMEM((KPAD, 8, D), F32)],
        compiler_params=_cparams(1),
    )(du1, du1, proj, proj, proj, proj, conv_w)


def _dproj_specs(tm, rows_first):
    def spec(lo, hi):
        def idx(a, b):
            i, k = (a, b) if rows_first else (b, a)
            col = jnp.clip(k - lo, 0, hi - lo - 1)
            if rows_first:
                return (i, col)
            return (jnp.where((k >= lo) & (k < hi), i, 0), col)
        return pl.BlockSpec((tm, D), idx)
    return [spec(0, 2), spec(2, 3), spec(3, 4), spec(4, 7)]


def _pick_dproj(k, refs, fn):
    vg, ag, mla, gates = refs

    @pl.when(k < 2)
    def _():
        fn(vg)

    @pl.when(k == 2)
    def _():
        fn(ag)

    @pl.when(k == 3)
    def _():
        fn(mla)

    @pl.when(k > 3)
    def _():
        fn(gates)


def _in_proj_bwd_x(dps, wt, x, dx2, norm_w, mod8):
    s = x.shape[0]
    tm = min(512, s)
    nk = wt.shape[0] // D

    def body(vg_ref, ag_ref, mla_ref, g_ref, w_ref, x_ref, dx2_ref, nw_ref, mod_ref, gx_ref, small_ref, acc_ref):
        i, k = pl.program_id(0), pl.program_id(1)

        @pl.when((i == 0) & (k == 0))
        def _():
            small_ref[...] = jnp.zeros(small_ref.shape, F32)

        @pl.when(k == 0)
        def _():
            acc_ref[...] = jnp.zeros(acc_ref.shape, F32)

        def add(ref):
            acc_ref[...] += _nn(ref[...], w_ref[...])

        _pick_dproj(k, (vg_ref, ag_ref, mla_ref, g_ref), add)

        @pl.when(k == nk - 1)
        def _():
            dh = acc_ref[...]
            xn, rstd = _rms_parts(x_ref[...])
            nw = nw_ref[...]
            hn = xn * nw
            small_ref[0:1, :] += jnp.sum(dh, axis=0, keepdims=True)
            small_ref[1:2, :] += jnp.sum(dh * hn, axis=0, keepdims=True)
            dhn = dh * (1.0 + mod_ref[1:2, :])
            small_ref[2:3, :] += jnp.sum(dhn * xn, axis=0, keepdims=True)
            dxn = dhn * nw
            gx_ref[...] = rstd * (dxn - xn * jnp.mean(dxn * xn, axis=-1, keepdims=True)) + dx2_ref[...]

    row = pl.BlockSpec((tm, D), lambda i, k: (i, 0))
    return pl.pallas_call(
        body, name="in_proj_bwd_x", grid=(s // tm, nk),
        in_specs=_dproj_specs(tm, True) + [pl.BlockSpec((D, D), lambda i, k: (k, 0)), row, row, _const((1, D)), _const((8, D))],
        out_specs=[row, _const((8, D))],
        out_shape=[jax.ShapeDtypeStruct((s, D), F32), jax.ShapeDtypeStruct((8, D), F32)],
        scratch_shapes=[pltpu.VMEM((tm, D), F32)],
        compiler_params=_cparams(2),
    )(*dps, wt, x, dx2, norm_w, mod8)


def _in_proj_bwd_w(dps, h, nk):
    s = h.shape[0]
    tm = min(512, s)

    def body(vg_ref, ag_ref, mla_ref, g_ref, h_ref, gw_ref):
        k, i = pl.program_id(0), pl.program_id(1)

        @pl.when(i == 0)
        def _():
            gw_ref[...] = jnp.zeros(gw_ref.shape, F32)

        def add(ref):
            gw_ref[...] += _tn(ref[...], h_ref[...])

        _pick_dproj(k, (vg_ref, ag_ref, mla_ref, g_ref), add)

    return pl.pallas_call(
        body, name="in_proj_bwd_w", grid=(nk, s // tm),
        in_specs=_dproj_specs(tm, False) + [pl.BlockSpec((tm, D), lambda k, i: (i, 0))],
        out_specs=pl.BlockSpec((D, D), lambda k, i: (k, 0)),
        out_shape=jax.ShapeDtypeStruct((nk * D, D), F32),
        compiler_params=_cparams(2),
    )(*dps, h)


def _small_slab(wuq, wukv, conv_w):
    cw = jnp.pad(conv_w.reshape(-1), (0, (ROWS_TAIL - ROWS_UQ - ROWS_UKV) * D - KCONV * 128)).reshape(-1, D)
    return jnp.concatenate([wuq.reshape(ROWS_UQ, D), wukv.reshape(ROWS_UKV, D), cw], axis=0)


def _split_small_slab(slab):
    return (slab[..., :ROWS_UQ, :], slab[..., ROWS_UQ:ROWS_UQ + ROWS_UKV, :],
            slab[..., ROWS_UQ + ROWS_UKV:ROWS_UQ + ROWS_UKV + ROWS_CW, :])


def _unpack_small_slab(slab):
    wuq, wukv, cw = _split_small_slab(slab)
    return (wuq.reshape(QL, NH * DQK // N_DEV), wukv.reshape(KVL, NH * 256 // N_DEV),
            cw.reshape(-1)[:KCONV * 128].reshape(KCONV, 128))


def _pack_shard(w_in, wco, wao, wout, wuq, wukv, conv_w):
    bf = lambda a: a.astype(BF16)
    return jnp.concatenate([jnp.pad(bf(w_in).T, ((0, OFF_SQ - ROWS_IN), (0, 0))), bf(wco), bf(wao), bf(wout),
                            bf(_small_slab(wuq, wukv, conv_w))], axis=0)


def _unpack_gathered(g):
    wt = g[:, :ROWS_IN].reshape(IN_COLS, D)
    split = 3 * D + MLA_COLS
    wt = jnp.concatenate([wt[:split], jnp.zeros((D - MLA_COLS, D), g.dtype), wt[split:]], axis=0)
    wuq, wukv, cw = _split_small_slab(g[:, OFF_TAIL:])
    wuq = wuq.reshape(N_DEV, QL, NH * DQK // N_DEV).transpose(1, 0, 2).reshape(QL, NH * DQK)
    wukv = wukv.reshape(N_DEV, KVL, NH * 256 // N_DEV).transpose(1, 0, 2).reshape(KVL, NH * 256)
    cw = cw.reshape(N_DEV, ROWS_CW * D)[:, :KCONV * 128].reshape(N_DEV, KCONV, 128).transpose(1, 0, 2).reshape(KCONV, D)
    return wt, wuq, wukv, cw


def _pack_grads(gwt, gw3, guq, gukv, gcw):
    split = 3 * D + MLA_COLS
    g_in = jnp.concatenate([gwt[:split], gwt[4 * D:]], axis=0).reshape(N_DEV, ROWS_IN, D)
    g_in = jnp.pad(g_in, ((0, 0), (0, OFF_SQ - ROWS_IN), (0, 0)))
    guq = guq.reshape(QL, N_DEV, -1).transpose(1, 0, 2).reshape(N_DEV, ROWS_UQ, D)
    gukv = gukv.reshape(KVL, N_DEV, -1).transpose(1, 0, 2).reshape(N_DEV, ROWS_UKV, D)
    gcw = gcw.reshape(KCONV, N_DEV, 128).transpose(1, 0, 2).reshape(N_DEV, KCONV * 128)
    gcw = jnp.pad(gcw, ((0, 0), (0, (ROWS_TAIL - ROWS_UQ - ROWS_UKV) * D - KCONV * 128))).reshape(N_DEV, -1, D)
    gsq = gw3.reshape(3, N_DEV, ROWS_SQ, D).transpose(1, 0, 2, 3).reshape(N_DEV, 3 * ROWS_SQ, D)
    slabs = jnp.concatenate([g_in, gsq, guq, gukv, gcw], axis=1).astype(BF16)
    return slabs.reshape(N_CHIP, 2, ROWS_PACK, D).transpose(1, 0, 2, 3)


def _pack_small(vecs):
    flat = jnp.concatenate([v.reshape(-1) for v in vecs])
    return jnp.pad(flat, (0, SMALL_LEN - flat.shape[0])).reshape(8, SMALL_COLS)


def _unpack_small(a, shapes):
    flat = a.reshape(-1)
    out, off = [], 0
    for shp, n in zip(shapes, SMALL_SIZES):
        out.append(flat[off:off + n].reshape(shp))
        off += n
    return out


def kernel(x, c, positions, w_ada, b_ada, norm_w, w_in, conv_w, conv_b, conv_ln_w, conv_ln_b, w_conv_out, q_norm_w, w_uq, kv_norm_w, w_ukv, w_attn_out, w_out, final_norm_w, loss_target, m_w_ada, m_b_ada, m_norm_w, m_w_in, m_conv_w, m_conv_b, m_conv_ln_w, m_conv_ln_b, m_w_conv_out, m_q_norm_w, m_w_uq, m_kv_norm_w, m_w_ukv, m_w_attn_out, m_w_out, m_final_norm_w, v_w_ada, v_b_ada, v_norm_w, v_w_in, v_conv_w, v_conv_b, v_conv_ln_w, v_conv_ln_b, v_w_conv_out, v_q_norm_w, v_w_uq, v_kv_norm_w, v_w_ukv, v_w_attn_out, v_w_out, v_final_norm_w):
    me = 4 * lax.axis_index("x") + 2 * lax.axis_index("y") + lax.axis_index("c")
    xs, tgt = x[0], loss_target[0]
    s = xs.shape[0]
    ada_cols = w_ada.shape[2]

    sharded = lambda t: tuple(a[0] for a in t)
    gathered = _all_gather(_pack_shard(*sharded((w_in, w_conv_out, w_attn_out, w_out, w_uq, w_ukv, conv_w))), "gather_weights")
    wt, wuq, wukv, cw = _unpack_gathered(gathered)
    cw32 = jnp.pad(cw.astype(F32), ((0, KPAD - KCONV), (0, 0)))

    c_all = _all_gather(jnp.broadcast_to(c, (8, D)), "gather_c")[:, 0, :]
    b_cols = lax.dynamic_slice(b_ada, (0, me * ada_cols), (1, ada_cols))
    mod_cols = _all_gather(_ada_mod(c_all, w_ada[0], b_cols), "gather_mod")
    mod = lax.dynamic_index_in_dim(mod_cols, me, axis=1, keepdims=False).reshape(3, D)
    mod8 = jnp.pad(mod, ((0, 5), (0, 0)))

    pos = positions.reshape(s, 1)
    inv_freq = (ROPE_THETA ** (-jnp.arange(0, ROPE, 2, dtype=F32) / ROPE)).reshape(1, HALF)
    proj, h = _in_proj(xs, norm_w, mod8, wt)
    u1, ya = _conv_fwd(proj, cw32, conv_b, conv_ln_w, conv_ln_b)
    q, k, v = _mla_prep(proj, pos, inv_freq, q_norm_w, kv_norm_w, wuq, wukv)
    o, lse = _attn_fwd(q, k, v)

    dya, do, delta, dp_gates, dx2, gw3, small_a = _merge_loss(xs, tgt, ya, o, proj, mod8, final_norm_w.reshape(1, D), gathered)
    dq, dk, dv = _attn_bwd(q, k, v, do, lse, delta)
    dp_mla, guq, gukv, small_b = _mla_bwd(dq, dk, dv, proj, pos, inv_freq, q_norm_w, kv_norm_w, wuq, wukv)
    du1, dp_ag, small_c = _conv_rows_bwd(dya, u1, proj, conv_ln_w, conv_ln_b)
    dp_vg, gcw, small_d = _conv_bwd(du1, proj, cw32)
    dps = [dp_vg, dp_ag, dp_mla, dp_gates]
    grad_x, small_e = _in_proj_bwd_x(dps, wt, xs, dx2, norm_w, mod8)
    gwt = _in_proj_bwd_w(dps, h, wt.shape[0] // D)

    packed = _pack_grads(gwt, gw3, guq, gukv, gcw[:KCONV])
    core = lax.axis_index("c").astype(jnp.int32).reshape(1)
    half = _pair_add(core, packed, _pair_exchange(packed), ROWS_PACK // 4)
    recv = _chip_exchange(half)
    big_in = _adam_w_in(recv, w_in[0], m_w_in[0], v_w_in[0])
    squares = (("w_conv_out", w_conv_out, m_w_conv_out, v_w_conv_out), ("w_attn_out", w_attn_out, m_w_attn_out, v_w_attn_out),
               ("w_out", w_out, m_w_out, v_w_out))
    big_sq = [_reduce_adam(recv, w[0], m[0], v[0], "adam_" + nm, ROWS_SQ, OFF_SQ // ROWS_SQ + j) for j, (nm, w, m, v) in enumerate(squares)]
    tail = _reduce_adam(recv, _small_slab(w_uq[0], w_ukv[0], conv_w[0]), _small_slab(m_w_uq[0], m_w_ukv[0], m_conv_w[0]),
                        _small_slab(v_w_uq[0], v_w_ukv[0], v_conv_w[0]), "adam_small_sharded", ROWS_TAIL, OFF_TAIL // ROWS_TAIL)
    tail = [_unpack_small_slab(a) for a in tail]
    big = [(big_in[i], big_sq[0][i], big_sq[1][i], big_sq[2][i], *tail[i]) for i in range(4)]

    dmod = jnp.concatenate([small_e[0], small_e[1], small_a[1]])
    payload = _pack_small([dmod, small_e[2], small_d[0], small_c[0], small_c[1], small_a[0], small_b[0], small_b[1], small_a[2, 0:1]])
    pay_all = _all_gather(payload, "gather_small")
    small_w = (b_ada, norm_w, conv_b, conv_ln_w, conv_ln_b, final_norm_w, q_norm_w, kv_norm_w)
    small_m = (m_b_ada, m_norm_w, m_conv_b, m_conv_ln_w, m_conv_ln_b, m_final_norm_w, m_q_norm_w, m_kv_norm_w)
    small_v = (v_b_ada, v_norm_w, v_conv_b, v_conv_ln_w, v_conv_ln_b, v_final_norm_w, v_q_norm_w, v_kv_norm_w)
    sm = _reduce_adam(pay_all, _pack_small(small_w), _pack_small(small_m), _pack_small(small_v), "adam_replicated", 8, 0)
    loss = sm[0].reshape(-1)[sum(SMALL_SIZES)]
    shapes = [t.shape for t in small_w]
    sm = [_unpack_small(a, shapes) for a in sm]

    dmod_all = pay_all.reshape(N_DEV, SMALL_LEN)[:, :3 * D]
    dmod_cols = lax.dynamic_slice(dmod_all, (0, me * ada_cols), (N_DEV, ada_cols))
    ada = _ada_bwd(c_all, dmod_cols, w_ada[0], m_w_ada[0], v_w_ada[0])

    def group(i):
        b_in, b_co, b_ao, b_out, b_uq, b_ukv, b_cw = big[i]
        s_bada, s_nw, s_cb, s_clw, s_clb, s_fnw, s_qnw, s_kvnw = sm[i]
        return (ada[i][None], s_bada, s_nw, b_in[None], b_cw[None], s_cb, s_clw, s_clb, b_co[None], s_qnw, b_uq[None], s_kvnw,
                b_ukv[None], b_ao[None], b_out[None], s_fnw)

    return (loss, grad_x[None], *group(0), *group(1), *group(2), *group(3))
```

```python
import functools

import jax
import jax.numpy as jnp
from jax import lax
from jax.experimental import pallas as pl
from jax.experimental.pallas import tpu as pltpu

F32 = jnp.float32
BF16 = jnp.bfloat16

D = 1024
NH = 8
NOPE = 128
ROPE = 64
HALF = ROPE // 2
DQK = NOPE + ROPE
DV = 128
QL = 256
KVL = 256
KCONV = 31
KPAD = 32
HALO = 32
IN_COLS = 6720
MLA_COLS = QL + KVL + ROPE
PROJ_COLS = 7 * D
EPS = 1e-6
ROPE_THETA = 10000.0
N_DEV = 8

ADAM_LR = 0.001
ADAM_B1 = 0.9
ADAM_B2 = 0.999
ADAM_EPS = 1e-08
ADAM_WD = 0.01
ADAM_STEP = 10

ROWS_IN = 840
ROWS_SQ = 128
ROWS_UQ = 48
ROWS_UKV = 64
ROWS_CW = 4
OFF_SQ = 896
OFF_TAIL = OFF_SQ + 3 * ROWS_SQ
ROWS_TAIL = 128
ROWS_PACK = OFF_TAIL + ROWS_TAIL
SMALL_SIZES = (3 * D, D, D, D, D, D, QL, KVL)
SMALL_COLS = 1152
SMALL_LEN = 8 * SMALL_COLS

MESH = pl.DeviceIdType.MESH
ANY = pl.BlockSpec(memory_space=pl.ANY)
V7X_VMEM_LIMIT = 56 * 1024 * 1024


def _cparams(n_axes, vmem=V7X_VMEM_LIMIT):
    return pltpu.CompilerParams(dimension_semantics=("arbitrary",) * n_axes, vmem_limit_bytes=vmem)


def _sig(x):
    return jax.nn.sigmoid(x)


def _nt(a, b):
    return lax.dot_general(a, b, (((1,), (1,)), ((), ())), preferred_element_type=F32)


def _tn(a, b):
    return lax.dot_general(a, b, (((0,), (0,)), ((), ())), preferred_element_type=F32)


def _nn(a, b):
    return jnp.dot(a, b, preferred_element_type=F32)


def _const(shape):
    return pl.BlockSpec(shape, lambda *_: (0,) * len(shape))


def _all_gather(block, name):
    r, c = block.shape

    def body(x_ref, out_ref, send_sems, recv_sems, local_sem):
        x, y, cc = lax.axis_index("x"), lax.axis_index("y"), lax.axis_index("c")
        me, sibling = (x, y, cc), (x, y, 1 - cc)
        chips = [(1 - x, y), (x, 1 - y), (1 - x, 1 - y)]

        def slot(px, py, pc):
            return out_ref.at[4 * px + 2 * py + pc]

        def copy(k, blk, to, src=None):
            return pltpu.make_async_remote_copy(
                src_ref=slot(*blk) if src is None else src, dst_ref=slot(*blk),
                send_sem=send_sems.at[k], recv_sem=recv_sems.at[k],
                device_id=to, device_id_type=MESH)

        mine = pltpu.make_async_copy(x_ref, slot(*me), local_sem)
        mine.start()
        first = [copy(0, me, sibling, src=x_ref)]
        first += [copy(1 + j, me, (*chip, cc), src=x_ref) for j, chip in enumerate(chips)]
        for cp in first:
            cp.start()
        passed = [copy(4 + j, (*chip, cc), sibling) for j, chip in enumerate(chips)]
        for j, chip in enumerate(chips):
            copy(1 + j, (*chip, cc), me).wait_recv()
            passed[j].start()
        copy(0, sibling, me).wait_recv()
        for j, chip in enumerate(chips):
            copy(4 + j, (*chip, 1 - cc), me).wait_recv()
        for cp in first + passed:
            cp.wait_send()
        mine.wait()

    return pl.pallas_call(
        body, name=name,
        out_shape=jax.ShapeDtypeStruct((N_DEV, r, c), block.dtype),
        in_specs=[ANY], out_specs=ANY,
        scratch_shapes=[pltpu.SemaphoreType.DMA((7,)), pltpu.SemaphoreType.DMA((7,)), pltpu.SemaphoreType.DMA],
    )(block)


N_CHIP = 4


def _pair_exchange(packed):
    _, _, r, c = packed.shape

    def body(src_ref, out_ref, send_sem, recv_sem):
        x, y, cc = lax.axis_index("x"), lax.axis_index("y"), lax.axis_index("c")
        cp = pltpu.make_async_remote_copy(
            src_ref=src_ref.at[1 - cc], dst_ref=out_ref, send_sem=send_sem, recv_sem=recv_sem,
            device_id=(x, y, 1 - cc), device_id_type=MESH)
        cp.start()
        cp.wait()

    return pl.pallas_call(
        body, name="pair_exchange",
        out_shape=jax.ShapeDtypeStruct((N_CHIP, r, c), packed.dtype),
        in_specs=[ANY], out_specs=ANY,
        scratch_shapes=[pltpu.SemaphoreType.DMA, pltpu.SemaphoreType.DMA],
    )(packed)


def _pair_add(core, packed, got, tr):
    _, _, r, c = packed.shape

    def body(core_ref, own_ref, got_ref, out_ref):
        out_ref[...] = (own_ref[0].astype(F32) + got_ref[...].astype(F32)).astype(out_ref.dtype)

    blk = pl.BlockSpec((1, tr, c), lambda j, i, core_ref: (j, i, 0))
    return pl.pallas_call(
        body, name="pair_add",
        grid_spec=pltpu.PrefetchScalarGridSpec(
            num_scalar_prefetch=1, grid=(N_CHIP, r // tr),
            in_specs=[pl.BlockSpec((1, 1, tr, c), lambda j, i, core_ref: (core_ref[0], j, i, 0)), blk],
            out_specs=blk),
        out_shape=jax.ShapeDtypeStruct((N_CHIP, r, c), packed.dtype),
        compiler_params=_cparams(2),
    )(core, packed, got)


HBM = pl.BlockSpec(memory_space=pltpu.HBM)
SEM = pl.BlockSpec(memory_space=pltpu.SEMAPHORE)
EFFECT = pltpu.SideEffectType.DATAFLOW_SIDE_EFFECTING


def _chip_copies(src_ref, land_ref, send_sems, recv_sems):
    x, y, cc = lax.axis_index("x"), lax.axis_index("y"), lax.axis_index("c")
    me = 2 * x + y
    copies = []
    for k in range(1, N_CHIP):
        px, py = (1 - x if k & 2 else x), (1 - y if k & 1 else y)
        copies.append(pltpu.make_async_remote_copy(
            src_ref=src_ref.at[2 * px + py], dst_ref=land_ref.at[me],
            send_sem=send_sems.at[k - 1], recv_sem=recv_sems.at[k - 1],
            device_id=(px, py, cc), device_id_type=MESH))
    return copies


def _chip_exchange_start(half):
    def body(src_ref, land_ref, send_sems, recv_sems, src_thru, land_thru, token):
        for cp in _chip_copies(src_ref, land_ref, send_sems, recv_sems):
            cp.start()
        token[...] = jnp.zeros_like(token)

    return pl.pallas_call(
        body, name="chip_exchange_start",
        out_shape=(pltpu.SemaphoreType.DMA((N_CHIP - 1,)), pltpu.SemaphoreType.DMA((N_CHIP - 1,)),
                   pltpu.HBM(half.shape, half.dtype), pltpu.HBM(half.shape, half.dtype), jax.ShapeDtypeStruct((8, 128), F32)),
        in_specs=(HBM, HBM), out_specs=(SEM, SEM, HBM, HBM, pl.BlockSpec(memory_space=pltpu.VMEM)),
        input_output_aliases={0: 2, 1: 3},
        compiler_params=pltpu.CompilerParams(has_side_effects=EFFECT),
    )(pltpu.with_memory_space_constraint(half, pltpu.HBM),
      pltpu.with_memory_space_constraint(lax.empty(half.shape, half.dtype), pltpu.HBM))


def _chip_exchange_wait(send_sems, recv_sems, src_thru, land_thru, after):
    def body(src_ref, land_ref, send_sems, recv_sems, after_ref, src_dead, got_ref):
        copies = _chip_copies(src_ref, land_ref, send_sems, recv_sems)
        for cp in copies:
            cp.wait_send()
        for cp in copies:
            cp.wait_recv()

    return pl.pallas_call(
        body, name="chip_exchange_wait",
        out_shape=(pltpu.HBM(src_thru.shape, src_thru.dtype), pltpu.HBM(land_thru.shape, land_thru.dtype)),
        in_specs=(HBM, HBM, SEM, SEM, ANY), out_specs=(HBM, HBM), input_output_aliases={0: 0, 1: 1},
        compiler_params=pltpu.CompilerParams(has_side_effects=EFFECT),
    )(src_thru, land_thru, send_sems, recv_sems, after)


def _adam(g, w, m, v):
    m = ADAM_B1 * m + (1.0 - ADAM_B1) * g
    v = ADAM_B2 * v + (1.0 - ADAM_B2) * (g * g)
    m_hat = m / (1.0 - ADAM_B1 ** ADAM_STEP)
    v_hat = v / (1.0 - ADAM_B2 ** ADAM_STEP)
    delta = -ADAM_LR * (m_hat / (jnp.sqrt(v_hat) + ADAM_EPS) + ADAM_WD * w)
    return delta, m, v


def _adam_w_in(parts, w, m, v):
    n = parts.shape[0]
    tc = 128

    def body(p_ref, w_ref, m_ref, v_ref, g_out, d_out, m_out, v_out):
        gt = p_ref[0].astype(F32)
        for j in range(1, n):
            gt = gt + p_ref[j].astype(F32)
        g = gt.T[:, :ROWS_IN]
        delta, nm, nv = _adam(g, w_ref[...], m_ref[...], v_ref[...])
        g_out[...] = g
        d_out[...] = delta
        m_out[...] = nm
        v_out[...] = nv

    row = pl.BlockSpec((tc, ROWS_IN), lambda i: (i, 0))
    return pl.pallas_call(
        body, name="adam_w_in", grid=(D // tc,),
        in_specs=[pl.BlockSpec((n, OFF_SQ, tc), lambda i: (0, 0, i)), row, row, row],
        out_specs=[row] * 4,
        out_shape=[jax.ShapeDtypeStruct((D, ROWS_IN), F32)] * 4,
        compiler_params=_cparams(1),
    )(parts, w, m, v)


def _reduce_adam(parts, w, m, v, name, tr, first_block):
    n, _, c = parts.shape
    r = w.shape[0]

    def body(p_ref, w_ref, m_ref, v_ref, g_out, d_out, m_out, v_out):
        g = p_ref[0].astype(F32)
        for j in range(1, n):
            g = g + p_ref[j].astype(F32)
        delta, nm, nv = _adam(g, w_ref[...], m_ref[...], v_ref[...])
        g_out[...] = g
        d_out[...] = delta
        m_out[...] = nm
        v_out[...] = nv

    row = pl.BlockSpec((tr, c), lambda i: (i, 0))
    return pl.pallas_call(
        body, name=name, grid=(r // tr,),
        in_specs=[pl.BlockSpec((n, tr, c), lambda i: (0, first_block + i, 0)), row, row, row],
        out_specs=[row] * 4,
        out_shape=[jax.ShapeDtypeStruct((r, c), F32)] * 4,
        compiler_params=_cparams(1),
    )(parts, w, m, v)


def _ada_mod(c_all, w_ada, b_cols):
    def body(c_ref, w_ref, b_ref, o_ref):
        cv = c_ref[...]
        act = (cv * _sig(cv)).astype(BF16)
        o_ref[...] = _nn(act, w_ref[...].astype(BF16)) + b_ref[...]

    return pl.pallas_call(body, name="ada_mod", out_shape=jax.ShapeDtypeStruct((N_DEV, w_ada.shape[1]), F32))(c_all, w_ada, b_cols)


def _ada_bwd(c_all, dmod_cols, w, m, v):
    def body(c_ref, d_ref, w_ref, m_ref, v_ref, g_out, d_out, m_out, v_out):
        cv = c_ref[...]
        act = (cv * _sig(cv)).astype(BF16)
        g = _tn(act, d_ref[...].astype(BF16))
        delta, nm, nv = _adam(g, w_ref[...], m_ref[...], v_ref[...])
        g_out[...] = g
        d_out[...] = delta
        m_out[...] = nm
        v_out[...] = nv

    return pl.pallas_call(body, name="ada_bwd", out_shape=[jax.ShapeDtypeStruct(w.shape, F32)] * 4)(c_all, dmod_cols, w, m, v)


def _in_proj(x, norm_w, mod8, wt):
    s = x.shape[0]
    tm = min(1024, s)
    nk = wt.shape[0] // D

    def body(x_ref, nw_ref, mod_ref, w_ref, proj_ref, h_ref, hs_ref):
        @pl.when(pl.program_id(1) == 0)
        def _():
            xv = x_ref[...]
            rstd = lax.rsqrt(jnp.mean(xv * xv, axis=-1, keepdims=True) + EPS)
            h = (xv * rstd) * nw_ref[...] * (1.0 + mod_ref[1:2, :]) + mod_ref[0:1, :]
            hs_ref[...] = h.astype(BF16)
            h_ref[...] = hs_ref[...]

        proj_ref[...] = _nt(hs_ref[...], w_ref[...])

    return pl.pallas_call(
        body, name="in_proj", grid=(s // tm, nk),
        in_specs=[pl.BlockSpec((tm, D), lambda i, k: (i, 0)), _const((1, D)), _const((8, D)),
                  pl.BlockSpec((D, D), lambda i, k: (k, 0))],
        out_specs=[pl.BlockSpec((tm, D), lambda i, k: (i, k)), pl.BlockSpec((tm, D), lambda i, k: (i, 0))],
        out_shape=[jax.ShapeDtypeStruct((s, nk * D), F32), jax.ShapeDtypeStruct((s, D), BF16)],
        scratch_shapes=[pltpu.VMEM((tm, D), BF16)],
        compiler_params=_cparams(2),
    )(x, norm_w, mod8, wt)


CONV_RC = 64
CONV_LC = 256


def _conv_fwd(proj, conv_w, conv_b, ln_w, ln_b):
    s = proj.shape[0]
    tm = min(256, s)
    hb = tm // HALO

    def body(av_ref, ag_ref, avh_ref, agh_ref, gate_ref, cw_ref, cb_ref, lw_ref, lb_ref, u1_ref, ya_ref, win_ref):
        i = pl.program_id(0)
        halo = avh_ref[...] * _sig(agh_ref[...])
        win_ref[0:HALO, :] = jnp.where(i > 0, halo, 0.0)
        win_ref[HALO:HALO + tm, :] = av_ref[...] * _sig(ag_ref[...])
        for r0 in range(0, tm, CONV_RC):
            for c0 in range(0, D, CONV_LC):
                acc = jnp.zeros((CONV_RC, CONV_LC), F32) + cb_ref[:, c0:c0 + CONV_LC]
                for j in range(KCONV):
                    acc = acc + win_ref[pl.ds(r0 + HALO - (KCONV - 1) + j, CONV_RC), pl.ds(c0, CONV_LC)] * cw_ref[j:j + 1, c0:c0 + CONV_LC]
                u1_ref[r0:r0 + CONV_RC, c0:c0 + CONV_LC] = acc
        u1 = u1_ref[...]
        mu = jnp.mean(u1, axis=-1, keepdims=True)
        xc = u1 - mu
        var = jnp.mean(xc * xc, axis=-1, keepdims=True)
        ln = xc * lax.rsqrt(var + EPS) * lw_ref[...] + lb_ref[...]
        gate = gate_ref[...]
        ya_ref[...] = ((ln * _sig(ln)) * (gate * _sig(gate))).astype(BF16)

    row = lambda k: pl.BlockSpec((tm, D), lambda i: (i, k))
    prev = lambda k: pl.BlockSpec((HALO, D), lambda i: (jnp.maximum(i * hb - 1, 0), k))
    return pl.pallas_call(
        body, name="conv_fwd", grid=(s // tm,),
        in_specs=[row(0), row(1), prev(0), prev(1), row(2), _const((KPAD, D)), _const((1, D)), _const((1, D)), _const((1, D))],
        out_specs=[pl.BlockSpec((tm, D), lambda i: (i, 0))] * 2,
        out_shape=[jax.ShapeDtypeStruct((s, D), F32), jax.ShapeDtypeStruct((s, D), BF16)],
        scratch_shapes=[pltpu.VMEM((tm + HALO, D), F32)],
        compiler_params=_cparams(1),
    )(proj, proj, proj, proj, proj, conv_w, conv_b, ln_w, ln_b)


def _rope_tables(pos_ref, if_ref):
    ang = pos_ref[...].astype(F32) * if_ref[...]
    return jnp.cos(ang), jnp.sin(ang)


def _rms_parts(x):
    rstd = lax.rsqrt(jnp.mean(x * x, axis=-1, keepdims=True) + EPS)
    return x * rstd, rstd


def _mla_prep(proj, pos, inv_freq, qnw, kvnw, wuq, wukv):
    s = proj.shape[0]
    tm = min(512, s)

    def body(p_ref, pos_ref, if_ref, qnw_ref, kvnw_ref, wuq_ref, wukv_ref, q_ref, k_ref, v_ref):
        blk = p_ref[...]
        cos, sin = _rope_tables(pos_ref, if_ref)

        def rope(r):
            x1, x2 = r[:, :HALF], r[:, HALF:]
            return jnp.concatenate([x1 * cos - x2 * sin, x1 * sin + x2 * cos], axis=-1)

        qlat = _rms_parts(blk[:, :QL])[0] * qnw_ref[...]
        kvlat = _rms_parts(blk[:, QL:QL + KVL])[0] * kvnw_ref[...]
        q = _nn(qlat.astype(BF16), wuq_ref[...])
        kv = _nn(kvlat.astype(BF16), wukv_ref[...])
        kr = rope(blk[:, QL + KVL:MLA_COLS])
        for h in range(NH):
            qh = q[:, h * DQK:(h + 1) * DQK]
            q_ref[h] = (jnp.concatenate([qh[:, :NOPE], rope(qh[:, NOPE:])], axis=-1) * ATTN_SCALE).astype(BF16)
            k_ref[h] = jnp.concatenate([kv[:, h * 256:h * 256 + NOPE], kr], axis=-1).astype(BF16)
            v_ref[h] = kv[:, h * 256 + NOPE:(h + 1) * 256].astype(BF16)

    hm = lambda d: pl.BlockSpec((NH, tm, d), lambda i: (0, i, 0))
    return pl.pallas_call(
        body, name="mla_prep", grid=(s // tm,),
        in_specs=[pl.BlockSpec((tm, D), lambda i: (i, 3)), pl.BlockSpec((tm, 1), lambda i: (i, 0)), _const((1, HALF)),
                  _const((1, QL)), _const((1, KVL)), _const((QL, NH * DQK)), _const((KVL, NH * 256))],
        out_specs=[hm(DQK), hm(DQK), hm(DV)],
        out_shape=[jax.ShapeDtypeStruct((NH, s, DQK), BF16), jax.ShapeDtypeStruct((NH, s, DQK), BF16),
                   jax.ShapeDtypeStruct((NH, s, DV), BF16)],
        compiler_params=_cparams(1),
    )(proj, pos, inv_freq, qnw, kvnw, wuq, wukv)


ATTN_SCALE = DQK ** -0.5


def _causal_mask(s, t):
    rows = lax.broadcasted_iota(jnp.int32, (t, t), 0)
    cols = lax.broadcasted_iota(jnp.int32, (t, t), 1)
    return jnp.where(cols <= rows, s, -jnp.inf)


def _attn_tile(s):
    return min(1024, s // 2)


def _attn_fwd(q, k, v):
    nh, s, _ = q.shape
    t = _attn_tile(s)

    def body(q_ref, k_ref, v_ref, o_ref, lse_ref):
        qi = pl.program_id(1)
        qv = q_ref[0]

        def chunk(c, carry, diag):
            m, l, acc = carry
            rows = pl.ds(pl.multiple_of(c * t, t), t)
            sc = _nt(qv, k_ref[0, rows, :])
            if diag:
                sc = _causal_mask(sc, t)
            m_new = jnp.maximum(m, jnp.max(sc, axis=-1, keepdims=True))
            alpha = jnp.exp(m - m_new)
            p = jnp.exp(sc - m_new)
            l = alpha * l + jnp.sum(p, axis=-1, keepdims=True)
            acc = alpha * acc + _nn(p.astype(BF16), v_ref[0, rows, :])
            return m_new, l, acc

        init = (jnp.full((t, 1), -jnp.inf, F32), jnp.zeros((t, 1), F32), jnp.zeros((t, DV), F32))
        carry = lax.fori_loop(0, qi, lambda c, cr: chunk(c, cr, False), init)
        m, l, acc = chunk(qi, carry, True)
        o_ref[...] = acc / l
        lse_ref[0] = jnp.broadcast_to(m + jnp.log(l), (t, DV))

    head = lambda d: pl.BlockSpec((1, s, d), lambda h, i: (h, 0, 0))
    return pl.pallas_call(
        body, name="attn_fwd", grid=(nh, s // t),
        in_specs=[pl.BlockSpec((1, t, DQK), lambda h, i: (h, i, 0)), head(DQK), head(DV)],
        out_specs=[pl.BlockSpec((t, DV), lambda h, i: (i, h)), pl.BlockSpec((1, t, DV), lambda h, i: (h, i, 0))],
        out_shape=[jax.ShapeDtypeStruct((s, nh * DV), F32), jax.ShapeDtypeStruct((nh, s, DV), F32)],
        compiler_params=_cparams(2),
    )(q, k, v)


def _merge_loss(x, target, ya, o, proj, mod8, fnw, gathered):
    s = x.shape[0]
    tm = min(256, s)
    n = s // tm

    def body(x_ref, t_ref, ya_ref, o_ref, bg_ref, ga_ref, gb_ref, mod_ref, fnw_ref, wco_ref, wao_ref, wout_ref,
             dya_ref, do_ref, delta_ref, dpg_ref, dx2_ref, gw_ref, small_ref, acc_ref):
        i = pl.program_id(0)

        @pl.when(i == 0)
        def _():
            acc_ref[...] = jnp.zeros(acc_ref.shape, F32)
            small_ref[...] = jnp.zeros(small_ref.shape, F32)

        bg = bg_ref[...]
        sbg = _sig(bg)
        sb = bg * sbg
        ov = o_ref[...]
        ya = ya_ref[...]
        yb = (ov * sb).astype(BF16)
        square = lambda ref: ref[...].reshape(D, D)
        y_a = _nn(ya, square(wco_ref))
        y_b = _nn(yb, square(wao_ref))
        sa = _sig(ga_ref[...])
        sgb = _sig(gb_ref[...])
        merged = (sa * y_a + sgb * y_b).astype(BF16)
        z = _nn(merged, square(wout_ref))
        gate = mod_ref[2:3, :]
        x2 = x_ref[...] + gate * z
        xn, rstd = _rms_parts(x2)
        fnw = fnw_ref[...]
        err = xn * fnw - t_ref[...]
        loss = jnp.sum(jnp.sum(err * err, axis=-1, keepdims=True), axis=0, keepdims=True) * (0.5 / D)
        dy = err * (1.0 / D)
        small_ref[0:1, :] += jnp.sum(dy * xn, axis=0, keepdims=True)
        dxn = dy * fnw
        dx2 = rstd * (dxn - xn * jnp.mean(dxn * xn, axis=-1, keepdims=True))
        dx2_ref[...] = dx2
        small_ref[1:2, :] += jnp.sum(dx2 * z, axis=0, keepdims=True)
        small_ref[2:3, :] += jnp.broadcast_to(loss, (1, D))
        dz = (dx2 * gate).astype(BF16)
        dmerged = _nt(dz, square(wout_ref))
        acc_ref[2] += _tn(merged, dz)
        dy_a = (dmerged * sa).astype(BF16)
        dy_b = (dmerged * sgb).astype(BF16)
        dpg_ref[:, D:2 * D] = (dmerged * y_a * (sa * (1.0 - sa))).astype(BF16)
        dpg_ref[:, 2 * D:3 * D] = (dmerged * y_b * (sgb * (1.0 - sgb))).astype(BF16)
        dya_ref[...] = _nt(dy_a, square(wco_ref))
        acc_ref[0] += _tn(ya, dy_a)
        dyb = _nt(dy_b, square(wao_ref))
        acc_ref[1] += _tn(yb, dy_b)
        do = dyb * sb
        do_ref[...] = do.astype(BF16)
        dpg_ref[:, 0:D] = (dyb * ov * (sbg * (1.0 + bg * (1.0 - sbg)))).astype(BF16)
        prod = do * ov
        for h in range(NH):
            delta_ref[h] = jnp.broadcast_to(jnp.sum(prod[:, h * DV:(h + 1) * DV], axis=-1, keepdims=True), (tm, DV))

        @pl.when(i == n - 1)
        def _():
            pltpu.sync_copy(acc_ref, gw_ref)

    row = pl.BlockSpec((tm, D), lambda i: (i, 0))
    col = lambda k: pl.BlockSpec((tm, D), lambda i: (i, k))
    wspec = lambda j: pl.BlockSpec((N_DEV, ROWS_SQ, D), lambda i: (0, OFF_SQ // ROWS_SQ + j, 0), pipeline_mode=pl.Buffered(1))
    return pl.pallas_call(
        body, name="merge_loss", grid=(n,),
        in_specs=[row, row, row, row, col(4), col(5), col(6), _const((8, D)), _const((1, D)), wspec(0), wspec(1), wspec(2)],
        out_specs=[row, row, pl.BlockSpec((NH, tm, DV), lambda i: (0, i, 0)), pl.BlockSpec((tm, 3 * D), lambda i: (i, 0)),
                   row, ANY, _const((8, D))],
        out_shape=[jax.ShapeDtypeStruct((s, D), F32), jax.ShapeDtypeStruct((s, D), BF16),
                   jax.ShapeDtypeStruct((NH, s, DV), F32), jax.ShapeDtypeStruct((s, 3 * D), BF16),
                   jax.ShapeDtypeStruct((s, D), F32), jax.ShapeDtypeStruct((3, D, D), F32),
                   jax.ShapeDtypeStruct((8, D), F32)],
        scratch_shapes=[pltpu.VMEM((3, D, D), F32)],
        compiler_params=_cparams(1),
    )(x, target, ya, o, proj, proj, proj, mod8, fnw, gathered, gathered, gathered)


def _attn_bwd(q, k, v, do, lse, delta):
    nh, s, _ = q.shape
    t = _attn_tile(s)
    nb = s // t

    def body(q_ref, k_ref, v_ref, do_ref, lse_ref, dl_ref, dq_ref, dk_ref, dv_ref):
        kj = pl.program_id(1)

        @pl.when(kj == 0)
        def _():
            dq_ref[...] = jnp.zeros(dq_ref.shape, F32)

        kv_, vv = k_ref[0], v_ref[0]

        def chunk(c, carry, diag):
            dk, dv = carry
            rows = pl.ds(pl.multiple_of(c * t, t), t)
            qv = q_ref[0, rows, :]
            dov = do_ref[rows, :]
            sc = _nt(qv, kv_)
            if diag:
                sc = _causal_mask(sc, t)
            p = jnp.exp(sc - lse_ref[0, rows, 0:1])
            dv = dv + _tn(p.astype(BF16), dov)
            dp = _nt(dov, vv)
            ds = (p * (dp - dl_ref[0, rows, 0:1])).astype(BF16)
            dk = dk + _tn(ds, qv)
            dq_ref[0, rows, :] += _nn(ds, kv_)
            return dk, dv

        carry = chunk(kj, (jnp.zeros((t, DQK), F32), jnp.zeros((t, DV), F32)), True)
        dk, dv = lax.fori_loop(kj + 1, nb, lambda c, cr: chunk(c, cr, False), carry)
        dk_ref[0] = dk
        dv_ref[0] = dv

    head = lambda d: pl.BlockSpec((1, s, d), lambda h, j: (h, 0, 0))
    blk = lambda d: pl.BlockSpec((1, t, d), lambda h, j: (h, j, 0))
    return pl.pallas_call(
        body, name="attn_bwd", grid=(nh, nb),
        in_specs=[head(DQK), blk(DQK), blk(DV), pl.BlockSpec((s, DV), lambda h, j: (0, h)), head(DV), head(DV)],
        out_specs=[head(DQK), blk(DQK), blk(DV)],
        out_shape=[jax.ShapeDtypeStruct((nh, s, DQK), F32), jax.ShapeDtypeStruct((nh, s, DQK), F32),
                   jax.ShapeDtypeStruct((nh, s, DV), F32)],
        compiler_params=_cparams(2),
    )(q, k, v, do, lse, delta)


def _mla_bwd(dq, dk, dv, proj, pos, inv_freq, qnw, kvnw, wuq, wukv):
    s = proj.shape[0]
    tm = min(512, s)

    def body(dq_ref, dk_ref, dv_ref, p_ref, pos_ref, if_ref, qnw_ref, kvnw_ref, wuq_ref, wukv_ref,
             dp_ref, guq_ref, gukv_ref, small_ref):
        @pl.when(pl.program_id(0) == 0)
        def _():
            guq_ref[...] = jnp.zeros(guq_ref.shape, F32)
            gukv_ref[...] = jnp.zeros(gukv_ref.shape, F32)
            small_ref[...] = jnp.zeros(small_ref.shape, F32)

        blk = p_ref[...]
        cos, sin = _rope_tables(pos_ref, if_ref)

        def unrope(g):
            g1, g2 = g[:, :HALF], g[:, HALF:]
            return jnp.concatenate([g1 * cos + g2 * sin, g2 * cos - g1 * sin], axis=-1)

        dq_cols, dkv_cols = [], []
        dkr = jnp.zeros((tm, ROPE), F32)
        for h in range(NH):
            dqh, dkh = dq_ref[h] * ATTN_SCALE, dk_ref[h]
            dq_cols += [dqh[:, :NOPE], unrope(dqh[:, NOPE:])]
            dkv_cols += [dkh[:, :NOPE], dv_ref[h]]
            dkr = dkr + dkh[:, NOPE:]
        dq_full = jnp.concatenate(dq_cols, axis=-1).astype(BF16)
        dkv_full = jnp.concatenate(dkv_cols, axis=-1).astype(BF16)

        def latent_bwd(c, nw_ref, d_up, w_ref, g_ref, srow):
            nrm, rstd = _rms_parts(c)
            nw = nw_ref[...]
            lat = (nrm * nw).astype(BF16)
            g_ref[...] += _tn(lat, d_up)
            dlat = _nt(d_up, w_ref[...])
            small_ref[srow:srow + 1, :] += jnp.sum(dlat * nrm, axis=0, keepdims=True)
            dn = dlat * nw
            return rstd * (dn - nrm * jnp.mean(dn * nrm, axis=-1, keepdims=True))

        dcq = latent_bwd(blk[:, :QL], qnw_ref, dq_full, wuq_ref, guq_ref, 0)
        dckv = latent_bwd(blk[:, QL:QL + KVL], kvnw_ref, dkv_full, wukv_ref, gukv_ref, 1)
        dp_ref[...] = jnp.concatenate([dcq, dckv, unrope(dkr), jnp.zeros((tm, D - MLA_COLS), F32)], axis=-1).astype(BF16)

    hm = lambda d: pl.BlockSpec((NH, tm, d), lambda i: (0, i, 0))
    return pl.pallas_call(
        body, name="mla_bwd", grid=(s // tm,),
        in_specs=[hm(DQK), hm(DQK), hm(DV), pl.BlockSpec((tm, D), lambda i: (i, 3)), pl.BlockSpec((tm, 1), lambda i: (i, 0)),
                  _const((1, HALF)), _const((1, QL)), _const((1, KVL)), _const((QL, NH * DQK)), _const((KVL, NH * 256))],
        out_specs=[pl.BlockSpec((tm, D), lambda i: (i, 0)), _const((QL, NH * DQK)), _const((KVL, NH * 256)), _const((8, QL))],
        out_shape=[jax.ShapeDtypeStruct((s, D), BF16), jax.ShapeDtypeStruct((QL, NH * DQK), F32),
                   jax.ShapeDtypeStruct((KVL, NH * 256), F32), jax.ShapeDtypeStruct((8, QL), F32)],
        compiler_params=_cparams(1),
    )(dq, dk, dv, proj, pos, inv_freq, qnw, kvnw, wuq, wukv)


def _conv_rows_bwd(dya, u1, proj, ln_w, ln_b):
    s = dya.shape[0]
    tm = min(512, s)

    def body(dya_ref, u1_ref, gate_ref, lw_ref, lb_ref, du1_ref, dag_ref, small_ref):
        @pl.when(pl.program_id(0) == 0)
        def _():
            small_ref[...] = jnp.zeros(small_ref.shape, F32)

        u1 = u1_ref[...]
        mu = jnp.mean(u1, axis=-1, keepdims=True)
        xc = u1 - mu
        rstd = lax.rsqrt(jnp.mean(xc * xc, axis=-1, keepdims=True) + EPS)
        xhat = xc * rstd
        lw = lw_ref[...]
        ln = xhat * lw + lb_ref[...]
        sl = _sig(ln)
        u2 = ln * sl
        gate = gate_ref[...]
        sg = _sig(gate)
        dya = dya_ref[...]
        dag_ref[...] = (dya * u2 * (sg * (1.0 + gate * (1.0 - sg)))).astype(BF16)
        dln = dya * (gate * sg) * (sl * (1.0 + ln * (1.0 - sl)))
        small_ref[0:1, :] += jnp.sum(dln * xhat, axis=0, keepdims=True)
        small_ref[1:2, :] += jnp.sum(dln, axis=0, keepdims=True)
        dxh = dln * lw
        du1_ref[...] = rstd * (dxh - jnp.mean(dxh, axis=-1, keepdims=True) - xhat * jnp.mean(dxh * xhat, axis=-1, keepdims=True))

    row = pl.BlockSpec((tm, D), lambda i: (i, 0))
    return pl.pallas_call(
        body, name="conv_rows_bwd", grid=(s // tm,),
        in_specs=[row, row, pl.BlockSpec((tm, D), lambda i: (i, 2)), _const((1, D)), _const((1, D))],
        out_specs=[row, row, _const((8, D))],
        out_shape=[jax.ShapeDtypeStruct((s, D), F32), jax.ShapeDtypeStruct((s, D), BF16), jax.ShapeDtypeStruct((8, D), F32)],
        compiler_params=_cparams(1),
    )(dya, u1, proj, ln_w, ln_b)


def _conv_bwd(du1, proj, conv_w):
    s = du1.shape[0]
    tm = min(256, s)
    hb = tm // HALO
    n = s // tm
    last32 = s // HALO - 1

    def body(d_ref, dn_ref, av_ref, ag_ref, avh_ref, agh_ref, cw_ref, dp_ref, gcw_ref, small_ref, dwin_ref, uwin_ref, acc_ref):
        i = pl.program_id(0)

        @pl.when(i == 0)
        def _():
            acc_ref[...] = jnp.zeros(acc_ref.shape, F32)
            small_ref[...] = jnp.zeros(small_ref.shape, F32)

        dwin_ref[0:tm, :] = d_ref[...]
        dwin_ref[tm:tm + HALO, :] = jnp.where(i < n - 1, dn_ref[...], 0.0)
        halo = avh_ref[...] * _sig(agh_ref[...])
        uwin_ref[0:HALO, :] = jnp.where(i > 0, halo, 0.0)
        av = av_ref[...]
        sg = _sig(ag_ref[...])
        uwin_ref[HALO:HALO + tm, :] = av * sg
        small_ref[0:1, :] += jnp.sum(d_ref[...], axis=0, keepdims=True)

        for c0 in range(0, D, CONV_LC):
            lanes = pl.ds(c0, CONV_LC)
            for r0 in range(0, tm, CONV_RC):
                acc = jnp.zeros((CONV_RC, CONV_LC), F32)
                for j in range(KCONV):
                    acc = acc + dwin_ref[pl.ds(r0 + KCONV - 1 - j, CONV_RC), lanes] * cw_ref[j:j + 1, c0:c0 + CONV_LC]
                a = av[r0:r0 + CONV_RC, c0:c0 + CONV_LC]
                g = sg[r0:r0 + CONV_RC, c0:c0 + CONV_LC]
                dp_ref[r0:r0 + CONV_RC, c0:c0 + CONV_LC] = (acc * g).astype(BF16)
                dp_ref[r0:r0 + CONV_RC, D + c0:D + c0 + CONV_LC] = (acc * a * (g * (1.0 - g))).astype(BF16)
            for j in range(KCONV):
                part = jnp.zeros((8, CONV_LC), F32)
                for r0 in range(0, tm, CONV_RC):
                    prod = dwin_ref[pl.ds(r0, CONV_RC), lanes] * uwin_ref[pl.ds(r0 + HALO - (KCONV - 1) + j, CONV_RC), lanes]
                    part = part + jnp.sum(prod.reshape(CONV_RC // 8, 8, CONV_LC), axis=0)
                acc_ref[j, :, c0:c0 + CONV_LC] += part

        @pl.when(i == n - 1)
        def _():
            gcw_ref[...] = jnp.sum(acc_ref[...], axis=1)

    row = lambda k: pl.BlockSpec((tm, D), lambda i: (i, k))
    prev = lambda k: pl.BlockSpec((HALO, D), lambda i: (jnp.maximum(i * hb - 1, 0), k))
    return pl.pallas_call(
        body, name="conv_bwd", grid=(n,),
        in_specs=[row(0), pl.BlockSpec((HALO, D), lambda i: (jnp.minimum((i + 1) * hb, last32), 0)),
                  row(0), row(1), prev(0), prev(1), _const((KPAD, D))],
        out_specs=[pl.BlockSpec((tm, 2 * D), lambda i: (i, 0)), _const((KPAD, D)), _const((8, D))],
        out_shape=[jax.ShapeDtypeStruct((s, 2 * D), BF16), jax.ShapeDtypeStruct((KPAD, D), F32), jax.ShapeDtypeStruct((8, D), F32)],
        scratch_shapes=[pltpu.VMEM((tm + HALO, D), F32), pltpu.VMEM((tm + HALO, D), F32), pltpu.VMEM((KPAD, 8, D), F32)],
        compiler_params=_cparams(1),
    )(du1, du1, proj, proj, proj, proj, conv_w)


def _dproj_specs(tm, rows_first):
    def spec(lo, hi):
        def idx(a, b):
            i, k = (a, b) if rows_first else (b, a)
            col = jnp.clip(k - lo, 0, hi - lo - 1)
            if rows_first:
                return (i, col)
            return (jnp.where((k >= lo) & (k < hi), i, 0), col)
        return pl.BlockSpec((tm, D), idx)
    return [spec(0, 2), spec(2, 3), spec(3, 4), spec(4, 7)]


def _pick_dproj(k, refs, fn):
    vg, ag, mla, gates = refs

    @pl.when(k < 2)
    def _():
        fn(vg)

    @pl.when(k == 2)
    def _():
        fn(ag)

    @pl.when(k == 3)
    def _():
        fn(mla)

    @pl.when(k > 3)
    def _():
        fn(gates)


def _in_proj_bwd_x(dps, wt, x, dx2, norm_w, mod8):
    s = x.shape[0]
    tm = min(512, s)
    nk = wt.shape[0] // D

    def body(vg_ref, ag_ref, mla_ref, g_ref, w_ref, x_ref, dx2_ref, nw_ref, mod_ref, gx_ref, small_ref, acc_ref):
        i, k = pl.program_id(0), pl.program_id(1)

        @pl.when((i == 0) & (k == 0))
        def _():
            small_ref[...] = jnp.zeros(small_ref.shape, F32)

        @pl.when(k == 0)
        def _():
            acc_ref[...] = jnp.zeros(acc_ref.shape, F32)

        def add(ref):
            acc_ref[...] += _nn(ref[...], w_ref[...])

        _pick_dproj(k, (vg_ref, ag_ref, mla_ref, g_ref), add)

        @pl.when(k == nk - 1)
        def _():
            dh = acc_ref[...]
            xn, rstd = _rms_parts(x_ref[...])
            nw = nw_ref[...]
            hn = xn * nw
            small_ref[0:1, :] += jnp.sum(dh, axis=0, keepdims=True)
            small_ref[1:2, :] += jnp.sum(dh * hn, axis=0, keepdims=True)
            dhn = dh * (1.0 + mod_ref[1:2, :])
            small_ref[2:3, :] += jnp.sum(dhn * xn, axis=0, keepdims=True)
            dxn = dhn * nw
            gx_ref[...] = rstd * (dxn - xn * jnp.mean(dxn * xn, axis=-1, keepdims=True)) + dx2_ref[...]

    row = pl.BlockSpec((tm, D), lambda i, k: (i, 0))
    return pl.pallas_call(
        body, name="in_proj_bwd_x", grid=(s // tm, nk),
        in_specs=_dproj_specs(tm, True) + [pl.BlockSpec((D, D), lambda i, k: (k, 0)), row, row, _const((1, D)), _const((8, D))],
        out_specs=[row, _const((8, D))],
        out_shape=[jax.ShapeDtypeStruct((s, D), F32), jax.ShapeDtypeStruct((8, D), F32)],
        scratch_shapes=[pltpu.VMEM((tm, D), F32)],
        compiler_params=_cparams(2),
    )(*dps, wt, x, dx2, norm_w, mod8)


def _in_proj_bwd_w(dps, h, nk):
    s = h.shape[0]
    tm = min(1024, s)

    def body(vg_ref, ag_ref, mla_ref, g_ref, h_ref, gw_ref):
        k, i = pl.program_id(0), pl.program_id(1)

        @pl.when(i == 0)
        def _():
            gw_ref[...] = jnp.zeros(gw_ref.shape, F32)

        def add(ref):
            gw_ref[...] += _tn(ref[...], h_ref[...])

        _pick_dproj(k, (vg_ref, ag_ref, mla_ref, g_ref), add)

    return pl.pallas_call(
        body, name="in_proj_bwd_w", grid=(nk, s // tm),
        in_specs=_dproj_specs(tm, False) + [pl.BlockSpec((tm, D), lambda k, i: (i, 0))],
        out_specs=pl.BlockSpec((D, D), lambda k, i: (k, 0)),
        out_shape=jax.ShapeDtypeStruct((nk * D, D), F32),
        compiler_params=_cparams(2),
    )(*dps, h)


def _small_slab(wuq, wukv, conv_w):
    cw = jnp.pad(conv_w.reshape(-1), (0, (ROWS_TAIL - ROWS_UQ - ROWS_UKV) * D - KCONV * 128)).reshape(-1, D)
    return jnp.concatenate([wuq.reshape(ROWS_UQ, D), wukv.reshape(ROWS_UKV, D), cw], axis=0)


def _split_small_slab(slab):
    return (slab[..., :ROWS_UQ, :], slab[..., ROWS_UQ:ROWS_UQ + ROWS_UKV, :],
            slab[..., ROWS_UQ + ROWS_UKV:ROWS_UQ + ROWS_UKV + ROWS_CW, :])


def _unpack_small_slab(slab):
    wuq, wukv, cw = _split_small_slab(slab)
    return (wuq.reshape(QL, NH * DQK // N_DEV), wukv.reshape(KVL, NH * 256 // N_DEV),
            cw.reshape(-1)[:KCONV * 128].reshape(KCONV, 128))


def _pack_shard(w_in, wco, wao, wout, wuq, wukv, conv_w):
    bf = lambda a: a.astype(BF16)
    return jnp.concatenate([jnp.pad(bf(w_in).T, ((0, OFF_SQ - ROWS_IN), (0, 0))), bf(wco), bf(wao), bf(wout),
                            bf(_small_slab(wuq, wukv, conv_w))], axis=0)


def _unpack_gathered(g):
    wt = g[:, :ROWS_IN].reshape(IN_COLS, D)
    split = 3 * D + MLA_COLS
    wt = jnp.concatenate([wt[:split], jnp.zeros((D - MLA_COLS, D), g.dtype), wt[split:]], axis=0)
    wuq, wukv, cw = _split_small_slab(g[:, OFF_TAIL:])
    wuq = wuq.reshape(N_DEV, QL, NH * DQK // N_DEV).transpose(1, 0, 2).reshape(QL, NH * DQK)
    wukv = wukv.reshape(N_DEV, KVL, NH * 256 // N_DEV).transpose(1, 0, 2).reshape(KVL, NH * 256)
    cw = cw.reshape(N_DEV, ROWS_CW * D)[:, :KCONV * 128].reshape(N_DEV, KCONV, 128).transpose(1, 0, 2).reshape(KCONV, D)
    return wt, wuq, wukv, cw


def _pack_grads(gwt, gw3, guq, gukv, gcw):
    split = 3 * D + MLA_COLS
    g_in = jnp.concatenate([gwt[:split], gwt[4 * D:]], axis=0).reshape(N_DEV, ROWS_IN, D)
    g_in = jnp.pad(g_in, ((0, 0), (0, OFF_SQ - ROWS_IN), (0, 0)))
    guq = guq.reshape(QL, N_DEV, -1).transpose(1, 0, 2).reshape(N_DEV, ROWS_UQ, D)
    gukv = gukv.reshape(KVL, N_DEV, -1).transpose(1, 0, 2).reshape(N_DEV, ROWS_UKV, D)
    gcw = gcw.reshape(KCONV, N_DEV, 128).transpose(1, 0, 2).reshape(N_DEV, KCONV * 128)
    gcw = jnp.pad(gcw, ((0, 0), (0, (ROWS_TAIL - ROWS_UQ - ROWS_UKV) * D - KCONV * 128))).reshape(N_DEV, -1, D)
    gsq = gw3.reshape(3, N_DEV, ROWS_SQ, D).transpose(1, 0, 2, 3).reshape(N_DEV, 3 * ROWS_SQ, D)
    slabs = jnp.concatenate([g_in, gsq, guq, gukv, gcw], axis=1).astype(BF16)
    return slabs.reshape(N_CHIP, 2, ROWS_PACK, D).transpose(1, 0, 2, 3)


def _pack_small(vecs):
    flat = jnp.concatenate([v.reshape(-1) for v in vecs])
    return jnp.pad(flat, (0, SMALL_LEN - flat.shape[0])).reshape(8, SMALL_COLS)


def _unpack_small(a, shapes):
    flat = a.reshape(-1)
    out, off = [], 0
    for shp, n in zip(shapes, SMALL_SIZES):
        out.append(flat[off:off + n].reshape(shp))
        off += n
    return out


def kernel(x, c, positions, w_ada, b_ada, norm_w, w_in, conv_w, conv_b, conv_ln_w, conv_ln_b, w_conv_out, q_norm_w, w_uq, kv_norm_w, w_ukv, w_attn_out, w_out, final_norm_w, loss_target, m_w_ada, m_b_ada, m_norm_w, m_w_in, m_conv_w, m_conv_b, m_conv_ln_w, m_conv_ln_b, m_w_conv_out, m_q_norm_w, m_w_uq, m_kv_norm_w, m_w_ukv, m_w_attn_out, m_w_out, m_final_norm_w, v_w_ada, v_b_ada, v_norm_w, v_w_in, v_conv_w, v_conv_b, v_conv_ln_w, v_conv_ln_b, v_w_conv_out, v_q_norm_w, v_w_uq, v_kv_norm_w, v_w_ukv, v_w_attn_out, v_w_out, v_final_norm_w):
    me = 4 * lax.axis_index("x") + 2 * lax.axis_index("y") + lax.axis_index("c")
    xs, tgt = x[0], loss_target[0]
    s = xs.shape[0]
    ada_cols = w_ada.shape[2]

    sharded = lambda t: tuple(a[0] for a in t)
    gathered = _all_gather(_pack_shard(*sharded((w_in, w_conv_out, w_attn_out, w_out, w_uq, w_ukv, conv_w))), "gather_weights")
    wt, wuq, wukv, cw = _unpack_gathered(gathered)
    cw32 = jnp.pad(cw.astype(F32), ((0, KPAD - KCONV), (0, 0)))

    c_all = _all_gather(jnp.broadcast_to(c, (8, D)), "gather_c")[:, 0, :]
    b_cols = lax.dynamic_slice(b_ada, (0, me * ada_cols), (1, ada_cols))
    mod_cols = _all_gather(_ada_mod(c_all, w_ada[0], b_cols), "gather_mod")
    mod = lax.dynamic_index_in_dim(mod_cols, me, axis=1, keepdims=False).reshape(3, D)
    mod8 = jnp.pad(mod, ((0, 5), (0, 0)))

    pos = positions.reshape(s, 1)
    inv_freq = (ROPE_THETA ** (-jnp.arange(0, ROPE, 2, dtype=F32) / ROPE)).reshape(1, HALF)
    proj, h = _in_proj(xs, norm_w, mod8, wt)
    u1, ya = _conv_fwd(proj, cw32, conv_b, conv_ln_w, conv_ln_b)
    q, k, v = _mla_prep(proj, pos, inv_freq, q_norm_w, kv_norm_w, wuq, wukv)
    o, lse = _attn_fwd(q, k, v)

    dya, do, delta, dp_gates, dx2, gw3, small_a = _merge_loss(xs, tgt, ya, o, proj, mod8, final_norm_w.reshape(1, D), gathered)
    dq, dk, dv = _attn_bwd(q, k, v, do, lse, delta)
    dp_mla, guq, gukv, small_b = _mla_bwd(dq, dk, dv, proj, pos, inv_freq, q_norm_w, kv_norm_w, wuq, wukv)
    du1, dp_ag, small_c = _conv_rows_bwd(dya, u1, proj, conv_ln_w, conv_ln_b)
    dp_vg, gcw, small_d = _conv_bwd(du1, proj, cw32)
    dps = [dp_vg, dp_ag, dp_mla, dp_gates]
    gwt = _in_proj_bwd_w(dps, h, wt.shape[0] // D)

    packed = _pack_grads(gwt, gw3, guq, gukv, gcw[:KCONV])
    core = lax.axis_index("c").astype(jnp.int32).reshape(1)
    half = _pair_add(core, packed, _pair_exchange(packed), ROWS_PACK // 4)
    send_sems, recv_sems, half_thru, land_thru, token = _chip_exchange_start(half)
    grad_x, small_e = _in_proj_bwd_x(dps, wt, xs, dx2, norm_w + token[0:1, 0:1], mod8)
    half, recv = _chip_exchange_wait(send_sems, recv_sems, half_thru, land_thru, small_e)
    chip = 2 * lax.axis_index("x") + lax.axis_index("y")
    recv = lax.dynamic_update_slice(recv, lax.dynamic_slice(half, (chip, 0, 0), (1,) + half.shape[1:]), (chip, 0, 0))
    big_in = _adam_w_in(recv, w_in[0], m_w_in[0], v_w_in[0])
    squares = (("w_conv_out", w_conv_out, m_w_conv_out, v_w_conv_out), ("w_attn_out", w_attn_out, m_w_attn_out, v_w_attn_out),
               ("w_out", w_out, m_w_out, v_w_out))
    big_sq = [_reduce_adam(recv, w[0], m[0], v[0], "adam_" + nm, ROWS_SQ, OFF_SQ // ROWS_SQ + j) for j, (nm, w, m, v) in enumerate(squares)]
    tail = _reduce_adam(recv, _small_slab(w_uq[0], w_ukv[0], conv_w[0]), _small_slab(m_w_uq[0], m_w_ukv[0], m_conv_w[0]),
                        _small_slab(v_w_uq[0], v_w_ukv[0], v_conv_w[0]), "adam_small_sharded", ROWS_TAIL, OFF_TAIL // ROWS_TAIL)
    tail = [_unpack_small_slab(a) for a in tail]
    big = [(big_in[i], big_sq[0][i], big_sq[1][i], big_sq[2][i], *tail[i]) for i in range(4)]

    dmod = jnp.concatenate([small_e[0], small_e[1], small_a[1]])
    payload = _pack_small([dmod, small_e[2], small_d[0], small_c[0], small_c[1], small_a[0], small_b[0], small_b[1], small_a[2, 0:1]])
    pay_all = _all_gather(payload, "gather_small")
    small_w = (b_ada, norm_w, conv_b, conv_ln_w, conv_ln_b, final_norm_w, q_norm_w, kv_norm_w)
    small_m = (m_b_ada, m_norm_w, m_conv_b, m_conv_ln_w, m_conv_ln_b, m_final_norm_w, m_q_norm_w, m_kv_norm_w)
    small_v = (v_b_ada, v_norm_w, v_conv_b, v_conv_ln_w, v_conv_ln_b, v_final_norm_w, v_q_norm_w, v_kv_norm_w)
    sm = _reduce_adam(pay_all, _pack_small(small_w), _pack_small(small_m), _pack_small(small_v), "adam_replicated", 8, 0)
    loss = sm[0].reshape(-1)[sum(SMALL_SIZES)]
    shapes = [t.shape for t in small_w]
    sm = [_unpack_small(a, shapes) for a in sm]

    dmod_all = pay_all.reshape(N_DEV, SMALL_LEN)[:, :3 * D]
    dmod_cols = lax.dynamic_slice(dmod_all, (0, me * ada_cols), (N_DEV, ada_cols))
    ada = _ada_bwd(c_all, dmod_cols, w_ada[0], m_w_ada[0], v_w_ada[0])

    def group(i):
        b_in, b_co, b_ao, b_out, b_uq, b_ukv, b_cw = big[i]
        s_bada, s_nw, s_cb, s_clw, s_clb, s_fnw, s_qnw, s_kvnw = sm[i]
        return (ada[i][None], s_bada, s_nw, b_in[None], b_cw[None], s_cb, s_clw, s_clb, b_co[None], s_qnw, b_uq[None], s_kvnw,
                b_ukv[None], b_ao[None], b_out[None], s_fnw)

    return (loss, grad_x[None], *group(0), *group(1), *group(2), *group(3))
```

```python
import functools

import jax
import jax.numpy as jnp
from jax import lax
from jax.experimental import pallas as pl
from jax.experimental.pallas import tpu as pltpu

F32 = jnp.float32
BF16 = jnp.bfloat16

D = 1024
NH = 8
NOPE = 128
ROPE = 64
HALF = ROPE // 2
DQK = NOPE + ROPE
DV = 128
QL = 256
KVL = 256
KCONV = 31
KPAD = 32
HALO = 32
IN_COLS = 6720
MLA_COLS = QL + KVL + ROPE
PROJ_COLS = 7 * D
EPS = 1e-6
ROPE_THETA = 10000.0
N_DEV = 8

ADAM_LR = 0.001
ADAM_B1 = 0.9
ADAM_B2 = 0.999
ADAM_EPS = 1e-08
ADAM_WD = 0.01
ADAM_STEP = 10

ROWS_IN = 840
ROWS_SQ = 128
ROWS_UQ = 48
ROWS_UKV = 64
ROWS_CW = 4
OFF_SQ = 896
OFF_TAIL = OFF_SQ + 3 * ROWS_SQ
ROWS_TAIL = 128
ROWS_PACK = OFF_TAIL + ROWS_TAIL
SMALL_SIZES = (3 * D, D, D, D, D, D, QL, KVL)
SMALL_COLS = 1152
SMALL_LEN = 8 * SMALL_COLS

MESH = pl.DeviceIdType.MESH
ANY = pl.BlockSpec(memory_space=pl.ANY)
V7X_VMEM_LIMIT = 56 * 1024 * 1024


def _cparams(n_axes, vmem=V7X_VMEM_LIMIT):
    return pltpu.CompilerParams(dimension_semantics=("arbitrary",) * n_axes, vmem_limit_bytes=vmem)


def _sig(x):
    return jax.nn.sigmoid(x)


def _nt(a, b):
    return lax.dot_general(a, b, (((1,), (1,)), ((), ())), preferred_element_type=F32)


def _tn(a, b):
    return lax.dot_general(a, b, (((0,), (0,)), ((), ())), preferred_element_type=F32)


def _nn(a, b):
    return jnp.dot(a, b, preferred_element_type=F32)


def _const(shape):
    return pl.BlockSpec(shape, lambda *_: (0,) * len(shape))


def _all_gather(block, name):
    r, c = block.shape

    def body(x_ref, out_ref, send_sems, recv_sems, local_sem):
        x, y, cc = lax.axis_index("x"), lax.axis_index("y"), lax.axis_index("c")
        me, sibling = (x, y, cc), (x, y, 1 - cc)
        chips = [(1 - x, y), (x, 1 - y), (1 - x, 1 - y)]

        def slot(px, py, pc):
            return out_ref.at[4 * px + 2 * py + pc]

        def copy(k, blk, to, src=None):
            return pltpu.make_async_remote_copy(
                src_ref=slot(*blk) if src is None else src, dst_ref=slot(*blk),
                send_sem=send_sems.at[k], recv_sem=recv_sems.at[k],
                device_id=to, device_id_type=MESH)

        mine = pltpu.make_async_copy(x_ref, slot(*me), local_sem)
        mine.start()
        first = [copy(0, me, sibling, src=x_ref)]
        first += [copy(1 + j, me, (*chip, cc), src=x_ref) for j, chip in enumerate(chips)]
        for cp in first:
            cp.start()
        passed = [copy(4 + j, (*chip, cc), sibling) for j, chip in enumerate(chips)]
        for j, chip in enumerate(chips):
            copy(1 + j, (*chip, cc), me).wait_recv()
            passed[j].start()
        copy(0, sibling, me).wait_recv()
        for j, chip in enumerate(chips):
            copy(4 + j, (*chip, 1 - cc), me).wait_recv()
        for cp in first + passed:
            cp.wait_send()
        mine.wait()

    return pl.pallas_call(
        body, name=name,
        out_shape=jax.ShapeDtypeStruct((N_DEV, r, c), block.dtype),
        in_specs=[ANY], out_specs=ANY,
        scratch_shapes=[pltpu.SemaphoreType.DMA((7,)), pltpu.SemaphoreType.DMA((7,)), pltpu.SemaphoreType.DMA],
    )(block)


N_CHIP = 4


def _pair_exchange(packed):
    _, _, r, c = packed.shape

    def body(src_ref, out_ref, send_sem, recv_sem):
        x, y, cc = lax.axis_index("x"), lax.axis_index("y"), lax.axis_index("c")
        cp = pltpu.make_async_remote_copy(
            src_ref=src_ref.at[1 - cc], dst_ref=out_ref, send_sem=send_sem, recv_sem=recv_sem,
            device_id=(x, y, 1 - cc), device_id_type=MESH)
        cp.start()
        cp.wait()

    return pl.pallas_call(
        body, name="pair_exchange",
        out_shape=jax.ShapeDtypeStruct((N_CHIP, r, c), packed.dtype),
        in_specs=[ANY], out_specs=ANY,
        scratch_shapes=[pltpu.SemaphoreType.DMA, pltpu.SemaphoreType.DMA],
    )(packed)


def _pair_add(core, packed, got, tr):
    _, _, r, c = packed.shape

    def body(core_ref, own_ref, got_ref, out_ref):
        out_ref[...] = (own_ref[0].astype(F32) + got_ref[...].astype(F32)).astype(out_ref.dtype)

    blk = pl.BlockSpec((1, tr, c), lambda j, i, core_ref: (j, i, 0))
    return pl.pallas_call(
        body, name="pair_add",
        grid_spec=pltpu.PrefetchScalarGridSpec(
            num_scalar_prefetch=1, grid=(N_CHIP, r // tr),
            in_specs=[pl.BlockSpec((1, 1, tr, c), lambda j, i, core_ref: (core_ref[0], j, i, 0)), blk],
            out_specs=blk),
        out_shape=jax.ShapeDtypeStruct((N_CHIP, r, c), packed.dtype),
        compiler_params=_cparams(2),
    )(core, packed, got)


HBM = pl.BlockSpec(memory_space=pltpu.HBM)
SEM = pl.BlockSpec(memory_space=pltpu.SEMAPHORE)
EFFECT = pltpu.SideEffectType.DATAFLOW_SIDE_EFFECTING


def _chip_copies(src_ref, land_ref, send_sems, recv_sems):
    x, y, cc = lax.axis_index("x"), lax.axis_index("y"), lax.axis_index("c")
    me = 2 * x + y
    copies = []
    for k in range(1, N_CHIP):
        px, py = (1 - x if k & 2 else x), (1 - y if k & 1 else y)
        copies.append(pltpu.make_async_remote_copy(
            src_ref=src_ref.at[2 * px + py], dst_ref=land_ref.at[me],
            send_sem=send_sems.at[k - 1], recv_sem=recv_sems.at[k - 1],
            device_id=(px, py, cc), device_id_type=MESH))
    return copies


def _chip_exchange_start(half):
    def body(src_ref, land_ref, send_sems, recv_sems, src_thru, land_thru, token):
        for cp in _chip_copies(src_ref, land_ref, send_sems, recv_sems):
            cp.start()
        token[...] = jnp.zeros_like(token)

    return pl.pallas_call(
        body, name="chip_exchange_start",
        out_shape=(pltpu.SemaphoreType.DMA((N_CHIP - 1,)), pltpu.SemaphoreType.DMA((N_CHIP - 1,)),
                   pltpu.HBM(half.shape, half.dtype), pltpu.HBM(half.shape, half.dtype), jax.ShapeDtypeStruct((8, 128), F32)),
        in_specs=(HBM, HBM), out_specs=(SEM, SEM, HBM, HBM, pl.BlockSpec(memory_space=pltpu.VMEM)),
        input_output_aliases={0: 2, 1: 3},
        compiler_params=pltpu.CompilerParams(has_side_effects=EFFECT),
    )(pltpu.with_memory_space_constraint(half, pltpu.HBM),
      pltpu.with_memory_space_constraint(lax.empty(half.shape, half.dtype), pltpu.HBM))


def _chip_exchange_wait(send_sems, recv_sems, src_thru, land_thru, after):
    def body(src_ref, land_ref, send_sems, recv_sems, after_ref, src_dead, got_ref):
        copies = _chip_copies(src_ref, land_ref, send_sems, recv_sems)
        for cp in copies:
            cp.wait_send()
        for cp in copies:
            cp.wait_recv()

    return pl.pallas_call(
        body, name="chip_exchange_wait",
        out_shape=(pltpu.HBM(src_thru.shape, src_thru.dtype), pltpu.HBM(land_thru.shape, land_thru.dtype)),
        in_specs=(HBM, HBM, SEM, SEM, ANY), out_specs=(HBM, HBM), input_output_aliases={0: 0, 1: 1},
        compiler_params=pltpu.CompilerParams(has_side_effects=EFFECT),
    )(src_thru, land_thru, send_sems, recv_sems, after)


def _adam(g, w, m, v):
    m = ADAM_B1 * m + (1.0 - ADAM_B1) * g
    v = ADAM_B2 * v + (1.0 - ADAM_B2) * (g * g)
    m_hat = m / (1.0 - ADAM_B1 ** ADAM_STEP)
    v_hat = v / (1.0 - ADAM_B2 ** ADAM_STEP)
    delta = -ADAM_LR * (m_hat / (jnp.sqrt(v_hat) + ADAM_EPS) + ADAM_WD * w)
    return delta, m, v


def _adam_w_in(parts, w, m, v):
    n = parts.shape[0]
    tc = 128

    def body(p_ref, w_ref, m_ref, v_ref, g_out, d_out, m_out, v_out):
        gt = p_ref[0].astype(F32)
        for j in range(1, n):
            gt = gt + p_ref[j].astype(F32)
        g = gt.T[:, :ROWS_IN]
        delta, nm, nv = _adam(g, w_ref[...], m_ref[...], v_ref[...])
        g_out[...] = g
        d_out[...] = delta
        m_out[...] = nm
        v_out[...] = nv

    row = pl.BlockSpec((tc, ROWS_IN), lambda i: (i, 0))
    return pl.pallas_call(
        body, name="adam_w_in", grid=(D // tc,),
        in_specs=[pl.BlockSpec((n, OFF_SQ, tc), lambda i: (0, 0, i)), row, row, row],
        out_specs=[row] * 4,
        out_shape=[jax.ShapeDtypeStruct((D, ROWS_IN), F32)] * 4,
        compiler_params=_cparams(1),
    )(parts, w, m, v)


def _reduce_adam(parts, w, m, v, name, tr, first_block):
    n, _, c = parts.shape
    r = w.shape[0]

    def body(p_ref, w_ref, m_ref, v_ref, g_out, d_out, m_out, v_out):
        g = p_ref[0].astype(F32)
        for j in range(1, n):
            g = g + p_ref[j].astype(F32)
        delta, nm, nv = _adam(g, w_ref[...], m_ref[...], v_ref[...])
        g_out[...] = g
        d_out[...] = delta
        m_out[...] = nm
        v_out[...] = nv

    row = pl.BlockSpec((tr, c), lambda i: (i, 0))
    return pl.pallas_call(
        body, name=name, grid=(r // tr,),
        in_specs=[pl.BlockSpec((n, tr, c), lambda i: (0, first_block + i, 0)), row, row, row],
        out_specs=[row] * 4,
        out_shape=[jax.ShapeDtypeStruct((r, c), F32)] * 4,
        compiler_params=_cparams(1),
    )(parts, w, m, v)


def _ada_mod(c_all, w_ada, b_cols):
    def body(c_ref, w_ref, b_ref, o_ref):
        cv = c_ref[...]
        act = (cv * _sig(cv)).astype(BF16)
        o_ref[...] = _nn(act, w_ref[...].astype(BF16)) + b_ref[...]

    return pl.pallas_call(body, name="ada_mod", out_shape=jax.ShapeDtypeStruct((N_DEV, w_ada.shape[1]), F32))(c_all, w_ada, b_cols)


def _ada_bwd(c_all, dmod_cols, w, m, v):
    def body(c_ref, d_ref, w_ref, m_ref, v_ref, g_out, d_out, m_out, v_out):
        cv = c_ref[...]
        act = (cv * _sig(cv)).astype(BF16)
        g = _tn(act, d_ref[...].astype(BF16))
        delta, nm, nv = _adam(g, w_ref[...], m_ref[...], v_ref[...])
        g_out[...] = g
        d_out[...] = delta
        m_out[...] = nm
        v_out[...] = nv

    return pl.pallas_call(body, name="ada_bwd", out_shape=[jax.ShapeDtypeStruct(w.shape, F32)] * 4)(c_all, dmod_cols, w, m, v)


def _in_proj(x, norm_w, mod8, wt):
    s = x.shape[0]
    tm = min(1024, s)
    nk = wt.shape[0] // D

    def body(x_ref, nw_ref, mod_ref, w_ref, proj_ref, h_ref, hs_ref):
        @pl.when(pl.program_id(1) == 0)
        def _():
            xv = x_ref[...]
            rstd = lax.rsqrt(jnp.mean(xv * xv, axis=-1, keepdims=True) + EPS)
            h = (xv * rstd) * nw_ref[...] * (1.0 + mod_ref[1:2, :]) + mod_ref[0:1, :]
            hs_ref[...] = h.astype(BF16)
            h_ref[...] = hs_ref[...]

        proj_ref[...] = _nt(hs_ref[...], w_ref[...])

    return pl.pallas_call(
        body, name="in_proj", grid=(s // tm, nk),
        in_specs=[pl.BlockSpec((tm, D), lambda i, k: (i, 0)), _const((1, D)), _const((8, D)),
                  pl.BlockSpec((D, D), lambda i, k: (k, 0))],
        out_specs=[pl.BlockSpec((tm, D), lambda i, k: (i, k)), pl.BlockSpec((tm, D), lambda i, k: (i, 0))],
        out_shape=[jax.ShapeDtypeStruct((s, nk * D), F32), jax.ShapeDtypeStruct((s, D), BF16)],
        scratch_shapes=[pltpu.VMEM((tm, D), BF16)],
        compiler_params=_cparams(2),
    )(x, norm_w, mod8, wt)


CONV_RC = 128
CONV_LC = 128
GW_RC = 32
SUBLANES = 8


def _shifted_taps(win_ref, weight_of, offsets, r0, lanes):
    acc = jnp.zeros((CONV_RC, CONV_LC), F32)
    for b in range(SUBLANES):
        group = [o for o in offsets if o % SUBLANES == b]
        if not group:
            continue
        rows = CONV_RC if b == 0 else CONV_RC + SUBLANES
        part = jnp.zeros((rows, CONV_LC), F32)
        for o in group:
            part = part + win_ref[pl.ds(r0 + o - b, rows), lanes] * weight_of(o)
        acc = acc + (part if b == 0 else part[b:b + CONV_RC])
    return acc


def _conv_fwd(proj, conv_w, conv_b, ln_w, ln_b):
    s = proj.shape[0]
    tm = min(256, s)
    hb = tm // HALO

    def body(av_ref, ag_ref, avh_ref, agh_ref, gate_ref, cw_ref, cb_ref, lw_ref, lb_ref, u1_ref, ya_ref, win_ref):
        i = pl.program_id(0)
        halo = avh_ref[...] * _sig(agh_ref[...])
        win_ref[0:HALO, :] = jnp.where(i > 0, halo, 0.0)
        win_ref[HALO:HALO + tm, :] = av_ref[...] * _sig(ag_ref[...])
        first = HALO - (KCONV - 1)
        for r0 in range(0, tm, CONV_RC):
            for c0 in range(0, D, CONV_LC):
                acc = _shifted_taps(win_ref, lambda o: cw_ref[o - first:o - first + 1, c0:c0 + CONV_LC],
                                    range(first, first + KCONV), r0, pl.ds(c0, CONV_LC))
                u1_ref[r0:r0 + CONV_RC, c0:c0 + CONV_LC] = acc + cb_ref[:, c0:c0 + CONV_LC]
        u1 = u1_ref[...]
        mu = jnp.mean(u1, axis=-1, keepdims=True)
        xc = u1 - mu
        var = jnp.mean(xc * xc, axis=-1, keepdims=True)
        ln = xc * lax.rsqrt(var + EPS) * lw_ref[...] + lb_ref[...]
        gate = gate_ref[...]
        ya_ref[...] = ((ln * _sig(ln)) * (gate * _sig(gate))).astype(BF16)

    row = lambda k: pl.BlockSpec((tm, D), lambda i: (i, k))
    prev = lambda k: pl.BlockSpec((HALO, D), lambda i: (jnp.maximum(i * hb - 1, 0), k))
    return pl.pallas_call(
        body, name="conv_fwd", grid=(s // tm,),
        in_specs=[row(0), row(1), prev(0), prev(1), row(2), _const((KPAD, D)), _const((1, D)), _const((1, D)), _const((1, D))],
        out_specs=[pl.BlockSpec((tm, D), lambda i: (i, 0))] * 2,
        out_shape=[jax.ShapeDtypeStruct((s, D), F32), jax.ShapeDtypeStruct((s, D), BF16)],
        scratch_shapes=[pltpu.VMEM((tm + HALO, D), F32)],
        compiler_params=_cparams(1),
    )(proj, proj, proj, proj, proj, conv_w, conv_b, ln_w, ln_b)


def _rope_tables(pos_ref, if_ref):
    ang = pos_ref[...].astype(F32) * if_ref[...]
    return jnp.cos(ang), jnp.sin(ang)


def _rms_parts(x):
    rstd = lax.rsqrt(jnp.mean(x * x, axis=-1, keepdims=True) + EPS)
    return x * rstd, rstd


def _mla_prep(proj, pos, inv_freq, qnw, kvnw, wuq, wukv):
    s = proj.shape[0]
    tm = min(512, s)

    def body(p_ref, pos_ref, if_ref, qnw_ref, kvnw_ref, wuq_ref, wukv_ref, q_ref, k_ref, v_ref):
        blk = p_ref[...]
        cos, sin = _rope_tables(pos_ref, if_ref)

        def rope(r):
            x1, x2 = r[:, :HALF], r[:, HALF:]
            return jnp.concatenate([x1 * cos - x2 * sin, x1 * sin + x2 * cos], axis=-1)

        qlat = _rms_parts(blk[:, :QL])[0] * qnw_ref[...]
        kvlat = _rms_parts(blk[:, QL:QL + KVL])[0] * kvnw_ref[...]
        q = _nn(qlat.astype(BF16), wuq_ref[...])
        kv = _nn(kvlat.astype(BF16), wukv_ref[...])
        kr = rope(blk[:, QL + KVL:MLA_COLS])
        for h in range(NH):
            qh = q[:, h * DQK:(h + 1) * DQK]
            q_ref[h] = (jnp.concatenate([qh[:, :NOPE], rope(qh[:, NOPE:])], axis=-1) * ATTN_SCALE).astype(BF16)
            k_ref[h] = jnp.concatenate([kv[:, h * 256:h * 256 + NOPE], kr], axis=-1).astype(BF16)
            v_ref[h] = kv[:, h * 256 + NOPE:(h + 1) * 256].astype(BF16)

    hm = lambda d: pl.BlockSpec((NH, tm, d), lambda i: (0, i, 0))
    return pl.pallas_call(
        body, name="mla_prep", grid=(s // tm,),
        in_specs=[pl.BlockSpec((tm, D), lambda i: (i, 3)), pl.BlockSpec((tm, 1), lambda i: (i, 0)), _const((1, HALF)),
                  _const((1, QL)), _const((1, KVL)), _const((QL, NH * DQK)), _const((KVL, NH * 256))],
        out_specs=[hm(DQK), hm(DQK), hm(DV)],
        out_shape=[jax.ShapeDtypeStruct((NH, s, DQK), BF16), jax.ShapeDtypeStruct((NH, s, DQK), BF16),
                   jax.ShapeDtypeStruct((NH, s, DV), BF16)],
        compiler_params=_cparams(1),
    )(proj, pos, inv_freq, qnw, kvnw, wuq, wukv)


ATTN_SCALE = DQK ** -0.5


def _causal_mask(s, t):
    rows = lax.broadcasted_iota(jnp.int32, (t, t), 0)
    cols = lax.broadcasted_iota(jnp.int32, (t, t), 1)
    return jnp.where(cols <= rows, s, -jnp.inf)


def _attn_tile(s):
    return min(1024, s // 2)


def _attn_fwd(q, k, v):
    nh, s, _ = q.shape
    t = _attn_tile(s)

    def body(q_ref, k_ref, v_ref, o_ref, lse_ref):
        qi = pl.program_id(1)
        qv = q_ref[0]

        def chunk(c, carry, diag):
            m, l, acc = carry
            rows = pl.ds(pl.multiple_of(c * t, t), t)
            sc = _nt(qv, k_ref[0, rows, :])
            if diag:
                sc = _causal_mask(sc, t)
            m_new = jnp.maximum(m, jnp.max(sc, axis=-1, keepdims=True))
            alpha = jnp.exp(m - m_new)
            p = jnp.exp(sc - m_new)
            l = alpha * l + jnp.sum(p, axis=-1, keepdims=True)
            acc = alpha * acc + _nn(p.astype(BF16), v_ref[0, rows, :])
            return m_new, l, acc

        init = (jnp.full((t, 1), -jnp.inf, F32), jnp.zeros((t, 1), F32), jnp.zeros((t, DV), F32))
        carry = lax.fori_loop(0, qi, lambda c, cr: chunk(c, cr, False), init)
        m, l, acc = chunk(qi, carry, True)
        o_ref[...] = acc / l
        lse_ref[0] = jnp.broadcast_to(m + jnp.log(l), (t, DV))

    head = lambda d: pl.BlockSpec((1, s, d), lambda h, i: (h, 0, 0))
    return pl.pallas_call(
        body, name="attn_fwd", grid=(nh, s // t),
        in_specs=[pl.BlockSpec((1, t, DQK), lambda h, i: (h, i, 0)), head(DQK), head(DV)],
        out_specs=[pl.BlockSpec((t, DV), lambda h, i: (i, h)), pl.BlockSpec((1, t, DV), lambda h, i: (h, i, 0))],
        out_shape=[jax.ShapeDtypeStruct((s, nh * DV), F32), jax.ShapeDtypeStruct((nh, s, DV), F32)],
        compiler_params=_cparams(2),
    )(q, k, v)


def _merge_loss(x, target, ya, o, proj, mod8, fnw, gathered):
    s = x.shape[0]
    tm = min(256, s)
    n = s // tm

    def body(x_ref, t_ref, ya_ref, o_ref, bg_ref, ga_ref, gb_ref, mod_ref, fnw_ref, wco_ref, wao_ref, wout_ref,
             dya_ref, do_ref, delta_ref, dpg_ref, dx2_ref, gw_ref, small_ref, acc_ref, cast_ref):
        i = pl.program_id(0)

        @pl.when(i == 0)
        def _():
            acc_ref[...] = jnp.zeros(acc_ref.shape, F32)
            small_ref[...] = jnp.zeros(small_ref.shape, F32)

        bg = bg_ref[...]
        sbg = _sig(bg)
        sb = bg * sbg
        ov = o_ref[...]
        ya = ya_ref[...]
        yb = (ov * sb).astype(BF16)
        square = lambda ref: ref[...].reshape(D, D)
        y_a = _nn(ya, square(wco_ref))
        y_b = _nn(yb, square(wao_ref))
        sa = _sig(ga_ref[...])
        sgb = _sig(gb_ref[...])
        merged = (sa * y_a + sgb * y_b).astype(BF16)
        z = _nn(merged, square(wout_ref))
        gate = mod_ref[2:3, :]
        x2 = x_ref[...] + gate * z
        xn, rstd = _rms_parts(x2)
        fnw = fnw_ref[...]
        err = xn * fnw - t_ref[...]
        loss = jnp.sum(jnp.sum(err * err, axis=-1, keepdims=True), axis=0, keepdims=True) * (0.5 / D)
        dy = err * (1.0 / D)
        small_ref[0:1, :] += jnp.sum(dy * xn, axis=0, keepdims=True)
        dxn = dy * fnw
        dx2 = rstd * (dxn - xn * jnp.mean(dxn * xn, axis=-1, keepdims=True))
        dx2_ref[...] = dx2
        small_ref[1:2, :] += jnp.sum(dx2 * z, axis=0, keepdims=True)
        small_ref[2:3, :] += jnp.broadcast_to(loss, (1, D))
        dz = (dx2 * gate).astype(BF16)
        dmerged = _nt(dz, square(wout_ref))
        acc_ref[2] += _tn(merged, dz)
        dy_a = (dmerged * sa).astype(BF16)
        dy_b = (dmerged * sgb).astype(BF16)
        dpg_ref[:, D:2 * D] = (dmerged * y_a * (sa * (1.0 - sa))).astype(BF16)
        dpg_ref[:, 2 * D:3 * D] = (dmerged * y_b * (sgb * (1.0 - sgb))).astype(BF16)
        dya_ref[...] = _nt(dy_a, square(wco_ref))
        acc_ref[0] += _tn(ya, dy_a)
        dyb = _nt(dy_b, square(wao_ref))
        acc_ref[1] += _tn(yb, dy_b)
        do = dyb * sb
        do_ref[...] = do.astype(BF16)
        dpg_ref[:, 0:D] = (dyb * ov * (sbg * (1.0 + bg * (1.0 - sbg)))).astype(BF16)
        prod = do * ov
        for h in range(NH):
            delta_ref[h] = jnp.broadcast_to(jnp.sum(prod[:, h * DV:(h + 1) * DV], axis=-1, keepdims=True), (tm, DV))

        @pl.when(i == n - 1)
        def _():
            for j in range(3):
                cast_ref[...] = acc_ref[j].astype(cast_ref.dtype)
                pltpu.sync_copy(cast_ref, gw_ref.at[j])

    row = pl.BlockSpec((tm, D), lambda i: (i, 0))
    col = lambda k: pl.BlockSpec((tm, D), lambda i: (i, k))
    wspec = lambda j: pl.BlockSpec((N_DEV, ROWS_SQ, D), lambda i: (0, OFF_SQ // ROWS_SQ + j, 0), pipeline_mode=pl.Buffered(1))
    return pl.pallas_call(
        body, name="merge_loss", grid=(n,),
        in_specs=[row, row, row, row, col(4), col(5), col(6), _const((8, D)), _const((1, D)), wspec(0), wspec(1), wspec(2)],
        out_specs=[row, row, pl.BlockSpec((NH, tm, DV), lambda i: (0, i, 0)), pl.BlockSpec((tm, 3 * D), lambda i: (i, 0)),
                   row, ANY, _const((8, D))],
        out_shape=[jax.ShapeDtypeStruct((s, D), F32), jax.ShapeDtypeStruct((s, D), BF16),
                   jax.ShapeDtypeStruct((NH, s, DV), F32), jax.ShapeDtypeStruct((s, 3 * D), BF16),
                   jax.ShapeDtypeStruct((s, D), F32), jax.ShapeDtypeStruct((3, D, D), BF16),
                   jax.ShapeDtypeStruct((8, D), F32)],
        scratch_shapes=[pltpu.VMEM((3, D, D), F32), pltpu.VMEM((D, D), BF16)],
        compiler_params=_cparams(1),
    )(x, target, ya, o, proj, proj, proj, mod8, fnw, gathered, gathered, gathered)


def _attn_bwd(q, k, v, do, lse, delta):
    nh, s, _ = q.shape
    t = _attn_tile(s)
    nb = s // t

    def body(q_ref, k_ref, v_ref, do_ref, lse_ref, dl_ref, dq_ref, dk_ref, dv_ref):
        kj = pl.program_id(1)

        @pl.when(kj == 0)
        def _():
            dq_ref[...] = jnp.zeros(dq_ref.shape, F32)

        kv_, vv = k_ref[0], v_ref[0]

        def chunk(c, carry, diag):
            dk, dv = carry
            rows = pl.ds(pl.multiple_of(c * t, t), t)
            qv = q_ref[0, rows, :]
            dov = do_ref[rows, :]
            sc = _nt(qv, kv_)
            if diag:
                sc = _causal_mask(sc, t)
            p = jnp.exp(sc - lse_ref[0, rows, 0:1])
            dv = dv + _tn(p.astype(BF16), dov)
            dp = _nt(dov, vv)
            ds = (p * (dp - dl_ref[0, rows, 0:1])).astype(BF16)
            dk = dk + _tn(ds, qv)
            dq_ref[0, rows, :] += _nn(ds, kv_)
            return dk, dv

        carry = chunk(kj, (jnp.zeros((t, DQK), F32), jnp.zeros((t, DV), F32)), True)
        dk, dv = lax.fori_loop(kj + 1, nb, lambda c, cr: chunk(c, cr, False), carry)
        dk_ref[0] = dk
        dv_ref[0] = dv

    head = lambda d: pl.BlockSpec((1, s, d), lambda h, j: (h, 0, 0))
    blk = lambda d: pl.BlockSpec((1, t, d), lambda h, j: (h, j, 0))
    return pl.pallas_call(
        body, name="attn_bwd", grid=(nh, nb),
        in_specs=[head(DQK), blk(DQK), blk(DV), pl.BlockSpec((s, DV), lambda h, j: (0, h)), head(DV), head(DV)],
        out_specs=[head(DQK), blk(DQK), blk(DV)],
        out_shape=[jax.ShapeDtypeStruct((nh, s, DQK), F32), jax.ShapeDtypeStruct((nh, s, DQK), F32),
                   jax.ShapeDtypeStruct((nh, s, DV), F32)],
        compiler_params=_cparams(2),
    )(q, k, v, do, lse, delta)


def _mla_bwd(dq, dk, dv, proj, pos, inv_freq, qnw, kvnw, wuq, wukv):
    s = proj.shape[0]
    tm = min(512, s)

    def body(dq_ref, dk_ref, dv_ref, p_ref, pos_ref, if_ref, qnw_ref, kvnw_ref, wuq_ref, wukv_ref,
             dp_ref, guq_ref, gukv_ref, small_ref):
        @pl.when(pl.program_id(0) == 0)
        def _():
            guq_ref[...] = jnp.zeros(guq_ref.shape, F32)
            gukv_ref[...] = jnp.zeros(gukv_ref.shape, F32)
            small_ref[...] = jnp.zeros(small_ref.shape, F32)

        blk = p_ref[...]
        cos, sin = _rope_tables(pos_ref, if_ref)

        def unrope(g):
            g1, g2 = g[:, :HALF], g[:, HALF:]
            return jnp.concatenate([g1 * cos + g2 * sin, g2 * cos - g1 * sin], axis=-1)

        dq_cols, dkv_cols = [], []
        dkr = jnp.zeros((tm, ROPE), F32)
        for h in range(NH):
            dqh, dkh = dq_ref[h] * ATTN_SCALE, dk_ref[h]
            dq_cols += [dqh[:, :NOPE], unrope(dqh[:, NOPE:])]
            dkv_cols += [dkh[:, :NOPE], dv_ref[h]]
            dkr = dkr + dkh[:, NOPE:]
        dq_full = jnp.concatenate(dq_cols, axis=-1).astype(BF16)
        dkv_full = jnp.concatenate(dkv_cols, axis=-1).astype(BF16)

        def latent_bwd(c, nw_ref, d_up, w_ref, g_ref, srow):
            nrm, rstd = _rms_parts(c)
            nw = nw_ref[...]
            lat = (nrm * nw).astype(BF16)
            g_ref[...] += _tn(lat, d_up)
            dlat = _nt(d_up, w_ref[...])
            small_ref[srow:srow + 1, :] += jnp.sum(dlat * nrm, axis=0, keepdims=True)
            dn = dlat * nw
            return rstd * (dn - nrm * jnp.mean(dn * nrm, axis=-1, keepdims=True))

        dcq = latent_bwd(blk[:, :QL], qnw_ref, dq_full, wuq_ref, guq_ref, 0)
        dckv = latent_bwd(blk[:, QL:QL + KVL], kvnw_ref, dkv_full, wukv_ref, gukv_ref, 1)
        dp_ref[...] = jnp.concatenate([dcq, dckv, unrope(dkr), jnp.zeros((tm, D - MLA_COLS), F32)], axis=-1).astype(BF16)

    hm = lambda d: pl.BlockSpec((NH, tm, d), lambda i: (0, i, 0))
    return pl.pallas_call(
        body, name="mla_bwd", grid=(s // tm,),
        in_specs=[hm(DQK), hm(DQK), hm(DV), pl.BlockSpec((tm, D), lambda i: (i, 3)), pl.BlockSpec((tm, 1), lambda i: (i, 0)),
                  _const((1, HALF)), _const((1, QL)), _const((1, KVL)), _const((QL, NH * DQK)), _const((KVL, NH * 256))],
        out_specs=[pl.BlockSpec((tm, D), lambda i: (i, 0)), _const((QL, NH * DQK)), _const((KVL, NH * 256)), _const((8, QL))],
        out_shape=[jax.ShapeDtypeStruct((s, D), BF16), jax.ShapeDtypeStruct((QL, NH * DQK), F32),
                   jax.ShapeDtypeStruct((KVL, NH * 256), F32), jax.ShapeDtypeStruct((8, QL), F32)],
        compiler_params=_cparams(1),
    )(dq, dk, dv, proj, pos, inv_freq, qnw, kvnw, wuq, wukv)


def _conv_rows_bwd(dya, u1, proj, ln_w, ln_b):
    s = dya.shape[0]
    tm = min(512, s)

    def body(dya_ref, u1_ref, gate_ref, lw_ref, lb_ref, du1_ref, dag_ref, small_ref):
        @pl.when(pl.program_id(0) == 0)
        def _():
            small_ref[...] = jnp.zeros(small_ref.shape, F32)

        u1 = u1_ref[...]
        mu = jnp.mean(u1, axis=-1, keepdims=True)
        xc = u1 - mu
        rstd = lax.rsqrt(jnp.mean(xc * xc, axis=-1, keepdims=True) + EPS)
        xhat = xc * rstd
        lw = lw_ref[...]
        ln = xhat * lw + lb_ref[...]
        sl = _sig(ln)
        u2 = ln * sl
        gate = gate_ref[...]
        sg = _sig(gate)
        dya = dya_ref[...]
        dag_ref[...] = (dya * u2 * (sg * (1.0 + gate * (1.0 - sg)))).astype(BF16)
        dln = dya * (gate * sg) * (sl * (1.0 + ln * (1.0 - sl)))
        small_ref[0:1, :] += jnp.sum(dln * xhat, axis=0, keepdims=True)
        small_ref[1:2, :] += jnp.sum(dln, axis=0, keepdims=True)
        dxh = dln * lw
        du1_ref[...] = rstd * (dxh - jnp.mean(dxh, axis=-1, keepdims=True) - xhat * jnp.mean(dxh * xhat, axis=-1, keepdims=True))

    row = pl.BlockSpec((tm, D), lambda i: (i, 0))
    return pl.pallas_call(
        body, name="conv_rows_bwd", grid=(s // tm,),
        in_specs=[row, row, pl.BlockSpec((tm, D), lambda i: (i, 2)), _const((1, D)), _const((1, D))],
        out_specs=[row, row, _const((8, D))],
        out_shape=[jax.ShapeDtypeStruct((s, D), F32), jax.ShapeDtypeStruct((s, D), BF16), jax.ShapeDtypeStruct((8, D), F32)],
        compiler_params=_cparams(1),
    )(dya, u1, proj, ln_w, ln_b)


def _conv_bwd(du1, proj, conv_w):
    s = du1.shape[0]
    tm = min(256, s)
    hb = tm // HALO
    n = s // tm
    last32 = s // HALO - 1

    def body(d_ref, dn_ref, av_ref, ag_ref, avh_ref, agh_ref, cw_ref, dp_ref, gcw_ref, small_ref,
             dwin_ref, dpad_ref, dsh_ref, uwin_ref, acc_ref):
        i = pl.program_id(0)

        @pl.when(i == 0)
        def _():
            acc_ref[...] = jnp.zeros(acc_ref.shape, F32)
            small_ref[...] = jnp.zeros(small_ref.shape, F32)
            dpad_ref[...] = jnp.zeros(dpad_ref.shape, F32)
            uwin_ref[...] = jnp.zeros(uwin_ref.shape, F32)

        dv = d_ref[...]
        dwin_ref[0:tm, :] = dv
        dwin_ref[tm:tm + HALO, :] = jnp.where(i < n - 1, dn_ref[...], 0.0)
        dpad_ref[SUBLANES:SUBLANES + tm, :] = dv
        halo = avh_ref[...] * _sig(agh_ref[...])
        uwin_ref[0:HALO, :] = jnp.where(i > 0, halo, 0.0)
        av = av_ref[...]
        sg = _sig(ag_ref[...])
        uwin_ref[HALO:HALO + tm, :] = av * sg
        small_ref[0:1, :] += jnp.sum(dv, axis=0, keepdims=True)

        for b in range(SUBLANES):
            dsh_ref[b] = dpad_ref[pl.ds(SUBLANES - b, tm + SUBLANES), :]

        first = HALO - (KCONV - 1)
        for c0 in range(0, D, CONV_LC):
            lanes = pl.ds(c0, CONV_LC)
            for r0 in range(0, tm, CONV_RC):
                acc = _shifted_taps(dwin_ref, lambda o: cw_ref[KCONV - 1 - o:KCONV - o, c0:c0 + CONV_LC], range(KCONV), r0, lanes)
                a = av[r0:r0 + CONV_RC, c0:c0 + CONV_LC]
                g = sg[r0:r0 + CONV_RC, c0:c0 + CONV_LC]
                dp_ref[r0:r0 + CONV_RC, c0:c0 + CONV_LC] = (acc * g).astype(BF16)
                dp_ref[r0:r0 + CONV_RC, D + c0:D + c0 + CONV_LC] = (acc * a * (g * (1.0 - g))).astype(BF16)
            for b in range(SUBLANES):
                group = [o for o in range(first, first + KCONV) if o % SUBLANES == b]
                parts = [jnp.zeros((SUBLANES, CONV_LC), F32) for _ in group]
                chunks = [(i0, GW_RC) for i0 in range(0, tm, GW_RC)] + ([(tm, SUBLANES)] if b else [])
                for i0, rows in chunks:
                    dsh = dsh_ref[b, pl.ds(i0, rows), lanes]
                    for n_, o in enumerate(group):
                        prod = dsh * uwin_ref[pl.ds(i0 + o - b, rows), lanes]
                        parts[n_] = parts[n_] + jnp.sum(prod.reshape(rows // SUBLANES, SUBLANES, CONV_LC), axis=0)
                for n_, o in enumerate(group):
                    acc_ref[o - first, :, c0:c0 + CONV_LC] += parts[n_]

        @pl.when(i == n - 1)
        def _():
            gcw_ref[...] = jnp.sum(acc_ref[...], axis=1)

    row = lambda k: pl.BlockSpec((tm, D), lambda i: (i, k))
    prev = lambda k: pl.BlockSpec((HALO, D), lambda i: (jnp.maximum(i * hb - 1, 0), k))
    return pl.pallas_call(
        body, name="conv_bwd", grid=(n,),
        in_specs=[row(0), pl.BlockSpec((HALO, D), lambda i: (jnp.minimum((i + 1) * hb, last32), 0)),
                  row(0), row(1), prev(0), prev(1), _const((KPAD, D))],
        out_specs=[pl.BlockSpec((tm, 2 * D), lambda i: (i, 0)), _const((KPAD, D)), _const((8, D))],
        out_shape=[jax.ShapeDtypeStruct((s, 2 * D), BF16), jax.ShapeDtypeStruct((KPAD, D), F32), jax.ShapeDtypeStruct((8, D), F32)],
        scratch_shapes=[pltpu.VMEM((tm + HALO, D), F32), pltpu.VMEM((tm + 2 * SUBLANES, D), F32),
                        pltpu.VMEM((SUBLANES, tm + SUBLANES, D), F32),
                        pltpu.VMEM((tm + HALO + SUBLANES, D), F32), pltpu.VMEM((KPAD, SUBLANES, D), F32)],
        compiler_params=_cparams(1),
    )(du1, du1, proj, proj, proj, proj, conv_w)


def _dproj_specs(tm, rows_first):
    def spec(lo, hi):
        def idx(a, b):
            i, k = (a, b) if rows_first else (b, a)
            col = jnp.clip(k - lo, 0, hi - lo - 1)
            if rows_first:
                return (i, col)
            return (jnp.where((k >= lo) & (k < hi), i, 0), col)
        return pl.BlockSpec((tm, D), idx)
    return [spec(0, 2), spec(2, 3), spec(3, 4), spec(4, 7)]


def _pick_dproj(k, refs, fn):
    vg, ag, mla, gates = refs

    @pl.when(k < 2)
    def _():
        fn(vg)

    @pl.when(k == 2)
    def _():
        fn(ag)

    @pl.when(k == 3)
    def _():
        fn(mla)

    @pl.when(k > 3)
    def _():
        fn(gates)


def _in_proj_bwd_x(dps, wt, x, dx2, norm_w, mod8):
    s = x.shape[0]
    tm = min(512, s)
    nk = wt.shape[0] // D

    def body(vg_ref, ag_ref, mla_ref, g_ref, w_ref, x_ref, dx2_ref, nw_ref, mod_ref, gx_ref, small_ref, acc_ref):
        i, k = pl.program_id(0), pl.program_id(1)

        @pl.when((i == 0) & (k == 0))
        def _():
            small_ref[...] = jnp.zeros(small_ref.shape, F32)

        @pl.when(k == 0)
        def _():
            acc_ref[...] = jnp.zeros(acc_ref.shape, F32)

        def add(ref):
            acc_ref[...] += _nn(ref[...], w_ref[...])

        _pick_dproj(k, (vg_ref, ag_ref, mla_ref, g_ref), add)

        @pl.when(k == nk - 1)
        def _():
            dh = acc_ref[...]
            xn, rstd = _rms_parts(x_ref[...])
            nw = nw_ref[...]
            hn = xn * nw
            small_ref[0:1, :] += jnp.sum(dh, axis=0, keepdims=True)
            small_ref[1:2, :] += jnp.sum(dh * hn, axis=0, keepdims=True)
            dhn = dh * (1.0 + mod_ref[1:2, :])
            small_ref[2:3, :] += jnp.sum(dhn * xn, axis=0, keepdims=True)
            dxn = dhn * nw
            gx_ref[...] = rstd * (dxn - xn * jnp.mean(dxn * xn, axis=-1, keepdims=True)) + dx2_ref[...]

    row = pl.BlockSpec((tm, D), lambda i, k: (i, 0))
    return pl.pallas_call(
        body, name="in_proj_bwd_x", grid=(s // tm, nk),
        in_specs=_dproj_specs(tm, True) + [pl.BlockSpec((D, D), lambda i, k: (k, 0)), row, row, _const((1, D)), _const((8, D))],
        out_specs=[row, _const((8, D))],
        out_shape=[jax.ShapeDtypeStruct((s, D), F32), jax.ShapeDtypeStruct((8, D), F32)],
        scratch_shapes=[pltpu.VMEM((tm, D), F32)],
        compiler_params=_cparams(2),
    )(*dps, wt, x, dx2, norm_w, mod8)


def _in_proj_bwd_w(dps, h, nk):
    s = h.shape[0]
    tm = min(1024, s)

    n = s // tm

    def body(vg_ref, ag_ref, mla_ref, g_ref, h_ref, gw_ref, acc_ref):
        k, i = pl.program_id(0), pl.program_id(1)

        @pl.when(i == 0)
        def _():
            acc_ref[...] = jnp.zeros(acc_ref.shape, F32)

        def add(ref):
            acc_ref[...] += _tn(ref[...], h_ref[...])

        _pick_dproj(k, (vg_ref, ag_ref, mla_ref, g_ref), add)

        @pl.when(i == n - 1)
        def _():
            gw_ref[...] = acc_ref[...].astype(gw_ref.dtype)

    return pl.pallas_call(
        body, name="in_proj_bwd_w", grid=(nk, n),
        in_specs=_dproj_specs(tm, False) + [pl.BlockSpec((tm, D), lambda k, i: (i, 0))],
        out_specs=pl.BlockSpec((D, D), lambda k, i: (k, 0)),
        out_shape=jax.ShapeDtypeStruct((nk * D, D), BF16),
        scratch_shapes=[pltpu.VMEM((D, D), F32)],
        compiler_params=_cparams(2),
    )(*dps, h)


def _small_slab(wuq, wukv, conv_w):
    cw = jnp.pad(conv_w.reshape(-1), (0, (ROWS_TAIL - ROWS_UQ - ROWS_UKV) * D - KCONV * 128)).reshape(-1, D)
    return jnp.concatenate([wuq.reshape(ROWS_UQ, D), wukv.reshape(ROWS_UKV, D), cw], axis=0)


def _split_small_slab(slab):
    return (slab[..., :ROWS_UQ, :], slab[..., ROWS_UQ:ROWS_UQ + ROWS_UKV, :],
            slab[..., ROWS_UQ + ROWS_UKV:ROWS_UQ + ROWS_UKV + ROWS_CW, :])


def _unpack_small_slab(slab):
    wuq, wukv, cw = _split_small_slab(slab)
    return (wuq.reshape(QL, NH * DQK // N_DEV), wukv.reshape(KVL, NH * 256 // N_DEV),
            cw.reshape(-1)[:KCONV * 128].reshape(KCONV, 128))


def _pack_shard(w_in, wco, wao, wout, wuq, wukv, conv_w):
    bf = lambda a: a.astype(BF16)
    return jnp.concatenate([jnp.pad(bf(w_in).T, ((0, OFF_SQ - ROWS_IN), (0, 0))), bf(wco), bf(wao), bf(wout),
                            bf(_small_slab(wuq, wukv, conv_w))], axis=0)


def _unpack_gathered(g):
    wt = g[:, :ROWS_IN].reshape(IN_COLS, D)
    split = 3 * D + MLA_COLS
    wt = jnp.concatenate([wt[:split], jnp.zeros((D - MLA_COLS, D), g.dtype), wt[split:]], axis=0)
    wuq, wukv, cw = _split_small_slab(g[:, OFF_TAIL:])
    wuq = wuq.reshape(N_DEV, QL, NH * DQK // N_DEV).transpose(1, 0, 2).reshape(QL, NH * DQK)
    wukv = wukv.reshape(N_DEV, KVL, NH * 256 // N_DEV).transpose(1, 0, 2).reshape(KVL, NH * 256)
    cw = cw.reshape(N_DEV, ROWS_CW * D)[:, :KCONV * 128].reshape(N_DEV, KCONV, 128).transpose(1, 0, 2).reshape(KCONV, D)
    return wt, wuq, wukv, cw


def _pack_grads(gwt, gw3, guq, gukv, gcw):
    split = 3 * D + MLA_COLS
    g_in = jnp.concatenate([gwt[:split], gwt[4 * D:]], axis=0).reshape(N_DEV, ROWS_IN, D)
    g_in = jnp.pad(g_in, ((0, 0), (0, OFF_SQ - ROWS_IN), (0, 0)))
    guq = guq.reshape(QL, N_DEV, -1).transpose(1, 0, 2).reshape(N_DEV, ROWS_UQ, D)
    gukv = gukv.reshape(KVL, N_DEV, -1).transpose(1, 0, 2).reshape(N_DEV, ROWS_UKV, D)
    gcw = gcw.reshape(KCONV, N_DEV, 128).transpose(1, 0, 2).reshape(N_DEV, KCONV * 128)
    gcw = jnp.pad(gcw, ((0, 0), (0, (ROWS_TAIL - ROWS_UQ - ROWS_UKV) * D - KCONV * 128))).reshape(N_DEV, -1, D)
    gsq = gw3.reshape(3, N_DEV, ROWS_SQ, D).transpose(1, 0, 2, 3).reshape(N_DEV, 3 * ROWS_SQ, D)
    slabs = jnp.concatenate([a.astype(BF16) for a in (g_in, gsq, guq, gukv, gcw)], axis=1)
    return slabs.reshape(N_CHIP, 2, ROWS_PACK, D).transpose(1, 0, 2, 3)


def _pack_small(vecs):
    flat = jnp.concatenate([v.reshape(-1) for v in vecs])
    return jnp.pad(flat, (0, SMALL_LEN - flat.shape[0])).reshape(8, SMALL_COLS)


def _unpack_small(a, shapes):
    flat = a.reshape(-1)
    out, off = [], 0
    for shp, n in zip(shapes, SMALL_SIZES):
        out.append(flat[off:off + n].reshape(shp))
        off += n
    return out


def kernel(x, c, positions, w_ada, b_ada, norm_w, w_in, conv_w, conv_b, conv_ln_w, conv_ln_b, w_conv_out, q_norm_w, w_uq, kv_norm_w, w_ukv, w_attn_out, w_out, final_norm_w, loss_target, m_w_ada, m_b_ada, m_norm_w, m_w_in, m_conv_w, m_conv_b, m_conv_ln_w, m_conv_ln_b, m_w_conv_out, m_q_norm_w, m_w_uq, m_kv_norm_w, m_w_ukv, m_w_attn_out, m_w_out, m_final_norm_w, v_w_ada, v_b_ada, v_norm_w, v_w_in, v_conv_w, v_conv_b, v_conv_ln_w, v_conv_ln_b, v_w_conv_out, v_q_norm_w, v_w_uq, v_kv_norm_w, v_w_ukv, v_w_attn_out, v_w_out, v_final_norm_w):
    me = 4 * lax.axis_index("x") + 2 * lax.axis_index("y") + lax.axis_index("c")
    xs, tgt = x[0], loss_target[0]
    s = xs.shape[0]
    ada_cols = w_ada.shape[2]

    sharded = lambda t: tuple(a[0] for a in t)
    gathered = _all_gather(_pack_shard(*sharded((w_in, w_conv_out, w_attn_out, w_out, w_uq, w_ukv, conv_w))), "gather_weights")
    wt, wuq, wukv, cw = _unpack_gathered(gathered)
    cw32 = jnp.pad(cw.astype(F32), ((0, KPAD - KCONV), (0, 0)))

    c_all = _all_gather(jnp.broadcast_to(c, (8, D)), "gather_c")[:, 0, :]
    b_cols = lax.dynamic_slice(b_ada, (0, me * ada_cols), (1, ada_cols))
    mod_cols = _all_gather(_ada_mod(c_all, w_ada[0], b_cols), "gather_mod")
    mod = lax.dynamic_index_in_dim(mod_cols, me, axis=1, keepdims=False).reshape(3, D)
    mod8 = jnp.pad(mod, ((0, 5), (0, 0)))

    pos = positions.reshape(s, 1)
    inv_freq = (ROPE_THETA ** (-jnp.arange(0, ROPE, 2, dtype=F32) / ROPE)).reshape(1, HALF)
    proj, h = _in_proj(xs, norm_w, mod8, wt)
    u1, ya = _conv_fwd(proj, cw32, conv_b, conv_ln_w, conv_ln_b)
    q, k, v = _mla_prep(proj, pos, inv_freq, q_norm_w, kv_norm_w, wuq, wukv)
    o, lse = _attn_fwd(q, k, v)

    dya, do, delta, dp_gates, dx2, gw3, small_a = _merge_loss(xs, tgt, ya, o, proj, mod8, final_norm_w.reshape(1, D), gathered)
    dq, dk, dv = _attn_bwd(q, k, v, do, lse, delta)
    dp_mla, guq, gukv, small_b = _mla_bwd(dq, dk, dv, proj, pos, inv_freq, q_norm_w, kv_norm_w, wuq, wukv)
    du1, dp_ag, small_c = _conv_rows_bwd(dya, u1, proj, conv_ln_w, conv_ln_b)
    dp_vg, gcw, small_d = _conv_bwd(du1, proj, cw32)
    dps = [dp_vg, dp_ag, dp_mla, dp_gates]
    gwt = _in_proj_bwd_w(dps, h, wt.shape[0] // D)

    packed = _pack_grads(gwt, gw3, guq, gukv, gcw[:KCONV])
    core = lax.axis_index("c").astype(jnp.int32).reshape(1)
    half = _pair_add(core, packed, _pair_exchange(packed), ROWS_PACK // 4)
    send_sems, recv_sems, half_thru, land_thru, token = _chip_exchange_start(half)
    grad_x, small_e = _in_proj_bwd_x(dps, wt, xs, dx2, norm_w + token[0:1, 0:1], mod8)
    half, recv = _chip_exchange_wait(send_sems, recv_sems, half_thru, land_thru, small_e)
    chip = 2 * lax.axis_index("x") + lax.axis_index("y")
    recv = lax.dynamic_update_slice(recv, lax.dynamic_slice(half, (chip, 0, 0), (1,) + half.shape[1:]), (chip, 0, 0))
    big_in = _adam_w_in(recv, w_in[0], m_w_in[0], v_w_in[0])
    squares = (("w_conv_out", w_conv_out, m_w_conv_out, v_w_conv_out), ("w_attn_out", w_attn_out, m_w_attn_out, v_w_attn_out),
               ("w_out", w_out, m_w_out, v_w_out))
    big_sq = [_reduce_adam(recv, w[0], m[0], v[0], "adam_" + nm, ROWS_SQ, OFF_SQ // ROWS_SQ + j) for j, (nm, w, m, v) in enumerate(squares)]
    tail = _reduce_adam(recv, _small_slab(w_uq[0], w_ukv[0], conv_w[0]), _small_slab(m_w_uq[0], m_w_ukv[0], m_conv_w[0]),
                        _small_slab(v_w_uq[0], v_w_ukv[0], v_conv_w[0]), "adam_small_sharded", ROWS_TAIL, OFF_TAIL // ROWS_TAIL)
    tail = [_unpack_small_slab(a) for a in tail]
    big = [(big_in[i], big_sq[0][i], big_sq[1][i], big_sq[2][i], *tail[i]) for i in range(4)]

    dmod = jnp.concatenate([small_e[0], small_e[1], small_a[1]])
    payload = _pack_small([dmod, small_e[2], small_d[0], small_c[0], small_c[1], small_a[0], small_b[0], small_b[1], small_a[2, 0:1]])
    pay_all = _all_gather(payload, "gather_small")
    small_w = (b_ada, norm_w, conv_b, conv_ln_w, conv_ln_b, final_norm_w, q_norm_w, kv_norm_w)
    small_m = (m_b_ada, m_norm_w, m_conv_b, m_conv_ln_w, m_conv_ln_b, m_final_norm_w, m_q_norm_w, m_kv_norm_w)
    small_v = (v_b_ada, v_norm_w, v_conv_b, v_conv_ln_w, v_conv_ln_b, v_final_norm_w, v_q_norm_w, v_kv_norm_w)
    sm = _reduce_adam(pay_all, _pack_small(small_w), _pack_small(small_m), _pack_small(small_v), "adam_replicated", 8, 0)
    loss = sm[0].reshape(-1)[sum(SMALL_SIZES)]
    shapes = [t.shape for t in small_w]
    sm = [_unpack_small(a, shapes) for a in sm]

    dmod_all = pay_all.reshape(N_DEV, SMALL_LEN)[:, :3 * D]
    dmod_cols = lax.dynamic_slice(dmod_all, (0, me * ada_cols), (N_DEV, ada_cols))
    ada = _ada_bwd(c_all, dmod_cols, w_ada[0], m_w_ada[0], v_w_ada[0])

    def group(i):
        b_in, b_co, b_ao, b_out, b_uq, b_ukv, b_cw = big[i]
        s_bada, s_nw, s_cb, s_clw, s_clb, s_fnw, s_qnw, s_kvnw = sm[i]
        return (ada[i][None], s_bada, s_nw, b_in[None], b_cw[None], s_cb, s_clw, s_clb, b_co[None], s_qnw, b_uq[None], s_kvnw,
                b_ukv[None], b_ao[None], b_out[None], s_fnw)

    return (loss, grad_x[None], *group(0), *group(1), *group(2), *group(3))
```

```python
import functools

import jax
import jax.numpy as jnp
from jax import lax
from jax.experimental import pallas as pl
from jax.experimental.pallas import tpu as pltpu

F32 = jnp.float32
BF16 = jnp.bfloat16

D = 1024
NH = 8
NOPE = 128
ROPE = 64
HALF = ROPE // 2
DQK = NOPE + ROPE
DV = 128
QL = 256
KVL = 256
KCONV = 31
KPAD = 32
HALO = 32
IN_COLS = 6720
MLA_COLS = QL + KVL + ROPE
PROJ_COLS = 7 * D
EPS = 1e-6
ROPE_THETA = 10000.0
N_DEV = 8

ADAM_LR = 0.001
ADAM_B1 = 0.9
ADAM_B2 = 0.999
ADAM_EPS = 1e-08
ADAM_WD = 0.01
ADAM_STEP = 10

ROWS_IN = 840
ROWS_SQ = 128
ROWS_UQ = 48
ROWS_UKV = 64
ROWS_CW = 4
ROWS_IN_PAD = 896
OFF_TAIL = ROWS_IN_PAD
ROWS_TAIL = 128
OFF_SQ = OFF_TAIL + ROWS_TAIL
ROWS_PACK = OFF_SQ + 3 * ROWS_SQ
SMALL_SIZES = (3 * D, D, D, D, D, D, QL, KVL)
SMALL_COLS = 1152
SMALL_LEN = 8 * SMALL_COLS

MESH = pl.DeviceIdType.MESH
ANY = pl.BlockSpec(memory_space=pl.ANY)
V7X_VMEM_LIMIT = 56 * 1024 * 1024


def _cparams(n_axes, vmem=V7X_VMEM_LIMIT):
    return pltpu.CompilerParams(dimension_semantics=("arbitrary",) * n_axes, vmem_limit_bytes=vmem)


def _sig(x):
    return jax.nn.sigmoid(x)


def _nt(a, b):
    return lax.dot_general(a, b, (((1,), (1,)), ((), ())), preferred_element_type=F32)


def _tn(a, b):
    return lax.dot_general(a, b, (((0,), (0,)), ((), ())), preferred_element_type=F32)


def _nn(a, b):
    return jnp.dot(a, b, preferred_element_type=F32)


def _const(shape):
    return pl.BlockSpec(shape, lambda *_: (0,) * len(shape))


def _all_gather(block, name):
    r, c = block.shape

    def body(x_ref, out_ref, send_sems, recv_sems, local_sem):
        x, y, cc = lax.axis_index("x"), lax.axis_index("y"), lax.axis_index("c")
        me, sibling = (x, y, cc), (x, y, 1 - cc)
        chips = [(1 - x, y), (x, 1 - y), (1 - x, 1 - y)]

        def slot(px, py, pc):
            return out_ref.at[4 * px + 2 * py + pc]

        def copy(k, blk, to, src=None):
            return pltpu.make_async_remote_copy(
                src_ref=slot(*blk) if src is None else src, dst_ref=slot(*blk),
                send_sem=send_sems.at[k], recv_sem=recv_sems.at[k],
                device_id=to, device_id_type=MESH)

        mine = pltpu.make_async_copy(x_ref, slot(*me), local_sem)
        mine.start()
        first = [copy(0, me, sibling, src=x_ref)]
        first += [copy(1 + j, me, (*chip, cc), src=x_ref) for j, chip in enumerate(chips)]
        for cp in first:
            cp.start()
        passed = [copy(4 + j, (*chip, cc), sibling) for j, chip in enumerate(chips)]
        for j, chip in enumerate(chips):
            copy(1 + j, (*chip, cc), me).wait_recv()
            passed[j].start()
        copy(0, sibling, me).wait_recv()
        for j, chip in enumerate(chips):
            copy(4 + j, (*chip, 1 - cc), me).wait_recv()
        for cp in first + passed:
            cp.wait_send()
        mine.wait()

    return pl.pallas_call(
        body, name=name,
        out_shape=jax.ShapeDtypeStruct((N_DEV, r, c), block.dtype),
        in_specs=[ANY], out_specs=ANY,
        scratch_shapes=[pltpu.SemaphoreType.DMA((7,)), pltpu.SemaphoreType.DMA((7,)), pltpu.SemaphoreType.DMA],
    )(block)


N_CHIP = 4


def _pair_exchange(packed):
    _, _, r, c = packed.shape

    def body(src_ref, out_ref, send_sem, recv_sem):
        x, y, cc = lax.axis_index("x"), lax.axis_index("y"), lax.axis_index("c")
        cp = pltpu.make_async_remote_copy(
            src_ref=src_ref.at[1 - cc], dst_ref=out_ref, send_sem=send_sem, recv_sem=recv_sem,
            device_id=(x, y, 1 - cc), device_id_type=MESH)
        cp.start()
        cp.wait()

    return pl.pallas_call(
        body, name="pair_exchange",
        out_shape=jax.ShapeDtypeStruct((N_CHIP, r, c), packed.dtype),
        in_specs=[ANY], out_specs=ANY,
        scratch_shapes=[pltpu.SemaphoreType.DMA, pltpu.SemaphoreType.DMA],
    )(packed)


def _pair_add(core, packed, got, tr):
    _, _, r, c = packed.shape

    def body(core_ref, own_ref, got_ref, out_ref):
        out_ref[...] = (own_ref[0].astype(F32) + got_ref[...].astype(F32)).astype(out_ref.dtype)

    blk = pl.BlockSpec((1, tr, c), lambda j, i, core_ref: (j, i, 0))
    return pl.pallas_call(
        body, name="pair_add",
        grid_spec=pltpu.PrefetchScalarGridSpec(
            num_scalar_prefetch=1, grid=(N_CHIP, r // tr),
            in_specs=[pl.BlockSpec((1, 1, tr, c), lambda j, i, core_ref: (core_ref[0], j, i, 0)), blk],
            out_specs=blk),
        out_shape=jax.ShapeDtypeStruct((N_CHIP, r, c), packed.dtype),
        compiler_params=_cparams(2),
    )(core, packed, got)


HBM = pl.BlockSpec(memory_space=pltpu.HBM)
SEM = pl.BlockSpec(memory_space=pltpu.SEMAPHORE)
EFFECT = pltpu.SideEffectType.DATAFLOW_SIDE_EFFECTING


def _chip_copies(src_ref, land_ref, send_sems, recv_sems):
    x, y, cc = lax.axis_index("x"), lax.axis_index("y"), lax.axis_index("c")
    me = 2 * x + y
    copies = []
    for k in range(1, N_CHIP):
        px, py = (1 - x if k & 2 else x), (1 - y if k & 1 else y)
        copies.append(pltpu.make_async_remote_copy(
            src_ref=src_ref.at[2 * px + py], dst_ref=land_ref.at[me],
            send_sem=send_sems.at[k - 1], recv_sem=recv_sems.at[k - 1],
            device_id=(px, py, cc), device_id_type=MESH))
    return copies


def _direct_copies(src_ref, land_ref, send_sems, recv_sems):
    x, y, cc = lax.axis_index("x"), lax.axis_index("y"), lax.axis_index("c")
    me = 4 * x + 2 * y + cc
    copies = []
    for k in range(1, N_DEV):
        peer = ((1 - x if k & 4 else x), (1 - y if k & 2 else y), (1 - cc if k & 1 else cc))
        copies.append(pltpu.make_async_remote_copy(
            src_ref=src_ref, dst_ref=land_ref.at[me], send_sem=send_sems.at[k - 1], recv_sem=recv_sems.at[k - 1],
            device_id=peer, device_id_type=MESH))
    return copies


def _exchange_start(src, land_shape, copies_of, n_copies, name):
    def body(src_ref, land_ref, send_sems, recv_sems, src_thru, land_thru, token):
        for cp in copies_of(src_ref, land_ref, send_sems, recv_sems):
            cp.start()
        token[...] = jnp.zeros_like(token)

    return pl.pallas_call(
        body, name=name,
        out_shape=(pltpu.SemaphoreType.DMA((n_copies,)), pltpu.SemaphoreType.DMA((n_copies,)),
                   pltpu.HBM(src.shape, src.dtype), pltpu.HBM(land_shape, src.dtype), jax.ShapeDtypeStruct((8, 128), F32)),
        in_specs=(HBM, HBM), out_specs=(SEM, SEM, HBM, HBM, pl.BlockSpec(memory_space=pltpu.VMEM)),
        input_output_aliases={0: 2, 1: 3},
        compiler_params=pltpu.CompilerParams(has_side_effects=EFFECT),
    )(pltpu.with_memory_space_constraint(src, pltpu.HBM),
      pltpu.with_memory_space_constraint(lax.empty(land_shape, src.dtype), pltpu.HBM))


def _exchange_wait(send_sems, recv_sems, src_thru, land_thru, after, copies_of, name):
    def body(src_ref, land_ref, send_sems, recv_sems, after_ref, src_dead, got_ref):
        copies = copies_of(src_ref, land_ref, send_sems, recv_sems)
        for cp in copies:
            cp.wait_send()
        for cp in copies:
            cp.wait_recv()

    return pl.pallas_call(
        body, name=name,
        out_shape=(pltpu.HBM(src_thru.shape, src_thru.dtype), pltpu.HBM(land_thru.shape, land_thru.dtype)),
        in_specs=(HBM, HBM, SEM, SEM, ANY), out_specs=(HBM, HBM), input_output_aliases={0: 0, 1: 1},
        compiler_params=pltpu.CompilerParams(has_side_effects=EFFECT),
    )(src_thru, land_thru, send_sems, recv_sems, after)


def _adam(g, w, m, v):
    m = ADAM_B1 * m + (1.0 - ADAM_B1) * g
    v = ADAM_B2 * v + (1.0 - ADAM_B2) * (g * g)
    m_hat = m / (1.0 - ADAM_B1 ** ADAM_STEP)
    v_hat = v / (1.0 - ADAM_B2 ** ADAM_STEP)
    delta = -ADAM_LR * (m_hat / (jnp.sqrt(v_hat) + ADAM_EPS) + ADAM_WD * w)
    return delta, m, v


def _adam_w_in(parts, w, m, v):
    n = parts.shape[0]
    tc = 128

    def body(p_ref, w_ref, m_ref, v_ref, g_out, d_out, m_out, v_out):
        gt = p_ref[0].astype(F32)
        for j in range(1, n):
            gt = gt + p_ref[j].astype(F32)
        g = gt.T[:, :ROWS_IN]
        delta, nm, nv = _adam(g, w_ref[...], m_ref[...], v_ref[...])
        g_out[...] = g
        d_out[...] = delta
        m_out[...] = nm
        v_out[...] = nv

    row = pl.BlockSpec((tc, ROWS_IN), lambda i: (i, 0))
    return pl.pallas_call(
        body, name="adam_w_in", grid=(D // tc,),
        in_specs=[pl.BlockSpec((n, ROWS_IN_PAD, tc), lambda i: (0, 0, i)), row, row, row],
        out_specs=[row] * 4,
        out_shape=[jax.ShapeDtypeStruct((D, ROWS_IN), F32)] * 4,
        compiler_params=_cparams(1),
    )(parts, w, m, v)


def _reduce_adam(parts, w, m, v, name, tr, first_block):
    n, _, c = parts.shape
    r = w.shape[0]

    def body(p_ref, w_ref, m_ref, v_ref, g_out, d_out, m_out, v_out):
        g = p_ref[0].astype(F32)
        for j in range(1, n):
            g = g + p_ref[j].astype(F32)
        delta, nm, nv = _adam(g, w_ref[...], m_ref[...], v_ref[...])
        g_out[...] = g
        d_out[...] = delta
        m_out[...] = nm
        v_out[...] = nv

    row = pl.BlockSpec((tr, c), lambda i: (i, 0))
    return pl.pallas_call(
        body, name=name, grid=(r // tr,),
        in_specs=[pl.BlockSpec((n, tr, c), lambda i: (0, first_block + i, 0)), row, row, row],
        out_specs=[row] * 4,
        out_shape=[jax.ShapeDtypeStruct((r, c), F32)] * 4,
        compiler_params=_cparams(1),
    )(parts, w, m, v)


def _ada_mod(c_all, w_ada, b_cols):
    def body(c_ref, w_ref, b_ref, o_ref):
        cv = c_ref[...]
        act = (cv * _sig(cv)).astype(BF16)
        o_ref[...] = _nn(act, w_ref[...].astype(BF16)) + b_ref[...]

    return pl.pallas_call(body, name="ada_mod", out_shape=jax.ShapeDtypeStruct((N_DEV, w_ada.shape[1]), F32))(c_all, w_ada, b_cols)


def _ada_bwd(c_all, dmod_cols, w, m, v):
    def body(c_ref, d_ref, w_ref, m_ref, v_ref, g_out, d_out, m_out, v_out):
        cv = c_ref[...]
        act = (cv * _sig(cv)).astype(BF16)
        g = _tn(act, d_ref[...].astype(BF16))
        delta, nm, nv = _adam(g, w_ref[...], m_ref[...], v_ref[...])
        g_out[...] = g
        d_out[...] = delta
        m_out[...] = nm
        v_out[...] = nv

    return pl.pallas_call(body, name="ada_bwd", out_shape=[jax.ShapeDtypeStruct(w.shape, F32)] * 4)(c_all, dmod_cols, w, m, v)


def _in_proj(x, norm_w, mod8, wt):
    s = x.shape[0]
    tm = min(1024, s)
    nk = wt.shape[0] // D

    def body(x_ref, nw_ref, mod_ref, w_ref, proj_ref, h_ref, hs_ref):
        @pl.when(pl.program_id(1) == 0)
        def _():
            xv = x_ref[...]
            rstd = lax.rsqrt(jnp.mean(xv * xv, axis=-1, keepdims=True) + EPS)
            h = (xv * rstd) * nw_ref[...] * (1.0 + mod_ref[1:2, :]) + mod_ref[0:1, :]
            hs_ref[...] = h.astype(BF16)
            h_ref[...] = hs_ref[...]

        proj_ref[...] = _nt(hs_ref[...], w_ref[...])

    return pl.pallas_call(
        body, name="in_proj", grid=(s // tm, nk),
        in_specs=[pl.BlockSpec((tm, D), lambda i, k: (i, 0)), _const((1, D)), _const((8, D)),
                  pl.BlockSpec((D, D), lambda i, k: (k, 0))],
        out_specs=[pl.BlockSpec((tm, D), lambda i, k: (i, k)), pl.BlockSpec((tm, D), lambda i, k: (i, 0))],
        out_shape=[jax.ShapeDtypeStruct((s, nk * D), F32), jax.ShapeDtypeStruct((s, D), BF16)],
        scratch_shapes=[pltpu.VMEM((tm, D), BF16)],
        compiler_params=_cparams(2),
    )(x, norm_w, mod8, wt)


CONV_RC = 128
CONV_LC = 128
GW_RC = 32
SUBLANES = 8


def _shifted_taps(win_ref, weight_of, offsets, r0, lanes):
    acc = jnp.zeros((CONV_RC, CONV_LC), F32)
    for b in range(SUBLANES):
        group = [o for o in offsets if o % SUBLANES == b]
        if not group:
            continue
        rows = CONV_RC if b == 0 else CONV_RC + SUBLANES
        part = jnp.zeros((rows, CONV_LC), F32)
        for o in group:
            part = part + win_ref[pl.ds(r0 + o - b, rows), lanes] * weight_of(o)
        acc = acc + (part if b == 0 else part[b:b + CONV_RC])
    return acc


def _conv_fwd(proj, conv_w, conv_b, ln_w, ln_b):
    s = proj.shape[0]
    tm = min(256, s)
    hb = tm // HALO

    def body(av_ref, ag_ref, avh_ref, agh_ref, gate_ref, cw_ref, cb_ref, lw_ref, lb_ref, u1_ref, ya_ref, win_ref):
        i = pl.program_id(0)
        halo = avh_ref[...] * _sig(agh_ref[...])
        win_ref[0:HALO, :] = jnp.where(i > 0, halo, 0.0)
        win_ref[HALO:HALO + tm, :] = av_ref[...] * _sig(ag_ref[...])
        first = HALO - (KCONV - 1)
        for r0 in range(0, tm, CONV_RC):
            for c0 in range(0, D, CONV_LC):
                acc = _shifted_taps(win_ref, lambda o: cw_ref[o - first:o - first + 1, c0:c0 + CONV_LC],
                                    range(first, first + KCONV), r0, pl.ds(c0, CONV_LC))
                u1_ref[r0:r0 + CONV_RC, c0:c0 + CONV_LC] = acc + cb_ref[:, c0:c0 + CONV_LC]
        u1 = u1_ref[...]
        mu = jnp.mean(u1, axis=-1, keepdims=True)
        xc = u1 - mu
        var = jnp.mean(xc * xc, axis=-1, keepdims=True)
        ln = xc * lax.rsqrt(var + EPS) * lw_ref[...] + lb_ref[...]
        gate = gate_ref[...]
        ya_ref[...] = ((ln * _sig(ln)) * (gate * _sig(gate))).astype(BF16)

    row = lambda k: pl.BlockSpec((tm, D), lambda i: (i, k))
    prev = lambda k: pl.BlockSpec((HALO, D), lambda i: (jnp.maximum(i * hb - 1, 0), k))
    return pl.pallas_call(
        body, name="conv_fwd", grid=(s // tm,),
        in_specs=[row(0), row(1), prev(0), prev(1), row(2), _const((KPAD, D)), _const((1, D)), _const((1, D)), _const((1, D))],
        out_specs=[pl.BlockSpec((tm, D), lambda i: (i, 0))] * 2,
        out_shape=[jax.ShapeDtypeStruct((s, D), F32), jax.ShapeDtypeStruct((s, D), BF16)],
        scratch_shapes=[pltpu.VMEM((tm + HALO, D), F32)],
        compiler_params=_cparams(1),
    )(proj, proj, proj, proj, proj, conv_w, conv_b, ln_w, ln_b)


def _rope_tables(pos_ref, if_ref):
    ang = pos_ref[...].astype(F32) * if_ref[...]
    return jnp.cos(ang), jnp.sin(ang)


def _rms_parts(x):
    rstd = lax.rsqrt(jnp.mean(x * x, axis=-1, keepdims=True) + EPS)
    return x * rstd, rstd


def _mla_prep(proj, pos, inv_freq, qnw, kvnw, wuq, wukv):
    s = proj.shape[0]
    tm = min(512, s)

    def body(p_ref, pos_ref, if_ref, qnw_ref, kvnw_ref, wuq_ref, wukv_ref, q_ref, k_ref, v_ref):
        blk = p_ref[...]
        cos, sin = _rope_tables(pos_ref, if_ref)

        def rope(r):
            x1, x2 = r[:, :HALF], r[:, HALF:]
            return jnp.concatenate([x1 * cos - x2 * sin, x1 * sin + x2 * cos], axis=-1)

        qlat = _rms_parts(blk[:, :QL])[0] * qnw_ref[...]
        kvlat = _rms_parts(blk[:, QL:QL + KVL])[0] * kvnw_ref[...]
        q = _nn(qlat.astype(BF16), wuq_ref[...])
        kv = _nn(kvlat.astype(BF16), wukv_ref[...])
        kr = rope(blk[:, QL + KVL:MLA_COLS])
        for h in range(NH):
            qh = q[:, h * DQK:(h + 1) * DQK]
            q_ref[h] = (jnp.concatenate([qh[:, :NOPE], rope(qh[:, NOPE:])], axis=-1) * ATTN_SCALE).astype(BF16)
            k_ref[h] = jnp.concatenate([kv[:, h * 256:h * 256 + NOPE], kr], axis=-1).astype(BF16)
            v_ref[h] = kv[:, h * 256 + NOPE:(h + 1) * 256].astype(BF16)

    hm = lambda d: pl.BlockSpec((NH, tm, d), lambda i: (0, i, 0))
    return pl.pallas_call(
        body, name="mla_prep", grid=(s // tm,),
        in_specs=[pl.BlockSpec((tm, D), lambda i: (i, 3)), pl.BlockSpec((tm, 1), lambda i: (i, 0)), _const((1, HALF)),
                  _const((1, QL)), _const((1, KVL)), _const((QL, NH * DQK)), _const((KVL, NH * 256))],
        out_specs=[hm(DQK), hm(DQK), hm(DV)],
        out_shape=[jax.ShapeDtypeStruct((NH, s, DQK), BF16), jax.ShapeDtypeStruct((NH, s, DQK), BF16),
                   jax.ShapeDtypeStruct((NH, s, DV), BF16)],
        compiler_params=_cparams(1),
    )(proj, pos, inv_freq, qnw, kvnw, wuq, wukv)


ATTN_SCALE = DQK ** -0.5


def _causal_mask(s, t):
    rows = lax.broadcasted_iota(jnp.int32, (t, t), 0)
    cols = lax.broadcasted_iota(jnp.int32, (t, t), 1)
    return jnp.where(cols <= rows, s, -jnp.inf)


def _attn_tile(s):
    return min(1024, s // 2)


def _attn_fwd(q, k, v):
    nh, s, _ = q.shape
    t = _attn_tile(s)

    def body(q_ref, k_ref, v_ref, o_ref, lse_ref):
        qi = pl.program_id(1)
        qv = q_ref[0]

        def chunk(c, carry, diag):
            m, l, acc = carry
            rows = pl.ds(pl.multiple_of(c * t, t), t)
            sc = _nt(qv, k_ref[0, rows, :])
            if diag:
                sc = _causal_mask(sc, t)
            m_new = jnp.maximum(m, jnp.max(sc, axis=-1, keepdims=True))
            alpha = jnp.exp(m - m_new)
            p = jnp.exp(sc - m_new)
            l = alpha * l + jnp.sum(p, axis=-1, keepdims=True)
            acc = alpha * acc + _nn(p.astype(BF16), v_ref[0, rows, :])
            return m_new, l, acc

        init = (jnp.full((t, 1), -jnp.inf, F32), jnp.zeros((t, 1), F32), jnp.zeros((t, DV), F32))
        carry = lax.fori_loop(0, qi, lambda c, cr: chunk(c, cr, False), init)
        m, l, acc = chunk(qi, carry, True)
        o_ref[...] = acc / l
        lse_ref[0] = jnp.broadcast_to(m + jnp.log(l), (t, DV))

    head = lambda d: pl.BlockSpec((1, s, d), lambda h, i: (h, 0, 0))
    return pl.pallas_call(
        body, name="attn_fwd", grid=(nh, s // t),
        in_specs=[pl.BlockSpec((1, t, DQK), lambda h, i: (h, i, 0)), head(DQK), head(DV)],
        out_specs=[pl.BlockSpec((t, DV), lambda h, i: (i, h)), pl.BlockSpec((1, t, DV), lambda h, i: (h, i, 0))],
        out_shape=[jax.ShapeDtypeStruct((s, nh * DV), F32), jax.ShapeDtypeStruct((nh, s, DV), F32)],
        compiler_params=_cparams(2),
    )(q, k, v)


def _merge_loss(x, target, ya, o, proj, mod8, fnw, gathered):
    s = x.shape[0]
    tm = min(256, s)
    n = s // tm

    def body(x_ref, t_ref, ya_ref, o_ref, bg_ref, ga_ref, gb_ref, mod_ref, fnw_ref, wco_ref, wao_ref, wout_ref,
             dya_ref, do_ref, delta_ref, dpg_ref, dx2_ref, gw_ref, small_ref, acc_ref, cast_ref):
        i = pl.program_id(0)

        @pl.when(i == 0)
        def _():
            acc_ref[...] = jnp.zeros(acc_ref.shape, F32)
            small_ref[...] = jnp.zeros(small_ref.shape, F32)

        bg = bg_ref[...]
        sbg = _sig(bg)
        sb = bg * sbg
        ov = o_ref[...]
        ya = ya_ref[...]
        yb = (ov * sb).astype(BF16)
        square = lambda ref: ref[...].reshape(D, D)
        y_a = _nn(ya, square(wco_ref))
        y_b = _nn(yb, square(wao_ref))
        sa = _sig(ga_ref[...])
        sgb = _sig(gb_ref[...])
        merged = (sa * y_a + sgb * y_b).astype(BF16)
        z = _nn(merged, square(wout_ref))
        gate = mod_ref[2:3, :]
        x2 = x_ref[...] + gate * z
        xn, rstd = _rms_parts(x2)
        fnw = fnw_ref[...]
        err = xn * fnw - t_ref[...]
        loss = jnp.sum(jnp.sum(err * err, axis=-1, keepdims=True), axis=0, keepdims=True) * (0.5 / D)
        dy = err * (1.0 / D)
        small_ref[0:1, :] += jnp.sum(dy * xn, axis=0, keepdims=True)
        dxn = dy * fnw
        dx2 = rstd * (dxn - xn * jnp.mean(dxn * xn, axis=-1, keepdims=True))
        dx2_ref[...] = dx2
        small_ref[1:2, :] += jnp.sum(dx2 * z, axis=0, keepdims=True)
        small_ref[2:3, :] += jnp.broadcast_to(loss, (1, D))
        dz = (dx2 * gate).astype(BF16)
        dmerged = _nt(dz, square(wout_ref))
        acc_ref[2] += _tn(merged, dz)
        dy_a = (dmerged * sa).astype(BF16)
        dy_b = (dmerged * sgb).astype(BF16)
        dpg_ref[:, D:2 * D] = (dmerged * y_a * (sa * (1.0 - sa))).astype(BF16)
        dpg_ref[:, 2 * D:3 * D] = (dmerged * y_b * (sgb * (1.0 - sgb))).astype(BF16)
        dya_ref[...] = _nt(dy_a, square(wco_ref))
        acc_ref[0] += _tn(ya, dy_a)
        dyb = _nt(dy_b, square(wao_ref))
        acc_ref[1] += _tn(yb, dy_b)
        do = dyb * sb
        do_ref[...] = do.astype(BF16)
        dpg_ref[:, 0:D] = (dyb * ov * (sbg * (1.0 + bg * (1.0 - sbg)))).astype(BF16)
        prod = do * ov
        for h in range(NH):
            delta_ref[h] = jnp.broadcast_to(jnp.sum(prod[:, h * DV:(h + 1) * DV], axis=-1, keepdims=True), (tm, DV))

        @pl.when(i == n - 1)
        def _():
            for j in range(3):
                cast_ref[...] = acc_ref[j].astype(cast_ref.dtype)
                pltpu.sync_copy(cast_ref, gw_ref.at[j])

    row = pl.BlockSpec((tm, D), lambda i: (i, 0))
    col = lambda k: pl.BlockSpec((tm, D), lambda i: (i, k))
    wspec = lambda j: pl.BlockSpec((N_DEV, ROWS_SQ, D), lambda i: (0, j, 0), pipeline_mode=pl.Buffered(1))
    return pl.pallas_call(
        body, name="merge_loss", grid=(n,),
        in_specs=[row, row, row, row, col(4), col(5), col(6), _const((8, D)), _const((1, D)), wspec(0), wspec(1), wspec(2)],
        out_specs=[row, row, pl.BlockSpec((NH, tm, DV), lambda i: (0, i, 0)), pl.BlockSpec((tm, 3 * D), lambda i: (i, 0)),
                   row, ANY, _const((8, D))],
        out_shape=[jax.ShapeDtypeStruct((s, D), F32), jax.ShapeDtypeStruct((s, D), BF16),
                   jax.ShapeDtypeStruct((NH, s, DV), F32), jax.ShapeDtypeStruct((s, 3 * D), BF16),
                   jax.ShapeDtypeStruct((s, D), F32), jax.ShapeDtypeStruct((3, D, D), BF16),
                   jax.ShapeDtypeStruct((8, D), F32)],
        scratch_shapes=[pltpu.VMEM((3, D, D), F32), pltpu.VMEM((D, D), BF16)],
        compiler_params=_cparams(1),
    )(x, target, ya, o, proj, proj, proj, mod8, fnw, gathered, gathered, gathered)


def _attn_bwd(q, k, v, do, lse, delta):
    nh, s, _ = q.shape
    t = _attn_tile(s)
    nb = s // t

    def body(q_ref, k_ref, v_ref, do_ref, lse_ref, dl_ref, dq_ref, dk_ref, dv_ref):
        kj = pl.program_id(1)

        @pl.when(kj == 0)
        def _():
            dq_ref[...] = jnp.zeros(dq_ref.shape, F32)

        kv_, vv = k_ref[0], v_ref[0]

        def chunk(c, carry, diag):
            dk, dv = carry
            rows = pl.ds(pl.multiple_of(c * t, t), t)
            qv = q_ref[0, rows, :]
            dov = do_ref[rows, :]
            sc = _nt(qv, kv_)
            if diag:
                sc = _causal_mask(sc, t)
            p = jnp.exp(sc - lse_ref[0, rows, 0:1])
            dv = dv + _tn(p.astype(BF16), dov)
            dp = _nt(dov, vv)
            ds = (p * (dp - dl_ref[0, rows, 0:1])).astype(BF16)
            dk = dk + _tn(ds, qv)
            dq_ref[0, rows, :] += _nn(ds, kv_)
            return dk, dv

        carry = chunk(kj, (jnp.zeros((t, DQK), F32), jnp.zeros((t, DV), F32)), True)
        dk, dv = lax.fori_loop(kj + 1, nb, lambda c, cr: chunk(c, cr, False), carry)
        dk_ref[0] = dk
        dv_ref[0] = dv

    head = lambda d: pl.BlockSpec((1, s, d), lambda h, j: (h, 0, 0))
    blk = lambda d: pl.BlockSpec((1, t, d), lambda h, j: (h, j, 0))
    return pl.pallas_call(
        body, name="attn_bwd", grid=(nh, nb),
        in_specs=[head(DQK), blk(DQK), blk(DV), pl.BlockSpec((s, DV), lambda h, j: (0, h)), head(DV), head(DV)],
        out_specs=[head(DQK), blk(DQK), blk(DV)],
        out_shape=[jax.ShapeDtypeStruct((nh, s, DQK), F32), jax.ShapeDtypeStruct((nh, s, DQK), F32),
                   jax.ShapeDtypeStruct((nh, s, DV), F32)],
        compiler_params=_cparams(2),
    )(q, k, v, do, lse, delta)


def _mla_bwd(dq, dk, dv, proj, pos, inv_freq, qnw, kvnw, wuq, wukv):
    s = proj.shape[0]
    tm = min(512, s)

    def body(dq_ref, dk_ref, dv_ref, p_ref, pos_ref, if_ref, qnw_ref, kvnw_ref, wuq_ref, wukv_ref,
             dp_ref, guq_ref, gukv_ref, small_ref):
        @pl.when(pl.program_id(0) == 0)
        def _():
            guq_ref[...] = jnp.zeros(guq_ref.shape, F32)
            gukv_ref[...] = jnp.zeros(gukv_ref.shape, F32)
            small_ref[...] = jnp.zeros(small_ref.shape, F32)

        blk = p_ref[...]
        cos, sin = _rope_tables(pos_ref, if_ref)

        def unrope(g):
            g1, g2 = g[:, :HALF], g[:, HALF:]
            return jnp.concatenate([g1 * cos + g2 * sin, g2 * cos - g1 * sin], axis=-1)

        dq_cols, dkv_cols = [], []
        dkr = jnp.zeros((tm, ROPE), F32)
        for h in range(NH):
            dqh, dkh = dq_ref[h] * ATTN_SCALE, dk_ref[h]
            dq_cols += [dqh[:, :NOPE], unrope(dqh[:, NOPE:])]
            dkv_cols += [dkh[:, :NOPE], dv_ref[h]]
            dkr = dkr + dkh[:, NOPE:]
        dq_full = jnp.concatenate(dq_cols, axis=-1).astype(BF16)
        dkv_full = jnp.concatenate(dkv_cols, axis=-1).astype(BF16)

        def latent_bwd(c, nw_ref, d_up, w_ref, g_ref, srow):
            nrm, rstd = _rms_parts(c)
            nw = nw_ref[...]
            lat = (nrm * nw).astype(BF16)
            g_ref[...] += _tn(lat, d_up)
            dlat = _nt(d_up, w_ref[...])
            small_ref[srow:srow + 1, :] += jnp.sum(dlat * nrm, axis=0, keepdims=True)
            dn = dlat * nw
            return rstd * (dn - nrm * jnp.mean(dn * nrm, axis=-1, keepdims=True))

        dcq = latent_bwd(blk[:, :QL], qnw_ref, dq_full, wuq_ref, guq_ref, 0)
        dckv = latent_bwd(blk[:, QL:QL + KVL], kvnw_ref, dkv_full, wukv_ref, gukv_ref, 1)
        dp_ref[...] = jnp.concatenate([dcq, dckv, unrope(dkr), jnp.zeros((tm, D - MLA_COLS), F32)], axis=-1).astype(BF16)

    hm = lambda d: pl.BlockSpec((NH, tm, d), lambda i: (0, i, 0))
    return pl.pallas_call(
        body, name="mla_bwd", grid=(s // tm,),
        in_specs=[hm(DQK), hm(DQK), hm(DV), pl.BlockSpec((tm, D), lambda i: (i, 3)), pl.BlockSpec((tm, 1), lambda i: (i, 0)),
                  _const((1, HALF)), _const((1, QL)), _const((1, KVL)), _const((QL, NH * DQK)), _const((KVL, NH * 256))],
        out_specs=[pl.BlockSpec((tm, D), lambda i: (i, 0)), _const((QL, NH * DQK)), _const((KVL, NH * 256)), _const((8, QL))],
        out_shape=[jax.ShapeDtypeStruct((s, D), BF16), jax.ShapeDtypeStruct((QL, NH * DQK), F32),
                   jax.ShapeDtypeStruct((KVL, NH * 256), F32), jax.ShapeDtypeStruct((8, QL), F32)],
        compiler_params=_cparams(1),
    )(dq, dk, dv, proj, pos, inv_freq, qnw, kvnw, wuq, wukv)


def _conv_rows_bwd(dya, u1, proj, ln_w, ln_b):
    s = dya.shape[0]
    tm = min(512, s)

    def body(dya_ref, u1_ref, gate_ref, lw_ref, lb_ref, du1_ref, dag_ref, small_ref):
        @pl.when(pl.program_id(0) == 0)
        def _():
            small_ref[...] = jnp.zeros(small_ref.shape, F32)

        u1 = u1_ref[...]
        mu = jnp.mean(u1, axis=-1, keepdims=True)
        xc = u1 - mu
        rstd = lax.rsqrt(jnp.mean(xc * xc, axis=-1, keepdims=True) + EPS)
        xhat = xc * rstd
        lw = lw_ref[...]
        ln = xhat * lw + lb_ref[...]
        sl = _sig(ln)
        u2 = ln * sl
        gate = gate_ref[...]
        sg = _sig(gate)
        dya = dya_ref[...]
        dag_ref[...] = (dya * u2 * (sg * (1.0 + gate * (1.0 - sg)))).astype(BF16)
        dln = dya * (gate * sg) * (sl * (1.0 + ln * (1.0 - sl)))
        small_ref[0:1, :] += jnp.sum(dln * xhat, axis=0, keepdims=True)
        small_ref[1:2, :] += jnp.sum(dln, axis=0, keepdims=True)
        dxh = dln * lw
        du1_ref[...] = rstd * (dxh - jnp.mean(dxh, axis=-1, keepdims=True) - xhat * jnp.mean(dxh * xhat, axis=-1, keepdims=True))

    row = pl.BlockSpec((tm, D), lambda i: (i, 0))
    return pl.pallas_call(
        body, name="conv_rows_bwd", grid=(s // tm,),
        in_specs=[row, row, pl.BlockSpec((tm, D), lambda i: (i, 2)), _const((1, D)), _const((1, D))],
        out_specs=[row, row, _const((8, D))],
        out_shape=[jax.ShapeDtypeStruct((s, D), F32), jax.ShapeDtypeStruct((s, D), BF16), jax.ShapeDtypeStruct((8, D), F32)],
        compiler_params=_cparams(1),
    )(dya, u1, proj, ln_w, ln_b)


def _conv_bwd(du1, proj, conv_w):
    s = du1.shape[0]
    tm = min(256, s)
    hb = tm // HALO
    n = s // tm
    last32 = s // HALO - 1

    def body(d_ref, dn_ref, av_ref, ag_ref, avh_ref, agh_ref, cw_ref, dp_ref, gcw_ref, small_ref,
             dwin_ref, dpad_ref, dsh_ref, uwin_ref, acc_ref):
        i = pl.program_id(0)

        @pl.when(i == 0)
        def _():
            acc_ref[...] = jnp.zeros(acc_ref.shape, F32)
            small_ref[...] = jnp.zeros(small_ref.shape, F32)
            dpad_ref[...] = jnp.zeros(dpad_ref.shape, F32)
            uwin_ref[...] = jnp.zeros(uwin_ref.shape, F32)

        dv = d_ref[...]
        dwin_ref[0:tm, :] = dv
        dwin_ref[tm:tm + HALO, :] = jnp.where(i < n - 1, dn_ref[...], 0.0)
        dpad_ref[SUBLANES:SUBLANES + tm, :] = dv
        halo = avh_ref[...] * _sig(agh_ref[...])
        uwin_ref[0:HALO, :] = jnp.where(i > 0, halo, 0.0)
        av = av_ref[...]
        sg = _sig(ag_ref[...])
        uwin_ref[HALO:HALO + tm, :] = av * sg
        small_ref[0:1, :] += jnp.sum(dv, axis=0, keepdims=True)

        for b in range(SUBLANES):
            dsh_ref[b] = dpad_ref[pl.ds(SUBLANES - b, tm + SUBLANES), :]

        first = HALO - (KCONV - 1)
        for c0 in range(0, D, CONV_LC):
            lanes = pl.ds(c0, CONV_LC)
            for r0 in range(0, tm, CONV_RC):
                acc = _shifted_taps(dwin_ref, lambda o: cw_ref[KCONV - 1 - o:KCONV - o, c0:c0 + CONV_LC], range(KCONV), r0, lanes)
                a = av[r0:r0 + CONV_RC, c0:c0 + CONV_LC]
                g = sg[r0:r0 + CONV_RC, c0:c0 + CONV_LC]
                dp_ref[r0:r0 + CONV_RC, c0:c0 + CONV_LC] = (acc * g).astype(BF16)
                dp_ref[r0:r0 + CONV_RC, D + c0:D + c0 + CONV_LC] = (acc * a * (g * (1.0 - g))).astype(BF16)
            for b in range(SUBLANES):
                group = [o for o in range(first, first + KCONV) if o % SUBLANES == b]
                parts = [jnp.zeros((SUBLANES, CONV_LC), F32) for _ in group]
                chunks = [(i0, GW_RC) for i0 in range(0, tm, GW_RC)] + ([(tm, SUBLANES)] if b else [])
                for i0, rows in chunks:
                    dsh = dsh_ref[b, pl.ds(i0, rows), lanes]
                    for n_, o in enumerate(group):
                        prod = dsh * uwin_ref[pl.ds(i0 + o - b, rows), lanes]
                        parts[n_] = parts[n_] + jnp.sum(prod.reshape(rows // SUBLANES, SUBLANES, CONV_LC), axis=0)
                for n_, o in enumerate(group):
                    acc_ref[o - first, :, c0:c0 + CONV_LC] += parts[n_]

        @pl.when(i == n - 1)
        def _():
            gcw_ref[...] = jnp.sum(acc_ref[...], axis=1)

    row = lambda k: pl.BlockSpec((tm, D), lambda i: (i, k))
    prev = lambda k: pl.BlockSpec((HALO, D), lambda i: (jnp.maximum(i * hb - 1, 0), k))
    return pl.pallas_call(
        body, name="conv_bwd", grid=(n,),
        in_specs=[row(0), pl.BlockSpec((HALO, D), lambda i: (jnp.minimum((i + 1) * hb, last32), 0)),
                  row(0), row(1), prev(0), prev(1), _const((KPAD, D))],
        out_specs=[pl.BlockSpec((tm, 2 * D), lambda i: (i, 0)), _const((KPAD, D)), _const((8, D))],
        out_shape=[jax.ShapeDtypeStruct((s, 2 * D), BF16), jax.ShapeDtypeStruct((KPAD, D), F32), jax.ShapeDtypeStruct((8, D), F32)],
        scratch_shapes=[pltpu.VMEM((tm + HALO, D), F32), pltpu.VMEM((tm + 2 * SUBLANES, D), F32),
                        pltpu.VMEM((SUBLANES, tm + SUBLANES, D), F32),
                        pltpu.VMEM((tm + HALO + SUBLANES, D), F32), pltpu.VMEM((KPAD, SUBLANES, D), F32)],
        compiler_params=_cparams(1),
    )(du1, du1, proj, proj, proj, proj, conv_w)


def _dproj_specs(tm, rows_first):
    def spec(lo, hi):
        def idx(a, b):
            i, k = (a, b) if rows_first else (b, a)
            col = jnp.clip(k - lo, 0, hi - lo - 1)
            if rows_first:
                return (i, col)
            return (jnp.where((k >= lo) & (k < hi), i, 0), col)
        return pl.BlockSpec((tm, D), idx)
    return [spec(0, 2), spec(2, 3), spec(3, 4), spec(4, 7)]


def _pick_dproj(k, refs, fn):
    vg, ag, mla, gates = refs

    @pl.when(k < 2)
    def _():
        fn(vg)

    @pl.when(k == 2)
    def _():
        fn(ag)

    @pl.when(k == 3)
    def _():
        fn(mla)

    @pl.when(k > 3)
    def _():
        fn(gates)


def _in_proj_bwd_x(dps, wt, x, dx2, norm_w, mod8):
    s = x.shape[0]
    tm = min(1024, s)
    nk = wt.shape[0] // D

    def body(vg_ref, ag_ref, mla_ref, g_ref, w_ref, x_ref, dx2_ref, nw_ref, mod_ref, gx_ref, small_ref, acc_ref):
        i, k = pl.program_id(0), pl.program_id(1)

        @pl.when((i == 0) & (k == 0))
        def _():
            small_ref[...] = jnp.zeros(small_ref.shape, F32)

        @pl.when(k == 0)
        def _():
            acc_ref[...] = jnp.zeros(acc_ref.shape, F32)

        def add(ref):
            acc_ref[...] += _nn(ref[...], w_ref[...])

        _pick_dproj(k, (vg_ref, ag_ref, mla_ref, g_ref), add)

        @pl.when(k == nk - 1)
        def _():
            dh = acc_ref[...]
            xn, rstd = _rms_parts(x_ref[...])
            nw = nw_ref[...]
            hn = xn * nw
            small_ref[0:1, :] += jnp.sum(dh, axis=0, keepdims=True)
            small_ref[1:2, :] += jnp.sum(dh * hn, axis=0, keepdims=True)
            dhn = dh * (1.0 + mod_ref[1:2, :])
            small_ref[2:3, :] += jnp.sum(dhn * xn, axis=0, keepdims=True)
            dxn = dhn * nw
            gx_ref[...] = rstd * (dxn - xn * jnp.mean(dxn * xn, axis=-1, keepdims=True)) + dx2_ref[...]

    row = pl.BlockSpec((tm, D), lambda i, k: (i, 0))
    return pl.pallas_call(
        body, name="in_proj_bwd_x", grid=(s // tm, nk),
        in_specs=_dproj_specs(tm, True) + [pl.BlockSpec((D, D), lambda i, k: (k, 0)), row, row, _const((1, D)), _const((8, D))],
        out_specs=[row, _const((8, D))],
        out_shape=[jax.ShapeDtypeStruct((s, D), F32), jax.ShapeDtypeStruct((8, D), F32)],
        scratch_shapes=[pltpu.VMEM((tm, D), F32)],
        compiler_params=_cparams(2),
    )(*dps, wt, x, dx2, norm_w, mod8)


def _in_proj_bwd_w(dps, h, nk):
    s = h.shape[0]
    tm = min(1024, s)

    n = s // tm

    def body(vg_ref, ag_ref, mla_ref, g_ref, h_ref, gw_ref, acc_ref):
        k, i = pl.program_id(0), pl.program_id(1)

        @pl.when(i == 0)
        def _():
            acc_ref[...] = jnp.zeros(acc_ref.shape, F32)

        def add(ref):
            acc_ref[...] += _tn(ref[...], h_ref[...])

        _pick_dproj(k, (vg_ref, ag_ref, mla_ref, g_ref), add)

        @pl.when(i == n - 1)
        def _():
            gw_ref[...] = acc_ref[...].astype(gw_ref.dtype)

    return pl.pallas_call(
        body, name="in_proj_bwd_w", grid=(nk, n),
        in_specs=_dproj_specs(tm, False) + [pl.BlockSpec((tm, D), lambda k, i: (i, 0))],
        out_specs=pl.BlockSpec((D, D), lambda k, i: (k, 0)),
        out_shape=jax.ShapeDtypeStruct((nk * D, D), BF16),
        scratch_shapes=[pltpu.VMEM((D, D), F32)],
        compiler_params=_cparams(2),
    )(*dps, h)


def _small_slab(wuq, wukv, conv_w):
    cw = jnp.pad(conv_w.reshape(-1), (0, (ROWS_TAIL - ROWS_UQ - ROWS_UKV) * D - KCONV * 128)).reshape(-1, D)
    return jnp.concatenate([wuq.reshape(ROWS_UQ, D), wukv.reshape(ROWS_UKV, D), cw], axis=0)


def _split_small_slab(slab):
    return (slab[..., :ROWS_UQ, :], slab[..., ROWS_UQ:ROWS_UQ + ROWS_UKV, :],
            slab[..., ROWS_UQ + ROWS_UKV:ROWS_UQ + ROWS_UKV + ROWS_CW, :])


def _unpack_small_slab(slab):
    wuq, wukv, cw = _split_small_slab(slab)
    return (wuq.reshape(QL, NH * DQK // N_DEV), wukv.reshape(KVL, NH * 256 // N_DEV),
            cw.reshape(-1)[:KCONV * 128].reshape(KCONV, 128))


def _pack_shard(w_in, wco, wao, wout, wuq, wukv, conv_w):
    bf = lambda a: a.astype(BF16)
    first = jnp.concatenate([jnp.pad(bf(w_in).T, ((0, ROWS_IN_PAD - ROWS_IN), (0, 0))), bf(_small_slab(wuq, wukv, conv_w))], axis=0)
    return first, jnp.concatenate([bf(wco), bf(wao), bf(wout)], axis=0)


def _unpack_gathered(g):
    wt = g[:, :ROWS_IN].reshape(IN_COLS, D)
    split = 3 * D + MLA_COLS
    wt = jnp.concatenate([wt[:split], jnp.zeros((D - MLA_COLS, D), g.dtype), wt[split:]], axis=0)
    wuq, wukv, cw = _split_small_slab(g[:, OFF_TAIL:OFF_TAIL + ROWS_TAIL])
    wuq = wuq.reshape(N_DEV, QL, NH * DQK // N_DEV).transpose(1, 0, 2).reshape(QL, NH * DQK)
    wukv = wukv.reshape(N_DEV, KVL, NH * 256 // N_DEV).transpose(1, 0, 2).reshape(KVL, NH * 256)
    cw = cw.reshape(N_DEV, ROWS_CW * D)[:, :KCONV * 128].reshape(N_DEV, KCONV, 128).transpose(1, 0, 2).reshape(KCONV, D)
    return wt, wuq, wukv, cw


def _pack_grads(gwt, gw3, guq, gukv, gcw):
    split = 3 * D + MLA_COLS
    g_in = jnp.concatenate([gwt[:split], gwt[4 * D:]], axis=0).reshape(N_DEV, ROWS_IN, D)
    g_in = jnp.pad(g_in, ((0, 0), (0, ROWS_IN_PAD - ROWS_IN), (0, 0)))
    guq = guq.reshape(QL, N_DEV, -1).transpose(1, 0, 2).reshape(N_DEV, ROWS_UQ, D)
    gukv = gukv.reshape(KVL, N_DEV, -1).transpose(1, 0, 2).reshape(N_DEV, ROWS_UKV, D)
    gcw = gcw.reshape(KCONV, N_DEV, 128).transpose(1, 0, 2).reshape(N_DEV, KCONV * 128)
    gcw = jnp.pad(gcw, ((0, 0), (0, (ROWS_TAIL - ROWS_UQ - ROWS_UKV) * D - KCONV * 128))).reshape(N_DEV, -1, D)
    gsq = gw3.reshape(3, N_DEV, ROWS_SQ, D).transpose(1, 0, 2, 3).reshape(N_DEV, 3 * ROWS_SQ, D)
    slabs = jnp.concatenate([a.astype(BF16) for a in (g_in, guq, gukv, gcw, gsq)], axis=1)
    return slabs.reshape(N_CHIP, 2, ROWS_PACK, D).transpose(1, 0, 2, 3)


def _pack_small(vecs):
    flat = jnp.concatenate([v.reshape(-1) for v in vecs])
    return jnp.pad(flat, (0, SMALL_LEN - flat.shape[0])).reshape(8, SMALL_COLS)


def _unpack_small(a, shapes):
    flat = a.reshape(-1)
    out, off = [], 0
    for shp, n in zip(shapes, SMALL_SIZES):
        out.append(flat[off:off + n].reshape(shp))
        off += n
    return out


def kernel(x, c, positions, w_ada, b_ada, norm_w, w_in, conv_w, conv_b, conv_ln_w, conv_ln_b, w_conv_out, q_norm_w, w_uq, kv_norm_w, w_ukv, w_attn_out, w_out, final_norm_w, loss_target, m_w_ada, m_b_ada, m_norm_w, m_w_in, m_conv_w, m_conv_b, m_conv_ln_w, m_conv_ln_b, m_w_conv_out, m_q_norm_w, m_w_uq, m_kv_norm_w, m_w_ukv, m_w_attn_out, m_w_out, m_final_norm_w, v_w_ada, v_b_ada, v_norm_w, v_w_in, v_conv_w, v_conv_b, v_conv_ln_w, v_conv_ln_b, v_w_conv_out, v_q_norm_w, v_w_uq, v_kv_norm_w, v_w_ukv, v_w_attn_out, v_w_out, v_final_norm_w):
    me = 4 * lax.axis_index("x") + 2 * lax.axis_index("y") + lax.axis_index("c")
    xs, tgt = x[0], loss_target[0]
    s = xs.shape[0]
    ada_cols = w_ada.shape[2]

    sharded = lambda t: tuple(a[0] for a in t)
    slab_first, slab_sq = _pack_shard(*sharded((w_in, w_conv_out, w_attn_out, w_out, w_uq, w_ukv, conv_w)))
    wt, wuq, wukv, cw = _unpack_gathered(_all_gather(slab_first, "gather_weights"))
    cw32 = jnp.pad(cw.astype(F32), ((0, KPAD - KCONV), (0, 0)))
    sq_send, sq_recv, sq_src, sq_land, sq_token = _exchange_start(slab_sq, (N_DEV,) + slab_sq.shape, _direct_copies, N_DEV - 1,
                                                                  "square_gather_start")

    c_all = _all_gather(jnp.broadcast_to(c, (8, D)), "gather_c")[:, 0, :]
    b_cols = lax.dynamic_slice(b_ada, (0, me * ada_cols), (1, ada_cols))
    mod_cols = _all_gather(_ada_mod(c_all, w_ada[0], b_cols), "gather_mod")
    mod = lax.dynamic_index_in_dim(mod_cols, me, axis=1, keepdims=False).reshape(3, D)
    mod8 = jnp.pad(mod, ((0, 5), (0, 0)))

    pos = positions.reshape(s, 1)
    inv_freq = (ROPE_THETA ** (-jnp.arange(0, ROPE, 2, dtype=F32) / ROPE)).reshape(1, HALF)
    proj, h = _in_proj(xs, norm_w + sq_token[0:1, 0:1], mod8, wt)
    u1, ya = _conv_fwd(proj, cw32, conv_b, conv_ln_w, conv_ln_b)
    q, k, v = _mla_prep(proj, pos, inv_freq, q_norm_w, kv_norm_w, wuq, wukv)
    o, lse = _attn_fwd(q, k, v)

    slab_sq, gathered_sq = _exchange_wait(sq_send, sq_recv, sq_src, sq_land, lse, _direct_copies, "square_gather_wait")
    gathered_sq = lax.dynamic_update_slice(gathered_sq, slab_sq[None], (me, 0, 0))
    dya, do, delta, dp_gates, dx2, gw3, small_a = _merge_loss(xs, tgt, ya, o, proj, mod8, final_norm_w.reshape(1, D), gathered_sq)
    dq, dk, dv = _attn_bwd(q, k, v, do, lse, delta)
    dp_mla, guq, gukv, small_b = _mla_bwd(dq, dk, dv, proj, pos, inv_freq, q_norm_w, kv_norm_w, wuq, wukv)
    du1, dp_ag, small_c = _conv_rows_bwd(dya, u1, proj, conv_ln_w, conv_ln_b)
    dp_vg, gcw, small_d = _conv_bwd(du1, proj, cw32)
    dps = [dp_vg, dp_ag, dp_mla, dp_gates]
    gwt = _in_proj_bwd_w(dps, h, wt.shape[0] // D)

    packed = _pack_grads(gwt, gw3, guq, gukv, gcw[:KCONV])
    core = lax.axis_index("c").astype(jnp.int32).reshape(1)
    half = _pair_add(core, packed, _pair_exchange(packed), ROWS_PACK // 4)
    send_sems, recv_sems, half_thru, land_thru, token = _exchange_start(half, half.shape, _chip_copies, N_CHIP - 1, "chip_exchange_start")
    grad_x, small_e = _in_proj_bwd_x(dps, wt, xs, dx2, norm_w + token[0:1, 0:1], mod8)
    half, recv = _exchange_wait(send_sems, recv_sems, half_thru, land_thru, small_e, _chip_copies, "chip_exchange_wait")
    chip = 2 * lax.axis_index("x") + lax.axis_index("y")
    recv = lax.dynamic_update_slice(recv, lax.dynamic_slice(half, (chip, 0, 0), (1,) + half.shape[1:]), (chip, 0, 0))
    big_in = _adam_w_in(recv, w_in[0], m_w_in[0], v_w_in[0])
    squares = (("w_conv_out", w_conv_out, m_w_conv_out, v_w_conv_out), ("w_attn_out", w_attn_out, m_w_attn_out, v_w_attn_out),
               ("w_out", w_out, m_w_out, v_w_out))
    big_sq = [_reduce_adam(recv, w[0], m[0], v[0], "adam_" + nm, ROWS_SQ, OFF_SQ // ROWS_SQ + j) for j, (nm, w, m, v) in enumerate(squares)]
    tail = _reduce_adam(recv, _small_slab(w_uq[0], w_ukv[0], conv_w[0]), _small_slab(m_w_uq[0], m_w_ukv[0], m_conv_w[0]),
                        _small_slab(v_w_uq[0], v_w_ukv[0], v_conv_w[0]), "adam_small_sharded", ROWS_TAIL, OFF_TAIL // ROWS_TAIL)
    tail = [_unpack_small_slab(a) for a in tail]
    big = [(big_in[i], big_sq[0][i], big_sq[1][i], big_sq[2][i], *tail[i]) for i in range(4)]

    dmod = jnp.concatenate([small_e[0], small_e[1], small_a[1]])
    payload = _pack_small([dmod, small_e[2], small_d[0], small_c[0], small_c[1], small_a[0], small_b[0], small_b[1], small_a[2, 0:1]])
    pay_all = _all_gather(payload, "gather_small")
    small_w = (b_ada, norm_w, conv_b, conv_ln_w, conv_ln_b, final_norm_w, q_norm_w, kv_norm_w)
    small_m = (m_b_ada, m_norm_w, m_conv_b, m_conv_ln_w, m_conv_ln_b, m_final_norm_w, m_q_norm_w, m_kv_norm_w)
    small_v = (v_b_ada, v_norm_w, v_conv_b, v_conv_ln_w, v_conv_ln_b, v_final_norm_w, v_q_norm_w, v_kv_norm_w)
    sm = _reduce_adam(pay_all, _pack_small(small_w), _pack_small(small_m), _pack_small(small_v), "adam_replicated", 8, 0)
    loss = sm[0].reshape(-1)[sum(SMALL_SIZES)]
    shapes = [t.shape for t in small_w]
    sm = [_unpack_small(a, shapes) for a in sm]

    dmod_all = pay_all.reshape(N_DEV, SMALL_LEN)[:, :3 * D]
    dmod_cols = lax.dynamic_slice(dmod_all, (0, me * ada_cols), (N_DEV, ada_cols))
    ada = _ada_bwd(c_all, dmod_cols, w_ada[0], m_w_ada[0], v_w_ada[0])

    def group(i):
        b_in, b_co, b_ao, b_out, b_uq, b_ukv, b_cw = big[i]
        s_bada, s_nw, s_cb, s_clw, s_clb, s_fnw, s_qnw, s_kvnw = sm[i]
        return (ada[i][None], s_bada, s_nw, b_in[None], b_cw[None], s_cb, s_clw, s_clb, b_co[None], s_qnw, b_uq[None], s_kvnw,
                b_ukv[None], b_ao[None], b_out[None], s_fnw)

    return (loss, grad_x[None], *group(0), *group(1), *group(2), *group(3))
```

```python
import functools

import jax
import jax.numpy as jnp
from jax import lax
from jax.experimental import pallas as pl
from jax.experimental.pallas import tpu as pltpu

F32 = jnp.float32
BF16 = jnp.bfloat16

D = 1024
NH = 8
NOPE = 128
ROPE = 64
HALF = ROPE // 2
DQK = NOPE + ROPE
DV = 128
QL = 256
KVL = 256
KCONV = 31
KPAD = 32
HALO = 32
IN_COLS = 6720
MLA_COLS = QL + KVL + ROPE
PROJ_COLS = 7 * D
EPS = 1e-6
ROPE_THETA = 10000.0
N_DEV = 8

ADAM_LR = 0.001
ADAM_B1 = 0.9
ADAM_B2 = 0.999
ADAM_EPS = 1e-08
ADAM_WD = 0.01
ADAM_STEP = 10

ROWS_IN = 840
ROWS_SQ = 128
ROWS_UQ = 48
ROWS_UKV = 64
ROWS_CW = 4
ROWS_IN_PAD = 896
OFF_TAIL = ROWS_IN_PAD
ROWS_TAIL = 128
OFF_SQ = OFF_TAIL + ROWS_TAIL
ROWS_PACK = OFF_SQ + 3 * ROWS_SQ
SMALL_SIZES = (3 * D, D, D, D, D, D, QL, KVL)
SMALL_COLS = 1152
SMALL_LEN = 8 * SMALL_COLS

MESH = pl.DeviceIdType.MESH
ANY = pl.BlockSpec(memory_space=pl.ANY)
V7X_VMEM_LIMIT = 56 * 1024 * 1024


def _cparams(n_axes, vmem=V7X_VMEM_LIMIT):
    return pltpu.CompilerParams(dimension_semantics=("arbitrary",) * n_axes, vmem_limit_bytes=vmem)


def _sig(x):
    return jax.nn.sigmoid(x)


def _nt(a, b):
    return lax.dot_general(a, b, (((1,), (1,)), ((), ())), preferred_element_type=F32)


def _tn(a, b):
    return lax.dot_general(a, b, (((0,), (0,)), ((), ())), preferred_element_type=F32)


def _nn(a, b):
    return jnp.dot(a, b, preferred_element_type=F32)


def _const(shape):
    return pl.BlockSpec(shape, lambda *_: (0,) * len(shape))


def _all_gather(block, name):
    r, c = block.shape

    def body(x_ref, out_ref, send_sems, recv_sems, local_sem):
        x, y, cc = lax.axis_index("x"), lax.axis_index("y"), lax.axis_index("c")
        me, sibling = (x, y, cc), (x, y, 1 - cc)
        chips = [(1 - x, y), (x, 1 - y), (1 - x, 1 - y)]

        def slot(px, py, pc):
            return out_ref.at[4 * px + 2 * py + pc]

        def copy(k, blk, to, src=None):
            return pltpu.make_async_remote_copy(
                src_ref=slot(*blk) if src is None else src, dst_ref=slot(*blk),
                send_sem=send_sems.at[k], recv_sem=recv_sems.at[k],
                device_id=to, device_id_type=MESH)

        mine = pltpu.make_async_copy(x_ref, slot(*me), local_sem)
        mine.start()
        first = [copy(0, me, sibling, src=x_ref)]
        first += [copy(1 + j, me, (*chip, cc), src=x_ref) for j, chip in enumerate(chips)]
        for cp in first:
            cp.start()
        passed = [copy(4 + j, (*chip, cc), sibling) for j, chip in enumerate(chips)]
        for j, chip in enumerate(chips):
            copy(1 + j, (*chip, cc), me).wait_recv()
            passed[j].start()
        copy(0, sibling, me).wait_recv()
        for j, chip in enumerate(chips):
            copy(4 + j, (*chip, 1 - cc), me).wait_recv()
        for cp in first + passed:
            cp.wait_send()
        mine.wait()

    return pl.pallas_call(
        body, name=name,
        out_shape=jax.ShapeDtypeStruct((N_DEV, r, c), block.dtype),
        in_specs=[ANY], out_specs=ANY,
        scratch_shapes=[pltpu.SemaphoreType.DMA((7,)), pltpu.SemaphoreType.DMA((7,)), pltpu.SemaphoreType.DMA],
    )(block)


N_CHIP = 4


def _pair_exchange(packed):
    _, _, r, c = packed.shape

    def body(src_ref, out_ref, send_sem, recv_sem):
        x, y, cc = lax.axis_index("x"), lax.axis_index("y"), lax.axis_index("c")
        cp = pltpu.make_async_remote_copy(
            src_ref=src_ref.at[1 - cc], dst_ref=out_ref, send_sem=send_sem, recv_sem=recv_sem,
            device_id=(x, y, 1 - cc), device_id_type=MESH)
        cp.start()
        cp.wait()

    return pl.pallas_call(
        body, name="pair_exchange",
        out_shape=jax.ShapeDtypeStruct((N_CHIP, r, c), packed.dtype),
        in_specs=[ANY], out_specs=ANY,
        scratch_shapes=[pltpu.SemaphoreType.DMA, pltpu.SemaphoreType.DMA],
    )(packed)


def _pair_add(core, packed, got, tr):
    _, _, r, c = packed.shape

    def body(core_ref, own_ref, got_ref, out_ref):
        out_ref[...] = (own_ref[0].astype(F32) + got_ref[...].astype(F32)).astype(out_ref.dtype)

    blk = pl.BlockSpec((1, tr, c), lambda j, i, core_ref: (j, i, 0))
    return pl.pallas_call(
        body, name="pair_add",
        grid_spec=pltpu.PrefetchScalarGridSpec(
            num_scalar_prefetch=1, grid=(N_CHIP, r // tr),
            in_specs=[pl.BlockSpec((1, 1, tr, c), lambda j, i, core_ref: (core_ref[0], j, i, 0)), blk],
            out_specs=blk),
        out_shape=jax.ShapeDtypeStruct((N_CHIP, r, c), packed.dtype),
        compiler_params=_cparams(2),
    )(core, packed, got)


HBM = pl.BlockSpec(memory_space=pltpu.HBM)
SEM = pl.BlockSpec(memory_space=pltpu.SEMAPHORE)
EFFECT = pltpu.SideEffectType.DATAFLOW_SIDE_EFFECTING


def _chip_copies(src_ref, land_ref, send_sems, recv_sems):
    x, y, cc = lax.axis_index("x"), lax.axis_index("y"), lax.axis_index("c")
    me = 2 * x + y
    copies = []
    for k in range(1, N_CHIP):
        px, py = (1 - x if k & 2 else x), (1 - y if k & 1 else y)
        copies.append(pltpu.make_async_remote_copy(
            src_ref=src_ref.at[2 * px + py], dst_ref=land_ref.at[me],
            send_sem=send_sems.at[k - 1], recv_sem=recv_sems.at[k - 1],
            device_id=(px, py, cc), device_id_type=MESH))
    return copies


def _direct_copies(src_ref, land_ref, send_sems, recv_sems):
    x, y, cc = lax.axis_index("x"), lax.axis_index("y"), lax.axis_index("c")
    me = 4 * x + 2 * y + cc
    copies = []
    for k in range(1, N_DEV):
        peer = ((1 - x if k & 4 else x), (1 - y if k & 2 else y), (1 - cc if k & 1 else cc))
        copies.append(pltpu.make_async_remote_copy(
            src_ref=src_ref, dst_ref=land_ref.at[me], send_sem=send_sems.at[k - 1], recv_sem=recv_sems.at[k - 1],
            device_id=peer, device_id_type=MESH))
    return copies


def _exchange_start(src, land_shape, copies_of, n_copies, name):
    def body(src_ref, land_ref, send_sems, recv_sems, src_thru, land_thru, token):
        for cp in copies_of(src_ref, land_ref, send_sems, recv_sems):
            cp.start()
        token[...] = jnp.zeros_like(token)

    return pl.pallas_call(
        body, name=name,
        out_shape=(pltpu.SemaphoreType.DMA((n_copies,)), pltpu.SemaphoreType.DMA((n_copies,)),
                   pltpu.HBM(src.shape, src.dtype), pltpu.HBM(land_shape, src.dtype), jax.ShapeDtypeStruct((8, 128), F32)),
        in_specs=(HBM, HBM), out_specs=(SEM, SEM, HBM, HBM, pl.BlockSpec(memory_space=pltpu.VMEM)),
        input_output_aliases={0: 2, 1: 3},
        compiler_params=pltpu.CompilerParams(has_side_effects=EFFECT),
    )(pltpu.with_memory_space_constraint(src, pltpu.HBM),
      pltpu.with_memory_space_constraint(lax.empty(land_shape, src.dtype), pltpu.HBM))


def _exchange_wait(send_sems, recv_sems, src_thru, land_thru, after, copies_of, name):
    def body(src_ref, land_ref, send_sems, recv_sems, after_ref, src_dead, got_ref):
        copies = copies_of(src_ref, land_ref, send_sems, recv_sems)
        for cp in copies:
            cp.wait_send()
        for cp in copies:
            cp.wait_recv()

    return pl.pallas_call(
        body, name=name,
        out_shape=(pltpu.HBM(src_thru.shape, src_thru.dtype), pltpu.HBM(land_thru.shape, land_thru.dtype)),
        in_specs=(HBM, HBM, SEM, SEM, ANY), out_specs=(HBM, HBM), input_output_aliases={0: 0, 1: 1},
        compiler_params=pltpu.CompilerParams(has_side_effects=EFFECT),
    )(src_thru, land_thru, send_sems, recv_sems, after)


def _adam(g, w, m, v):
    m = ADAM_B1 * m + (1.0 - ADAM_B1) * g
    v = ADAM_B2 * v + (1.0 - ADAM_B2) * (g * g)
    m_hat = m / (1.0 - ADAM_B1 ** ADAM_STEP)
    v_hat = v / (1.0 - ADAM_B2 ** ADAM_STEP)
    delta = -ADAM_LR * (m_hat / (jnp.sqrt(v_hat) + ADAM_EPS) + ADAM_WD * w)
    return delta, m, v


def _adam_w_in(parts, w, m, v):
    n = parts.shape[0]
    tc = 128

    def body(p_ref, w_ref, m_ref, v_ref, g_out, d_out, m_out, v_out):
        gt = p_ref[0].astype(F32)
        for j in range(1, n):
            gt = gt + p_ref[j].astype(F32)
        g = gt.T[:, :ROWS_IN]
        delta, nm, nv = _adam(g, w_ref[...], m_ref[...], v_ref[...])
        g_out[...] = g
        d_out[...] = delta
        m_out[...] = nm
        v_out[...] = nv

    row = pl.BlockSpec((tc, ROWS_IN), lambda i: (i, 0))
    return pl.pallas_call(
        body, name="adam_w_in", grid=(D // tc,),
        in_specs=[pl.BlockSpec((n, ROWS_IN_PAD, tc), lambda i: (0, 0, i)), row, row, row],
        out_specs=[row] * 4,
        out_shape=[jax.ShapeDtypeStruct((D, ROWS_IN), F32)] * 4,
        compiler_params=_cparams(1),
    )(parts, w, m, v)


def _reduce_adam(parts, w, m, v, name, tr, first_block):
    n, _, c = parts.shape
    r = w.shape[0]

    def body(p_ref, w_ref, m_ref, v_ref, g_out, d_out, m_out, v_out):
        g = p_ref[0].astype(F32)
        for j in range(1, n):
            g = g + p_ref[j].astype(F32)
        delta, nm, nv = _adam(g, w_ref[...], m_ref[...], v_ref[...])
        g_out[...] = g
        d_out[...] = delta
        m_out[...] = nm
        v_out[...] = nv

    row = pl.BlockSpec((tr, c), lambda i: (i, 0))
    return pl.pallas_call(
        body, name=name, grid=(r // tr,),
        in_specs=[pl.BlockSpec((n, tr, c), lambda i: (0, first_block + i, 0)), row, row, row],
        out_specs=[row] * 4,
        out_shape=[jax.ShapeDtypeStruct((r, c), F32)] * 4,
        compiler_params=_cparams(1),
    )(parts, w, m, v)


def _ada_mod(c_all, w_ada, b_cols):
    def body(c_ref, w_ref, b_ref, o_ref):
        cv = c_ref[...]
        act = (cv * _sig(cv)).astype(BF16)
        o_ref[...] = _nn(act, w_ref[...].astype(BF16)) + b_ref[...]

    return pl.pallas_call(body, name="ada_mod", out_shape=jax.ShapeDtypeStruct((N_DEV, w_ada.shape[1]), F32))(c_all, w_ada, b_cols)


def _ada_bwd(c_all, dmod_cols, w, m, v):
    def body(c_ref, d_ref, w_ref, m_ref, v_ref, g_out, d_out, m_out, v_out):
        cv = c_ref[...]
        act = (cv * _sig(cv)).astype(BF16)
        g = _tn(act, d_ref[...].astype(BF16))
        delta, nm, nv = _adam(g, w_ref[...], m_ref[...], v_ref[...])
        g_out[...] = g
        d_out[...] = delta
        m_out[...] = nm
        v_out[...] = nv

    return pl.pallas_call(body, name="ada_bwd", out_shape=[jax.ShapeDtypeStruct(w.shape, F32)] * 4)(c_all, dmod_cols, w, m, v)


def _in_proj(x, norm_w, mod8, wt):
    s = x.shape[0]
    tm = min(1024, s)
    nk = wt.shape[0] // D

    def body(x_ref, nw_ref, mod_ref, w_ref, proj_ref, h_ref, hs_ref):
        @pl.when(pl.program_id(1) == 0)
        def _():
            xv = x_ref[...]
            rstd = lax.rsqrt(jnp.mean(xv * xv, axis=-1, keepdims=True) + EPS)
            h = (xv * rstd) * nw_ref[...] * (1.0 + mod_ref[1:2, :]) + mod_ref[0:1, :]
            hs_ref[...] = h.astype(BF16)
            h_ref[...] = hs_ref[...]

        proj_ref[...] = _nt(hs_ref[...], w_ref[...])

    return pl.pallas_call(
        body, name="in_proj", grid=(s // tm, nk),
        in_specs=[pl.BlockSpec((tm, D), lambda i, k: (i, 0)), _const((1, D)), _const((8, D)),
                  pl.BlockSpec((D, D), lambda i, k: (k, 0))],
        out_specs=[pl.BlockSpec((tm, D), lambda i, k: (i, k)), pl.BlockSpec((tm, D), lambda i, k: (i, 0))],
        out_shape=[jax.ShapeDtypeStruct((s, nk * D), F32), jax.ShapeDtypeStruct((s, D), BF16)],
        scratch_shapes=[pltpu.VMEM((tm, D), BF16)],
        compiler_params=_cparams(2),
    )(x, norm_w, mod8, wt)


CONV_RC = 128
CONV_LC = 128
GW_RC = 32
SUBLANES = 8


def _shifted_taps(win_ref, weight_of, offsets, r0, lanes):
    acc = jnp.zeros((CONV_RC, CONV_LC), F32)
    for b in range(SUBLANES):
        group = [o for o in offsets if o % SUBLANES == b]
        if not group:
            continue
        rows = CONV_RC if b == 0 else CONV_RC + SUBLANES
        part = jnp.zeros((rows, CONV_LC), F32)
        for o in group:
            part = part + win_ref[pl.ds(r0 + o - b, rows), lanes] * weight_of(o)
        acc = acc + (part if b == 0 else part[b:b + CONV_RC])
    return acc


def _conv_fwd(proj, conv_w, conv_b, ln_w, ln_b):
    s = proj.shape[0]
    tm = min(256, s)
    hb = tm // HALO

    def body(av_ref, ag_ref, avh_ref, agh_ref, gate_ref, cw_ref, cb_ref, lw_ref, lb_ref, u1_ref, ya_ref, win_ref):
        i = pl.program_id(0)
        halo = avh_ref[...] * _sig(agh_ref[...])
        win_ref[0:HALO, :] = jnp.where(i > 0, halo, 0.0)
        win_ref[HALO:HALO + tm, :] = av_ref[...] * _sig(ag_ref[...])
        first = HALO - (KCONV - 1)
        for r0 in range(0, tm, CONV_RC):
            for c0 in range(0, D, CONV_LC):
                acc = _shifted_taps(win_ref, lambda o: cw_ref[o - first:o - first + 1, c0:c0 + CONV_LC],
                                    range(first, first + KCONV), r0, pl.ds(c0, CONV_LC))
                u1_ref[r0:r0 + CONV_RC, c0:c0 + CONV_LC] = acc + cb_ref[:, c0:c0 + CONV_LC]
        u1 = u1_ref[...]
        mu = jnp.mean(u1, axis=-1, keepdims=True)
        xc = u1 - mu
        var = jnp.mean(xc * xc, axis=-1, keepdims=True)
        ln = xc * lax.rsqrt(var + EPS) * lw_ref[...] + lb_ref[...]
        gate = gate_ref[...]
        ya_ref[...] = ((ln * _sig(ln)) * (gate * _sig(gate))).astype(BF16)

    row = lambda k: pl.BlockSpec((tm, D), lambda i: (i, k))
    prev = lambda k: pl.BlockSpec((HALO, D), lambda i: (jnp.maximum(i * hb - 1, 0), k))
    return pl.pallas_call(
        body, name="conv_fwd", grid=(s // tm,),
        in_specs=[row(0), row(1), prev(0), prev(1), row(2), _const((KPAD, D)), _const((1, D)), _const((1, D)), _const((1, D))],
        out_specs=[pl.BlockSpec((tm, D), lambda i: (i, 0))] * 2,
        out_shape=[jax.ShapeDtypeStruct((s, D), F32), jax.ShapeDtypeStruct((s, D), BF16)],
        scratch_shapes=[pltpu.VMEM((tm + HALO, D), F32)],
        compiler_params=_cparams(1),
    )(proj, proj, proj, proj, proj, conv_w, conv_b, ln_w, ln_b)


def _rope_tables(pos_ref, if_ref):
    ang = pos_ref[...].astype(F32) * if_ref[...]
    return jnp.cos(ang), jnp.sin(ang)


def _rms_parts(x):
    rstd = lax.rsqrt(jnp.mean(x * x, axis=-1, keepdims=True) + EPS)
    return x * rstd, rstd


def _mla_prep(proj, pos, inv_freq, qnw, kvnw, wuq, wukv):
    s = proj.shape[0]
    tm = min(512, s)

    def body(p_ref, pos_ref, if_ref, qnw_ref, kvnw_ref, wuq_ref, wukv_ref, q_ref, k_ref, v_ref):
        blk = p_ref[...]
        cos, sin = _rope_tables(pos_ref, if_ref)

        def rope(r):
            x1, x2 = r[:, :HALF], r[:, HALF:]
            return jnp.concatenate([x1 * cos - x2 * sin, x1 * sin + x2 * cos], axis=-1)

        qlat = _rms_parts(blk[:, :QL])[0] * qnw_ref[...]
        kvlat = _rms_parts(blk[:, QL:QL + KVL])[0] * kvnw_ref[...]
        q = _nn(qlat.astype(BF16), wuq_ref[...])
        kv = _nn(kvlat.astype(BF16), wukv_ref[...])
        kr = rope(blk[:, QL + KVL:MLA_COLS])
        for h in range(NH):
            qh = q[:, h * DQK:(h + 1) * DQK]
            q_ref[h] = (jnp.concatenate([qh[:, :NOPE], rope(qh[:, NOPE:])], axis=-1) * ATTN_SCALE).astype(BF16)
            k_ref[h] = jnp.concatenate([kv[:, h * 256:h * 256 + NOPE], kr], axis=-1).astype(BF16)
            v_ref[h] = kv[:, h * 256 + NOPE:(h + 1) * 256].astype(BF16)

    hm = lambda d: pl.BlockSpec((NH, tm, d), lambda i: (0, i, 0))
    return pl.pallas_call(
        body, name="mla_prep", grid=(s // tm,),
        in_specs=[pl.BlockSpec((tm, D), lambda i: (i, 3)), pl.BlockSpec((tm, 1), lambda i: (i, 0)), _const((1, HALF)),
                  _const((1, QL)), _const((1, KVL)), _const((QL, NH * DQK)), _const((KVL, NH * 256))],
        out_specs=[hm(DQK), hm(DQK), hm(DV)],
        out_shape=[jax.ShapeDtypeStruct((NH, s, DQK), BF16), jax.ShapeDtypeStruct((NH, s, DQK), BF16),
                   jax.ShapeDtypeStruct((NH, s, DV), BF16)],
        compiler_params=_cparams(1),
    )(proj, pos, inv_freq, qnw, kvnw, wuq, wukv)


ATTN_SCALE = DQK ** -0.5


def _causal_mask(s, t):
    rows = lax.broadcasted_iota(jnp.int32, (t, t), 0)
    cols = lax.broadcasted_iota(jnp.int32, (t, t), 1)
    return jnp.where(cols <= rows, s, -jnp.inf)


def _attn_tile(s):
    return min(1024, s // 2)


def _attn_fwd(q, k, v):
    nh, s, _ = q.shape
    t = _attn_tile(s)

    def body(q_ref, k_ref, v_ref, o_ref, lse_ref):
        qi = pl.program_id(1)
        qv = q_ref[0]

        def chunk(c, carry, diag):
            m, l, acc = carry
            rows = pl.ds(pl.multiple_of(c * t, t), t)
            sc = _nt(qv, k_ref[0, rows, :])
            if diag:
                sc = _causal_mask(sc, t)
            m_new = jnp.maximum(m, jnp.max(sc, axis=-1, keepdims=True))
            alpha = jnp.exp(m - m_new)
            p = jnp.exp(sc - m_new)
            l = alpha * l + jnp.sum(p, axis=-1, keepdims=True)
            acc = alpha * acc + _nn(p.astype(BF16), v_ref[0, rows, :])
            return m_new, l, acc

        init = (jnp.full((t, 1), -jnp.inf, F32), jnp.zeros((t, 1), F32), jnp.zeros((t, DV), F32))
        carry = lax.fori_loop(0, qi, lambda c, cr: chunk(c, cr, False), init)
        m, l, acc = chunk(qi, carry, True)
        o_ref[...] = acc / l
        lse_ref[0] = jnp.broadcast_to(m + jnp.log(l), (t, DV))

    head = lambda d: pl.BlockSpec((1, s, d), lambda h, i: (h, 0, 0))
    return pl.pallas_call(
        body, name="attn_fwd", grid=(nh, s // t),
        in_specs=[pl.BlockSpec((1, t, DQK), lambda h, i: (h, i, 0)), head(DQK), head(DV)],
        out_specs=[pl.BlockSpec((t, DV), lambda h, i: (i, h)), pl.BlockSpec((1, t, DV), lambda h, i: (h, i, 0))],
        out_shape=[jax.ShapeDtypeStruct((s, nh * DV), F32), jax.ShapeDtypeStruct((nh, s, DV), F32)],
        compiler_params=_cparams(2),
    )(q, k, v)


def _merge_loss(x, target, ya, o, proj, mod8, fnw, gathered):
    s = x.shape[0]
    tm = min(256, s)
    n = s // tm

    def body(x_ref, t_ref, ya_ref, o_ref, bg_ref, ga_ref, gb_ref, mod_ref, fnw_ref, wco_ref, wao_ref, wout_ref,
             dya_ref, do_ref, delta_ref, dpg_ref, dx2_ref, gw_ref, small_ref, acc_ref, cast_ref):
        i = pl.program_id(0)

        @pl.when(i == 0)
        def _():
            acc_ref[...] = jnp.zeros(acc_ref.shape, F32)
            small_ref[...] = jnp.zeros(small_ref.shape, F32)

        bg = bg_ref[...]
        sbg = _sig(bg)
        sb = bg * sbg
        ov = o_ref[...]
        ya = ya_ref[...]
        yb = (ov * sb).astype(BF16)
        square = lambda ref: ref[...].reshape(D, D)
        y_a = _nn(ya, square(wco_ref))
        y_b = _nn(yb, square(wao_ref))
        sa = _sig(ga_ref[...])
        sgb = _sig(gb_ref[...])
        merged = (sa * y_a + sgb * y_b).astype(BF16)
        z = _nn(merged, square(wout_ref))
        gate = mod_ref[2:3, :]
        x2 = x_ref[...] + gate * z
        xn, rstd = _rms_parts(x2)
        fnw = fnw_ref[...]
        err = xn * fnw - t_ref[...]
        loss = jnp.sum(jnp.sum(err * err, axis=-1, keepdims=True), axis=0, keepdims=True) * (0.5 / D)
        dy = err * (1.0 / D)
        small_ref[0:1, :] += jnp.sum(dy * xn, axis=0, keepdims=True)
        dxn = dy * fnw
        dx2 = rstd * (dxn - xn * jnp.mean(dxn * xn, axis=-1, keepdims=True))
        dx2_ref[...] = dx2
        small_ref[1:2, :] += jnp.sum(dx2 * z, axis=0, keepdims=True)
        small_ref[2:3, :] += jnp.broadcast_to(loss, (1, D))
        dz = (dx2 * gate).astype(BF16)
        dmerged = _nt(dz, square(wout_ref))
        acc_ref[2] += _tn(merged, dz)
        dy_a = (dmerged * sa).astype(BF16)
        dy_b = (dmerged * sgb).astype(BF16)
        dpg_ref[:, D:2 * D] = (dmerged * y_a * (sa * (1.0 - sa))).astype(BF16)
        dpg_ref[:, 2 * D:3 * D] = (dmerged * y_b * (sgb * (1.0 - sgb))).astype(BF16)
        dya_ref[...] = _nt(dy_a, square(wco_ref))
        acc_ref[0] += _tn(ya, dy_a)
        dyb = _nt(dy_b, square(wao_ref))
        acc_ref[1] += _tn(yb, dy_b)
        do = dyb * sb
        do_ref[...] = do.astype(BF16)
        dpg_ref[:, 0:D] = (dyb * ov * (sbg * (1.0 + bg * (1.0 - sbg)))).astype(BF16)
        prod = do * ov
        for h in range(NH):
            delta_ref[h] = jnp.broadcast_to(jnp.sum(prod[:, h * DV:(h + 1) * DV], axis=-1, keepdims=True), (tm, DV))

        @pl.when(i == n - 1)
        def _():
            for j in range(3):
                cast_ref[...] = acc_ref[j].astype(cast_ref.dtype)
                pltpu.sync_copy(cast_ref, gw_ref.at[j])

    row = pl.BlockSpec((tm, D), lambda i: (i, 0))
    col = lambda k: pl.BlockSpec((tm, D), lambda i: (i, k))
    wspec = lambda j: pl.BlockSpec((N_DEV, ROWS_SQ, D), lambda i: (0, j, 0), pipeline_mode=pl.Buffered(1))
    return pl.pallas_call(
        body, name="merge_loss", grid=(n,),
        in_specs=[row, row, row, row, col(4), col(5), col(6), _const((8, D)), _const((1, D)), wspec(0), wspec(1), wspec(2)],
        out_specs=[row, row, pl.BlockSpec((NH, tm, DV), lambda i: (0, i, 0)), pl.BlockSpec((tm, 3 * D), lambda i: (i, 0)),
                   row, ANY, _const((8, D))],
        out_shape=[jax.ShapeDtypeStruct((s, D), F32), jax.ShapeDtypeStruct((s, D), BF16),
                   jax.ShapeDtypeStruct((NH, s, DV), F32), jax.ShapeDtypeStruct((s, 3 * D), BF16),
                   jax.ShapeDtypeStruct((s, D), F32), jax.ShapeDtypeStruct((3, D, D), BF16),
                   jax.ShapeDtypeStruct((8, D), F32)],
        scratch_shapes=[pltpu.VMEM((3, D, D), F32), pltpu.VMEM((D, D), BF16)],
        compiler_params=_cparams(1),
    )(x, target, ya, o, proj, proj, proj, mod8, fnw, gathered, gathered, gathered)


def _attn_bwd(q, k, v, do, lse, delta):
    nh, s, _ = q.shape
    t = _attn_tile(s)
    nb = s // t

    def body(q_ref, k_ref, v_ref, do_ref, lse_ref, dl_ref, dq_ref, dk_ref, dv_ref):
        kj = pl.program_id(1)

        @pl.when(kj == 0)
        def _():
            dq_ref[...] = jnp.zeros(dq_ref.shape, F32)

        kv_, vv = k_ref[0], v_ref[0]

        def chunk(c, carry, diag):
            dk, dv = carry
            rows = pl.ds(pl.multiple_of(c * t, t), t)
            qv = q_ref[0, rows, :]
            dov = do_ref[rows, :]
            sc = _nt(qv, kv_)
            if diag:
                sc = _causal_mask(sc, t)
            p = jnp.exp(sc - lse_ref[0, rows, 0:1])
            dv = dv + _tn(p.astype(BF16), dov)
            dp = _nt(dov, vv)
            ds = (p * (dp - dl_ref[0, rows, 0:1])).astype(BF16)
            dk = dk + _tn(ds, qv)
            dq_ref[0, rows, :] += _nn(ds, kv_)
            return dk, dv

        carry = chunk(kj, (jnp.zeros((t, DQK), F32), jnp.zeros((t, DV), F32)), True)
        dk, dv = lax.fori_loop(kj + 1, nb, lambda c, cr: chunk(c, cr, False), carry)
        dk_ref[0] = dk
        dv_ref[0] = dv

    head = lambda d: pl.BlockSpec((1, s, d), lambda h, j: (h, 0, 0))
    blk = lambda d: pl.BlockSpec((1, t, d), lambda h, j: (h, j, 0))
    return pl.pallas_call(
        body, name="attn_bwd", grid=(nh, nb),
        in_specs=[head(DQK), blk(DQK), blk(DV), pl.BlockSpec((s, DV), lambda h, j: (0, h)), head(DV), head(DV)],
        out_specs=[head(DQK), blk(DQK), blk(DV)],
        out_shape=[jax.ShapeDtypeStruct((nh, s, DQK), F32), jax.ShapeDtypeStruct((nh, s, DQK), F32),
                   jax.ShapeDtypeStruct((nh, s, DV), F32)],
        compiler_params=_cparams(2),
    )(q, k, v, do, lse, delta)


def _mla_bwd(dq, dk, dv, proj, pos, inv_freq, qnw, kvnw, wuq, wukv):
    s = proj.shape[0]
    tm = min(512, s)

    def body(dq_ref, dk_ref, dv_ref, p_ref, pos_ref, if_ref, qnw_ref, kvnw_ref, wuq_ref, wukv_ref,
             dp_ref, guq_ref, gukv_ref, small_ref):
        @pl.when(pl.program_id(0) == 0)
        def _():
            guq_ref[...] = jnp.zeros(guq_ref.shape, F32)
            gukv_ref[...] = jnp.zeros(gukv_ref.shape, F32)
            small_ref[...] = jnp.zeros(small_ref.shape, F32)

        blk = p_ref[...]
        cos, sin = _rope_tables(pos_ref, if_ref)

        def unrope(g):
            g1, g2 = g[:, :HALF], g[:, HALF:]
            return jnp.concatenate([g1 * cos + g2 * sin, g2 * cos - g1 * sin], axis=-1)

        dq_cols, dkv_cols = [], []
        dkr = jnp.zeros((tm, ROPE), F32)
        for h in range(NH):
            dqh, dkh = dq_ref[h] * ATTN_SCALE, dk_ref[h]
            dq_cols += [dqh[:, :NOPE], unrope(dqh[:, NOPE:])]
            dkv_cols += [dkh[:, :NOPE], dv_ref[h]]
            dkr = dkr + dkh[:, NOPE:]
        dq_full = jnp.concatenate(dq_cols, axis=-1).astype(BF16)
        dkv_full = jnp.concatenate(dkv_cols, axis=-1).astype(BF16)

        def latent_bwd(c, nw_ref, d_up, w_ref, g_ref, srow):
            nrm, rstd = _rms_parts(c)
            nw = nw_ref[...]
            lat = (nrm * nw).astype(BF16)
            g_ref[...] += _tn(lat, d_up)
            dlat = _nt(d_up, w_ref[...])
            small_ref[srow:srow + 1, :] += jnp.sum(dlat * nrm, axis=0, keepdims=True)
            dn = dlat * nw
            return rstd * (dn - nrm * jnp.mean(dn * nrm, axis=-1, keepdims=True))

        dcq = latent_bwd(blk[:, :QL], qnw_ref, dq_full, wuq_ref, guq_ref, 0)
        dckv = latent_bwd(blk[:, QL:QL + KVL], kvnw_ref, dkv_full, wukv_ref, gukv_ref, 1)
        dp_ref[...] = jnp.concatenate([dcq, dckv, unrope(dkr), jnp.zeros((tm, D - MLA_COLS), F32)], axis=-1).astype(BF16)

    hm = lambda d: pl.BlockSpec((NH, tm, d), lambda i: (0, i, 0))
    return pl.pallas_call(
        body, name="mla_bwd", grid=(s // tm,),
        in_specs=[hm(DQK), hm(DQK), hm(DV), pl.BlockSpec((tm, D), lambda i: (i, 3)), pl.BlockSpec((tm, 1), lambda i: (i, 0)),
                  _const((1, HALF)), _const((1, QL)), _const((1, KVL)), _const((QL, NH * DQK)), _const((KVL, NH * 256))],
        out_specs=[pl.BlockSpec((tm, D), lambda i: (i, 0)), _const((QL, NH * DQK)), _const((KVL, NH * 256)), _const((8, QL))],
        out_shape=[jax.ShapeDtypeStruct((s, D), BF16), jax.ShapeDtypeStruct((QL, NH * DQK), F32),
                   jax.ShapeDtypeStruct((KVL, NH * 256), F32), jax.ShapeDtypeStruct((8, QL), F32)],
        compiler_params=_cparams(1),
    )(dq, dk, dv, proj, pos, inv_freq, qnw, kvnw, wuq, wukv)


def _conv_rows_bwd(dya, u1, proj, ln_w, ln_b):
    s = dya.shape[0]
    tm = min(512, s)

    def body(dya_ref, u1_ref, gate_ref, lw_ref, lb_ref, du1_ref, dag_ref, small_ref):
        @pl.when(pl.program_id(0) == 0)
        def _():
            small_ref[...] = jnp.zeros(small_ref.shape, F32)

        u1 = u1_ref[...]
        mu = jnp.mean(u1, axis=-1, keepdims=True)
        xc = u1 - mu
        rstd = lax.rsqrt(jnp.mean(xc * xc, axis=-1, keepdims=True) + EPS)
        xhat = xc * rstd
        lw = lw_ref[...]
        ln = xhat * lw + lb_ref[...]
        sl = _sig(ln)
        u2 = ln * sl
        gate = gate_ref[...]
        sg = _sig(gate)
        dya = dya_ref[...]
        dag_ref[...] = (dya * u2 * (sg * (1.0 + gate * (1.0 - sg)))).astype(BF16)
        dln = dya * (gate * sg) * (sl * (1.0 + ln * (1.0 - sl)))
        small_ref[0:1, :] += jnp.sum(dln * xhat, axis=0, keepdims=True)
        small_ref[1:2, :] += jnp.sum(dln, axis=0, keepdims=True)
        dxh = dln * lw
        du1_ref[...] = rstd * (dxh - jnp.mean(dxh, axis=-1, keepdims=True) - xhat * jnp.mean(dxh * xhat, axis=-1, keepdims=True))

    row = pl.BlockSpec((tm, D), lambda i: (i, 0))
    return pl.pallas_call(
        body, name="conv_rows_bwd", grid=(s // tm,),
        in_specs=[row, row, pl.BlockSpec((tm, D), lambda i: (i, 2)), _const((1, D)), _const((1, D))],
        out_specs=[row, row, _const((8, D))],
        out_shape=[jax.ShapeDtypeStruct((s, D), F32), jax.ShapeDtypeStruct((s, D), BF16), jax.ShapeDtypeStruct((8, D), F32)],
        compiler_params=_cparams(1),
    )(dya, u1, proj, ln_w, ln_b)


def _conv_bwd(du1, proj, conv_w):
    s = du1.shape[0]
    tm = min(256, s)
    hb = tm // HALO
    n = s // tm
    last32 = s // HALO - 1

    def body(d_ref, dn_ref, av_ref, ag_ref, avh_ref, agh_ref, cw_ref, dp_ref, gcw_ref, small_ref,
             dwin_ref, dpad_ref, dsh_ref, uwin_ref, acc_ref):
        i = pl.program_id(0)

        @pl.when(i == 0)
        def _():
            acc_ref[...] = jnp.zeros(acc_ref.shape, F32)
            small_ref[...] = jnp.zeros(small_ref.shape, F32)
            dpad_ref[...] = jnp.zeros(dpad_ref.shape, F32)
            uwin_ref[...] = jnp.zeros(uwin_ref.shape, F32)

        dv = d_ref[...]
        dwin_ref[0:tm, :] = dv
        dwin_ref[tm:tm + HALO, :] = jnp.where(i < n - 1, dn_ref[...], 0.0)
        dpad_ref[SUBLANES:SUBLANES + tm, :] = dv
        halo = avh_ref[...] * _sig(agh_ref[...])
        uwin_ref[0:HALO, :] = jnp.where(i > 0, halo, 0.0)
        av = av_ref[...]
        sg = _sig(ag_ref[...])
        uwin_ref[HALO:HALO + tm, :] = av * sg
        small_ref[0:1, :] += jnp.sum(dv, axis=0, keepdims=True)

        for b in range(SUBLANES):
            dsh_ref[b] = dpad_ref[pl.ds(SUBLANES - b, tm + SUBLANES), :]

        first = HALO - (KCONV - 1)
        for c0 in range(0, D, CONV_LC):
            lanes = pl.ds(c0, CONV_LC)
            for r0 in range(0, tm, CONV_RC):
                acc = _shifted_taps(dwin_ref, lambda o: cw_ref[KCONV - 1 - o:KCONV - o, c0:c0 + CONV_LC], range(KCONV), r0, lanes)
                a = av[r0:r0 + CONV_RC, c0:c0 + CONV_LC]
                g = sg[r0:r0 + CONV_RC, c0:c0 + CONV_LC]
                dp_ref[r0:r0 + CONV_RC, c0:c0 + CONV_LC] = (acc * g).astype(BF16)
                dp_ref[r0:r0 + CONV_RC, D + c0:D + c0 + CONV_LC] = (acc * a * (g * (1.0 - g))).astype(BF16)
            for b in range(SUBLANES):
                group = [o for o in range(first, first + KCONV) if o % SUBLANES == b]
                parts = [jnp.zeros((SUBLANES, CONV_LC), F32) for _ in group]
                chunks = [(i0, GW_RC) for i0 in range(0, tm, GW_RC)] + ([(tm, SUBLANES)] if b else [])
                for i0, rows in chunks:
                    dsh = dsh_ref[b, pl.ds(i0, rows), lanes]
                    for n_, o in enumerate(group):
                        prod = dsh * uwin_ref[pl.ds(i0 + o - b, rows), lanes]
                        parts[n_] = parts[n_] + jnp.sum(prod.reshape(rows // SUBLANES, SUBLANES, CONV_LC), axis=0)
                for n_, o in enumerate(group):
                    acc_ref[o - first, :, c0:c0 + CONV_LC] += parts[n_]

        @pl.when(i == n - 1)
        def _():
            gcw_ref[...] = jnp.sum(acc_ref[...], axis=1)

    row = lambda k: pl.BlockSpec((tm, D), lambda i: (i, k))
    prev = lambda k: pl.BlockSpec((HALO, D), lambda i: (jnp.maximum(i * hb - 1, 0), k))
    return pl.pallas_call(
        body, name="conv_bwd", grid=(n,),
        in_specs=[row(0), pl.BlockSpec((HALO, D), lambda i: (jnp.minimum((i + 1) * hb, last32), 0)),
                  row(0), row(1), prev(0), prev(1), _const((KPAD, D))],
        out_specs=[pl.BlockSpec((tm, 2 * D), lambda i: (i, 0)), _const((KPAD, D)), _const((8, D))],
        out_shape=[jax.ShapeDtypeStruct((s, 2 * D), BF16), jax.ShapeDtypeStruct((KPAD, D), F32), jax.ShapeDtypeStruct((8, D), F32)],
        scratch_shapes=[pltpu.VMEM((tm + HALO, D), F32), pltpu.VMEM((tm + 2 * SUBLANES, D), F32),
                        pltpu.VMEM((SUBLANES, tm + SUBLANES, D), F32),
                        pltpu.VMEM((tm + HALO + SUBLANES, D), F32), pltpu.VMEM((KPAD, SUBLANES, D), F32)],
        compiler_params=_cparams(1),
    )(du1, du1, proj, proj, proj, proj, conv_w)


def _dproj_specs(tm, rows_first):
    def spec(lo, hi):
        def idx(a, b):
            i, k = (a, b) if rows_first else (b, a)
            col = jnp.clip(k - lo, 0, hi - lo - 1)
            if rows_first:
                return (i, col)
            return (jnp.where((k >= lo) & (k < hi), i, 0), col)
        return pl.BlockSpec((tm, D), idx)
    return [spec(0, 2), spec(2, 3), spec(3, 4), spec(4, 7)]


def _pick_dproj(k, refs, fn):
    vg, ag, mla, gates = refs

    @pl.when(k < 2)
    def _():
        fn(vg)

    @pl.when(k == 2)
    def _():
        fn(ag)

    @pl.when(k == 3)
    def _():
        fn(mla)

    @pl.when(k > 3)
    def _():
        fn(gates)


def _in_proj_bwd_x(dps, wt, x, dx2, norm_w, mod8):
    s = x.shape[0]
    tm = min(1024, s)
    nk = wt.shape[0] // D

    def body(vg_ref, ag_ref, mla_ref, g_ref, w_ref, x_ref, dx2_ref, nw_ref, mod_ref, gx_ref, small_ref, acc_ref):
        i, k = pl.program_id(0), pl.program_id(1)

        @pl.when((i == 0) & (k == 0))
        def _():
            small_ref[...] = jnp.zeros(small_ref.shape, F32)

        @pl.when(k == 0)
        def _():
            acc_ref[...] = jnp.zeros(acc_ref.shape, F32)

        def add(ref):
            acc_ref[...] += _nn(ref[...], w_ref[...])

        _pick_dproj(k, (vg_ref, ag_ref, mla_ref, g_ref), add)

        @pl.when(k == nk - 1)
        def _():
            dh = acc_ref[...]
            xn, rstd = _rms_parts(x_ref[...])
            nw = nw_ref[...]
            hn = xn * nw
            small_ref[0:1, :] += jnp.sum(dh, axis=0, keepdims=True)
            small_ref[1:2, :] += jnp.sum(dh * hn, axis=0, keepdims=True)
            dhn = dh * (1.0 + mod_ref[1:2, :])
            small_ref[2:3, :] += jnp.sum(dhn * xn, axis=0, keepdims=True)
            dxn = dhn * nw
            gx_ref[...] = rstd * (dxn - xn * jnp.mean(dxn * xn, axis=-1, keepdims=True)) + dx2_ref[...]

    row = pl.BlockSpec((tm, D), lambda i, k: (i, 0))
    return pl.pallas_call(
        body, name="in_proj_bwd_x", grid=(s // tm, nk),
        in_specs=_dproj_specs(tm, True) + [pl.BlockSpec((D, D), lambda i, k: (k, 0)), row, row, _const((1, D)), _const((8, D))],
        out_specs=[row, _const((8, D))],
        out_shape=[jax.ShapeDtypeStruct((s, D), F32), jax.ShapeDtypeStruct((8, D), F32)],
        scratch_shapes=[pltpu.VMEM((tm, D), F32)],
        compiler_params=_cparams(2),
    )(*dps, wt, x, dx2, norm_w, mod8)


def _in_proj_bwd_w(dps, h, nk):
    s = h.shape[0]
    tm = min(1024, s)

    n = s // tm

    def body(vg_ref, ag_ref, mla_ref, g_ref, h_ref, gw_ref, acc_ref):
        k, i = pl.program_id(0), pl.program_id(1)

        @pl.when(i == 0)
        def _():
            acc_ref[...] = jnp.zeros(acc_ref.shape, F32)

        def add(ref):
            acc_ref[...] += _tn(ref[...], h_ref[...])

        _pick_dproj(k, (vg_ref, ag_ref, mla_ref, g_ref), add)

        @pl.when(i == n - 1)
        def _():
            gw_ref[...] = acc_ref[...].astype(gw_ref.dtype)

    return pl.pallas_call(
        body, name="in_proj_bwd_w", grid=(nk, n),
        in_specs=_dproj_specs(tm, False) + [pl.BlockSpec((tm, D), lambda k, i: (i, 0))],
        out_specs=pl.BlockSpec((D, D), lambda k, i: (k, 0)),
        out_shape=jax.ShapeDtypeStruct((nk * D, D), BF16),
        scratch_shapes=[pltpu.VMEM((D, D), F32)],
        compiler_params=_cparams(2),
    )(*dps, h)


def _small_slab(wuq, wukv, conv_w):
    cw = jnp.pad(conv_w.reshape(-1), (0, (ROWS_TAIL - ROWS_UQ - ROWS_UKV) * D - KCONV * 128)).reshape(-1, D)
    return jnp.concatenate([wuq.reshape(ROWS_UQ, D), wukv.reshape(ROWS_UKV, D), cw], axis=0)


def _split_small_slab(slab):
    return (slab[..., :ROWS_UQ, :], slab[..., ROWS_UQ:ROWS_UQ + ROWS_UKV, :],
            slab[..., ROWS_UQ + ROWS_UKV:ROWS_UQ + ROWS_UKV + ROWS_CW, :])


def _unpack_small_slab(slab):
    wuq, wukv, cw = _split_small_slab(slab)
    return (wuq.reshape(QL, NH * DQK // N_DEV), wukv.reshape(KVL, NH * 256 // N_DEV),
            cw.reshape(-1)[:KCONV * 128].reshape(KCONV, 128))


def _pack_shard(w_in, wco, wao, wout, wuq, wukv, conv_w):
    bf = lambda a: a.astype(BF16)
    first = jnp.concatenate([jnp.pad(bf(w_in).T, ((0, ROWS_IN_PAD - ROWS_IN), (0, 0))), bf(_small_slab(wuq, wukv, conv_w))], axis=0)
    return first, jnp.concatenate([bf(wco), bf(wao), bf(wout)], axis=0)


def _unpack_gathered(g):
    wt = g[:, :ROWS_IN].reshape(IN_COLS, D)
    split = 3 * D + MLA_COLS
    wt = jnp.concatenate([wt[:split], jnp.zeros((D - MLA_COLS, D), g.dtype), wt[split:]], axis=0)
    wuq, wukv, cw = _split_small_slab(g[:, OFF_TAIL:OFF_TAIL + ROWS_TAIL])
    wuq = wuq.reshape(N_DEV, QL, NH * DQK // N_DEV).transpose(1, 0, 2).reshape(QL, NH * DQK)
    wukv = wukv.reshape(N_DEV, KVL, NH * 256 // N_DEV).transpose(1, 0, 2).reshape(KVL, NH * 256)
    cw = cw.reshape(N_DEV, ROWS_CW * D)[:, :KCONV * 128].reshape(N_DEV, KCONV, 128).transpose(1, 0, 2).reshape(KCONV, D)
    return wt, wuq, wukv, cw


def _pack_grads(gwt, gw3, guq, gukv, gcw):
    split = 3 * D + MLA_COLS
    g_in = jnp.concatenate([gwt[:split], gwt[4 * D:]], axis=0).reshape(N_DEV, ROWS_IN, D)
    g_in = jnp.pad(g_in, ((0, 0), (0, ROWS_IN_PAD - ROWS_IN), (0, 0)))
    guq = guq.reshape(QL, N_DEV, -1).transpose(1, 0, 2).reshape(N_DEV, ROWS_UQ, D)
    gukv = gukv.reshape(KVL, N_DEV, -1).transpose(1, 0, 2).reshape(N_DEV, ROWS_UKV, D)
    gcw = gcw.reshape(KCONV, N_DEV, 128).transpose(1, 0, 2).reshape(N_DEV, KCONV * 128)
    gcw = jnp.pad(gcw, ((0, 0), (0, (ROWS_TAIL - ROWS_UQ - ROWS_UKV) * D - KCONV * 128))).reshape(N_DEV, -1, D)
    gsq = gw3.reshape(3, N_DEV, ROWS_SQ, D).transpose(1, 0, 2, 3).reshape(N_DEV, 3 * ROWS_SQ, D)
    slabs = jnp.concatenate([a.astype(BF16) for a in (g_in, guq, gukv, gcw, gsq)], axis=1)
    return slabs.reshape(N_CHIP, 2, ROWS_PACK, D).transpose(1, 0, 2, 3)


def _pack_small(vecs):
    flat = jnp.concatenate([v.reshape(-1) for v in vecs])
    return jnp.pad(flat, (0, SMALL_LEN - flat.shape[0])).reshape(8, SMALL_COLS)


def _unpack_small(a, shapes):
    flat = a.reshape(-1)
    out, off = [], 0
    for shp, n in zip(shapes, SMALL_SIZES):
        out.append(flat[off:off + n].reshape(shp))
        off += n
    return out


def kernel(x, c, positions, w_ada, b_ada, norm_w, w_in, conv_w, conv_b, conv_ln_w, conv_ln_b, w_conv_out, q_norm_w, w_uq, kv_norm_w, w_ukv, w_attn_out, w_out, final_norm_w, loss_target, m_w_ada, m_b_ada, m_norm_w, m_w_in, m_conv_w, m_conv_b, m_conv_ln_w, m_conv_ln_b, m_w_conv_out, m_q_norm_w, m_w_uq, m_kv_norm_w, m_w_ukv, m_w_attn_out, m_w_out, m_final_norm_w, v_w_ada, v_b_ada, v_norm_w, v_w_in, v_conv_w, v_conv_b, v_conv_ln_w, v_conv_ln_b, v_w_conv_out, v_q_norm_w, v_w_uq, v_kv_norm_w, v_w_ukv, v_w_attn_out, v_w_out, v_final_norm_w):
    me = 4 * lax.axis_index("x") + 2 * lax.axis_index("y") + lax.axis_index("c")
    xs, tgt = x[0], loss_target[0]
    s = xs.shape[0]
    ada_cols = w_ada.shape[2]

    sharded = lambda t: tuple(a[0] for a in t)
    slab_first, slab_sq = _pack_shard(*sharded((w_in, w_conv_out, w_attn_out, w_out, w_uq, w_ukv, conv_w)))
    wt, wuq, wukv, cw = _unpack_gathered(_all_gather(slab_first, "gather_weights"))
    cw32 = jnp.pad(cw.astype(F32), ((0, KPAD - KCONV), (0, 0)))

    c_all = _all_gather(jnp.broadcast_to(c, (8, D)), "gather_c")[:, 0, :]
    b_cols = lax.dynamic_slice(b_ada, (0, me * ada_cols), (1, ada_cols))
    mod_cols = _all_gather(_ada_mod(c_all, w_ada[0], b_cols), "gather_mod")
    mod = lax.dynamic_index_in_dim(mod_cols, me, axis=1, keepdims=False).reshape(3, D)
    mod8 = jnp.pad(mod, ((0, 5), (0, 0)))

    slab_sq, mod8 = lax.optimization_barrier((slab_sq, mod8))
    sq_send, sq_recv, sq_src, sq_land, sq_token = _exchange_start(slab_sq, (N_DEV,) + slab_sq.shape, _direct_copies, N_DEV - 1,
                                                                  "square_gather_start")

    pos = positions.reshape(s, 1)
    inv_freq = (ROPE_THETA ** (-jnp.arange(0, ROPE, 2, dtype=F32) / ROPE)).reshape(1, HALF)
    proj, h = _in_proj(xs, norm_w + sq_token[0:1, 0:1], mod8, wt)
    u1, ya = _conv_fwd(proj, cw32, conv_b, conv_ln_w, conv_ln_b)
    q, k, v = _mla_prep(proj, pos, inv_freq, q_norm_w, kv_norm_w, wuq, wukv)
    o, lse = _attn_fwd(q, k, v)

    slab_sq, gathered_sq = _exchange_wait(sq_send, sq_recv, sq_src, sq_land, lse, _direct_copies, "square_gather_wait")
    gathered_sq = lax.dynamic_update_slice(gathered_sq, slab_sq[None], (me, 0, 0))
    dya, do, delta, dp_gates, dx2, gw3, small_a = _merge_loss(xs, tgt, ya, o, proj, mod8, final_norm_w.reshape(1, D), gathered_sq)
    dq, dk, dv = _attn_bwd(q, k, v, do, lse, delta)
    dp_mla, guq, gukv, small_b = _mla_bwd(dq, dk, dv, proj, pos, inv_freq, q_norm_w, kv_norm_w, wuq, wukv)
    du1, dp_ag, small_c = _conv_rows_bwd(dya, u1, proj, conv_ln_w, conv_ln_b)
    dp_vg, gcw, small_d = _conv_bwd(du1, proj, cw32)
    dps = [dp_vg, dp_ag, dp_mla, dp_gates]
    gwt = _in_proj_bwd_w(dps, h, wt.shape[0] // D)

    packed = _pack_grads(gwt, gw3, guq, gukv, gcw[:KCONV])
    core = lax.axis_index("c").astype(jnp.int32).reshape(1)
    half = _pair_add(core, packed, _pair_exchange(packed), ROWS_PACK // 4)
    send_sems, recv_sems, half_thru, land_thru, token = _exchange_start(half, half.shape, _chip_copies, N_CHIP - 1, "chip_exchange_start")
    grad_x, small_e = _in_proj_bwd_x(dps, wt, xs, dx2, norm_w + token[0:1, 0:1], mod8)
    half, recv = _exchange_wait(send_sems, recv_sems, half_thru, land_thru, small_e, _chip_copies, "chip_exchange_wait")
    chip = 2 * lax.axis_index("x") + lax.axis_index("y")
    recv = lax.dynamic_update_slice(recv, lax.dynamic_slice(half, (chip, 0, 0), (1,) + half.shape[1:]), (chip, 0, 0))
    big_in = _adam_w_in(recv, w_in[0], m_w_in[0], v_w_in[0])
    squares = (("w_conv_out", w_conv_out, m_w_conv_out, v_w_conv_out), ("w_attn_out", w_attn_out, m_w_attn_out, v_w_attn_out),
               ("w_out", w_out, m_w_out, v_w_out))
    big_sq = [_reduce_adam(recv, w[0], m[0], v[0], "adam_" + nm, ROWS_SQ, OFF_SQ // ROWS_SQ + j) for j, (nm, w, m, v) in enumerate(squares)]
    tail = _reduce_adam(recv, _small_slab(w_uq[0], w_ukv[0], conv_w[0]), _small_slab(m_w_uq[0], m_w_ukv[0], m_conv_w[0]),
                        _small_slab(v_w_uq[0], v_w_ukv[0], v_conv_w[0]), "adam_small_sharded", ROWS_TAIL, OFF_TAIL // ROWS_TAIL)
    tail = [_unpack_small_slab(a) for a in tail]
    big = [(big_in[i], big_sq[0][i], big_sq[1][i], big_sq[2][i], *tail[i]) for i in range(4)]

    dmod = jnp.concatenate([small_e[0], small_e[1], small_a[1]])
    payload = _pack_small([dmod, small_e[2], small_d[0], small_c[0], small_c[1], small_a[0], small_b[0], small_b[1], small_a[2, 0:1]])
    pay_all = _all_gather(payload, "gather_small")
    small_w = (b_ada, norm_w, conv_b, conv_ln_w, conv_ln_b, final_norm_w, q_norm_w, kv_norm_w)
    small_m = (m_b_ada, m_norm_w, m_conv_b, m_conv_ln_w, m_conv_ln_b, m_final_norm_w, m_q_norm_w, m_kv_norm_w)
    small_v = (v_b_ada, v_norm_w, v_conv_b, v_conv_ln_w, v_conv_ln_b, v_final_norm_w, v_q_norm_w, v_kv_norm_w)
    sm = _reduce_adam(pay_all, _pack_small(small_w), _pack_small(small_m), _pack_small(small_v), "adam_replicated", 8, 0)
    loss = sm[0].reshape(-1)[sum(SMALL_SIZES)]
    shapes = [t.shape for t in small_w]
    sm = [_unpack_small(a, shapes) for a in sm]

    dmod_all = pay_all.reshape(N_DEV, SMALL_LEN)[:, :3 * D]
    dmod_cols = lax.dynamic_slice(dmod_all, (0, me * ada_cols), (N_DEV, ada_cols))
    ada = _ada_bwd(c_all, dmod_cols, w_ada[0], m_w_ada[0], v_w_ada[0])

    def group(i):
        b_in, b_co, b_ao, b_out, b_uq, b_ukv, b_cw = big[i]
        s_bada, s_nw, s_cb, s_clw, s_clb, s_fnw, s_qnw, s_kvnw = sm[i]
        return (ada[i][None], s_bada, s_nw, b_in[None], b_cw[None], s_cb, s_clw, s_clb, b_co[None], s_qnw, b_uq[None], s_kvnw,
                b_ukv[None], b_ao[None], b_out[None], s_fnw)

    return (loss, grad_x[None], *group(0), *group(1), *group(2), *group(3))
```

```python
import functools

import jax
import jax.numpy as jnp
from jax import lax
from jax.experimental import pallas as pl
from jax.experimental.pallas import tpu as pltpu

F32 = jnp.float32
BF16 = jnp.bfloat16

D = 1024
NH = 8
NOPE = 128
ROPE = 64
HALF = ROPE // 2
DQK = NOPE + ROPE
DV = 128
QL = 256
KVL = 256
KCONV = 31
KPAD = 32
HALO = 32
IN_COLS = 6720
MLA_COLS = QL + KVL + ROPE
PROJ_COLS = 7 * D
EPS = 1e-6
ROPE_THETA = 10000.0
N_DEV = 8

ADAM_LR = 0.001
ADAM_B1 = 0.9
ADAM_B2 = 0.999
ADAM_EPS = 1e-08
ADAM_WD = 0.01
ADAM_STEP = 10

ROWS_IN = 840
ROWS_SQ = 128
ROWS_UQ = 48
ROWS_UKV = 64
ROWS_CW = 4
ROWS_IN_PAD = 896
OFF_TAIL = ROWS_IN_PAD
ROWS_TAIL = 128
OFF_SQ = OFF_TAIL + ROWS_TAIL
ROWS_PACK = OFF_SQ + 3 * ROWS_SQ
SMALL_SIZES = (3 * D, D, D, D, D, D, QL, KVL)
SMALL_COLS = 1152
SMALL_LEN = 8 * SMALL_COLS

MESH = pl.DeviceIdType.MESH
ANY = pl.BlockSpec(memory_space=pl.ANY)
V7X_VMEM_LIMIT = 56 * 1024 * 1024


def _cparams(n_axes, vmem=V7X_VMEM_LIMIT):
    return pltpu.CompilerParams(dimension_semantics=("arbitrary",) * n_axes, vmem_limit_bytes=vmem)


def _sig(x):
    return jax.nn.sigmoid(x)


def _nt(a, b):
    return lax.dot_general(a, b, (((1,), (1,)), ((), ())), preferred_element_type=F32)


def _tn(a, b):
    return lax.dot_general(a, b, (((0,), (0,)), ((), ())), preferred_element_type=F32)


def _nn(a, b):
    return jnp.dot(a, b, preferred_element_type=F32)


def _const(shape):
    return pl.BlockSpec(shape, lambda *_: (0,) * len(shape))


def _all_gather(block, name):
    r, c = block.shape

    def body(x_ref, out_ref, send_sems, recv_sems, local_sem):
        x, y, cc = lax.axis_index("x"), lax.axis_index("y"), lax.axis_index("c")
        me, sibling = (x, y, cc), (x, y, 1 - cc)
        chips = [(1 - x, y), (x, 1 - y), (1 - x, 1 - y)]

        def slot(px, py, pc):
            return out_ref.at[4 * px + 2 * py + pc]

        def copy(k, blk, to, src=None):
            return pltpu.make_async_remote_copy(
                src_ref=slot(*blk) if src is None else src, dst_ref=slot(*blk),
                send_sem=send_sems.at[k], recv_sem=recv_sems.at[k],
                device_id=to, device_id_type=MESH)

        mine = pltpu.make_async_copy(x_ref, slot(*me), local_sem)
        mine.start()
        first = [copy(0, me, sibling, src=x_ref)]
        first += [copy(1 + j, me, (*chip, cc), src=x_ref) for j, chip in enumerate(chips)]
        for cp in first:
            cp.start()
        passed = [copy(4 + j, (*chip, cc), sibling) for j, chip in enumerate(chips)]
        for j, chip in enumerate(chips):
            copy(1 + j, (*chip, cc), me).wait_recv()
            passed[j].start()
        copy(0, sibling, me).wait_recv()
        for j, chip in enumerate(chips):
            copy(4 + j, (*chip, 1 - cc), me).wait_recv()
        for cp in first + passed:
            cp.wait_send()
        mine.wait()

    return pl.pallas_call(
        body, name=name,
        out_shape=jax.ShapeDtypeStruct((N_DEV, r, c), block.dtype),
        in_specs=[ANY], out_specs=ANY,
        scratch_shapes=[pltpu.SemaphoreType.DMA((7,)), pltpu.SemaphoreType.DMA((7,)), pltpu.SemaphoreType.DMA],
    )(block)


N_CHIP = 4


def _pair_exchange(packed):
    _, _, r, c = packed.shape

    def body(src_ref, out_ref, send_sem, recv_sem):
        x, y, cc = lax.axis_index("x"), lax.axis_index("y"), lax.axis_index("c")
        cp = pltpu.make_async_remote_copy(
            src_ref=src_ref.at[1 - cc], dst_ref=out_ref, send_sem=send_sem, recv_sem=recv_sem,
            device_id=(x, y, 1 - cc), device_id_type=MESH)
        cp.start()
        cp.wait()

    return pl.pallas_call(
        body, name="pair_exchange",
        out_shape=jax.ShapeDtypeStruct((N_CHIP, r, c), packed.dtype),
        in_specs=[ANY], out_specs=ANY,
        scratch_shapes=[pltpu.SemaphoreType.DMA, pltpu.SemaphoreType.DMA],
    )(packed)


def _pair_add(core, packed, got, tr):
    _, _, r, c = packed.shape

    def body(core_ref, own_ref, got_ref, out_ref):
        out_ref[...] = (own_ref[0].astype(F32) + got_ref[...].astype(F32)).astype(out_ref.dtype)

    blk = pl.BlockSpec((1, tr, c), lambda j, i, core_ref: (j, i, 0))
    return pl.pallas_call(
        body, name="pair_add",
        grid_spec=pltpu.PrefetchScalarGridSpec(
            num_scalar_prefetch=1, grid=(N_CHIP, r // tr),
            in_specs=[pl.BlockSpec((1, 1, tr, c), lambda j, i, core_ref: (core_ref[0], j, i, 0)), blk],
            out_specs=blk),
        out_shape=jax.ShapeDtypeStruct((N_CHIP, r, c), packed.dtype),
        compiler_params=_cparams(2),
    )(core, packed, got)


HBM = pl.BlockSpec(memory_space=pltpu.HBM)
SEM = pl.BlockSpec(memory_space=pltpu.SEMAPHORE)
EFFECT = pltpu.SideEffectType.DATAFLOW_SIDE_EFFECTING


def _chip_copies(src_ref, land_ref, send_sems, recv_sems):
    x, y, cc = lax.axis_index("x"), lax.axis_index("y"), lax.axis_index("c")
    me = 2 * x + y
    copies = []
    for k in range(1, N_CHIP):
        px, py = (1 - x if k & 2 else x), (1 - y if k & 1 else y)
        copies.append(pltpu.make_async_remote_copy(
            src_ref=src_ref.at[2 * px + py], dst_ref=land_ref.at[me],
            send_sem=send_sems.at[k - 1], recv_sem=recv_sems.at[k - 1],
            device_id=(px, py, cc), device_id_type=MESH))
    return copies


def _direct_copies(src_ref, land_ref, send_sems, recv_sems):
    x, y, cc = lax.axis_index("x"), lax.axis_index("y"), lax.axis_index("c")
    me = 4 * x + 2 * y + cc
    copies = []
    for k in range(1, N_DEV):
        peer = ((1 - x if k & 4 else x), (1 - y if k & 2 else y), (1 - cc if k & 1 else cc))
        copies.append(pltpu.make_async_remote_copy(
            src_ref=src_ref, dst_ref=land_ref.at[me], send_sem=send_sems.at[k - 1], recv_sem=recv_sems.at[k - 1],
            device_id=peer, device_id_type=MESH))
    return copies


def _exchange_start(src, land_shape, copies_of, n_copies, name):
    def body(src_ref, land_ref, send_sems, recv_sems, src_thru, land_thru, token):
        for cp in copies_of(src_ref, land_ref, send_sems, recv_sems):
            cp.start()
        token[...] = jnp.zeros_like(token)

    return pl.pallas_call(
        body, name=name,
        out_shape=(pltpu.SemaphoreType.DMA((n_copies,)), pltpu.SemaphoreType.DMA((n_copies,)),
                   pltpu.HBM(src.shape, src.dtype), pltpu.HBM(land_shape, src.dtype), jax.ShapeDtypeStruct((8, 128), F32)),
        in_specs=(HBM, HBM), out_specs=(SEM, SEM, HBM, HBM, pl.BlockSpec(memory_space=pltpu.VMEM)),
        input_output_aliases={0: 2, 1: 3},
        compiler_params=pltpu.CompilerParams(has_side_effects=EFFECT),
    )(pltpu.with_memory_space_constraint(src, pltpu.HBM),
      pltpu.with_memory_space_constraint(lax.empty(land_shape, src.dtype), pltpu.HBM))


def _exchange_wait(send_sems, recv_sems, src_thru, land_thru, after, copies_of, name):
    def body(src_ref, land_ref, send_sems, recv_sems, after_ref, src_dead, got_ref):
        copies = copies_of(src_ref, land_ref, send_sems, recv_sems)
        for cp in copies:
            cp.wait_send()
        for cp in copies:
            cp.wait_recv()

    return pl.pallas_call(
        body, name=name,
        out_shape=(pltpu.HBM(src_thru.shape, src_thru.dtype), pltpu.HBM(land_thru.shape, land_thru.dtype)),
        in_specs=(HBM, HBM, SEM, SEM, ANY), out_specs=(HBM, HBM), input_output_aliases={0: 0, 1: 1},
        compiler_params=pltpu.CompilerParams(has_side_effects=EFFECT),
    )(src_thru, land_thru, send_sems, recv_sems, after)


def _adam(g, w, m, v):
    m = ADAM_B1 * m + (1.0 - ADAM_B1) * g
    v = ADAM_B2 * v + (1.0 - ADAM_B2) * (g * g)
    m_hat = m / (1.0 - ADAM_B1 ** ADAM_STEP)
    v_hat = v / (1.0 - ADAM_B2 ** ADAM_STEP)
    delta = -ADAM_LR * (m_hat / (jnp.sqrt(v_hat) + ADAM_EPS) + ADAM_WD * w)
    return delta, m, v


def _adam_w_in(parts, w, m, v):
    n = parts.shape[0]
    tc = 128

    def body(p_ref, w_ref, m_ref, v_ref, g_out, d_out, m_out, v_out):
        gt = p_ref[0].astype(F32)
        for j in range(1, n):
            gt = gt + p_ref[j].astype(F32)
        g = gt.T[:, :ROWS_IN]
        delta, nm, nv = _adam(g, w_ref[...], m_ref[...], v_ref[...])
        g_out[...] = g
        d_out[...] = delta
        m_out[...] = nm
        v_out[...] = nv

    row = pl.BlockSpec((tc, ROWS_IN), lambda i: (i, 0))
    return pl.pallas_call(
        body, name="adam_w_in", grid=(D // tc,),
        in_specs=[pl.BlockSpec((n, ROWS_IN_PAD, tc), lambda i: (0, 0, i)), row, row, row],
        out_specs=[row] * 4,
        out_shape=[jax.ShapeDtypeStruct((D, ROWS_IN), F32)] * 4,
        compiler_params=_cparams(1),
    )(parts, w, m, v)


def _reduce_adam(parts, w, m, v, name, tr, first_block):
    n, _, c = parts.shape
    r = w.shape[0]

    def body(p_ref, w_ref, m_ref, v_ref, g_out, d_out, m_out, v_out):
        g = p_ref[0].astype(F32)
        for j in range(1, n):
            g = g + p_ref[j].astype(F32)
        delta, nm, nv = _adam(g, w_ref[...], m_ref[...], v_ref[...])
        g_out[...] = g
        d_out[...] = delta
        m_out[...] = nm
        v_out[...] = nv

    row = pl.BlockSpec((tr, c), lambda i: (i, 0))
    return pl.pallas_call(
        body, name=name, grid=(r // tr,),
        in_specs=[pl.BlockSpec((n, tr, c), lambda i: (0, first_block + i, 0)), row, row, row],
        out_specs=[row] * 4,
        out_shape=[jax.ShapeDtypeStruct((r, c), F32)] * 4,
        compiler_params=_cparams(1),
    )(parts, w, m, v)


def _ada_mod(c_all, w_ada, b_cols):
    def body(c_ref, w_ref, b_ref, o_ref):
        cv = c_ref[...]
        act = (cv * _sig(cv)).astype(BF16)
        o_ref[...] = _nn(act, w_ref[...].astype(BF16)) + b_ref[...]

    return pl.pallas_call(body, name="ada_mod", out_shape=jax.ShapeDtypeStruct((N_DEV, w_ada.shape[1]), F32))(c_all, w_ada, b_cols)


def _ada_bwd(c_all, dmod_cols, w, m, v):
    def body(c_ref, d_ref, w_ref, m_ref, v_ref, g_out, d_out, m_out, v_out):
        cv = c_ref[...]
        act = (cv * _sig(cv)).astype(BF16)
        g = _tn(act, d_ref[...].astype(BF16))
        delta, nm, nv = _adam(g, w_ref[...], m_ref[...], v_ref[...])
        g_out[...] = g
        d_out[...] = delta
        m_out[...] = nm
        v_out[...] = nv

    return pl.pallas_call(body, name="ada_bwd", out_shape=[jax.ShapeDtypeStruct(w.shape, F32)] * 4)(c_all, dmod_cols, w, m, v)


def _in_proj(x, norm_w, mod8, wt):
    s = x.shape[0]
    tm = min(1024, s)
    nk = wt.shape[0] // D

    def body(x_ref, nw_ref, mod_ref, w_ref, proj_ref, h_ref, hs_ref):
        @pl.when(pl.program_id(1) == 0)
        def _():
            xv = x_ref[...]
            rstd = lax.rsqrt(jnp.mean(xv * xv, axis=-1, keepdims=True) + EPS)
            h = (xv * rstd) * nw_ref[...] * (1.0 + mod_ref[1:2, :]) + mod_ref[0:1, :]
            hs_ref[...] = h.astype(BF16)
            h_ref[...] = hs_ref[...]

        proj_ref[...] = _nt(hs_ref[...], w_ref[...]).astype(proj_ref.dtype)

    return pl.pallas_call(
        body, name="in_proj", grid=(s // tm, nk),
        in_specs=[pl.BlockSpec((tm, D), lambda i, k: (i, 0)), _const((1, D)), _const((8, D)),
                  pl.BlockSpec((D, D), lambda i, k: (k, 0))],
        out_specs=[pl.BlockSpec((tm, D), lambda i, k: (i, k)), pl.BlockSpec((tm, D), lambda i, k: (i, 0))],
        out_shape=[jax.ShapeDtypeStruct((s, nk * D), BF16), jax.ShapeDtypeStruct((s, D), BF16)],
        scratch_shapes=[pltpu.VMEM((tm, D), BF16)],
        compiler_params=_cparams(2),
    )(x, norm_w, mod8, wt)


CONV_RC = 128
CONV_LC = 128
GW_RC = 32
SUBLANES = 8


def _shifted_taps(win_ref, weight_of, offsets, r0, lanes):
    acc = jnp.zeros((CONV_RC, CONV_LC), F32)
    for b in range(SUBLANES):
        group = [o for o in offsets if o % SUBLANES == b]
        if not group:
            continue
        rows = CONV_RC if b == 0 else CONV_RC + SUBLANES
        part = jnp.zeros((rows, CONV_LC), F32)
        for o in group:
            part = part + win_ref[pl.ds(r0 + o - b, rows), lanes] * weight_of(o)
        acc = acc + (part if b == 0 else part[b:b + CONV_RC])
    return acc


def _conv_fwd(proj, conv_w, conv_b, ln_w, ln_b):
    s = proj.shape[0]
    tm = min(256, s)
    hb = tm // HALO

    def body(av_ref, ag_ref, avh_ref, agh_ref, gate_ref, cw_ref, cb_ref, lw_ref, lb_ref, u1_ref, ya_ref, win_ref):
        i = pl.program_id(0)
        halo = avh_ref[...].astype(F32) * _sig(agh_ref[...].astype(F32))
        win_ref[0:HALO, :] = jnp.where(i > 0, halo, 0.0)
        win_ref[HALO:HALO + tm, :] = av_ref[...].astype(F32) * _sig(ag_ref[...].astype(F32))
        first = HALO - (KCONV - 1)
        for r0 in range(0, tm, CONV_RC):
            for c0 in range(0, D, CONV_LC):
                acc = _shifted_taps(win_ref, lambda o: cw_ref[o - first:o - first + 1, c0:c0 + CONV_LC],
                                    range(first, first + KCONV), r0, pl.ds(c0, CONV_LC))
                u1_ref[r0:r0 + CONV_RC, c0:c0 + CONV_LC] = acc + cb_ref[:, c0:c0 + CONV_LC]
        u1 = u1_ref[...]
        mu = jnp.mean(u1, axis=-1, keepdims=True)
        xc = u1 - mu
        var = jnp.mean(xc * xc, axis=-1, keepdims=True)
        ln = xc * lax.rsqrt(var + EPS) * lw_ref[...] + lb_ref[...]
        gate = gate_ref[...].astype(F32)
        ya_ref[...] = ((ln * _sig(ln)) * (gate * _sig(gate))).astype(BF16)

    row = lambda k: pl.BlockSpec((tm, D), lambda i: (i, k))
    prev = lambda k: pl.BlockSpec((HALO, D), lambda i: (jnp.maximum(i * hb - 1, 0), k))
    return pl.pallas_call(
        body, name="conv_fwd", grid=(s // tm,),
        in_specs=[row(0), row(1), prev(0), prev(1), row(2), _const((KPAD, D)), _const((1, D)), _const((1, D)), _const((1, D))],
        out_specs=[pl.BlockSpec((tm, D), lambda i: (i, 0))] * 2,
        out_shape=[jax.ShapeDtypeStruct((s, D), F32), jax.ShapeDtypeStruct((s, D), BF16)],
        scratch_shapes=[pltpu.VMEM((tm + HALO, D), F32)],
        compiler_params=_cparams(1),
    )(proj, proj, proj, proj, proj, conv_w, conv_b, ln_w, ln_b)


def _rope_tables(pos_ref, if_ref):
    ang = pos_ref[...].astype(F32) * if_ref[...]
    return jnp.cos(ang), jnp.sin(ang)


def _rms_parts(x):
    rstd = lax.rsqrt(jnp.mean(x * x, axis=-1, keepdims=True) + EPS)
    return x * rstd, rstd


def _mla_prep(proj, pos, inv_freq, qnw, kvnw, wuq, wukv):
    s = proj.shape[0]
    tm = min(512, s)

    def body(p_ref, pos_ref, if_ref, qnw_ref, kvnw_ref, wuq_ref, wukv_ref, q_ref, k_ref, v_ref):
        blk = p_ref[...].astype(F32)
        cos, sin = _rope_tables(pos_ref, if_ref)

        def rope(r):
            x1, x2 = r[:, :HALF], r[:, HALF:]
            return jnp.concatenate([x1 * cos - x2 * sin, x1 * sin + x2 * cos], axis=-1)

        qlat = _rms_parts(blk[:, :QL])[0] * qnw_ref[...]
        kvlat = _rms_parts(blk[:, QL:QL + KVL])[0] * kvnw_ref[...]
        q = _nn(qlat.astype(BF16), wuq_ref[...])
        kv = _nn(kvlat.astype(BF16), wukv_ref[...])
        kr = rope(blk[:, QL + KVL:MLA_COLS])
        for h in range(NH):
            qh = q[:, h * DQK:(h + 1) * DQK]
            q_ref[h] = (jnp.concatenate([qh[:, :NOPE], rope(qh[:, NOPE:])], axis=-1) * ATTN_SCALE).astype(BF16)
            k_ref[h] = jnp.concatenate([kv[:, h * 256:h * 256 + NOPE], kr], axis=-1).astype(BF16)
            v_ref[h] = kv[:, h * 256 + NOPE:(h + 1) * 256].astype(BF16)

    hm = lambda d: pl.BlockSpec((NH, tm, d), lambda i: (0, i, 0))
    return pl.pallas_call(
        body, name="mla_prep", grid=(s // tm,),
        in_specs=[pl.BlockSpec((tm, D), lambda i: (i, 3)), pl.BlockSpec((tm, 1), lambda i: (i, 0)), _const((1, HALF)),
                  _const((1, QL)), _const((1, KVL)), _const((QL, NH * DQK)), _const((KVL, NH * 256))],
        out_specs=[hm(DQK), hm(DQK), hm(DV)],
        out_shape=[jax.ShapeDtypeStruct((NH, s, DQK), BF16), jax.ShapeDtypeStruct((NH, s, DQK), BF16),
                   jax.ShapeDtypeStruct((NH, s, DV), BF16)],
        compiler_params=_cparams(1),
    )(proj, pos, inv_freq, qnw, kvnw, wuq, wukv)


ATTN_SCALE = DQK ** -0.5


def _causal_mask(s, t):
    rows = lax.broadcasted_iota(jnp.int32, (t, t), 0)
    cols = lax.broadcasted_iota(jnp.int32, (t, t), 1)
    return jnp.where(cols <= rows, s, -jnp.inf)


def _attn_tile(s):
    return min(1024, s // 2)


def _attn_fwd(q, k, v):
    nh, s, _ = q.shape
    t = _attn_tile(s)

    def body(q_ref, k_ref, v_ref, o_ref, lse_ref):
        qi = pl.program_id(1)
        qv = q_ref[0]

        def chunk(c, carry, diag):
            m, l, acc = carry
            rows = pl.ds(pl.multiple_of(c * t, t), t)
            sc = _nt(qv, k_ref[0, rows, :])
            if diag:
                sc = _causal_mask(sc, t)
            m_new = jnp.maximum(m, jnp.max(sc, axis=-1, keepdims=True))
            alpha = jnp.exp(m - m_new)
            p = jnp.exp(sc - m_new)
            l = alpha * l + jnp.sum(p, axis=-1, keepdims=True)
            acc = alpha * acc + _nn(p.astype(BF16), v_ref[0, rows, :])
            return m_new, l, acc

        init = (jnp.full((t, 1), -jnp.inf, F32), jnp.zeros((t, 1), F32), jnp.zeros((t, DV), F32))
        carry = lax.fori_loop(0, qi, lambda c, cr: chunk(c, cr, False), init)
        m, l, acc = chunk(qi, carry, True)
        o_ref[...] = acc / l
        lse_ref[0] = jnp.broadcast_to(m + jnp.log(l), (t, DV))

    head = lambda d: pl.BlockSpec((1, s, d), lambda h, i: (h, 0, 0))
    return pl.pallas_call(
        body, name="attn_fwd", grid=(nh, s // t),
        in_specs=[pl.BlockSpec((1, t, DQK), lambda h, i: (h, i, 0)), head(DQK), head(DV)],
        out_specs=[pl.BlockSpec((t, DV), lambda h, i: (i, h)), pl.BlockSpec((1, t, DV), lambda h, i: (h, i, 0))],
        out_shape=[jax.ShapeDtypeStruct((s, nh * DV), F32), jax.ShapeDtypeStruct((nh, s, DV), F32)],
        compiler_params=_cparams(2),
    )(q, k, v)


def _merge_loss(x, target, ya, o, proj, mod8, fnw, gathered):
    s = x.shape[0]
    tm = min(256, s)
    n = s // tm

    def body(x_ref, t_ref, ya_ref, o_ref, bg_ref, ga_ref, gb_ref, mod_ref, fnw_ref, wco_ref, wao_ref, wout_ref,
             dya_ref, do_ref, delta_ref, dpg_ref, dx2_ref, gw_ref, small_ref, acc_ref, cast_ref):
        i = pl.program_id(0)

        @pl.when(i == 0)
        def _():
            acc_ref[...] = jnp.zeros(acc_ref.shape, F32)
            small_ref[...] = jnp.zeros(small_ref.shape, F32)

        bg = bg_ref[...].astype(F32)
        sbg = _sig(bg)
        sb = bg * sbg
        ov = o_ref[...]
        ya = ya_ref[...]
        yb = (ov * sb).astype(BF16)
        square = lambda ref: ref[...].reshape(D, D)
        y_a = _nn(ya, square(wco_ref))
        y_b = _nn(yb, square(wao_ref))
        sa = _sig(ga_ref[...].astype(F32))
        sgb = _sig(gb_ref[...].astype(F32))
        merged = (sa * y_a + sgb * y_b).astype(BF16)
        z = _nn(merged, square(wout_ref))
        gate = mod_ref[2:3, :]
        x2 = x_ref[...] + gate * z
        xn, rstd = _rms_parts(x2)
        fnw = fnw_ref[...]
        err = xn * fnw - t_ref[...]
        loss = jnp.sum(jnp.sum(err * err, axis=-1, keepdims=True), axis=0, keepdims=True) * (0.5 / D)
        dy = err * (1.0 / D)
        small_ref[0:1, :] += jnp.sum(dy * xn, axis=0, keepdims=True)
        dxn = dy * fnw
        dx2 = rstd * (dxn - xn * jnp.mean(dxn * xn, axis=-1, keepdims=True))
        dx2_ref[...] = dx2
        small_ref[1:2, :] += jnp.sum(dx2 * z, axis=0, keepdims=True)
        small_ref[2:3, :] += jnp.broadcast_to(loss, (1, D))
        dz = (dx2 * gate).astype(BF16)
        dmerged = _nt(dz, square(wout_ref))
        acc_ref[2] += _tn(merged, dz)
        dy_a = (dmerged * sa).astype(BF16)
        dy_b = (dmerged * sgb).astype(BF16)
        dpg_ref[:, D:2 * D] = (dmerged * y_a * (sa * (1.0 - sa))).astype(BF16)
        dpg_ref[:, 2 * D:3 * D] = (dmerged * y_b * (sgb * (1.0 - sgb))).astype(BF16)
        dya_ref[...] = _nt(dy_a, square(wco_ref))
        acc_ref[0] += _tn(ya, dy_a)
        dyb = _nt(dy_b, square(wao_ref))
        acc_ref[1] += _tn(yb, dy_b)
        do = dyb * sb
        do_ref[...] = do.astype(BF16)
        dpg_ref[:, 0:D] = (dyb * ov * (sbg * (1.0 + bg * (1.0 - sbg)))).astype(BF16)
        prod = do * ov
        for h in range(NH):
            delta_ref[h] = jnp.broadcast_to(jnp.sum(prod[:, h * DV:(h + 1) * DV], axis=-1, keepdims=True), (tm, DV))

        @pl.when(i == n - 1)
        def _():
            for j in range(3):
                cast_ref[...] = acc_ref[j].astype(cast_ref.dtype)
                pltpu.sync_copy(cast_ref, gw_ref.at[j])

    row = pl.BlockSpec((tm, D), lambda i: (i, 0))
    col = lambda k: pl.BlockSpec((tm, D), lambda i: (i, k))
    wspec = lambda j: pl.BlockSpec((N_DEV, ROWS_SQ, D), lambda i: (0, j, 0), pipeline_mode=pl.Buffered(1))
    return pl.pallas_call(
        body, name="merge_loss", grid=(n,),
        in_specs=[row, row, row, row, col(4), col(5), col(6), _const((8, D)), _const((1, D)), wspec(0), wspec(1), wspec(2)],
        out_specs=[row, row, pl.BlockSpec((NH, tm, DV), lambda i: (0, i, 0)), pl.BlockSpec((tm, 3 * D), lambda i: (i, 0)),
                   row, ANY, _const((8, D))],
        out_shape=[jax.ShapeDtypeStruct((s, D), F32), jax.ShapeDtypeStruct((s, D), BF16),
                   jax.ShapeDtypeStruct((NH, s, DV), F32), jax.ShapeDtypeStruct((s, 3 * D), BF16),
                   jax.ShapeDtypeStruct((s, D), F32), jax.ShapeDtypeStruct((3, D, D), BF16),
                   jax.ShapeDtypeStruct((8, D), F32)],
        scratch_shapes=[pltpu.VMEM((3, D, D), F32), pltpu.VMEM((D, D), BF16)],
        compiler_params=_cparams(1),
    )(x, target, ya, o, proj, proj, proj, mod8, fnw, gathered, gathered, gathered)


def _attn_bwd(q, k, v, do, lse, delta):
    nh, s, _ = q.shape
    t = _attn_tile(s)
    nb = s // t

    def body(q_ref, k_ref, v_ref, do_ref, lse_ref, dl_ref, dq_ref, dk_ref, dv_ref):
        kj = pl.program_id(1)

        @pl.when(kj == 0)
        def _():
            dq_ref[...] = jnp.zeros(dq_ref.shape, F32)

        kv_, vv = k_ref[0], v_ref[0]

        def chunk(c, carry, diag):
            dk, dv = carry
            rows = pl.ds(pl.multiple_of(c * t, t), t)
            qv = q_ref[0, rows, :]
            dov = do_ref[rows, :]
            sc = _nt(qv, kv_)
            if diag:
                sc = _causal_mask(sc, t)
            p = jnp.exp(sc - lse_ref[0, rows, 0:1])
            dv = dv + _tn(p.astype(BF16), dov)
            dp = _nt(dov, vv)
            ds = (p * (dp - dl_ref[0, rows, 0:1])).astype(BF16)
            dk = dk + _tn(ds, qv)
            dq_ref[0, rows, :] += _nn(ds, kv_)
            return dk, dv

        carry = chunk(kj, (jnp.zeros((t, DQK), F32), jnp.zeros((t, DV), F32)), True)
        dk, dv = lax.fori_loop(kj + 1, nb, lambda c, cr: chunk(c, cr, False), carry)
        dk_ref[0] = dk
        dv_ref[0] = dv

    head = lambda d: pl.BlockSpec((1, s, d), lambda h, j: (h, 0, 0))
    blk = lambda d: pl.BlockSpec((1, t, d), lambda h, j: (h, j, 0))
    return pl.pallas_call(
        body, name="attn_bwd", grid=(nh, nb),
        in_specs=[head(DQK), blk(DQK), blk(DV), pl.BlockSpec((s, DV), lambda h, j: (0, h)), head(DV), head(DV)],
        out_specs=[head(DQK), blk(DQK), blk(DV)],
        out_shape=[jax.ShapeDtypeStruct((nh, s, DQK), F32), jax.ShapeDtypeStruct((nh, s, DQK), F32),
                   jax.ShapeDtypeStruct((nh, s, DV), F32)],
        compiler_params=_cparams(2),
    )(q, k, v, do, lse, delta)


def _mla_bwd(dq, dk, dv, proj, pos, inv_freq, qnw, kvnw, wuq, wukv):
    s = proj.shape[0]
    tm = min(512, s)

    def body(dq_ref, dk_ref, dv_ref, p_ref, pos_ref, if_ref, qnw_ref, kvnw_ref, wuq_ref, wukv_ref,
             dp_ref, guq_ref, gukv_ref, small_ref):
        @pl.when(pl.program_id(0) == 0)
        def _():
            guq_ref[...] = jnp.zeros(guq_ref.shape, F32)
            gukv_ref[...] = jnp.zeros(gukv_ref.shape, F32)
            small_ref[...] = jnp.zeros(small_ref.shape, F32)

        blk = p_ref[...].astype(F32)
        cos, sin = _rope_tables(pos_ref, if_ref)

        def unrope(g):
            g1, g2 = g[:, :HALF], g[:, HALF:]
            return jnp.concatenate([g1 * cos + g2 * sin, g2 * cos - g1 * sin], axis=-1)

        dq_cols, dkv_cols = [], []
        dkr = jnp.zeros((tm, ROPE), F32)
        for h in range(NH):
            dqh, dkh = dq_ref[h] * ATTN_SCALE, dk_ref[h]
            dq_cols += [dqh[:, :NOPE], unrope(dqh[:, NOPE:])]
            dkv_cols += [dkh[:, :NOPE], dv_ref[h]]
            dkr = dkr + dkh[:, NOPE:]
        dq_full = jnp.concatenate(dq_cols, axis=-1).astype(BF16)
        dkv_full = jnp.concatenate(dkv_cols, axis=-1).astype(BF16)

        def latent_bwd(c, nw_ref, d_up, w_ref, g_ref, srow):
            nrm, rstd = _rms_parts(c)
            nw = nw_ref[...]
            lat = (nrm * nw).astype(BF16)
            g_ref[...] += _tn(lat, d_up)
            dlat = _nt(d_up, w_ref[...])
            small_ref[srow:srow + 1, :] += jnp.sum(dlat * nrm, axis=0, keepdims=True)
            dn = dlat * nw
            return rstd * (dn - nrm * jnp.mean(dn * nrm, axis=-1, keepdims=True))

        dcq = latent_bwd(blk[:, :QL], qnw_ref, dq_full, wuq_ref, guq_ref, 0)
        dckv = latent_bwd(blk[:, QL:QL + KVL], kvnw_ref, dkv_full, wukv_ref, gukv_ref, 1)
        dp_ref[...] = jnp.concatenate([dcq, dckv, unrope(dkr), jnp.zeros((tm, D - MLA_COLS), F32)], axis=-1).astype(BF16)

    hm = lambda d: pl.BlockSpec((NH, tm, d), lambda i: (0, i, 0))
    return pl.pallas_call(
        body, name="mla_bwd", grid=(s // tm,),
        in_specs=[hm(DQK), hm(DQK), hm(DV), pl.BlockSpec((tm, D), lambda i: (i, 3)), pl.BlockSpec((tm, 1), lambda i: (i, 0)),
                  _const((1, HALF)), _const((1, QL)), _const((1, KVL)), _const((QL, NH * DQK)), _const((KVL, NH * 256))],
        out_specs=[pl.BlockSpec((tm, D), lambda i: (i, 0)), _const((QL, NH * DQK)), _const((KVL, NH * 256)), _const((8, QL))],
        out_shape=[jax.ShapeDtypeStruct((s, D), BF16), jax.ShapeDtypeStruct((QL, NH * DQK), F32),
                   jax.ShapeDtypeStruct((KVL, NH * 256), F32), jax.ShapeDtypeStruct((8, QL), F32)],
        compiler_params=_cparams(1),
    )(dq, dk, dv, proj, pos, inv_freq, qnw, kvnw, wuq, wukv)


def _conv_rows_bwd(dya, u1, proj, ln_w, ln_b):
    s = dya.shape[0]
    tm = min(512, s)

    def body(dya_ref, u1_ref, gate_ref, lw_ref, lb_ref, du1_ref, dag_ref, small_ref):
        @pl.when(pl.program_id(0) == 0)
        def _():
            small_ref[...] = jnp.zeros(small_ref.shape, F32)

        u1 = u1_ref[...]
        mu = jnp.mean(u1, axis=-1, keepdims=True)
        xc = u1 - mu
        rstd = lax.rsqrt(jnp.mean(xc * xc, axis=-1, keepdims=True) + EPS)
        xhat = xc * rstd
        lw = lw_ref[...]
        ln = xhat * lw + lb_ref[...]
        sl = _sig(ln)
        u2 = ln * sl
        gate = gate_ref[...].astype(F32)
        sg = _sig(gate)
        dya = dya_ref[...]
        dag_ref[...] = (dya * u2 * (sg * (1.0 + gate * (1.0 - sg)))).astype(BF16)
        dln = dya * (gate * sg) * (sl * (1.0 + ln * (1.0 - sl)))
        small_ref[0:1, :] += jnp.sum(dln * xhat, axis=0, keepdims=True)
        small_ref[1:2, :] += jnp.sum(dln, axis=0, keepdims=True)
        dxh = dln * lw
        du1_ref[...] = rstd * (dxh - jnp.mean(dxh, axis=-1, keepdims=True) - xhat * jnp.mean(dxh * xhat, axis=-1, keepdims=True))

    row = pl.BlockSpec((tm, D), lambda i: (i, 0))
    return pl.pallas_call(
        body, name="conv_rows_bwd", grid=(s // tm,),
        in_specs=[row, row, pl.BlockSpec((tm, D), lambda i: (i, 2)), _const((1, D)), _const((1, D))],
        out_specs=[row, row, _const((8, D))],
        out_shape=[jax.ShapeDtypeStruct((s, D), F32), jax.ShapeDtypeStruct((s, D), BF16), jax.ShapeDtypeStruct((8, D), F32)],
        compiler_params=_cparams(1),
    )(dya, u1, proj, ln_w, ln_b)


def _conv_bwd(du1, proj, conv_w):
    s = du1.shape[0]
    tm = min(256, s)
    hb = tm // HALO
    n = s // tm
    last32 = s // HALO - 1

    def body(d_ref, dn_ref, av_ref, ag_ref, avh_ref, agh_ref, cw_ref, dp_ref, gcw_ref, small_ref,
             dwin_ref, dpad_ref, dsh_ref, uwin_ref, acc_ref):
        i = pl.program_id(0)

        @pl.when(i == 0)
        def _():
            acc_ref[...] = jnp.zeros(acc_ref.shape, F32)
            small_ref[...] = jnp.zeros(small_ref.shape, F32)
            dpad_ref[...] = jnp.zeros(dpad_ref.shape, F32)
            uwin_ref[...] = jnp.zeros(uwin_ref.shape, F32)

        dv = d_ref[...]
        dwin_ref[0:tm, :] = dv
        dwin_ref[tm:tm + HALO, :] = jnp.where(i < n - 1, dn_ref[...], 0.0)
        dpad_ref[SUBLANES:SUBLANES + tm, :] = dv
        halo = avh_ref[...].astype(F32) * _sig(agh_ref[...].astype(F32))
        uwin_ref[0:HALO, :] = jnp.where(i > 0, halo, 0.0)
        av = av_ref[...].astype(F32)
        sg = _sig(ag_ref[...].astype(F32))
        uwin_ref[HALO:HALO + tm, :] = av * sg
        small_ref[0:1, :] += jnp.sum(dv, axis=0, keepdims=True)

        for b in range(SUBLANES):
            dsh_ref[b] = dpad_ref[pl.ds(SUBLANES - b, tm + SUBLANES), :]

        first = HALO - (KCONV - 1)
        for c0 in range(0, D, CONV_LC):
            lanes = pl.ds(c0, CONV_LC)
            for r0 in range(0, tm, CONV_RC):
                acc = _shifted_taps(dwin_ref, lambda o: cw_ref[KCONV - 1 - o:KCONV - o, c0:c0 + CONV_LC], range(KCONV), r0, lanes)
                a = av[r0:r0 + CONV_RC, c0:c0 + CONV_LC]
                g = sg[r0:r0 + CONV_RC, c0:c0 + CONV_LC]
                dp_ref[r0:r0 + CONV_RC, c0:c0 + CONV_LC] = (acc * g).astype(BF16)
                dp_ref[r0:r0 + CONV_RC, D + c0:D + c0 + CONV_LC] = (acc * a * (g * (1.0 - g))).astype(BF16)
            for b in range(SUBLANES):
                group = [o for o in range(first, first + KCONV) if o % SUBLANES == b]
                parts = [jnp.zeros((SUBLANES, CONV_LC), F32) for _ in group]
                chunks = [(i0, GW_RC) for i0 in range(0, tm, GW_RC)] + ([(tm, SUBLANES)] if b else [])
                for i0, rows in chunks:
                    dsh = dsh_ref[b, pl.ds(i0, rows), lanes]
                    for n_, o in enumerate(group):
                        prod = dsh * uwin_ref[pl.ds(i0 + o - b, rows), lanes]
                        parts[n_] = parts[n_] + jnp.sum(prod.reshape(rows // SUBLANES, SUBLANES, CONV_LC), axis=0)
                for n_, o in enumerate(group):
                    acc_ref[o - first, :, c0:c0 + CONV_LC] += parts[n_]

        @pl.when(i == n - 1)
        def _():
            gcw_ref[...] = jnp.sum(acc_ref[...], axis=1)

    row = lambda k: pl.BlockSpec((tm, D), lambda i: (i, k))
    prev = lambda k: pl.BlockSpec((HALO, D), lambda i: (jnp.maximum(i * hb - 1, 0), k))
    return pl.pallas_call(
        body, name="conv_bwd", grid=(n,),
        in_specs=[row(0), pl.BlockSpec((HALO, D), lambda i: (jnp.minimum((i + 1) * hb, last32), 0)),
                  row(0), row(1), prev(0), prev(1), _const((KPAD, D))],
        out_specs=[pl.BlockSpec((tm, 2 * D), lambda i: (i, 0)), _const((KPAD, D)), _const((8, D))],
        out_shape=[jax.ShapeDtypeStruct((s, 2 * D), BF16), jax.ShapeDtypeStruct((KPAD, D), F32), jax.ShapeDtypeStruct((8, D), F32)],
        scratch_shapes=[pltpu.VMEM((tm + HALO, D), F32), pltpu.VMEM((tm + 2 * SUBLANES, D), F32),
                        pltpu.VMEM((SUBLANES, tm + SUBLANES, D), F32),
                        pltpu.VMEM((tm + HALO + SUBLANES, D), F32), pltpu.VMEM((KPAD, SUBLANES, D), F32)],
        compiler_params=_cparams(1),
    )(du1, du1, proj, proj, proj, proj, conv_w)


def _dproj_specs(tm, rows_first):
    def spec(lo, hi):
        def idx(a, b):
            i, k = (a, b) if rows_first else (b, a)
            col = jnp.clip(k - lo, 0, hi - lo - 1)
            if rows_first:
                return (i, col)
            return (jnp.where((k >= lo) & (k < hi), i, 0), col)
        return pl.BlockSpec((tm, D), idx)
    return [spec(0, 2), spec(2, 3), spec(3, 4), spec(4, 7)]


def _pick_dproj(k, refs, fn):
    vg, ag, mla, gates = refs

    @pl.when(k < 2)
    def _():
        fn(vg)

    @pl.when(k == 2)
    def _():
        fn(ag)

    @pl.when(k == 3)
    def _():
        fn(mla)

    @pl.when(k > 3)
    def _():
        fn(gates)


def _in_proj_bwd_x(dps, wt, x, dx2, norm_w, mod8):
    s = x.shape[0]
    tm = min(1024, s)
    nk = wt.shape[0] // D

    def body(vg_ref, ag_ref, mla_ref, g_ref, w_ref, x_ref, dx2_ref, nw_ref, mod_ref, gx_ref, small_ref, acc_ref):
        i, k = pl.program_id(0), pl.program_id(1)

        @pl.when((i == 0) & (k == 0))
        def _():
            small_ref[...] = jnp.zeros(small_ref.shape, F32)

        @pl.when(k == 0)
        def _():
            acc_ref[...] = jnp.zeros(acc_ref.shape, F32)

        def add(ref):
            acc_ref[...] += _nn(ref[...], w_ref[...])

        _pick_dproj(k, (vg_ref, ag_ref, mla_ref, g_ref), add)

        @pl.when(k == nk - 1)
        def _():
            dh = acc_ref[...]
            xn, rstd = _rms_parts(x_ref[...])
            nw = nw_ref[...]
            hn = xn * nw
            small_ref[0:1, :] += jnp.sum(dh, axis=0, keepdims=True)
            small_ref[1:2, :] += jnp.sum(dh * hn, axis=0, keepdims=True)
            dhn = dh * (1.0 + mod_ref[1:2, :])
            small_ref[2:3, :] += jnp.sum(dhn * xn, axis=0, keepdims=True)
            dxn = dhn * nw
            gx_ref[...] = rstd * (dxn - xn * jnp.mean(dxn * xn, axis=-1, keepdims=True)) + dx2_ref[...]

    row = pl.BlockSpec((tm, D), lambda i, k: (i, 0))
    return pl.pallas_call(
        body, name="in_proj_bwd_x", grid=(s // tm, nk),
        in_specs=_dproj_specs(tm, True) + [pl.BlockSpec((D, D), lambda i, k: (k, 0)), row, row, _const((1, D)), _const((8, D))],
        out_specs=[row, _const((8, D))],
        out_shape=[jax.ShapeDtypeStruct((s, D), F32), jax.ShapeDtypeStruct((8, D), F32)],
        scratch_shapes=[pltpu.VMEM((tm, D), F32)],
        compiler_params=_cparams(2),
    )(*dps, wt, x, dx2, norm_w, mod8)


def _in_proj_bwd_w(dps, h, nk):
    s = h.shape[0]
    tm = min(1024, s)

    n = s // tm

    def body(vg_ref, ag_ref, mla_ref, g_ref, h_ref, gw_ref, acc_ref):
        k, i = pl.program_id(0), pl.program_id(1)

        @pl.when(i == 0)
        def _():
            acc_ref[...] = jnp.zeros(acc_ref.shape, F32)

        def add(ref):
            acc_ref[...] += _tn(ref[...], h_ref[...])

        _pick_dproj(k, (vg_ref, ag_ref, mla_ref, g_ref), add)

        @pl.when(i == n - 1)
        def _():
            gw_ref[...] = acc_ref[...].astype(gw_ref.dtype)

    return pl.pallas_call(
        body, name="in_proj_bwd_w", grid=(nk, n),
        in_specs=_dproj_specs(tm, False) + [pl.BlockSpec((tm, D), lambda k, i: (i, 0))],
        out_specs=pl.BlockSpec((D, D), lambda k, i: (k, 0)),
        out_shape=jax.ShapeDtypeStruct((nk * D, D), BF16),
        scratch_shapes=[pltpu.VMEM((D, D), F32)],
        compiler_params=_cparams(2),
    )(*dps, h)


def _small_slab(wuq, wukv, conv_w):
    cw = jnp.pad(conv_w.reshape(-1), (0, (ROWS_TAIL - ROWS_UQ - ROWS_UKV) * D - KCONV * 128)).reshape(-1, D)
    return jnp.concatenate([wuq.reshape(ROWS_UQ, D), wukv.reshape(ROWS_UKV, D), cw], axis=0)


def _split_small_slab(slab):
    return (slab[..., :ROWS_UQ, :], slab[..., ROWS_UQ:ROWS_UQ + ROWS_UKV, :],
            slab[..., ROWS_UQ + ROWS_UKV:ROWS_UQ + ROWS_UKV + ROWS_CW, :])


def _unpack_small_slab(slab):
    wuq, wukv, cw = _split_small_slab(slab)
    return (wuq.reshape(QL, NH * DQK // N_DEV), wukv.reshape(KVL, NH * 256 // N_DEV),
            cw.reshape(-1)[:KCONV * 128].reshape(KCONV, 128))


def _pack_shard(w_in, wco, wao, wout, wuq, wukv, conv_w):
    bf = lambda a: a.astype(BF16)
    first = jnp.concatenate([jnp.pad(bf(w_in).T, ((0, ROWS_IN_PAD - ROWS_IN), (0, 0))), bf(_small_slab(wuq, wukv, conv_w))], axis=0)
    return first, jnp.concatenate([bf(wco), bf(wao), bf(wout)], axis=0)


def _unpack_gathered(g):
    wt = g[:, :ROWS_IN].reshape(IN_COLS, D)
    split = 3 * D + MLA_COLS
    wt = jnp.concatenate([wt[:split], jnp.zeros((D - MLA_COLS, D), g.dtype), wt[split:]], axis=0)
    wuq, wukv, cw = _split_small_slab(g[:, OFF_TAIL:OFF_TAIL + ROWS_TAIL])
    wuq = wuq.reshape(N_DEV, QL, NH * DQK // N_DEV).transpose(1, 0, 2).reshape(QL, NH * DQK)
    wukv = wukv.reshape(N_DEV, KVL, NH * 256 // N_DEV).transpose(1, 0, 2).reshape(KVL, NH * 256)
    cw = cw.reshape(N_DEV, ROWS_CW * D)[:, :KCONV * 128].reshape(N_DEV, KCONV, 128).transpose(1, 0, 2).reshape(KCONV, D)
    return wt, wuq, wukv, cw


def _pack_grads(gwt, gw3, guq, gukv, gcw):
    split = 3 * D + MLA_COLS
    g_in = jnp.concatenate([gwt[:split], gwt[4 * D:]], axis=0).reshape(N_DEV, ROWS_IN, D)
    g_in = jnp.pad(g_in, ((0, 0), (0, ROWS_IN_PAD - ROWS_IN), (0, 0)))
    guq = guq.reshape(QL, N_DEV, -1).transpose(1, 0, 2).reshape(N_DEV, ROWS_UQ, D)
    gukv = gukv.reshape(KVL, N_DEV, -1).transpose(1, 0, 2).reshape(N_DEV, ROWS_UKV, D)
    gcw = gcw.reshape(KCONV, N_DEV, 128).transpose(1, 0, 2).reshape(N_DEV, KCONV * 128)
    gcw = jnp.pad(gcw, ((0, 0), (0, (ROWS_TAIL - ROWS_UQ - ROWS_UKV) * D - KCONV * 128))).reshape(N_DEV, -1, D)
    gsq = gw3.reshape(3, N_DEV, ROWS_SQ, D).transpose(1, 0, 2, 3).reshape(N_DEV, 3 * ROWS_SQ, D)
    slabs = jnp.concatenate([a.astype(BF16) for a in (g_in, guq, gukv, gcw, gsq)], axis=1)
    return slabs.reshape(N_CHIP, 2, ROWS_PACK, D).transpose(1, 0, 2, 3)


def _pack_small(vecs):
    flat = jnp.concatenate([v.reshape(-1) for v in vecs])
    return jnp.pad(flat, (0, SMALL_LEN - flat.shape[0])).reshape(8, SMALL_COLS)


def _unpack_small(a, shapes):
    flat = a.reshape(-1)
    out, off = [], 0
    for shp, n in zip(shapes, SMALL_SIZES):
        out.append(flat[off:off + n].reshape(shp))
        off += n
    return out


def kernel(x, c, positions, w_ada, b_ada, norm_w, w_in, conv_w, conv_b, conv_ln_w, conv_ln_b, w_conv_out, q_norm_w, w_uq, kv_norm_w, w_ukv, w_attn_out, w_out, final_norm_w, loss_target, m_w_ada, m_b_ada, m_norm_w, m_w_in, m_conv_w, m_conv_b, m_conv_ln_w, m_conv_ln_b, m_w_conv_out, m_q_norm_w, m_w_uq, m_kv_norm_w, m_w_ukv, m_w_attn_out, m_w_out, m_final_norm_w, v_w_ada, v_b_ada, v_norm_w, v_w_in, v_conv_w, v_conv_b, v_conv_ln_w, v_conv_ln_b, v_w_conv_out, v_q_norm_w, v_w_uq, v_kv_norm_w, v_w_ukv, v_w_attn_out, v_w_out, v_final_norm_w):
    me = 4 * lax.axis_index("x") + 2 * lax.axis_index("y") + lax.axis_index("c")
    xs, tgt = x[0], loss_target[0]
    s = xs.shape[0]
    ada_cols = w_ada.shape[2]

    sharded = lambda t: tuple(a[0] for a in t)
    slab_first, slab_sq = _pack_shard(*sharded((w_in, w_conv_out, w_attn_out, w_out, w_uq, w_ukv, conv_w)))
    wt, wuq, wukv, cw = _unpack_gathered(_all_gather(slab_first, "gather_weights"))
    cw32 = jnp.pad(cw.astype(F32), ((0, KPAD - KCONV), (0, 0)))

    c_all = _all_gather(jnp.broadcast_to(c, (8, D)), "gather_c")[:, 0, :]
    b_cols = lax.dynamic_slice(b_ada, (0, me * ada_cols), (1, ada_cols))
    mod_cols = _all_gather(_ada_mod(c_all, w_ada[0], b_cols), "gather_mod")
    mod = lax.dynamic_index_in_dim(mod_cols, me, axis=1, keepdims=False).reshape(3, D)
    mod8 = jnp.pad(mod, ((0, 5), (0, 0)))

    slab_sq, mod8 = lax.optimization_barrier((slab_sq, mod8))
    sq_send, sq_recv, sq_src, sq_land, sq_token = _exchange_start(slab_sq, (N_DEV,) + slab_sq.shape, _direct_copies, N_DEV - 1,
                                                                  "square_gather_start")

    pos = positions.reshape(s, 1)
    inv_freq = (ROPE_THETA ** (-jnp.arange(0, ROPE, 2, dtype=F32) / ROPE)).reshape(1, HALF)
    proj, h = _in_proj(xs, norm_w + sq_token[0:1, 0:1], mod8, wt)
    u1, ya = _conv_fwd(proj, cw32, conv_b, conv_ln_w, conv_ln_b)
    q, k, v = _mla_prep(proj, pos, inv_freq, q_norm_w, kv_norm_w, wuq, wukv)
    o, lse = _attn_fwd(q, k, v)

    slab_sq, gathered_sq = _exchange_wait(sq_send, sq_recv, sq_src, sq_land, lse, _direct_copies, "square_gather_wait")
    gathered_sq = lax.dynamic_update_slice(gathered_sq, slab_sq[None], (me, 0, 0))
    dya, do, delta, dp_gates, dx2, gw3, small_a = _merge_loss(xs, tgt, ya, o, proj, mod8, final_norm_w.reshape(1, D), gathered_sq)
    dq, dk, dv = _attn_bwd(q, k, v, do, lse, delta)
    dp_mla, guq, gukv, small_b = _mla_bwd(dq, dk, dv, proj, pos, inv_freq, q_norm_w, kv_norm_w, wuq, wukv)
    du1, dp_ag, small_c = _conv_rows_bwd(dya, u1, proj, conv_ln_w, conv_ln_b)
    dp_vg, gcw, small_d = _conv_bwd(du1, proj, cw32)
    dps = [dp_vg, dp_ag, dp_mla, dp_gates]
    gwt = _in_proj_bwd_w(dps, h, wt.shape[0] // D)

    packed = _pack_grads(gwt, gw3, guq, gukv, gcw[:KCONV])
    core = lax.axis_index("c").astype(jnp.int32).reshape(1)
    half = _pair_add(core, packed, _pair_exchange(packed), ROWS_PACK // 4)
    send_sems, recv_sems, half_thru, land_thru, token = _exchange_start(half, half.shape, _chip_copies, N_CHIP - 1, "chip_exchange_start")
    grad_x, small_e = _in_proj_bwd_x(dps, wt, xs, dx2, norm_w + token[0:1, 0:1], mod8)
    half, recv = _exchange_wait(send_sems, recv_sems, half_thru, land_thru, small_e, _chip_copies, "chip_exchange_wait")
    chip = 2 * lax.axis_index("x") + lax.axis_index("y")
    recv = lax.dynamic_update_slice(recv, lax.dynamic_slice(half, (chip, 0, 0), (1,) + half.shape[1:]), (chip, 0, 0))
    big_in = _adam_w_in(recv, w_in[0], m_w_in[0], v_w_in[0])
    squares = (("w_conv_out", w_conv_out, m_w_conv_out, v_w_conv_out), ("w_attn_out", w_attn_out, m_w_attn_out, v_w_attn_out),
               ("w_out", w_out, m_w_out, v_w_out))
    big_sq = [_reduce_adam(recv, w[0], m[0], v[0], "adam_" + nm, ROWS_SQ, OFF_SQ // ROWS_SQ + j) for j, (nm, w, m, v) in enumerate(squares)]
    tail = _reduce_adam(recv, _small_slab(w_uq[0], w_ukv[0], conv_w[0]), _small_slab(m_w_uq[0], m_w_ukv[0], m_conv_w[0]),
                        _small_slab(v_w_uq[0], v_w_ukv[0], v_conv_w[0]), "adam_small_sharded", ROWS_TAIL, OFF_TAIL // ROWS_TAIL)
    tail = [_unpack_small_slab(a) for a in tail]
    big = [(big_in[i], big_sq[0][i], big_sq[1][i], big_sq[2][i], *tail[i]) for i in range(4)]

    dmod = jnp.concatenate([small_e[0], small_e[1], small_a[1]])
    payload = _pack_small([dmod, small_e[2], small_d[0], small_c[0], small_c[1], small_a[0], small_b[0], small_b[1], small_a[2, 0:1]])
    pay_all = _all_gather(payload, "gather_small")
    small_w = (b_ada, norm_w, conv_b, conv_ln_w, conv_ln_b, final_norm_w, q_norm_w, kv_norm_w)
    small_m = (m_b_ada, m_norm_w, m_conv_b, m_conv_ln_w, m_conv_ln_b, m_final_norm_w, m_q_norm_w, m_kv_norm_w)
    small_v = (v_b_ada, v_norm_w, v_conv_b, v_conv_ln_w, v_conv_ln_b, v_final_norm_w, v_q_norm_w, v_kv_norm_w)
    sm = _reduce_adam(pay_all, _pack_small(small_w), _pack_small(small_m), _pack_small(small_v), "adam_replicated", 8, 0)
    loss = sm[0].reshape(-1)[sum(SMALL_SIZES)]
    shapes = [t.shape for t in small_w]
    sm = [_unpack_small(a, shapes) for a in sm]

    dmod_all = pay_all.reshape(N_DEV, SMALL_LEN)[:, :3 * D]
    dmod_cols = lax.dynamic_slice(dmod_all, (0, me * ada_cols), (N_DEV, ada_cols))
    ada = _ada_bwd(c_all, dmod_cols, w_ada[0], m_w_ada[0], v_w_ada[0])

    def group(i):
        b_in, b_co, b_ao, b_out, b_uq, b_ukv, b_cw = big[i]
        s_bada, s_nw, s_cb, s_clw, s_clb, s_fnw, s_qnw, s_kvnw = sm[i]
        return (ada[i][None], s_bada, s_nw, b_in[None], b_cw[None], s_cb, s_clw, s_clb, b_co[None], s_qnw, b_uq[None], s_kvnw,
                b_ukv[None], b_ao[None], b_out[None], s_fnw)

    return (loss, grad_x[None], *group(0), *group(1), *group(2), *group(3))
```

```python
import functools

import jax
import jax.numpy as jnp
from jax import lax
from jax.experimental import pallas as pl
from jax.experimental.pallas import tpu as pltpu

F32 = jnp.float32
BF16 = jnp.bfloat16

D = 1024
NH = 8
NOPE = 128
ROPE = 64
HALF = ROPE // 2
DQK = NOPE + ROPE
DV = 128
QL = 256
KVL = 256
KCONV = 31
KPAD = 32
HALO = 32
IN_COLS = 6720
MLA_COLS = QL + KVL + ROPE
PROJ_COLS = 7 * D
EPS = 1e-6
ROPE_THETA = 10000.0
N_DEV = 8

ADAM_LR = 0.001
ADAM_B1 = 0.9
ADAM_B2 = 0.999
ADAM_EPS = 1e-08
ADAM_WD = 0.01
ADAM_STEP = 10

ROWS_IN = 840
ROWS_SQ = 128
ROWS_UQ = 48
ROWS_UKV = 64
ROWS_CW = 4
ROWS_IN_PAD = 896
OFF_TAIL = ROWS_IN_PAD
ROWS_TAIL = 128
OFF_SQ = OFF_TAIL + ROWS_TAIL
ROWS_PACK = OFF_SQ + 3 * ROWS_SQ
SMALL_SIZES = (3 * D, D, D, D, D, D, QL, KVL)
SMALL_COLS = 1152
SMALL_LEN = 8 * SMALL_COLS

MESH = pl.DeviceIdType.MESH
ANY = pl.BlockSpec(memory_space=pl.ANY)
V7X_VMEM_LIMIT = 56 * 1024 * 1024


def _cparams(n_axes, vmem=V7X_VMEM_LIMIT):
    return pltpu.CompilerParams(dimension_semantics=("arbitrary",) * n_axes, vmem_limit_bytes=vmem)


def _sig(x):
    return jax.nn.sigmoid(x)


def _nt(a, b):
    return lax.dot_general(a, b, (((1,), (1,)), ((), ())), preferred_element_type=F32)


def _tn(a, b):
    return lax.dot_general(a, b, (((0,), (0,)), ((), ())), preferred_element_type=F32)


def _nn(a, b):
    return jnp.dot(a, b, preferred_element_type=F32)


def _const(shape):
    return pl.BlockSpec(shape, lambda *_: (0,) * len(shape))


def _all_gather(block, name):
    r, c = block.shape

    def body(x_ref, out_ref, send_sems, recv_sems, local_sem):
        x, y, cc = lax.axis_index("x"), lax.axis_index("y"), lax.axis_index("c")
        me, sibling = (x, y, cc), (x, y, 1 - cc)
        chips = [(1 - x, y), (x, 1 - y), (1 - x, 1 - y)]

        def slot(px, py, pc):
            return out_ref.at[4 * px + 2 * py + pc]

        def copy(k, blk, to, src=None):
            return pltpu.make_async_remote_copy(
                src_ref=slot(*blk) if src is None else src, dst_ref=slot(*blk),
                send_sem=send_sems.at[k], recv_sem=recv_sems.at[k],
                device_id=to, device_id_type=MESH)

        mine = pltpu.make_async_copy(x_ref, slot(*me), local_sem)
        mine.start()
        first = [copy(0, me, sibling, src=x_ref)]
        first += [copy(1 + j, me, (*chip, cc), src=x_ref) for j, chip in enumerate(chips)]
        for cp in first:
            cp.start()
        passed = [copy(4 + j, (*chip, cc), sibling) for j, chip in enumerate(chips)]
        for j, chip in enumerate(chips):
            copy(1 + j, (*chip, cc), me).wait_recv()
            passed[j].start()
        copy(0, sibling, me).wait_recv()
        for j, chip in enumerate(chips):
            copy(4 + j, (*chip, 1 - cc), me).wait_recv()
        for cp in first + passed:
            cp.wait_send()
        mine.wait()

    return pl.pallas_call(
        body, name=name,
        out_shape=jax.ShapeDtypeStruct((N_DEV, r, c), block.dtype),
        in_specs=[ANY], out_specs=ANY,
        scratch_shapes=[pltpu.SemaphoreType.DMA((7,)), pltpu.SemaphoreType.DMA((7,)), pltpu.SemaphoreType.DMA],
    )(block)


N_CHIP = 4


def _pair_exchange(packed):
    _, _, r, c = packed.shape

    def body(src_ref, out_ref, send_sem, recv_sem):
        x, y, cc = lax.axis_index("x"), lax.axis_index("y"), lax.axis_index("c")
        cp = pltpu.make_async_remote_copy(
            src_ref=src_ref.at[1 - cc], dst_ref=out_ref, send_sem=send_sem, recv_sem=recv_sem,
            device_id=(x, y, 1 - cc), device_id_type=MESH)
        cp.start()
        cp.wait()

    return pl.pallas_call(
        body, name="pair_exchange",
        out_shape=jax.ShapeDtypeStruct((N_CHIP, r, c), packed.dtype),
        in_specs=[ANY], out_specs=ANY,
        scratch_shapes=[pltpu.SemaphoreType.DMA, pltpu.SemaphoreType.DMA],
    )(packed)


def _pair_add(core, packed, got, tr):
    _, _, r, c = packed.shape

    def body(core_ref, own_ref, got_ref, out_ref):
        out_ref[...] = (own_ref[0].astype(F32) + got_ref[...].astype(F32)).astype(out_ref.dtype)

    blk = pl.BlockSpec((1, tr, c), lambda j, i, core_ref: (j, i, 0))
    return pl.pallas_call(
        body, name="pair_add",
        grid_spec=pltpu.PrefetchScalarGridSpec(
            num_scalar_prefetch=1, grid=(N_CHIP, r // tr),
            in_specs=[pl.BlockSpec((1, 1, tr, c), lambda j, i, core_ref: (core_ref[0], j, i, 0)), blk],
            out_specs=blk),
        out_shape=jax.ShapeDtypeStruct((N_CHIP, r, c), packed.dtype),
        compiler_params=_cparams(2),
    )(core, packed, got)


HBM = pl.BlockSpec(memory_space=pltpu.HBM)
SEM = pl.BlockSpec(memory_space=pltpu.SEMAPHORE)
EFFECT = pltpu.SideEffectType.DATAFLOW_SIDE_EFFECTING


def _chip_copies(src_ref, land_ref, send_sems, recv_sems):
    x, y, cc = lax.axis_index("x"), lax.axis_index("y"), lax.axis_index("c")
    me = 2 * x + y
    copies = []
    for k in range(1, N_CHIP):
        px, py = (1 - x if k & 2 else x), (1 - y if k & 1 else y)
        copies.append(pltpu.make_async_remote_copy(
            src_ref=src_ref.at[2 * px + py], dst_ref=land_ref.at[me],
            send_sem=send_sems.at[k - 1], recv_sem=recv_sems.at[k - 1],
            device_id=(px, py, cc), device_id_type=MESH))
    return copies


def _direct_copies(src_ref, land_ref, send_sems, recv_sems):
    x, y, cc = lax.axis_index("x"), lax.axis_index("y"), lax.axis_index("c")
    me = 4 * x + 2 * y + cc
    copies = []
    for k in range(1, N_DEV):
        peer = ((1 - x if k & 4 else x), (1 - y if k & 2 else y), (1 - cc if k & 1 else cc))
        copies.append(pltpu.make_async_remote_copy(
            src_ref=src_ref, dst_ref=land_ref.at[me], send_sem=send_sems.at[k - 1], recv_sem=recv_sems.at[k - 1],
            device_id=peer, device_id_type=MESH))
    return copies


def _exchange_start(src, land_shape, copies_of, n_copies, name):
    def body(src_ref, land_ref, send_sems, recv_sems, src_thru, land_thru, token):
        for cp in copies_of(src_ref, land_ref, send_sems, recv_sems):
            cp.start()
        token[...] = jnp.zeros_like(token)

    return pl.pallas_call(
        body, name=name,
        out_shape=(pltpu.SemaphoreType.DMA((n_copies,)), pltpu.SemaphoreType.DMA((n_copies,)),
                   pltpu.HBM(src.shape, src.dtype), pltpu.HBM(land_shape, src.dtype), jax.ShapeDtypeStruct((8, 128), F32)),
        in_specs=(HBM, HBM), out_specs=(SEM, SEM, HBM, HBM, pl.BlockSpec(memory_space=pltpu.VMEM)),
        input_output_aliases={0: 2, 1: 3},
        compiler_params=pltpu.CompilerParams(has_side_effects=EFFECT),
    )(pltpu.with_memory_space_constraint(src, pltpu.HBM),
      pltpu.with_memory_space_constraint(lax.empty(land_shape, src.dtype), pltpu.HBM))


def _exchange_wait(send_sems, recv_sems, src_thru, land_thru, after, copies_of, name):
    def body(src_ref, land_ref, send_sems, recv_sems, after_ref, src_dead, got_ref):
        copies = copies_of(src_ref, land_ref, send_sems, recv_sems)
        for cp in copies:
            cp.wait_send()
        for cp in copies:
            cp.wait_recv()

    return pl.pallas_call(
        body, name=name,
        out_shape=(pltpu.HBM(src_thru.shape, src_thru.dtype), pltpu.HBM(land_thru.shape, land_thru.dtype)),
        in_specs=(HBM, HBM, SEM, SEM, ANY), out_specs=(HBM, HBM), input_output_aliases={0: 0, 1: 1},
        compiler_params=pltpu.CompilerParams(has_side_effects=EFFECT),
    )(src_thru, land_thru, send_sems, recv_sems, after)


def _adam(g, w, m, v):
    m = ADAM_B1 * m + (1.0 - ADAM_B1) * g
    v = ADAM_B2 * v + (1.0 - ADAM_B2) * (g * g)
    m_hat = m / (1.0 - ADAM_B1 ** ADAM_STEP)
    v_hat = v / (1.0 - ADAM_B2 ** ADAM_STEP)
    delta = -ADAM_LR * (m_hat / (jnp.sqrt(v_hat) + ADAM_EPS) + ADAM_WD * w)
    return delta, m, v


def _adam_w_in(parts, wt, mt, vt):
    n = parts.shape[0]
    tc = 256

    def body(p_ref, w_ref, m_ref, v_ref, g_out, d_out, m_out, v_out):
        g = p_ref[0].astype(F32)
        for j in range(1, n):
            g = g + p_ref[j].astype(F32)
        g = g[:ROWS_IN]
        delta, nm, nv = _adam(g, w_ref[...], m_ref[...], v_ref[...])
        g_out[...] = g
        d_out[...] = delta
        m_out[...] = nm
        v_out[...] = nv

    col = pl.BlockSpec((ROWS_IN, tc), lambda i: (0, i))
    return pl.pallas_call(
        body, name="adam_w_in", grid=(D // tc,),
        in_specs=[pl.BlockSpec((n, ROWS_IN_PAD, tc), lambda i: (0, 0, i)), col, col, col],
        out_specs=[col] * 4,
        out_shape=[jax.ShapeDtypeStruct((ROWS_IN, D), F32)] * 4,
        compiler_params=_cparams(1),
    )(parts, wt, mt, vt)


def _reduce_adam(parts, w, m, v, name, tr, first_block):
    n, _, c = parts.shape
    r = w.shape[0]

    def body(p_ref, w_ref, m_ref, v_ref, g_out, d_out, m_out, v_out):
        g = p_ref[0].astype(F32)
        for j in range(1, n):
            g = g + p_ref[j].astype(F32)
        delta, nm, nv = _adam(g, w_ref[...], m_ref[...], v_ref[...])
        g_out[...] = g
        d_out[...] = delta
        m_out[...] = nm
        v_out[...] = nv

    row = pl.BlockSpec((tr, c), lambda i: (i, 0))
    return pl.pallas_call(
        body, name=name, grid=(r // tr,),
        in_specs=[pl.BlockSpec((n, tr, c), lambda i: (0, first_block + i, 0)), row, row, row],
        out_specs=[row] * 4,
        out_shape=[jax.ShapeDtypeStruct((r, c), F32)] * 4,
        compiler_params=_cparams(1),
    )(parts, w, m, v)


def _ada_mod(c_all, w_ada, b_cols):
    def body(c_ref, w_ref, b_ref, o_ref):
        cv = c_ref[...]
        act = (cv * _sig(cv)).astype(BF16)
        o_ref[...] = _nn(act, w_ref[...].astype(BF16)) + b_ref[...]

    return pl.pallas_call(body, name="ada_mod", out_shape=jax.ShapeDtypeStruct((N_DEV, w_ada.shape[1]), F32))(c_all, w_ada, b_cols)


def _ada_bwd(c_all, dmod_cols, w, m, v):
    def body(c_ref, d_ref, w_ref, m_ref, v_ref, g_out, d_out, m_out, v_out):
        cv = c_ref[...]
        act = (cv * _sig(cv)).astype(BF16)
        g = _tn(act, d_ref[...].astype(BF16))
        delta, nm, nv = _adam(g, w_ref[...], m_ref[...], v_ref[...])
        g_out[...] = g
        d_out[...] = delta
        m_out[...] = nm
        v_out[...] = nv

    return pl.pallas_call(body, name="ada_bwd", out_shape=[jax.ShapeDtypeStruct(w.shape, F32)] * 4)(c_all, dmod_cols, w, m, v)


def _in_proj(x, norm_w, mod8, wt):
    s = x.shape[0]
    tm = min(1024, s)
    nk = wt.shape[0] // D

    def body(x_ref, nw_ref, mod_ref, w_ref, proj_ref, h_ref, hs_ref):
        @pl.when(pl.program_id(1) == 0)
        def _():
            xv = x_ref[...]
            rstd = lax.rsqrt(jnp.mean(xv * xv, axis=-1, keepdims=True) + EPS)
            h = (xv * rstd) * nw_ref[...] * (1.0 + mod_ref[1:2, :]) + mod_ref[0:1, :]
            hs_ref[...] = h.astype(BF16)
            h_ref[...] = hs_ref[...]

        proj_ref[...] = _nt(hs_ref[...], w_ref[...]).astype(proj_ref.dtype)

    return pl.pallas_call(
        body, name="in_proj", grid=(s // tm, nk),
        in_specs=[pl.BlockSpec((tm, D), lambda i, k: (i, 0)), _const((1, D)), _const((8, D)),
                  pl.BlockSpec((D, D), lambda i, k: (k, 0))],
        out_specs=[pl.BlockSpec((tm, D), lambda i, k: (i, k)), pl.BlockSpec((tm, D), lambda i, k: (i, 0))],
        out_shape=[jax.ShapeDtypeStruct((s, nk * D), BF16), jax.ShapeDtypeStruct((s, D), BF16)],
        scratch_shapes=[pltpu.VMEM((tm, D), BF16)],
        compiler_params=_cparams(2),
    )(x, norm_w, mod8, wt)


CONV_RC = 128
CONV_LC = 128
GW_RC = 32
SUBLANES = 8


def _shifted_taps(win_ref, weight_of, offsets, r0, lanes):
    acc = jnp.zeros((CONV_RC, CONV_LC), F32)
    for b in range(SUBLANES):
        group = [o for o in offsets if o % SUBLANES == b]
        if not group:
            continue
        rows = CONV_RC if b == 0 else CONV_RC + SUBLANES
        part = jnp.zeros((rows, CONV_LC), F32)
        for o in group:
            part = part + win_ref[pl.ds(r0 + o - b, rows), lanes] * weight_of(o)
        acc = acc + (part if b == 0 else part[b:b + CONV_RC])
    return acc


def _conv_fwd(proj, conv_w, conv_b, ln_w, ln_b):
    s = proj.shape[0]
    tm = min(256, s)
    hb = tm // HALO

    def body(av_ref, ag_ref, avh_ref, agh_ref, gate_ref, cw_ref, cb_ref, lw_ref, lb_ref, u1_ref, ya_ref, win_ref):
        i = pl.program_id(0)
        halo = avh_ref[...].astype(F32) * _sig(agh_ref[...].astype(F32))
        win_ref[0:HALO, :] = jnp.where(i > 0, halo, 0.0)
        win_ref[HALO:HALO + tm, :] = av_ref[...].astype(F32) * _sig(ag_ref[...].astype(F32))
        first = HALO - (KCONV - 1)
        for r0 in range(0, tm, CONV_RC):
            for c0 in range(0, D, CONV_LC):
                acc = _shifted_taps(win_ref, lambda o: cw_ref[o - first:o - first + 1, c0:c0 + CONV_LC],
                                    range(first, first + KCONV), r0, pl.ds(c0, CONV_LC))
                u1_ref[r0:r0 + CONV_RC, c0:c0 + CONV_LC] = acc + cb_ref[:, c0:c0 + CONV_LC]
        u1 = u1_ref[...]
        mu = jnp.mean(u1, axis=-1, keepdims=True)
        xc = u1 - mu
        var = jnp.mean(xc * xc, axis=-1, keepdims=True)
        ln = xc * lax.rsqrt(var + EPS) * lw_ref[...] + lb_ref[...]
        gate = gate_ref[...].astype(F32)
        ya_ref[...] = ((ln * _sig(ln)) * (gate * _sig(gate))).astype(BF16)

    row = lambda k: pl.BlockSpec((tm, D), lambda i: (i, k))
    prev = lambda k: pl.BlockSpec((HALO, D), lambda i: (jnp.maximum(i * hb - 1, 0), k))
    return pl.pallas_call(
        body, name="conv_fwd", grid=(s // tm,),
        in_specs=[row(0), row(1), prev(0), prev(1), row(2), _const((KPAD, D)), _const((1, D)), _const((1, D)), _const((1, D))],
        out_specs=[pl.BlockSpec((tm, D), lambda i: (i, 0))] * 2,
        out_shape=[jax.ShapeDtypeStruct((s, D), F32), jax.ShapeDtypeStruct((s, D), BF16)],
        scratch_shapes=[pltpu.VMEM((tm + HALO, D), F32)],
        compiler_params=_cparams(1),
    )(proj, proj, proj, proj, proj, conv_w, conv_b, ln_w, ln_b)


def _rope_tables(pos_ref, if_ref):
    ang = pos_ref[...].astype(F32) * if_ref[...]
    return jnp.cos(ang), jnp.sin(ang)


def _rms_parts(x):
    rstd = lax.rsqrt(jnp.mean(x * x, axis=-1, keepdims=True) + EPS)
    return x * rstd, rstd


def _mla_prep(proj, pos, inv_freq, qnw, kvnw, wuq, wukv):
    s = proj.shape[0]
    tm = min(512, s)

    def body(p_ref, pos_ref, if_ref, qnw_ref, kvnw_ref, wuq_ref, wukv_ref, q_ref, k_ref, v_ref):
        blk = p_ref[...].astype(F32)
        cos, sin = _rope_tables(pos_ref, if_ref)

        def rope(r):
            x1, x2 = r[:, :HALF], r[:, HALF:]
            return jnp.concatenate([x1 * cos - x2 * sin, x1 * sin + x2 * cos], axis=-1)

        qlat = _rms_parts(blk[:, :QL])[0] * qnw_ref[...]
        kvlat = _rms_parts(blk[:, QL:QL + KVL])[0] * kvnw_ref[...]
        q = _nn(qlat.astype(BF16), wuq_ref[...])
        kv = _nn(kvlat.astype(BF16), wukv_ref[...])
        kr = rope(blk[:, QL + KVL:MLA_COLS])
        for h in range(NH):
            qh = q[:, h * DQK:(h + 1) * DQK]
            q_ref[h] = (jnp.concatenate([qh[:, :NOPE], rope(qh[:, NOPE:])], axis=-1) * ATTN_SCALE).astype(BF16)
            k_ref[h] = jnp.concatenate([kv[:, h * 256:h * 256 + NOPE], kr], axis=-1).astype(BF16)
            v_ref[h] = kv[:, h * 256 + NOPE:(h + 1) * 256].astype(BF16)

    hm = lambda d: pl.BlockSpec((NH, tm, d), lambda i: (0, i, 0))
    return pl.pallas_call(
        body, name="mla_prep", grid=(s // tm,),
        in_specs=[pl.BlockSpec((tm, D), lambda i: (i, 3)), pl.BlockSpec((tm, 1), lambda i: (i, 0)), _const((1, HALF)),
                  _const((1, QL)), _const((1, KVL)), _const((QL, NH * DQK)), _const((KVL, NH * 256))],
        out_specs=[hm(DQK), hm(DQK), hm(DV)],
        out_shape=[jax.ShapeDtypeStruct((NH, s, DQK), BF16), jax.ShapeDtypeStruct((NH, s, DQK), BF16),
                   jax.ShapeDtypeStruct((NH, s, DV), BF16)],
        compiler_params=_cparams(1),
    )(proj, pos, inv_freq, qnw, kvnw, wuq, wukv)


ATTN_SCALE = DQK ** -0.5


def _causal_mask(s, t):
    rows = lax.broadcasted_iota(jnp.int32, (t, t), 0)
    cols = lax.broadcasted_iota(jnp.int32, (t, t), 1)
    return jnp.where(cols <= rows, s, -jnp.inf)


def _attn_tile(s):
    return min(1024, s // 2)


def _attn_fwd(q, k, v):
    nh, s, _ = q.shape
    t = _attn_tile(s)

    def body(q_ref, k_ref, v_ref, o_ref, lse_ref):
        qi = pl.program_id(1)
        qv = q_ref[0]

        def chunk(c, carry, diag):
            m, l, acc = carry
            rows = pl.ds(pl.multiple_of(c * t, t), t)
            sc = _nt(qv, k_ref[0, rows, :])
            if diag:
                sc = _causal_mask(sc, t)
            m_new = jnp.maximum(m, jnp.max(sc, axis=-1, keepdims=True))
            alpha = jnp.exp(m - m_new)
            p = jnp.exp(sc - m_new)
            l = alpha * l + jnp.sum(p, axis=-1, keepdims=True)
            acc = alpha * acc + _nn(p.astype(BF16), v_ref[0, rows, :])
            return m_new, l, acc

        init = (jnp.full((t, 1), -jnp.inf, F32), jnp.zeros((t, 1), F32), jnp.zeros((t, DV), F32))
        carry = lax.fori_loop(0, qi, lambda c, cr: chunk(c, cr, False), init)
        m, l, acc = chunk(qi, carry, True)
        o_ref[...] = acc / l
        lse_ref[0] = jnp.broadcast_to(m + jnp.log(l), (t, DV))

    head = lambda d: pl.BlockSpec((1, s, d), lambda h, i: (h, 0, 0))
    return pl.pallas_call(
        body, name="attn_fwd", grid=(nh, s // t),
        in_specs=[pl.BlockSpec((1, t, DQK), lambda h, i: (h, i, 0)), head(DQK), head(DV)],
        out_specs=[pl.BlockSpec((t, DV), lambda h, i: (i, h)), pl.BlockSpec((1, t, DV), lambda h, i: (h, i, 0))],
        out_shape=[jax.ShapeDtypeStruct((s, nh * DV), F32), jax.ShapeDtypeStruct((nh, s, DV), F32)],
        compiler_params=_cparams(2),
    )(q, k, v)


def _merge_loss(x, target, ya, o, proj, mod8, fnw, gathered):
    s = x.shape[0]
    tm = min(256, s)
    n = s // tm

    def body(x_ref, t_ref, ya_ref, o_ref, bg_ref, ga_ref, gb_ref, mod_ref, fnw_ref, wco_ref, wao_ref, wout_ref,
             dya_ref, do_ref, delta_ref, dpg_ref, dx2_ref, gw_ref, small_ref, acc_ref, cast_ref):
        i = pl.program_id(0)

        @pl.when(i == 0)
        def _():
            acc_ref[...] = jnp.zeros(acc_ref.shape, F32)
            small_ref[...] = jnp.zeros(small_ref.shape, F32)

        bg = bg_ref[...].astype(F32)
        sbg = _sig(bg)
        sb = bg * sbg
        ov = o_ref[...]
        ya = ya_ref[...]
        yb = (ov * sb).astype(BF16)
        square = lambda ref: ref[...].reshape(D, D)
        y_a = _nn(ya, square(wco_ref))
        y_b = _nn(yb, square(wao_ref))
        sa = _sig(ga_ref[...].astype(F32))
        sgb = _sig(gb_ref[...].astype(F32))
        merged = (sa * y_a + sgb * y_b).astype(BF16)
        z = _nn(merged, square(wout_ref))
        gate = mod_ref[2:3, :]
        x2 = x_ref[...] + gate * z
        xn, rstd = _rms_parts(x2)
        fnw = fnw_ref[...]
        err = xn * fnw - t_ref[...]
        loss = jnp.sum(jnp.sum(err * err, axis=-1, keepdims=True), axis=0, keepdims=True) * (0.5 / D)
        dy = err * (1.0 / D)
        small_ref[0:1, :] += jnp.sum(dy * xn, axis=0, keepdims=True)
        dxn = dy * fnw
        dx2 = rstd * (dxn - xn * jnp.mean(dxn * xn, axis=-1, keepdims=True))
        dx2_ref[...] = dx2
        small_ref[1:2, :] += jnp.sum(dx2 * z, axis=0, keepdims=True)
        small_ref[2:3, :] += jnp.broadcast_to(loss, (1, D))
        dz = (dx2 * gate).astype(BF16)
        dmerged = _nt(dz, square(wout_ref))
        acc_ref[2] += _tn(merged, dz)
        dy_a = (dmerged * sa).astype(BF16)
        dy_b = (dmerged * sgb).astype(BF16)
        dpg_ref[:, D:2 * D] = (dmerged * y_a * (sa * (1.0 - sa))).astype(BF16)
        dpg_ref[:, 2 * D:3 * D] = (dmerged * y_b * (sgb * (1.0 - sgb))).astype(BF16)
        dya_ref[...] = _nt(dy_a, square(wco_ref))
        acc_ref[0] += _tn(ya, dy_a)
        dyb = _nt(dy_b, square(wao_ref))
        acc_ref[1] += _tn(yb, dy_b)
        do = dyb * sb
        do_ref[...] = do.astype(BF16)
        dpg_ref[:, 0:D] = (dyb * ov * (sbg * (1.0 + bg * (1.0 - sbg)))).astype(BF16)
        prod = do * ov
        for h in range(NH):
            delta_ref[h] = jnp.broadcast_to(jnp.sum(prod[:, h * DV:(h + 1) * DV], axis=-1, keepdims=True), (tm, DV))

        @pl.when(i == n - 1)
        def _():
            for j in range(3):
                cast_ref[...] = acc_ref[j].astype(cast_ref.dtype)
                pltpu.sync_copy(cast_ref, gw_ref.at[j])

    row = pl.BlockSpec((tm, D), lambda i: (i, 0))
    col = lambda k: pl.BlockSpec((tm, D), lambda i: (i, k))
    wspec = lambda j: pl.BlockSpec((N_DEV, ROWS_SQ, D), lambda i: (0, j, 0), pipeline_mode=pl.Buffered(1))
    return pl.pallas_call(
        body, name="merge_loss", grid=(n,),
        in_specs=[row, row, row, row, col(4), col(5), col(6), _const((8, D)), _const((1, D)), wspec(0), wspec(1), wspec(2)],
        out_specs=[row, row, pl.BlockSpec((NH, tm, DV), lambda i: (0, i, 0)), pl.BlockSpec((tm, 3 * D), lambda i: (i, 0)),
                   row, ANY, _const((8, D))],
        out_shape=[jax.ShapeDtypeStruct((s, D), F32), jax.ShapeDtypeStruct((s, D), BF16),
                   jax.ShapeDtypeStruct((NH, s, DV), F32), jax.ShapeDtypeStruct((s, 3 * D), BF16),
                   jax.ShapeDtypeStruct((s, D), F32), jax.ShapeDtypeStruct((3, D, D), BF16),
                   jax.ShapeDtypeStruct((8, D), F32)],
        scratch_shapes=[pltpu.VMEM((3, D, D), F32), pltpu.VMEM((D, D), BF16)],
        compiler_params=_cparams(1),
    )(x, target, ya, o, proj, proj, proj, mod8, fnw, gathered, gathered, gathered)


def _attn_bwd(q, k, v, do, lse, delta):
    nh, s, _ = q.shape
    t = _attn_tile(s)
    nb = s // t

    def body(q_ref, k_ref, v_ref, do_ref, lse_ref, dl_ref, dq_ref, dk_ref, dv_ref):
        kj = pl.program_id(1)

        @pl.when(kj == 0)
        def _():
            dq_ref[...] = jnp.zeros(dq_ref.shape, F32)

        kv_, vv = k_ref[0], v_ref[0]

        def chunk(c, carry, diag):
            dk, dv = carry
            rows = pl.ds(pl.multiple_of(c * t, t), t)
            qv = q_ref[0, rows, :]
            dov = do_ref[rows, :]
            sc = _nt(qv, kv_)
            if diag:
                sc = _causal_mask(sc, t)
            p = jnp.exp(sc - lse_ref[0, rows, 0:1])
            dv = dv + _tn(p.astype(BF16), dov)
            dp = _nt(dov, vv)
            ds = (p * (dp - dl_ref[0, rows, 0:1])).astype(BF16)
            dk = dk + _tn(ds, qv)
            dq_ref[0, rows, :] += _nn(ds, kv_)
            return dk, dv

        carry = chunk(kj, (jnp.zeros((t, DQK), F32), jnp.zeros((t, DV), F32)), True)
        dk, dv = lax.fori_loop(kj + 1, nb, lambda c, cr: chunk(c, cr, False), carry)
        dk_ref[0] = dk
        dv_ref[0] = dv

    head = lambda d: pl.BlockSpec((1, s, d), lambda h, j: (h, 0, 0))
    blk = lambda d: pl.BlockSpec((1, t, d), lambda h, j: (h, j, 0))
    return pl.pallas_call(
        body, name="attn_bwd", grid=(nh, nb),
        in_specs=[head(DQK), blk(DQK), blk(DV), pl.BlockSpec((s, DV), lambda h, j: (0, h)), head(DV), head(DV)],
        out_specs=[head(DQK), blk(DQK), blk(DV)],
        out_shape=[jax.ShapeDtypeStruct((nh, s, DQK), F32), jax.ShapeDtypeStruct((nh, s, DQK), F32),
                   jax.ShapeDtypeStruct((nh, s, DV), F32)],
        compiler_params=_cparams(2),
    )(q, k, v, do, lse, delta)


def _mla_bwd(dq, dk, dv, proj, pos, inv_freq, qnw, kvnw, wuq, wukv):
    s = proj.shape[0]
    tm = min(512, s)

    def body(dq_ref, dk_ref, dv_ref, p_ref, pos_ref, if_ref, qnw_ref, kvnw_ref, wuq_ref, wukv_ref,
             dp_ref, guq_ref, gukv_ref, small_ref):
        @pl.when(pl.program_id(0) == 0)
        def _():
            guq_ref[...] = jnp.zeros(guq_ref.shape, F32)
            gukv_ref[...] = jnp.zeros(gukv_ref.shape, F32)
            small_ref[...] = jnp.zeros(small_ref.shape, F32)

        blk = p_ref[...].astype(F32)
        cos, sin = _rope_tables(pos_ref, if_ref)

        def unrope(g):
            g1, g2 = g[:, :HALF], g[:, HALF:]
            return jnp.concatenate([g1 * cos + g2 * sin, g2 * cos - g1 * sin], axis=-1)

        dq_cols, dkv_cols = [], []
        dkr = jnp.zeros((tm, ROPE), F32)
        for h in range(NH):
            dqh, dkh = dq_ref[h] * ATTN_SCALE, dk_ref[h]
            dq_cols += [dqh[:, :NOPE], unrope(dqh[:, NOPE:])]
            dkv_cols += [dkh[:, :NOPE], dv_ref[h]]
            dkr = dkr + dkh[:, NOPE:]
        dq_full = jnp.concatenate(dq_cols, axis=-1).astype(BF16)
        dkv_full = jnp.concatenate(dkv_cols, axis=-1).astype(BF16)

        def latent_bwd(c, nw_ref, d_up, w_ref, g_ref, srow):
            nrm, rstd = _rms_parts(c)
            nw = nw_ref[...]
            lat = (nrm * nw).astype(BF16)
            g_ref[...] += _tn(lat, d_up)
            dlat = _nt(d_up, w_ref[...])
            small_ref[srow:srow + 1, :] += jnp.sum(dlat * nrm, axis=0, keepdims=True)
            dn = dlat * nw
            return rstd * (dn - nrm * jnp.mean(dn * nrm, axis=-1, keepdims=True))

        dcq = latent_bwd(blk[:, :QL], qnw_ref, dq_full, wuq_ref, guq_ref, 0)
        dckv = latent_bwd(blk[:, QL:QL + KVL], kvnw_ref, dkv_full, wukv_ref, gukv_ref, 1)
        dp_ref[...] = jnp.concatenate([dcq, dckv, unrope(dkr), jnp.zeros((tm, D - MLA_COLS), F32)], axis=-1).astype(BF16)

    hm = lambda d: pl.BlockSpec((NH, tm, d), lambda i: (0, i, 0))
    return pl.pallas_call(
        body, name="mla_bwd", grid=(s // tm,),
        in_specs=[hm(DQK), hm(DQK), hm(DV), pl.BlockSpec((tm, D), lambda i: (i, 3)), pl.BlockSpec((tm, 1), lambda i: (i, 0)),
                  _const((1, HALF)), _const((1, QL)), _const((1, KVL)), _const((QL, NH * DQK)), _const((KVL, NH * 256))],
        out_specs=[pl.BlockSpec((tm, D), lambda i: (i, 0)), _const((QL, NH * DQK)), _const((KVL, NH * 256)), _const((8, QL))],
        out_shape=[jax.ShapeDtypeStruct((s, D), BF16), jax.ShapeDtypeStruct((QL, NH * DQK), F32),
                   jax.ShapeDtypeStruct((KVL, NH * 256), F32), jax.ShapeDtypeStruct((8, QL), F32)],
        compiler_params=_cparams(1),
    )(dq, dk, dv, proj, pos, inv_freq, qnw, kvnw, wuq, wukv)


def _conv_rows_bwd(dya, u1, proj, ln_w, ln_b):
    s = dya.shape[0]
    tm = min(512, s)

    def body(dya_ref, u1_ref, gate_ref, lw_ref, lb_ref, du1_ref, dag_ref, small_ref):
        @pl.when(pl.program_id(0) == 0)
        def _():
            small_ref[...] = jnp.zeros(small_ref.shape, F32)

        u1 = u1_ref[...]
        mu = jnp.mean(u1, axis=-1, keepdims=True)
        xc = u1 - mu
        rstd = lax.rsqrt(jnp.mean(xc * xc, axis=-1, keepdims=True) + EPS)
        xhat = xc * rstd
        lw = lw_ref[...]
        ln = xhat * lw + lb_ref[...]
        sl = _sig(ln)
        u2 = ln * sl
        gate = gate_ref[...].astype(F32)
        sg = _sig(gate)
        dya = dya_ref[...]
        dag_ref[...] = (dya * u2 * (sg * (1.0 + gate * (1.0 - sg)))).astype(BF16)
        dln = dya * (gate * sg) * (sl * (1.0 + ln * (1.0 - sl)))
        small_ref[0:1, :] += jnp.sum(dln * xhat, axis=0, keepdims=True)
        small_ref[1:2, :] += jnp.sum(dln, axis=0, keepdims=True)
        dxh = dln * lw
        du1_ref[...] = rstd * (dxh - jnp.mean(dxh, axis=-1, keepdims=True) - xhat * jnp.mean(dxh * xhat, axis=-1, keepdims=True))

    row = pl.BlockSpec((tm, D), lambda i: (i, 0))
    return pl.pallas_call(
        body, name="conv_rows_bwd", grid=(s // tm,),
        in_specs=[row, row, pl.BlockSpec((tm, D), lambda i: (i, 2)), _const((1, D)), _const((1, D))],
        out_specs=[row, row, _const((8, D))],
        out_shape=[jax.ShapeDtypeStruct((s, D), F32), jax.ShapeDtypeStruct((s, D), BF16), jax.ShapeDtypeStruct((8, D), F32)],
        compiler_params=_cparams(1),
    )(dya, u1, proj, ln_w, ln_b)


def _conv_bwd(du1, proj, conv_w):
    s = du1.shape[0]
    tm = min(256, s)
    hb = tm // HALO
    n = s // tm
    last32 = s // HALO - 1

    def body(d_ref, dn_ref, av_ref, ag_ref, avh_ref, agh_ref, cw_ref, dp_ref, gcw_ref, small_ref,
             dwin_ref, dpad_ref, dsh_ref, uwin_ref, acc_ref):
        i = pl.program_id(0)

        @pl.when(i == 0)
        def _():
            acc_ref[...] = jnp.zeros(acc_ref.shape, F32)
            small_ref[...] = jnp.zeros(small_ref.shape, F32)
            dpad_ref[...] = jnp.zeros(dpad_ref.shape, F32)
            uwin_ref[...] = jnp.zeros(uwin_ref.shape, F32)

        dv = d_ref[...]
        dwin_ref[0:tm, :] = dv
        dwin_ref[tm:tm + HALO, :] = jnp.where(i < n - 1, dn_ref[...], 0.0)
        dpad_ref[SUBLANES:SUBLANES + tm, :] = dv
        halo = avh_ref[...].astype(F32) * _sig(agh_ref[...].astype(F32))
        uwin_ref[0:HALO, :] = jnp.where(i > 0, halo, 0.0)
        av = av_ref[...].astype(F32)
        sg = _sig(ag_ref[...].astype(F32))
        uwin_ref[HALO:HALO + tm, :] = av * sg
        small_ref[0:1, :] += jnp.sum(dv, axis=0, keepdims=True)

        for b in range(SUBLANES):
            dsh_ref[b] = dpad_ref[pl.ds(SUBLANES - b, tm + SUBLANES), :]

        first = HALO - (KCONV - 1)
        for c0 in range(0, D, CONV_LC):
            lanes = pl.ds(c0, CONV_LC)
            for r0 in range(0, tm, CONV_RC):
                acc = _shifted_taps(dwin_ref, lambda o: cw_ref[KCONV - 1 - o:KCONV - o, c0:c0 + CONV_LC], range(KCONV), r0, lanes)
                a = av[r0:r0 + CONV_RC, c0:c0 + CONV_LC]
                g = sg[r0:r0 + CONV_RC, c0:c0 + CONV_LC]
                dp_ref[r0:r0 + CONV_RC, c0:c0 + CONV_LC] = (acc * g).astype(BF16)
                dp_ref[r0:r0 + CONV_RC, D + c0:D + c0 + CONV_LC] = (acc * a * (g * (1.0 - g))).astype(BF16)
            for b in range(SUBLANES):
                group = [o for o in range(first, first + KCONV) if o % SUBLANES == b]
                parts = [jnp.zeros((SUBLANES, CONV_LC), F32) for _ in group]
                chunks = [(i0, GW_RC) for i0 in range(0, tm, GW_RC)] + ([(tm, SUBLANES)] if b else [])
                for i0, rows in chunks:
                    dsh = dsh_ref[b, pl.ds(i0, rows), lanes]
                    for n_, o in enumerate(group):
                        prod = dsh * uwin_ref[pl.ds(i0 + o - b, rows), lanes]
                        parts[n_] = parts[n_] + jnp.sum(prod.reshape(rows // SUBLANES, SUBLANES, CONV_LC), axis=0)
                for n_, o in enumerate(group):
                    acc_ref[o - first, :, c0:c0 + CONV_LC] += parts[n_]

        @pl.when(i == n - 1)
        def _():
            gcw_ref[...] = jnp.sum(acc_ref[...], axis=1)

    row = lambda k: pl.BlockSpec((tm, D), lambda i: (i, k))
    prev = lambda k: pl.BlockSpec((HALO, D), lambda i: (jnp.maximum(i * hb - 1, 0), k))
    return pl.pallas_call(
        body, name="conv_bwd", grid=(n,),
        in_specs=[row(0), pl.BlockSpec((HALO, D), lambda i: (jnp.minimum((i + 1) * hb, last32), 0)),
                  row(0), row(1), prev(0), prev(1), _const((KPAD, D))],
        out_specs=[pl.BlockSpec((tm, 2 * D), lambda i: (i, 0)), _const((KPAD, D)), _const((8, D))],
        out_shape=[jax.ShapeDtypeStruct((s, 2 * D), BF16), jax.ShapeDtypeStruct((KPAD, D), F32), jax.ShapeDtypeStruct((8, D), F32)],
        scratch_shapes=[pltpu.VMEM((tm + HALO, D), F32), pltpu.VMEM((tm + 2 * SUBLANES, D), F32),
                        pltpu.VMEM((SUBLANES, tm + SUBLANES, D), F32),
                        pltpu.VMEM((tm + HALO + SUBLANES, D), F32), pltpu.VMEM((KPAD, SUBLANES, D), F32)],
        compiler_params=_cparams(1),
    )(du1, du1, proj, proj, proj, proj, conv_w)


def _dproj_specs(tm, rows_first):
    def spec(lo, hi):
        def idx(a, b):
            i, k = (a, b) if rows_first else (b, a)
            col = jnp.clip(k - lo, 0, hi - lo - 1)
            if rows_first:
                return (i, col)
            return (jnp.where((k >= lo) & (k < hi), i, 0), col)
        return pl.BlockSpec((tm, D), idx)
    return [spec(0, 2), spec(2, 3), spec(3, 4), spec(4, 7)]


def _pick_dproj(k, refs, fn):
    vg, ag, mla, gates = refs

    @pl.when(k < 2)
    def _():
        fn(vg)

    @pl.when(k == 2)
    def _():
        fn(ag)

    @pl.when(k == 3)
    def _():
        fn(mla)

    @pl.when(k > 3)
    def _():
        fn(gates)


def _in_proj_bwd_x(dps, wt, x, dx2, norm_w, mod8):
    s = x.shape[0]
    tm = min(1024, s)
    nk = wt.shape[0] // D

    def body(vg_ref, ag_ref, mla_ref, g_ref, w_ref, x_ref, dx2_ref, nw_ref, mod_ref, gx_ref, small_ref, acc_ref):
        i, k = pl.program_id(0), pl.program_id(1)

        @pl.when((i == 0) & (k == 0))
        def _():
            small_ref[...] = jnp.zeros(small_ref.shape, F32)

        @pl.when(k == 0)
        def _():
            acc_ref[...] = jnp.zeros(acc_ref.shape, F32)

        def add(ref):
            acc_ref[...] += _nn(ref[...], w_ref[...])

        _pick_dproj(k, (vg_ref, ag_ref, mla_ref, g_ref), add)

        @pl.when(k == nk - 1)
        def _():
            dh = acc_ref[...]
            xn, rstd = _rms_parts(x_ref[...])
            nw = nw_ref[...]
            hn = xn * nw
            small_ref[0:1, :] += jnp.sum(dh, axis=0, keepdims=True)
            small_ref[1:2, :] += jnp.sum(dh * hn, axis=0, keepdims=True)
            dhn = dh * (1.0 + mod_ref[1:2, :])
            small_ref[2:3, :] += jnp.sum(dhn * xn, axis=0, keepdims=True)
            dxn = dhn * nw
            gx_ref[...] = rstd * (dxn - xn * jnp.mean(dxn * xn, axis=-1, keepdims=True)) + dx2_ref[...]

    row = pl.BlockSpec((tm, D), lambda i, k: (i, 0))
    return pl.pallas_call(
        body, name="in_proj_bwd_x", grid=(s // tm, nk),
        in_specs=_dproj_specs(tm, True) + [pl.BlockSpec((D, D), lambda i, k: (k, 0)), row, row, _const((1, D)), _const((8, D))],
        out_specs=[row, _const((8, D))],
        out_shape=[jax.ShapeDtypeStruct((s, D), F32), jax.ShapeDtypeStruct((8, D), F32)],
        scratch_shapes=[pltpu.VMEM((tm, D), F32)],
        compiler_params=_cparams(2),
    )(*dps, wt, x, dx2, norm_w, mod8)


def _in_proj_bwd_w(dps, h, nk):
    s = h.shape[0]
    tm = min(1024, s)

    n = s // tm

    def body(vg_ref, ag_ref, mla_ref, g_ref, h_ref, gw_ref, acc_ref):
        k, i = pl.program_id(0), pl.program_id(1)

        @pl.when(i == 0)
        def _():
            acc_ref[...] = jnp.zeros(acc_ref.shape, F32)

        def add(ref):
            acc_ref[...] += _tn(ref[...], h_ref[...])

        _pick_dproj(k, (vg_ref, ag_ref, mla_ref, g_ref), add)

        @pl.when(i == n - 1)
        def _():
            gw_ref[...] = acc_ref[...].astype(gw_ref.dtype)

    return pl.pallas_call(
        body, name="in_proj_bwd_w", grid=(nk, n),
        in_specs=_dproj_specs(tm, False) + [pl.BlockSpec((tm, D), lambda k, i: (i, 0))],
        out_specs=pl.BlockSpec((D, D), lambda k, i: (k, 0)),
        out_shape=jax.ShapeDtypeStruct((nk * D, D), BF16),
        scratch_shapes=[pltpu.VMEM((D, D), F32)],
        compiler_params=_cparams(2),
    )(*dps, h)


def _small_slab(wuq, wukv, conv_w):
    cw = jnp.pad(conv_w.reshape(-1), (0, (ROWS_TAIL - ROWS_UQ - ROWS_UKV) * D - KCONV * 128)).reshape(-1, D)
    return jnp.concatenate([wuq.reshape(ROWS_UQ, D), wukv.reshape(ROWS_UKV, D), cw], axis=0)


def _split_small_slab(slab):
    return (slab[..., :ROWS_UQ, :], slab[..., ROWS_UQ:ROWS_UQ + ROWS_UKV, :],
            slab[..., ROWS_UQ + ROWS_UKV:ROWS_UQ + ROWS_UKV + ROWS_CW, :])


def _unpack_small_slab(slab):
    wuq, wukv, cw = _split_small_slab(slab)
    return (wuq.reshape(QL, NH * DQK // N_DEV), wukv.reshape(KVL, NH * 256 // N_DEV),
            cw.reshape(-1)[:KCONV * 128].reshape(KCONV, 128))


def _pack_shard(w_in, wco, wao, wout, wuq, wukv, conv_w):
    bf = lambda a: a.astype(BF16)
    first = jnp.concatenate([jnp.pad(bf(w_in).T, ((0, ROWS_IN_PAD - ROWS_IN), (0, 0))), bf(_small_slab(wuq, wukv, conv_w))], axis=0)
    return first, jnp.concatenate([bf(wco), bf(wao), bf(wout)], axis=0)


def _unpack_gathered(g):
    wt = g[:, :ROWS_IN].reshape(IN_COLS, D)
    split = 3 * D + MLA_COLS
    wt = jnp.concatenate([wt[:split], jnp.zeros((D - MLA_COLS, D), g.dtype), wt[split:]], axis=0)
    wuq, wukv, cw = _split_small_slab(g[:, OFF_TAIL:OFF_TAIL + ROWS_TAIL])
    wuq = wuq.reshape(N_DEV, QL, NH * DQK // N_DEV).transpose(1, 0, 2).reshape(QL, NH * DQK)
    wukv = wukv.reshape(N_DEV, KVL, NH * 256 // N_DEV).transpose(1, 0, 2).reshape(KVL, NH * 256)
    cw = cw.reshape(N_DEV, ROWS_CW * D)[:, :KCONV * 128].reshape(N_DEV, KCONV, 128).transpose(1, 0, 2).reshape(KCONV, D)
    return wt, wuq, wukv, cw


def _pack_grads(gwt, gw3, guq, gukv, gcw):
    split = 3 * D + MLA_COLS
    g_in = jnp.concatenate([gwt[:split], gwt[4 * D:]], axis=0).reshape(N_DEV, ROWS_IN, D)
    g_in = jnp.pad(g_in, ((0, 0), (0, ROWS_IN_PAD - ROWS_IN), (0, 0)))
    guq = guq.reshape(QL, N_DEV, -1).transpose(1, 0, 2).reshape(N_DEV, ROWS_UQ, D)
    gukv = gukv.reshape(KVL, N_DEV, -1).transpose(1, 0, 2).reshape(N_DEV, ROWS_UKV, D)
    gcw = gcw.reshape(KCONV, N_DEV, 128).transpose(1, 0, 2).reshape(N_DEV, KCONV * 128)
    gcw = jnp.pad(gcw, ((0, 0), (0, (ROWS_TAIL - ROWS_UQ - ROWS_UKV) * D - KCONV * 128))).reshape(N_DEV, -1, D)
    gsq = gw3.reshape(3, N_DEV, ROWS_SQ, D).transpose(1, 0, 2, 3).reshape(N_DEV, 3 * ROWS_SQ, D)
    slabs = jnp.concatenate([a.astype(BF16) for a in (g_in, guq, gukv, gcw, gsq)], axis=1)
    return slabs.reshape(N_CHIP, 2, ROWS_PACK, D).transpose(1, 0, 2, 3)


def _pack_small(vecs):
    flat = jnp.concatenate([v.reshape(-1) for v in vecs])
    return jnp.pad(flat, (0, SMALL_LEN - flat.shape[0])).reshape(8, SMALL_COLS)


def _unpack_small(a, shapes):
    flat = a.reshape(-1)
    out, off = [], 0
    for shp, n in zip(shapes, SMALL_SIZES):
        out.append(flat[off:off + n].reshape(shp))
        off += n
    return out


def kernel(x, c, positions, w_ada, b_ada, norm_w, w_in, conv_w, conv_b, conv_ln_w, conv_ln_b, w_conv_out, q_norm_w, w_uq, kv_norm_w, w_ukv, w_attn_out, w_out, final_norm_w, loss_target, m_w_ada, m_b_ada, m_norm_w, m_w_in, m_conv_w, m_conv_b, m_conv_ln_w, m_conv_ln_b, m_w_conv_out, m_q_norm_w, m_w_uq, m_kv_norm_w, m_w_ukv, m_w_attn_out, m_w_out, m_final_norm_w, v_w_ada, v_b_ada, v_norm_w, v_w_in, v_conv_w, v_conv_b, v_conv_ln_w, v_conv_ln_b, v_w_conv_out, v_q_norm_w, v_w_uq, v_kv_norm_w, v_w_ukv, v_w_attn_out, v_w_out, v_final_norm_w):
    me = 4 * lax.axis_index("x") + 2 * lax.axis_index("y") + lax.axis_index("c")
    xs, tgt = x[0], loss_target[0]
    s = xs.shape[0]
    ada_cols = w_ada.shape[2]

    sharded = lambda t: tuple(a[0] for a in t)
    slab_first, slab_sq = _pack_shard(*sharded((w_in, w_conv_out, w_attn_out, w_out, w_uq, w_ukv, conv_w)))
    wt, wuq, wukv, cw = _unpack_gathered(_all_gather(slab_first, "gather_weights"))
    cw32 = jnp.pad(cw.astype(F32), ((0, KPAD - KCONV), (0, 0)))

    c_all = _all_gather(jnp.broadcast_to(c, (8, D)), "gather_c")[:, 0, :]
    b_cols = lax.dynamic_slice(b_ada, (0, me * ada_cols), (1, ada_cols))
    mod_cols = _all_gather(_ada_mod(c_all, w_ada[0], b_cols), "gather_mod")
    mod = lax.dynamic_index_in_dim(mod_cols, me, axis=1, keepdims=False).reshape(3, D)
    mod8 = jnp.pad(mod, ((0, 5), (0, 0)))

    slab_sq, mod8 = lax.optimization_barrier((slab_sq, mod8))
    sq_send, sq_recv, sq_src, sq_land, sq_token = _exchange_start(slab_sq, (N_DEV,) + slab_sq.shape, _direct_copies, N_DEV - 1,
                                                                  "square_gather_start")

    pos = positions.reshape(s, 1)
    inv_freq = (ROPE_THETA ** (-jnp.arange(0, ROPE, 2, dtype=F32) / ROPE)).reshape(1, HALF)
    proj, h = _in_proj(xs, norm_w + sq_token[0:1, 0:1], mod8, wt)
    u1, ya = _conv_fwd(proj, cw32, conv_b, conv_ln_w, conv_ln_b)
    q, k, v = _mla_prep(proj, pos, inv_freq, q_norm_w, kv_norm_w, wuq, wukv)
    o, lse = _attn_fwd(q, k, v)

    slab_sq, gathered_sq = _exchange_wait(sq_send, sq_recv, sq_src, sq_land, lse, _direct_copies, "square_gather_wait")
    gathered_sq = lax.dynamic_update_slice(gathered_sq, slab_sq[None], (me, 0, 0))
    dya, do, delta, dp_gates, dx2, gw3, small_a = _merge_loss(xs, tgt, ya, o, proj, mod8, final_norm_w.reshape(1, D), gathered_sq)
    dq, dk, dv = _attn_bwd(q, k, v, do, lse, delta)
    dp_mla, guq, gukv, small_b = _mla_bwd(dq, dk, dv, proj, pos, inv_freq, q_norm_w, kv_norm_w, wuq, wukv)
    du1, dp_ag, small_c = _conv_rows_bwd(dya, u1, proj, conv_ln_w, conv_ln_b)
    dp_vg, gcw, small_d = _conv_bwd(du1, proj, cw32)
    dps = [dp_vg, dp_ag, dp_mla, dp_gates]
    gwt = _in_proj_bwd_w(dps, h, wt.shape[0] // D)

    packed = _pack_grads(gwt, gw3, guq, gukv, gcw[:KCONV])
    core = lax.axis_index("c").astype(jnp.int32).reshape(1)
    half = _pair_add(core, packed, _pair_exchange(packed), ROWS_PACK // 4)
    send_sems, recv_sems, half_thru, land_thru, token = _exchange_start(half, half.shape, _chip_copies, N_CHIP - 1, "chip_exchange_start")
    grad_x, small_e = _in_proj_bwd_x(dps, wt, xs, dx2, norm_w + token[0:1, 0:1], mod8)
    half, recv = _exchange_wait(send_sems, recv_sems, half_thru, land_thru, small_e, _chip_copies, "chip_exchange_wait")
    chip = 2 * lax.axis_index("x") + lax.axis_index("y")
    recv = lax.dynamic_update_slice(recv, lax.dynamic_slice(half, (chip, 0, 0), (1,) + half.shape[1:]), (chip, 0, 0))
    big_in = [a.T for a in _adam_w_in(recv, w_in[0].T, m_w_in[0].T, v_w_in[0].T)]
    squares = (("w_conv_out", w_conv_out, m_w_conv_out, v_w_conv_out), ("w_attn_out", w_attn_out, m_w_attn_out, v_w_attn_out),
               ("w_out", w_out, m_w_out, v_w_out))
    big_sq = [_reduce_adam(recv, w[0], m[0], v[0], "adam_" + nm, ROWS_SQ, OFF_SQ // ROWS_SQ + j) for j, (nm, w, m, v) in enumerate(squares)]
    tail = _reduce_adam(recv, _small_slab(w_uq[0], w_ukv[0], conv_w[0]), _small_slab(m_w_uq[0], m_w_ukv[0], m_conv_w[0]),
                        _small_slab(v_w_uq[0], v_w_ukv[0], v_conv_w[0]), "adam_small_sharded", ROWS_TAIL, OFF_TAIL // ROWS_TAIL)
    tail = [_unpack_small_slab(a) for a in tail]
    big = [(big_in[i], big_sq[0][i], big_sq[1][i], big_sq[2][i], *tail[i]) for i in range(4)]

    dmod = jnp.concatenate([small_e[0], small_e[1], small_a[1]])
    payload = _pack_small([dmod, small_e[2], small_d[0], small_c[0], small_c[1], small_a[0], small_b[0], small_b[1], small_a[2, 0:1]])
    pay_all = _all_gather(payload, "gather_small")
    small_w = (b_ada, norm_w, conv_b, conv_ln_w, conv_ln_b, final_norm_w, q_norm_w, kv_norm_w)
    small_m = (m_b_ada, m_norm_w, m_conv_b, m_conv_ln_w, m_conv_ln_b, m_final_norm_w, m_q_norm_w, m_kv_norm_w)
    small_v = (v_b_ada, v_norm_w, v_conv_b, v_conv_ln_w, v_conv_ln_b, v_final_norm_w, v_q_norm_w, v_kv_norm_w)
    sm = _reduce_adam(pay_all, _pack_small(small_w), _pack_small(small_m), _pack_small(small_v), "adam_replicated", 8, 0)
    loss = sm[0].reshape(-1)[sum(SMALL_SIZES)]
    shapes = [t.shape for t in small_w]
    sm = [_unpack_small(a, shapes) for a in sm]

    dmod_all = pay_all.reshape(N_DEV, SMALL_LEN)[:, :3 * D]
    dmod_cols = lax.dynamic_slice(dmod_all, (0, me * ada_cols), (N_DEV, ada_cols))
    ada = _ada_bwd(c_all, dmod_cols, w_ada[0], m_w_ada[0], v_w_ada[0])

    def group(i):
        b_in, b_co, b_ao, b_out, b_uq, b_ukv, b_cw = big[i]
        s_bada, s_nw, s_cb, s_clw, s_clb, s_fnw, s_qnw, s_kvnw = sm[i]
        return (ada[i][None], s_bada, s_nw, b_in[None], b_cw[None], s_cb, s_clw, s_clb, b_co[None], s_qnw, b_uq[None], s_kvnw,
                b_ukv[None], b_ao[None], b_out[None], s_fnw)

    return (loss, grad_x[None], *group(0), *group(1), *group(2), *group(3))
```

```python
import functools

import jax
import jax.numpy as jnp
from jax import lax
from jax.experimental import pallas as pl
from jax.experimental.pallas import tpu as pltpu

F32 = jnp.float32
BF16 = jnp.bfloat16

D = 1024
NH = 8
NOPE = 128
ROPE = 64
HALF = ROPE // 2
DQK = NOPE + ROPE
DV = 128
QL = 256
KVL = 256
KCONV = 31
KPAD = 32
HALO = 32
IN_COLS = 6720
MLA_COLS = QL + KVL + ROPE
PROJ_COLS = 7 * D
EPS = 1e-6
ROPE_THETA = 10000.0
N_DEV = 8

ADAM_LR = 0.001
ADAM_B1 = 0.9
ADAM_B2 = 0.999
ADAM_EPS = 1e-08
ADAM_WD = 0.01
ADAM_STEP = 10

ROWS_IN = 840
ROWS_SQ = 128
ROWS_UQ = 48
ROWS_UKV = 64
ROWS_CW = 4
ROWS_IN_PAD = 896
OFF_TAIL = ROWS_IN_PAD
ROWS_TAIL = 128
OFF_SQ = OFF_TAIL + ROWS_TAIL
ROWS_PACK = OFF_SQ + 3 * ROWS_SQ
SMALL_SIZES = (3 * D, D, D, D, D, D, QL, KVL)
SMALL_COLS = 1152
SMALL_LEN = 8 * SMALL_COLS

MESH = pl.DeviceIdType.MESH
ANY = pl.BlockSpec(memory_space=pl.ANY)
V7X_VMEM_LIMIT = 56 * 1024 * 1024


def _cparams(n_axes, vmem=V7X_VMEM_LIMIT):
    return pltpu.CompilerParams(dimension_semantics=("arbitrary",) * n_axes, vmem_limit_bytes=vmem)


def _sig(x):
    return jax.nn.sigmoid(x)


def _nt(a, b):
    return lax.dot_general(a, b, (((1,), (1,)), ((), ())), preferred_element_type=F32)


def _tn(a, b):
    return lax.dot_general(a, b, (((0,), (0,)), ((), ())), preferred_element_type=F32)


def _nn(a, b):
    return jnp.dot(a, b, preferred_element_type=F32)


def _const(shape):
    return pl.BlockSpec(shape, lambda *_: (0,) * len(shape))


def _all_gather(block, name):
    r, c = block.shape

    def body(x_ref, out_ref, send_sems, recv_sems, local_sem):
        x, y, cc = lax.axis_index("x"), lax.axis_index("y"), lax.axis_index("c")
        me, sibling = (x, y, cc), (x, y, 1 - cc)
        chips = [(1 - x, y), (x, 1 - y), (1 - x, 1 - y)]

        def slot(px, py, pc):
            return out_ref.at[4 * px + 2 * py + pc]

        def copy(k, blk, to, src=None):
            return pltpu.make_async_remote_copy(
                src_ref=slot(*blk) if src is None else src, dst_ref=slot(*blk),
                send_sem=send_sems.at[k], recv_sem=recv_sems.at[k],
                device_id=to, device_id_type=MESH)

        mine = pltpu.make_async_copy(x_ref, slot(*me), local_sem)
        mine.start()
        first = [copy(0, me, sibling, src=x_ref)]
        first += [copy(1 + j, me, (*chip, cc), src=x_ref) for j, chip in enumerate(chips)]
        for cp in first:
            cp.start()
        passed = [copy(4 + j, (*chip, cc), sibling) for j, chip in enumerate(chips)]
        for j, chip in enumerate(chips):
            copy(1 + j, (*chip, cc), me).wait_recv()
            passed[j].start()
        copy(0, sibling, me).wait_recv()
        for j, chip in enumerate(chips):
            copy(4 + j, (*chip, 1 - cc), me).wait_recv()
        for cp in first + passed:
            cp.wait_send()
        mine.wait()

    return pl.pallas_call(
        body, name=name,
        out_shape=jax.ShapeDtypeStruct((N_DEV, r, c), block.dtype),
        in_specs=[ANY], out_specs=ANY,
        scratch_shapes=[pltpu.SemaphoreType.DMA((7,)), pltpu.SemaphoreType.DMA((7,)), pltpu.SemaphoreType.DMA],
    )(block)


N_CHIP = 4


def _pair_exchange(packed, name):
    _, _, r, c = packed.shape

    def body(src_ref, out_ref, send_sem, recv_sem):
        x, y, cc = lax.axis_index("x"), lax.axis_index("y"), lax.axis_index("c")
        cp = pltpu.make_async_remote_copy(
            src_ref=src_ref.at[1 - cc], dst_ref=out_ref, send_sem=send_sem, recv_sem=recv_sem,
            device_id=(x, y, 1 - cc), device_id_type=MESH)
        cp.start()
        cp.wait()

    return pl.pallas_call(
        body, name=name,
        out_shape=jax.ShapeDtypeStruct((N_CHIP, r, c), packed.dtype),
        in_specs=[ANY], out_specs=ANY,
        scratch_shapes=[pltpu.SemaphoreType.DMA, pltpu.SemaphoreType.DMA],
    )(packed)


def _pair_add(core, packed, got, tr, name):
    _, _, r, c = packed.shape

    def body(core_ref, own_ref, got_ref, out_ref):
        out_ref[...] = (own_ref[0].astype(F32) + got_ref[...].astype(F32)).astype(out_ref.dtype)

    blk = pl.BlockSpec((1, tr, c), lambda j, i, core_ref: (j, i, 0))
    return pl.pallas_call(
        body, name=name,
        grid_spec=pltpu.PrefetchScalarGridSpec(
            num_scalar_prefetch=1, grid=(N_CHIP, r // tr),
            in_specs=[pl.BlockSpec((1, 1, tr, c), lambda j, i, core_ref: (core_ref[0], j, i, 0)), blk],
            out_specs=blk),
        out_shape=jax.ShapeDtypeStruct((N_CHIP, r, c), packed.dtype),
        compiler_params=_cparams(2),
    )(core, packed, got)


HBM = pl.BlockSpec(memory_space=pltpu.HBM)
SEM = pl.BlockSpec(memory_space=pltpu.SEMAPHORE)
EFFECT = pltpu.SideEffectType.DATAFLOW_SIDE_EFFECTING


def _chip_copies(src_ref, land_ref, send_sems, recv_sems):
    x, y, cc = lax.axis_index("x"), lax.axis_index("y"), lax.axis_index("c")
    me = 2 * x + y
    copies = []
    for k in range(1, N_CHIP):
        px, py = (1 - x if k & 2 else x), (1 - y if k & 1 else y)
        copies.append(pltpu.make_async_remote_copy(
            src_ref=src_ref.at[2 * px + py], dst_ref=land_ref.at[me],
            send_sem=send_sems.at[k - 1], recv_sem=recv_sems.at[k - 1],
            device_id=(px, py, cc), device_id_type=MESH))
    return copies


def _direct_copies(src_ref, land_ref, send_sems, recv_sems):
    x, y, cc = lax.axis_index("x"), lax.axis_index("y"), lax.axis_index("c")
    me = 4 * x + 2 * y + cc
    copies = []
    for k in range(1, N_DEV):
        peer = ((1 - x if k & 4 else x), (1 - y if k & 2 else y), (1 - cc if k & 1 else cc))
        copies.append(pltpu.make_async_remote_copy(
            src_ref=src_ref, dst_ref=land_ref.at[me], send_sem=send_sems.at[k - 1], recv_sem=recv_sems.at[k - 1],
            device_id=peer, device_id_type=MESH))
    return copies


def _exchange_start(src, land_shape, copies_of, n_copies, name):
    def body(src_ref, land_ref, send_sems, recv_sems, src_thru, land_thru, token):
        for cp in copies_of(src_ref, land_ref, send_sems, recv_sems):
            cp.start()
        token[...] = jnp.zeros_like(token)

    return pl.pallas_call(
        body, name=name,
        out_shape=(pltpu.SemaphoreType.DMA((n_copies,)), pltpu.SemaphoreType.DMA((n_copies,)),
                   pltpu.HBM(src.shape, src.dtype), pltpu.HBM(land_shape, src.dtype), jax.ShapeDtypeStruct((8, 128), F32)),
        in_specs=(HBM, HBM), out_specs=(SEM, SEM, HBM, HBM, pl.BlockSpec(memory_space=pltpu.VMEM)),
        input_output_aliases={0: 2, 1: 3},
        compiler_params=pltpu.CompilerParams(has_side_effects=EFFECT),
    )(pltpu.with_memory_space_constraint(src, pltpu.HBM),
      pltpu.with_memory_space_constraint(lax.empty(land_shape, src.dtype), pltpu.HBM))


def _exchange_wait(send_sems, recv_sems, src_thru, land_thru, after, copies_of, name):
    def body(src_ref, land_ref, send_sems, recv_sems, after_ref, src_dead, got_ref):
        copies = copies_of(src_ref, land_ref, send_sems, recv_sems)
        for cp in copies:
            cp.wait_send()
        for cp in copies:
            cp.wait_recv()

    return pl.pallas_call(
        body, name=name,
        out_shape=(pltpu.HBM(src_thru.shape, src_thru.dtype), pltpu.HBM(land_thru.shape, land_thru.dtype)),
        in_specs=(HBM, HBM, SEM, SEM, ANY), out_specs=(HBM, HBM), input_output_aliases={0: 0, 1: 1},
        compiler_params=pltpu.CompilerParams(has_side_effects=EFFECT),
    )(src_thru, land_thru, send_sems, recv_sems, after)


def _adam(g, w, m, v):
    m = ADAM_B1 * m + (1.0 - ADAM_B1) * g
    v = ADAM_B2 * v + (1.0 - ADAM_B2) * (g * g)
    m_hat = m / (1.0 - ADAM_B1 ** ADAM_STEP)
    v_hat = v / (1.0 - ADAM_B2 ** ADAM_STEP)
    delta = -ADAM_LR * (m_hat / (jnp.sqrt(v_hat) + ADAM_EPS) + ADAM_WD * w)
    return delta, m, v


def _adam_w_in(parts, wt, mt, vt):
    n = parts.shape[0]
    tc = 256

    def body(p_ref, w_ref, m_ref, v_ref, g_out, d_out, m_out, v_out):
        g = p_ref[0].astype(F32)
        for j in range(1, n):
            g = g + p_ref[j].astype(F32)
        g = g[:ROWS_IN]
        delta, nm, nv = _adam(g, w_ref[...], m_ref[...], v_ref[...])
        g_out[...] = g
        d_out[...] = delta
        m_out[...] = nm
        v_out[...] = nv

    col = pl.BlockSpec((ROWS_IN, tc), lambda i: (0, i))
    return pl.pallas_call(
        body, name="adam_w_in", grid=(D // tc,),
        in_specs=[pl.BlockSpec((n, ROWS_IN_PAD, tc), lambda i: (0, 0, i)), col, col, col],
        out_specs=[col] * 4,
        out_shape=[jax.ShapeDtypeStruct((ROWS_IN, D), F32)] * 4,
        compiler_params=_cparams(1),
    )(parts, wt, mt, vt)


def _reduce_adam(parts, w, m, v, name, tr, first_block):
    n, _, c = parts.shape
    r = w.shape[0]

    def body(p_ref, w_ref, m_ref, v_ref, g_out, d_out, m_out, v_out):
        g = p_ref[0].astype(F32)
        for j in range(1, n):
            g = g + p_ref[j].astype(F32)
        delta, nm, nv = _adam(g, w_ref[...], m_ref[...], v_ref[...])
        g_out[...] = g
        d_out[...] = delta
        m_out[...] = nm
        v_out[...] = nv

    row = pl.BlockSpec((tr, c), lambda i: (i, 0))
    return pl.pallas_call(
        body, name=name, grid=(r // tr,),
        in_specs=[pl.BlockSpec((n, tr, c), lambda i: (0, first_block + i, 0)), row, row, row],
        out_specs=[row] * 4,
        out_shape=[jax.ShapeDtypeStruct((r, c), F32)] * 4,
        compiler_params=_cparams(1),
    )(parts, w, m, v)


def _ada_mod(c_all, w_ada, b_cols):
    def body(c_ref, w_ref, b_ref, o_ref):
        cv = c_ref[...]
        act = (cv * _sig(cv)).astype(BF16)
        o_ref[...] = _nn(act, w_ref[...].astype(BF16)) + b_ref[...]

    return pl.pallas_call(body, name="ada_mod", out_shape=jax.ShapeDtypeStruct((N_DEV, w_ada.shape[1]), F32))(c_all, w_ada, b_cols)


def _ada_bwd(c_all, dmod_cols, w, m, v):
    def body(c_ref, d_ref, w_ref, m_ref, v_ref, g_out, d_out, m_out, v_out):
        cv = c_ref[...]
        act = (cv * _sig(cv)).astype(BF16)
        g = _tn(act, d_ref[...].astype(BF16))
        delta, nm, nv = _adam(g, w_ref[...], m_ref[...], v_ref[...])
        g_out[...] = g
        d_out[...] = delta
        m_out[...] = nm
        v_out[...] = nv

    return pl.pallas_call(body, name="ada_bwd", out_shape=[jax.ShapeDtypeStruct(w.shape, F32)] * 4)(c_all, dmod_cols, w, m, v)


def _in_proj(x, norm_w, mod8, wt):
    s = x.shape[0]
    tm = min(1024, s)
    nk = wt.shape[0] // D

    def body(x_ref, nw_ref, mod_ref, w_ref, proj_ref, h_ref, hs_ref):
        @pl.when(pl.program_id(1) == 0)
        def _():
            xv = x_ref[...]
            rstd = lax.rsqrt(jnp.mean(xv * xv, axis=-1, keepdims=True) + EPS)
            h = (xv * rstd) * nw_ref[...] * (1.0 + mod_ref[1:2, :]) + mod_ref[0:1, :]
            hs_ref[...] = h.astype(BF16)
            h_ref[...] = hs_ref[...]

        proj_ref[...] = _nt(hs_ref[...], w_ref[...]).astype(proj_ref.dtype)

    return pl.pallas_call(
        body, name="in_proj", grid=(s // tm, nk),
        in_specs=[pl.BlockSpec((tm, D), lambda i, k: (i, 0)), _const((1, D)), _const((8, D)),
                  pl.BlockSpec((D, D), lambda i, k: (k, 0))],
        out_specs=[pl.BlockSpec((tm, D), lambda i, k: (i, k)), pl.BlockSpec((tm, D), lambda i, k: (i, 0))],
        out_shape=[jax.ShapeDtypeStruct((s, nk * D), BF16), jax.ShapeDtypeStruct((s, D), BF16)],
        scratch_shapes=[pltpu.VMEM((tm, D), BF16)],
        compiler_params=_cparams(2),
    )(x, norm_w, mod8, wt)


CONV_RC = 128
CONV_LC = 128
GW_RC = 32
SUBLANES = 8


def _shifted_taps(win_ref, weight_of, offsets, r0, lanes):
    acc = jnp.zeros((CONV_RC, CONV_LC), F32)
    for b in range(SUBLANES):
        group = [o for o in offsets if o % SUBLANES == b]
        if not group:
            continue
        rows = CONV_RC if b == 0 else CONV_RC + SUBLANES
        part = jnp.zeros((rows, CONV_LC), F32)
        for o in group:
            part = part + win_ref[pl.ds(r0 + o - b, rows), lanes] * weight_of(o)
        acc = acc + (part if b == 0 else part[b:b + CONV_RC])
    return acc


def _conv_fwd(proj, conv_w, conv_b, ln_w, ln_b):
    s = proj.shape[0]
    tm = min(256, s)
    hb = tm // HALO

    def body(av_ref, ag_ref, avh_ref, agh_ref, gate_ref, cw_ref, cb_ref, lw_ref, lb_ref, u1_ref, ya_ref, win_ref):
        i = pl.program_id(0)
        halo = avh_ref[...].astype(F32) * _sig(agh_ref[...].astype(F32))
        win_ref[0:HALO, :] = jnp.where(i > 0, halo, 0.0)
        win_ref[HALO:HALO + tm, :] = av_ref[...].astype(F32) * _sig(ag_ref[...].astype(F32))
        first = HALO - (KCONV - 1)
        for r0 in range(0, tm, CONV_RC):
            for c0 in range(0, D, CONV_LC):
                acc = _shifted_taps(win_ref, lambda o: cw_ref[o - first:o - first + 1, c0:c0 + CONV_LC],
                                    range(first, first + KCONV), r0, pl.ds(c0, CONV_LC))
                u1_ref[r0:r0 + CONV_RC, c0:c0 + CONV_LC] = acc + cb_ref[:, c0:c0 + CONV_LC]
        u1 = u1_ref[...]
        mu = jnp.mean(u1, axis=-1, keepdims=True)
        xc = u1 - mu
        var = jnp.mean(xc * xc, axis=-1, keepdims=True)
        ln = xc * lax.rsqrt(var + EPS) * lw_ref[...] + lb_ref[...]
        gate = gate_ref[...].astype(F32)
        ya_ref[...] = ((ln * _sig(ln)) * (gate * _sig(gate))).astype(BF16)

    row = lambda k: pl.BlockSpec((tm, D), lambda i: (i, k))
    prev = lambda k: pl.BlockSpec((HALO, D), lambda i: (jnp.maximum(i * hb - 1, 0), k))
    return pl.pallas_call(
        body, name="conv_fwd", grid=(s // tm,),
        in_specs=[row(0), row(1), prev(0), prev(1), row(2), _const((KPAD, D)), _const((1, D)), _const((1, D)), _const((1, D))],
        out_specs=[pl.BlockSpec((tm, D), lambda i: (i, 0))] * 2,
        out_shape=[jax.ShapeDtypeStruct((s, D), F32), jax.ShapeDtypeStruct((s, D), BF16)],
        scratch_shapes=[pltpu.VMEM((tm + HALO, D), F32)],
        compiler_params=_cparams(1),
    )(proj, proj, proj, proj, proj, conv_w, conv_b, ln_w, ln_b)


def _rope_tables(pos_ref, if_ref):
    ang = pos_ref[...].astype(F32) * if_ref[...]
    return jnp.cos(ang), jnp.sin(ang)


def _rms_parts(x):
    rstd = lax.rsqrt(jnp.mean(x * x, axis=-1, keepdims=True) + EPS)
    return x * rstd, rstd


def _mla_prep(proj, pos, inv_freq, qnw, kvnw, wuq, wukv):
    s = proj.shape[0]
    tm = min(512, s)

    def body(p_ref, pos_ref, if_ref, qnw_ref, kvnw_ref, wuq_ref, wukv_ref, q_ref, k_ref, v_ref):
        blk = p_ref[...].astype(F32)
        cos, sin = _rope_tables(pos_ref, if_ref)

        def rope(r):
            x1, x2 = r[:, :HALF], r[:, HALF:]
            return jnp.concatenate([x1 * cos - x2 * sin, x1 * sin + x2 * cos], axis=-1)

        qlat = _rms_parts(blk[:, :QL])[0] * qnw_ref[...]
        kvlat = _rms_parts(blk[:, QL:QL + KVL])[0] * kvnw_ref[...]
        q = _nn(qlat.astype(BF16), wuq_ref[...])
        kv = _nn(kvlat.astype(BF16), wukv_ref[...])
        kr = rope(blk[:, QL + KVL:MLA_COLS])
        for h in range(NH):
            qh = q[:, h * DQK:(h + 1) * DQK]
            q_ref[h] = (jnp.concatenate([qh[:, :NOPE], rope(qh[:, NOPE:])], axis=-1) * ATTN_SCALE).astype(BF16)
            k_ref[h] = jnp.concatenate([kv[:, h * 256:h * 256 + NOPE], kr], axis=-1).astype(BF16)
            v_ref[h] = kv[:, h * 256 + NOPE:(h + 1) * 256].astype(BF16)

    hm = lambda d: pl.BlockSpec((NH, tm, d), lambda i: (0, i, 0))
    return pl.pallas_call(
        body, name="mla_prep", grid=(s // tm,),
        in_specs=[pl.BlockSpec((tm, D), lambda i: (i, 3)), pl.BlockSpec((tm, 1), lambda i: (i, 0)), _const((1, HALF)),
                  _const((1, QL)), _const((1, KVL)), _const((QL, NH * DQK)), _const((KVL, NH * 256))],
        out_specs=[hm(DQK), hm(DQK), hm(DV)],
        out_shape=[jax.ShapeDtypeStruct((NH, s, DQK), BF16), jax.ShapeDtypeStruct((NH, s, DQK), BF16),
                   jax.ShapeDtypeStruct((NH, s, DV), BF16)],
        compiler_params=_cparams(1),
    )(proj, pos, inv_freq, qnw, kvnw, wuq, wukv)


ATTN_SCALE = DQK ** -0.5


def _causal_mask(s, t):
    rows = lax.broadcasted_iota(jnp.int32, (t, t), 0)
    cols = lax.broadcasted_iota(jnp.int32, (t, t), 1)
    return jnp.where(cols <= rows, s, -jnp.inf)


def _attn_tile(s):
    return min(1024, s // 2)


def _attn_fwd(q, k, v):
    nh, s, _ = q.shape
    t = _attn_tile(s)

    def body(q_ref, k_ref, v_ref, o_ref, lse_ref):
        qi = pl.program_id(1)
        qv = q_ref[0]

        def chunk(c, carry, diag):
            m, l, acc = carry
            rows = pl.ds(pl.multiple_of(c * t, t), t)
            sc = _nt(qv, k_ref[0, rows, :])
            if diag:
                sc = _causal_mask(sc, t)
            m_new = jnp.maximum(m, jnp.max(sc, axis=-1, keepdims=True))
            alpha = jnp.exp(m - m_new)
            p = jnp.exp(sc - m_new)
            l = alpha * l + jnp.sum(p, axis=-1, keepdims=True)
            acc = alpha * acc + _nn(p.astype(BF16), v_ref[0, rows, :])
            return m_new, l, acc

        init = (jnp.full((t, 1), -jnp.inf, F32), jnp.zeros((t, 1), F32), jnp.zeros((t, DV), F32))
        carry = lax.fori_loop(0, qi, lambda c, cr: chunk(c, cr, False), init)
        m, l, acc = chunk(qi, carry, True)
        o_ref[...] = acc / l
        lse_ref[0] = jnp.broadcast_to(m + jnp.log(l), (t, DV))

    head = lambda d: pl.BlockSpec((1, s, d), lambda h, i: (h, 0, 0))
    return pl.pallas_call(
        body, name="attn_fwd", grid=(nh, s // t),
        in_specs=[pl.BlockSpec((1, t, DQK), lambda h, i: (h, i, 0)), head(DQK), head(DV)],
        out_specs=[pl.BlockSpec((t, DV), lambda h, i: (i, h)), pl.BlockSpec((1, t, DV), lambda h, i: (h, i, 0))],
        out_shape=[jax.ShapeDtypeStruct((s, nh * DV), F32), jax.ShapeDtypeStruct((nh, s, DV), F32)],
        compiler_params=_cparams(2),
    )(q, k, v)


def _merge_loss(x, target, ya, o, proj, mod8, fnw, gathered):
    s = x.shape[0]
    tm = min(256, s)
    n = s // tm

    def body(x_ref, t_ref, ya_ref, o_ref, bg_ref, ga_ref, gb_ref, mod_ref, fnw_ref, wco_ref, wao_ref, wout_ref,
             dya_ref, do_ref, delta_ref, dpg_ref, dx2_ref, gw_ref, small_ref, acc_ref, cast_ref):
        i = pl.program_id(0)

        @pl.when(i == 0)
        def _():
            acc_ref[...] = jnp.zeros(acc_ref.shape, F32)
            small_ref[...] = jnp.zeros(small_ref.shape, F32)

        bg = bg_ref[...].astype(F32)
        sbg = _sig(bg)
        sb = bg * sbg
        ov = o_ref[...]
        ya = ya_ref[...]
        yb = (ov * sb).astype(BF16)
        square = lambda ref: ref[...].reshape(D, D)
        y_a = _nn(ya, square(wco_ref))
        y_b = _nn(yb, square(wao_ref))
        sa = _sig(ga_ref[...].astype(F32))
        sgb = _sig(gb_ref[...].astype(F32))
        merged = (sa * y_a + sgb * y_b).astype(BF16)
        z = _nn(merged, square(wout_ref))
        gate = mod_ref[2:3, :]
        x2 = x_ref[...] + gate * z
        xn, rstd = _rms_parts(x2)
        fnw = fnw_ref[...]
        err = xn * fnw - t_ref[...]
        loss = jnp.sum(jnp.sum(err * err, axis=-1, keepdims=True), axis=0, keepdims=True) * (0.5 / D)
        dy = err * (1.0 / D)
        small_ref[0:1, :] += jnp.sum(dy * xn, axis=0, keepdims=True)
        dxn = dy * fnw
        dx2 = rstd * (dxn - xn * jnp.mean(dxn * xn, axis=-1, keepdims=True))
        dx2_ref[...] = dx2
        small_ref[1:2, :] += jnp.sum(dx2 * z, axis=0, keepdims=True)
        small_ref[2:3, :] += jnp.broadcast_to(loss, (1, D))
        dz = (dx2 * gate).astype(BF16)
        dmerged = _nt(dz, square(wout_ref))
        acc_ref[2] += _tn(merged, dz)
        dy_a = (dmerged * sa).astype(BF16)
        dy_b = (dmerged * sgb).astype(BF16)
        dpg_ref[:, D:2 * D] = (dmerged * y_a * (sa * (1.0 - sa))).astype(BF16)
        dpg_ref[:, 2 * D:3 * D] = (dmerged * y_b * (sgb * (1.0 - sgb))).astype(BF16)
        dya_ref[...] = _nt(dy_a, square(wco_ref))
        acc_ref[0] += _tn(ya, dy_a)
        dyb = _nt(dy_b, square(wao_ref))
        acc_ref[1] += _tn(yb, dy_b)
        do = dyb * sb
        do_ref[...] = do.astype(BF16)
        dpg_ref[:, 0:D] = (dyb * ov * (sbg * (1.0 + bg * (1.0 - sbg)))).astype(BF16)
        prod = do * ov
        for h in range(NH):
            delta_ref[h] = jnp.broadcast_to(jnp.sum(prod[:, h * DV:(h + 1) * DV], axis=-1, keepdims=True), (tm, DV))

        @pl.when(i == n - 1)
        def _():
            for j in range(3):
                cast_ref[...] = acc_ref[j].astype(cast_ref.dtype)
                pltpu.sync_copy(cast_ref, gw_ref.at[j])

    row = pl.BlockSpec((tm, D), lambda i: (i, 0))
    col = lambda k: pl.BlockSpec((tm, D), lambda i: (i, k))
    wspec = lambda j: pl.BlockSpec((N_DEV, ROWS_SQ, D), lambda i: (0, j, 0), pipeline_mode=pl.Buffered(1))
    return pl.pallas_call(
        body, name="merge_loss", grid=(n,),
        in_specs=[row, row, row, row, col(4), col(5), col(6), _const((8, D)), _const((1, D)), wspec(0), wspec(1), wspec(2)],
        out_specs=[row, row, pl.BlockSpec((NH, tm, DV), lambda i: (0, i, 0)), pl.BlockSpec((tm, 3 * D), lambda i: (i, 0)),
                   row, ANY, _const((8, D))],
        out_shape=[jax.ShapeDtypeStruct((s, D), F32), jax.ShapeDtypeStruct((s, D), BF16),
                   jax.ShapeDtypeStruct((NH, s, DV), F32), jax.ShapeDtypeStruct((s, 3 * D), BF16),
                   jax.ShapeDtypeStruct((s, D), F32), jax.ShapeDtypeStruct((3, D, D), BF16),
                   jax.ShapeDtypeStruct((8, D), F32)],
        scratch_shapes=[pltpu.VMEM((3, D, D), F32), pltpu.VMEM((D, D), BF16)],
        compiler_params=_cparams(1),
    )(x, target, ya, o, proj, proj, proj, mod8, fnw, gathered, gathered, gathered)


def _attn_bwd(q, k, v, do, lse, delta):
    nh, s, _ = q.shape
    t = _attn_tile(s)
    nb = s // t

    def body(q_ref, k_ref, v_ref, do_ref, lse_ref, dl_ref, dq_ref, dk_ref, dv_ref):
        kj = pl.program_id(1)

        @pl.when(kj == 0)
        def _():
            dq_ref[...] = jnp.zeros(dq_ref.shape, F32)

        kv_, vv = k_ref[0], v_ref[0]

        def chunk(c, carry, diag):
            dk, dv = carry
            rows = pl.ds(pl.multiple_of(c * t, t), t)
            qv = q_ref[0, rows, :]
            dov = do_ref[rows, :]
            sc = _nt(qv, kv_)
            if diag:
                sc = _causal_mask(sc, t)
            p = jnp.exp(sc - lse_ref[0, rows, 0:1])
            dv = dv + _tn(p.astype(BF16), dov)
            dp = _nt(dov, vv)
            ds = (p * (dp - dl_ref[0, rows, 0:1])).astype(BF16)
            dk = dk + _tn(ds, qv)
            dq_ref[0, rows, :] += _nn(ds, kv_)
            return dk, dv

        carry = chunk(kj, (jnp.zeros((t, DQK), F32), jnp.zeros((t, DV), F32)), True)
        dk, dv = lax.fori_loop(kj + 1, nb, lambda c, cr: chunk(c, cr, False), carry)
        dk_ref[0] = dk
        dv_ref[0] = dv

    head = lambda d: pl.BlockSpec((1, s, d), lambda h, j: (h, 0, 0))
    blk = lambda d: pl.BlockSpec((1, t, d), lambda h, j: (h, j, 0))
    return pl.pallas_call(
        body, name="attn_bwd", grid=(nh, nb),
        in_specs=[head(DQK), blk(DQK), blk(DV), pl.BlockSpec((s, DV), lambda h, j: (0, h)), head(DV), head(DV)],
        out_specs=[head(DQK), blk(DQK), blk(DV)],
        out_shape=[jax.ShapeDtypeStruct((nh, s, DQK), F32), jax.ShapeDtypeStruct((nh, s, DQK), F32),
                   jax.ShapeDtypeStruct((nh, s, DV), F32)],
        compiler_params=_cparams(2),
    )(q, k, v, do, lse, delta)


def _mla_bwd(dq, dk, dv, proj, pos, inv_freq, qnw, kvnw, wuq, wukv):
    s = proj.shape[0]
    tm = min(512, s)

    def body(dq_ref, dk_ref, dv_ref, p_ref, pos_ref, if_ref, qnw_ref, kvnw_ref, wuq_ref, wukv_ref,
             dp_ref, guq_ref, gukv_ref, small_ref):
        @pl.when(pl.program_id(0) == 0)
        def _():
            guq_ref[...] = jnp.zeros(guq_ref.shape, F32)
            gukv_ref[...] = jnp.zeros(gukv_ref.shape, F32)
            small_ref[...] = jnp.zeros(small_ref.shape, F32)

        blk = p_ref[...].astype(F32)
        cos, sin = _rope_tables(pos_ref, if_ref)

        def unrope(g):
            g1, g2 = g[:, :HALF], g[:, HALF:]
            return jnp.concatenate([g1 * cos + g2 * sin, g2 * cos - g1 * sin], axis=-1)

        dq_cols, dkv_cols = [], []
        dkr = jnp.zeros((tm, ROPE), F32)
        for h in range(NH):
            dqh, dkh = dq_ref[h] * ATTN_SCALE, dk_ref[h]
            dq_cols += [dqh[:, :NOPE], unrope(dqh[:, NOPE:])]
            dkv_cols += [dkh[:, :NOPE], dv_ref[h]]
            dkr = dkr + dkh[:, NOPE:]
        dq_full = jnp.concatenate(dq_cols, axis=-1).astype(BF16)
        dkv_full = jnp.concatenate(dkv_cols, axis=-1).astype(BF16)

        def latent_bwd(c, nw_ref, d_up, w_ref, g_ref, srow):
            nrm, rstd = _rms_parts(c)
            nw = nw_ref[...]
            lat = (nrm * nw).astype(BF16)
            g_ref[...] += _tn(lat, d_up)
            dlat = _nt(d_up, w_ref[...])
            small_ref[srow:srow + 1, :] += jnp.sum(dlat * nrm, axis=0, keepdims=True)
            dn = dlat * nw
            return rstd * (dn - nrm * jnp.mean(dn * nrm, axis=-1, keepdims=True))

        dcq = latent_bwd(blk[:, :QL], qnw_ref, dq_full, wuq_ref, guq_ref, 0)
        dckv = latent_bwd(blk[:, QL:QL + KVL], kvnw_ref, dkv_full, wukv_ref, gukv_ref, 1)
        dp_ref[...] = jnp.concatenate([dcq, dckv, unrope(dkr), jnp.zeros((tm, D - MLA_COLS), F32)], axis=-1).astype(BF16)

    hm = lambda d: pl.BlockSpec((NH, tm, d), lambda i: (0, i, 0))
    return pl.pallas_call(
        body, name="mla_bwd", grid=(s // tm,),
        in_specs=[hm(DQK), hm(DQK), hm(DV), pl.BlockSpec((tm, D), lambda i: (i, 3)), pl.BlockSpec((tm, 1), lambda i: (i, 0)),
                  _const((1, HALF)), _const((1, QL)), _const((1, KVL)), _const((QL, NH * DQK)), _const((KVL, NH * 256))],
        out_specs=[pl.BlockSpec((tm, D), lambda i: (i, 0)), _const((QL, NH * DQK)), _const((KVL, NH * 256)), _const((8, QL))],
        out_shape=[jax.ShapeDtypeStruct((s, D), BF16), jax.ShapeDtypeStruct((QL, NH * DQK), F32),
                   jax.ShapeDtypeStruct((KVL, NH * 256), F32), jax.ShapeDtypeStruct((8, QL), F32)],
        compiler_params=_cparams(1),
    )(dq, dk, dv, proj, pos, inv_freq, qnw, kvnw, wuq, wukv)


def _conv_rows_bwd(dya, u1, proj, ln_w, ln_b):
    s = dya.shape[0]
    tm = min(512, s)

    def body(dya_ref, u1_ref, gate_ref, lw_ref, lb_ref, du1_ref, dag_ref, small_ref):
        @pl.when(pl.program_id(0) == 0)
        def _():
            small_ref[...] = jnp.zeros(small_ref.shape, F32)

        u1 = u1_ref[...]
        mu = jnp.mean(u1, axis=-1, keepdims=True)
        xc = u1 - mu
        rstd = lax.rsqrt(jnp.mean(xc * xc, axis=-1, keepdims=True) + EPS)
        xhat = xc * rstd
        lw = lw_ref[...]
        ln = xhat * lw + lb_ref[...]
        sl = _sig(ln)
        u2 = ln * sl
        gate = gate_ref[...].astype(F32)
        sg = _sig(gate)
        dya = dya_ref[...]
        dag_ref[...] = (dya * u2 * (sg * (1.0 + gate * (1.0 - sg)))).astype(BF16)
        dln = dya * (gate * sg) * (sl * (1.0 + ln * (1.0 - sl)))
        small_ref[0:1, :] += jnp.sum(dln * xhat, axis=0, keepdims=True)
        small_ref[1:2, :] += jnp.sum(dln, axis=0, keepdims=True)
        dxh = dln * lw
        du1_ref[...] = rstd * (dxh - jnp.mean(dxh, axis=-1, keepdims=True) - xhat * jnp.mean(dxh * xhat, axis=-1, keepdims=True))

    row = pl.BlockSpec((tm, D), lambda i: (i, 0))
    return pl.pallas_call(
        body, name="conv_rows_bwd", grid=(s // tm,),
        in_specs=[row, row, pl.BlockSpec((tm, D), lambda i: (i, 2)), _const((1, D)), _const((1, D))],
        out_specs=[row, row, _const((8, D))],
        out_shape=[jax.ShapeDtypeStruct((s, D), F32), jax.ShapeDtypeStruct((s, D), BF16), jax.ShapeDtypeStruct((8, D), F32)],
        compiler_params=_cparams(1),
    )(dya, u1, proj, ln_w, ln_b)


def _conv_bwd(du1, proj, conv_w):
    s = du1.shape[0]
    tm = min(256, s)
    hb = tm // HALO
    n = s // tm
    last32 = s // HALO - 1

    def body(d_ref, dn_ref, av_ref, ag_ref, avh_ref, agh_ref, cw_ref, dp_ref, gcw_ref, small_ref,
             dwin_ref, dpad_ref, dsh_ref, uwin_ref, acc_ref):
        i = pl.program_id(0)

        @pl.when(i == 0)
        def _():
            acc_ref[...] = jnp.zeros(acc_ref.shape, F32)
            small_ref[...] = jnp.zeros(small_ref.shape, F32)
            dpad_ref[...] = jnp.zeros(dpad_ref.shape, F32)
            uwin_ref[...] = jnp.zeros(uwin_ref.shape, F32)

        dv = d_ref[...]
        dwin_ref[0:tm, :] = dv
        dwin_ref[tm:tm + HALO, :] = jnp.where(i < n - 1, dn_ref[...], 0.0)
        dpad_ref[SUBLANES:SUBLANES + tm, :] = dv
        halo = avh_ref[...].astype(F32) * _sig(agh_ref[...].astype(F32))
        uwin_ref[0:HALO, :] = jnp.where(i > 0, halo, 0.0)
        av = av_ref[...].astype(F32)
        sg = _sig(ag_ref[...].astype(F32))
        uwin_ref[HALO:HALO + tm, :] = av * sg
        small_ref[0:1, :] += jnp.sum(dv, axis=0, keepdims=True)

        for b in range(SUBLANES):
            dsh_ref[b] = dpad_ref[pl.ds(SUBLANES - b, tm + SUBLANES), :]

        first = HALO - (KCONV - 1)
        for c0 in range(0, D, CONV_LC):
            lanes = pl.ds(c0, CONV_LC)
            for r0 in range(0, tm, CONV_RC):
                acc = _shifted_taps(dwin_ref, lambda o: cw_ref[KCONV - 1 - o:KCONV - o, c0:c0 + CONV_LC], range(KCONV), r0, lanes)
                a = av[r0:r0 + CONV_RC, c0:c0 + CONV_LC]
                g = sg[r0:r0 + CONV_RC, c0:c0 + CONV_LC]
                dp_ref[r0:r0 + CONV_RC, c0:c0 + CONV_LC] = (acc * g).astype(BF16)
                dp_ref[r0:r0 + CONV_RC, D + c0:D + c0 + CONV_LC] = (acc * a * (g * (1.0 - g))).astype(BF16)
            for b in range(SUBLANES):
                group = [o for o in range(first, first + KCONV) if o % SUBLANES == b]
                parts = [jnp.zeros((SUBLANES, CONV_LC), F32) for _ in group]
                chunks = [(i0, GW_RC) for i0 in range(0, tm, GW_RC)] + ([(tm, SUBLANES)] if b else [])
                for i0, rows in chunks:
                    dsh = dsh_ref[b, pl.ds(i0, rows), lanes]
                    for n_, o in enumerate(group):
                        prod = dsh * uwin_ref[pl.ds(i0 + o - b, rows), lanes]
                        parts[n_] = parts[n_] + jnp.sum(prod.reshape(rows // SUBLANES, SUBLANES, CONV_LC), axis=0)
                for n_, o in enumerate(group):
                    acc_ref[o - first, :, c0:c0 + CONV_LC] += parts[n_]

        @pl.when(i == n - 1)
        def _():
            gcw_ref[...] = jnp.sum(acc_ref[...], axis=1)

    row = lambda k: pl.BlockSpec((tm, D), lambda i: (i, k))
    prev = lambda k: pl.BlockSpec((HALO, D), lambda i: (jnp.maximum(i * hb - 1, 0), k))
    return pl.pallas_call(
        body, name="conv_bwd", grid=(n,),
        in_specs=[row(0), pl.BlockSpec((HALO, D), lambda i: (jnp.minimum((i + 1) * hb, last32), 0)),
                  row(0), row(1), prev(0), prev(1), _const((KPAD, D))],
        out_specs=[pl.BlockSpec((tm, 2 * D), lambda i: (i, 0)), _const((KPAD, D)), _const((8, D))],
        out_shape=[jax.ShapeDtypeStruct((s, 2 * D), BF16), jax.ShapeDtypeStruct((KPAD, D), F32), jax.ShapeDtypeStruct((8, D), F32)],
        scratch_shapes=[pltpu.VMEM((tm + HALO, D), F32), pltpu.VMEM((tm + 2 * SUBLANES, D), F32),
                        pltpu.VMEM((SUBLANES, tm + SUBLANES, D), F32),
                        pltpu.VMEM((tm + HALO + SUBLANES, D), F32), pltpu.VMEM((KPAD, SUBLANES, D), F32)],
        compiler_params=_cparams(1),
    )(du1, du1, proj, proj, proj, proj, conv_w)


def _dproj_specs(tm, rows_first):
    def spec(lo, hi):
        def idx(a, b):
            i, k = (a, b) if rows_first else (b, a)
            col = jnp.clip(k - lo, 0, hi - lo - 1)
            if rows_first:
                return (i, col)
            return (jnp.where((k >= lo) & (k < hi), i, 0), col)
        return pl.BlockSpec((tm, D), idx)
    return [spec(0, 2), spec(2, 3), spec(3, 4), spec(4, 7)]


def _pick_dproj(k, refs, fn):
    vg, ag, mla, gates = refs

    @pl.when(k < 2)
    def _():
        fn(vg)

    @pl.when(k == 2)
    def _():
        fn(ag)

    @pl.when(k == 3)
    def _():
        fn(mla)

    @pl.when(k > 3)
    def _():
        fn(gates)


def _in_proj_bwd_x(dps, wt, x, dx2, norm_w, mod8):
    s = x.shape[0]
    tm = min(1024, s)
    nk = wt.shape[0] // D

    def body(vg_ref, ag_ref, mla_ref, g_ref, w_ref, x_ref, dx2_ref, nw_ref, mod_ref, gx_ref, small_ref, acc_ref):
        i, k = pl.program_id(0), pl.program_id(1)

        @pl.when((i == 0) & (k == 0))
        def _():
            small_ref[...] = jnp.zeros(small_ref.shape, F32)

        @pl.when(k == 0)
        def _():
            acc_ref[...] = jnp.zeros(acc_ref.shape, F32)

        def add(ref):
            acc_ref[...] += _nn(ref[...], w_ref[...])

        _pick_dproj(k, (vg_ref, ag_ref, mla_ref, g_ref), add)

        @pl.when(k == nk - 1)
        def _():
            dh = acc_ref[...]
            xn, rstd = _rms_parts(x_ref[...])
            nw = nw_ref[...]
            hn = xn * nw
            small_ref[0:1, :] += jnp.sum(dh, axis=0, keepdims=True)
            small_ref[1:2, :] += jnp.sum(dh * hn, axis=0, keepdims=True)
            dhn = dh * (1.0 + mod_ref[1:2, :])
            small_ref[2:3, :] += jnp.sum(dhn * xn, axis=0, keepdims=True)
            dxn = dhn * nw
            gx_ref[...] = rstd * (dxn - xn * jnp.mean(dxn * xn, axis=-1, keepdims=True)) + dx2_ref[...]

    row = pl.BlockSpec((tm, D), lambda i, k: (i, 0))
    return pl.pallas_call(
        body, name="in_proj_bwd_x", grid=(s // tm, nk),
        in_specs=_dproj_specs(tm, True) + [pl.BlockSpec((D, D), lambda i, k: (k, 0)), row, row, _const((1, D)), _const((8, D))],
        out_specs=[row, _const((8, D))],
        out_shape=[jax.ShapeDtypeStruct((s, D), F32), jax.ShapeDtypeStruct((8, D), F32)],
        scratch_shapes=[pltpu.VMEM((tm, D), F32)],
        compiler_params=_cparams(2),
    )(*dps, wt, x, dx2, norm_w, mod8)


def _in_proj_bwd_w(dps, h, nk, token):
    s = h.shape[0]
    tm = min(1024, s)

    n = s // tm

    def body(vg_ref, ag_ref, mla_ref, g_ref, h_ref, token_ref, gw_ref, acc_ref):
        k, i = pl.program_id(0), pl.program_id(1)

        @pl.when(i == 0)
        def _():
            acc_ref[...] = jnp.zeros(acc_ref.shape, F32)

        def add(ref):
            acc_ref[...] += _tn(ref[...], h_ref[...])

        _pick_dproj(k, (vg_ref, ag_ref, mla_ref, g_ref), add)

        @pl.when(i == n - 1)
        def _():
            gw_ref[...] = acc_ref[...].astype(gw_ref.dtype)

    return pl.pallas_call(
        body, name="in_proj_bwd_w", grid=(nk, n),
        in_specs=_dproj_specs(tm, False) + [pl.BlockSpec((tm, D), lambda k, i: (i, 0)), _const((8, 128))],
        out_specs=pl.BlockSpec((D, D), lambda k, i: (k, 0)),
        out_shape=jax.ShapeDtypeStruct((nk * D, D), BF16),
        scratch_shapes=[pltpu.VMEM((D, D), F32)],
        compiler_params=_cparams(2),
    )(*dps, h, token)


def _small_slab(wuq, wukv, conv_w):
    cw = jnp.pad(conv_w.reshape(-1), (0, (ROWS_TAIL - ROWS_UQ - ROWS_UKV) * D - KCONV * 128)).reshape(-1, D)
    return jnp.concatenate([wuq.reshape(ROWS_UQ, D), wukv.reshape(ROWS_UKV, D), cw], axis=0)


def _split_small_slab(slab):
    return (slab[..., :ROWS_UQ, :], slab[..., ROWS_UQ:ROWS_UQ + ROWS_UKV, :],
            slab[..., ROWS_UQ + ROWS_UKV:ROWS_UQ + ROWS_UKV + ROWS_CW, :])


def _unpack_small_slab(slab):
    wuq, wukv, cw = _split_small_slab(slab)
    return (wuq.reshape(QL, NH * DQK // N_DEV), wukv.reshape(KVL, NH * 256 // N_DEV),
            cw.reshape(-1)[:KCONV * 128].reshape(KCONV, 128))


def _pack_shard(w_in, wco, wao, wout, wuq, wukv, conv_w):
    bf = lambda a: a.astype(BF16)
    first = jnp.concatenate([jnp.pad(bf(w_in).T, ((0, ROWS_IN_PAD - ROWS_IN), (0, 0))), bf(_small_slab(wuq, wukv, conv_w))], axis=0)
    return first, jnp.concatenate([bf(wco), bf(wao), bf(wout)], axis=0)


def _unpack_gathered(g):
    wt = g[:, :ROWS_IN].reshape(IN_COLS, D)
    split = 3 * D + MLA_COLS
    wt = jnp.concatenate([wt[:split], jnp.zeros((D - MLA_COLS, D), g.dtype), wt[split:]], axis=0)
    wuq, wukv, cw = _split_small_slab(g[:, OFF_TAIL:OFF_TAIL + ROWS_TAIL])
    wuq = wuq.reshape(N_DEV, QL, NH * DQK // N_DEV).transpose(1, 0, 2).reshape(QL, NH * DQK)
    wukv = wukv.reshape(N_DEV, KVL, NH * 256 // N_DEV).transpose(1, 0, 2).reshape(KVL, NH * 256)
    cw = cw.reshape(N_DEV, ROWS_CW * D)[:, :KCONV * 128].reshape(N_DEV, KCONV, 128).transpose(1, 0, 2).reshape(KCONV, D)
    return wt, wuq, wukv, cw


def _by_side(slabs):
    return slabs.reshape(N_CHIP, 2, slabs.shape[1], D).transpose(1, 0, 2, 3)


def _pack_grads_late(gwt):
    split = 3 * D + MLA_COLS
    g_in = jnp.concatenate([gwt[:split], gwt[4 * D:]], axis=0).reshape(N_DEV, ROWS_IN, D)
    return _by_side(jnp.pad(g_in, ((0, 0), (0, ROWS_IN_PAD - ROWS_IN), (0, 0))))


def _pack_grads_early(gw3, guq, gukv, gcw):
    guq = guq.reshape(QL, N_DEV, -1).transpose(1, 0, 2).reshape(N_DEV, ROWS_UQ, D)
    gukv = gukv.reshape(KVL, N_DEV, -1).transpose(1, 0, 2).reshape(N_DEV, ROWS_UKV, D)
    gcw = gcw.reshape(KCONV, N_DEV, 128).transpose(1, 0, 2).reshape(N_DEV, KCONV * 128)
    gcw = jnp.pad(gcw, ((0, 0), (0, (ROWS_TAIL - ROWS_UQ - ROWS_UKV) * D - KCONV * 128))).reshape(N_DEV, -1, D)
    gsq = gw3.reshape(3, N_DEV, ROWS_SQ, D).transpose(1, 0, 2, 3).reshape(N_DEV, 3 * ROWS_SQ, D)
    return _by_side(jnp.concatenate([a.astype(BF16) for a in (guq, gukv, gcw, gsq)], axis=1))


def _pack_small(vecs):
    flat = jnp.concatenate([v.reshape(-1) for v in vecs])
    return jnp.pad(flat, (0, SMALL_LEN - flat.shape[0])).reshape(8, SMALL_COLS)


def _unpack_small(a, shapes):
    flat = a.reshape(-1)
    out, off = [], 0
    for shp, n in zip(shapes, SMALL_SIZES):
        out.append(flat[off:off + n].reshape(shp))
        off += n
    return out


def kernel(x, c, positions, w_ada, b_ada, norm_w, w_in, conv_w, conv_b, conv_ln_w, conv_ln_b, w_conv_out, q_norm_w, w_uq, kv_norm_w, w_ukv, w_attn_out, w_out, final_norm_w, loss_target, m_w_ada, m_b_ada, m_norm_w, m_w_in, m_conv_w, m_conv_b, m_conv_ln_w, m_conv_ln_b, m_w_conv_out, m_q_norm_w, m_w_uq, m_kv_norm_w, m_w_ukv, m_w_attn_out, m_w_out, m_final_norm_w, v_w_ada, v_b_ada, v_norm_w, v_w_in, v_conv_w, v_conv_b, v_conv_ln_w, v_conv_ln_b, v_w_conv_out, v_q_norm_w, v_w_uq, v_kv_norm_w, v_w_ukv, v_w_attn_out, v_w_out, v_final_norm_w):
    me = 4 * lax.axis_index("x") + 2 * lax.axis_index("y") + lax.axis_index("c")
    xs, tgt = x[0], loss_target[0]
    s = xs.shape[0]
    ada_cols = w_ada.shape[2]

    sharded = lambda t: tuple(a[0] for a in t)
    slab_first, slab_sq = _pack_shard(*sharded((w_in, w_conv_out, w_attn_out, w_out, w_uq, w_ukv, conv_w)))
    wt, wuq, wukv, cw = _unpack_gathered(_all_gather(slab_first, "gather_weights"))
    cw32 = jnp.pad(cw.astype(F32), ((0, KPAD - KCONV), (0, 0)))

    c_all = _all_gather(jnp.broadcast_to(c, (8, D)), "gather_c")[:, 0, :]
    b_cols = lax.dynamic_slice(b_ada, (0, me * ada_cols), (1, ada_cols))
    mod_cols = _all_gather(_ada_mod(c_all, w_ada[0], b_cols), "gather_mod")
    mod = lax.dynamic_index_in_dim(mod_cols, me, axis=1, keepdims=False).reshape(3, D)
    mod8 = jnp.pad(mod, ((0, 5), (0, 0)))

    slab_sq, mod8 = lax.optimization_barrier((slab_sq, mod8))
    sq_send, sq_recv, sq_src, sq_land, sq_token = _exchange_start(slab_sq, (N_DEV,) + slab_sq.shape, _direct_copies, N_DEV - 1,
                                                                  "square_gather_start")

    pos = positions.reshape(s, 1)
    inv_freq = (ROPE_THETA ** (-jnp.arange(0, ROPE, 2, dtype=F32) / ROPE)).reshape(1, HALF)
    proj, h = _in_proj(xs, norm_w + sq_token[0:1, 0:1], mod8, wt)
    u1, ya = _conv_fwd(proj, cw32, conv_b, conv_ln_w, conv_ln_b)
    q, k, v = _mla_prep(proj, pos, inv_freq, q_norm_w, kv_norm_w, wuq, wukv)
    o, lse = _attn_fwd(q, k, v)

    slab_sq, gathered_sq = _exchange_wait(sq_send, sq_recv, sq_src, sq_land, lse, _direct_copies, "square_gather_wait")
    gathered_sq = lax.dynamic_update_slice(gathered_sq, slab_sq[None], (me, 0, 0))
    dya, do, delta, dp_gates, dx2, gw3, small_a = _merge_loss(xs, tgt, ya, o, proj, mod8, final_norm_w.reshape(1, D), gathered_sq)
    dq, dk, dv = _attn_bwd(q, k, v, do, lse, delta)
    dp_mla, guq, gukv, small_b = _mla_bwd(dq, dk, dv, proj, pos, inv_freq, q_norm_w, kv_norm_w, wuq, wukv)
    du1, dp_ag, small_c = _conv_rows_bwd(dya, u1, proj, conv_ln_w, conv_ln_b)
    dp_vg, gcw, small_d = _conv_bwd(du1, proj, cw32)
    dps = [dp_vg, dp_ag, dp_mla, dp_gates]
    core = lax.axis_index("c").astype(jnp.int32).reshape(1)
    chip = 2 * lax.axis_index("x") + lax.axis_index("y")

    def pair_sum(packed, tag):
        return _pair_add(core, packed, _pair_exchange(packed, "pair_exchange_" + tag), packed.shape[2] // 2, "pair_add_" + tag)

    def own_slot(half, recv):
        return lax.dynamic_update_slice(recv, lax.dynamic_slice(half, (chip, 0, 0), (1,) + half.shape[1:]), (chip, 0, 0))

    half_e = pair_sum(_pack_grads_early(gw3, guq, gukv, gcw[:KCONV]), "early")
    send_e, recv_e, half_e, land_e, token_e = _exchange_start(half_e, half_e.shape, _chip_copies, N_CHIP - 1, "chip_exchange_start_early")
    gwt = _in_proj_bwd_w(dps, h, wt.shape[0] // D, token_e)
    half_l = pair_sum(_pack_grads_late(gwt), "late")
    send_l, recv_l, half_l, land_l, token_l = _exchange_start(half_l, half_l.shape, _chip_copies, N_CHIP - 1, "chip_exchange_start_late")
    grad_x, small_e = _in_proj_bwd_x(dps, wt, xs, dx2, norm_w + token_l[0:1, 0:1], mod8)
    half_e, land_e = _exchange_wait(send_e, recv_e, half_e, land_e, small_e, _chip_copies, "chip_exchange_wait_early")
    half_l, land_l = _exchange_wait(send_l, recv_l, half_l, land_l, small_e, _chip_copies, "chip_exchange_wait_late")
    got_early, got_late = own_slot(half_e, land_e), own_slot(half_l, land_l)
    big_in = [a.T for a in _adam_w_in(got_late, w_in[0].T, m_w_in[0].T, v_w_in[0].T)]
    squares = (("w_conv_out", w_conv_out, m_w_conv_out, v_w_conv_out), ("w_attn_out", w_attn_out, m_w_attn_out, v_w_attn_out),
               ("w_out", w_out, m_w_out, v_w_out))
    big_sq = [_reduce_adam(got_early, w[0], m[0], v[0], "adam_" + nm, ROWS_SQ, ROWS_TAIL // ROWS_SQ + j) for j, (nm, w, m, v) in enumerate(squares)]
    tail = _reduce_adam(got_early, _small_slab(w_uq[0], w_ukv[0], conv_w[0]), _small_slab(m_w_uq[0], m_w_ukv[0], m_conv_w[0]),
                        _small_slab(v_w_uq[0], v_w_ukv[0], v_conv_w[0]), "adam_small_sharded", ROWS_TAIL, 0)
    tail = [_unpack_small_slab(a) for a in tail]
    big = [(big_in[i], big_sq[0][i], big_sq[1][i], big_sq[2][i], *tail[i]) for i in range(4)]

    dmod = jnp.concatenate([small_e[0], small_e[1], small_a[1]])
    payload = _pack_small([dmod, small_e[2], small_d[0], small_c[0], small_c[1], small_a[0], small_b[0], small_b[1], small_a[2, 0:1]])
    pay_all = _all_gather(payload, "gather_small")
    small_w = (b_ada, norm_w, conv_b, conv_ln_w, conv_ln_b, final_norm_w, q_norm_w, kv_norm_w)
    small_m = (m_b_ada, m_norm_w, m_conv_b, m_conv_ln_w, m_conv_ln_b, m_final_norm_w, m_q_norm_w, m_kv_norm_w)
    small_v = (v_b_ada, v_norm_w, v_conv_b, v_conv_ln_w, v_conv_ln_b, v_final_norm_w, v_q_norm_w, v_kv_norm_w)
    sm = _reduce_adam(pay_all, _pack_small(small_w), _pack_small(small_m), _pack_small(small_v), "adam_replicated", 8, 0)
    loss = sm[0].reshape(-1)[sum(SMALL_SIZES)]
    shapes = [t.shape for t in small_w]
    sm = [_unpack_small(a, shapes) for a in sm]

    dmod_all = pay_all.reshape(N_DEV, SMALL_LEN)[:, :3 * D]
    dmod_cols = lax.dynamic_slice(dmod_all, (0, me * ada_cols), (N_DEV, ada_cols))
    ada = _ada_bwd(c_all, dmod_cols, w_ada[0], m_w_ada[0], v_w_ada[0])

    def group(i):
        b_in, b_co, b_ao, b_out, b_uq, b_ukv, b_cw = big[i]
        s_bada, s_nw, s_cb, s_clw, s_clb, s_fnw, s_qnw, s_kvnw = sm[i]
        return (ada[i][None], s_bada, s_nw, b_in[None], b_cw[None], s_cb, s_clw, s_clb, b_co[None], s_qnw, b_uq[None], s_kvnw,
                b_ukv[None], b_ao[None], b_out[None], s_fnw)

    return (loss, grad_x[None], *group(0), *group(1), *group(2), *group(3))
```

```python
import functools

import jax
import jax.numpy as jnp
from jax import lax
from jax.experimental import pallas as pl
from jax.experimental.pallas import tpu as pltpu

F32 = jnp.float32
BF16 = jnp.bfloat16

D = 1024
NH = 8
NOPE = 128
ROPE = 64
HALF = ROPE // 2
DQK = NOPE + ROPE
DV = 128
QL = 256
KVL = 256
KCONV = 31
KPAD = 32
HALO = 32
IN_COLS = 6720
MLA_COLS = QL + KVL + ROPE
PROJ_COLS = 7 * D
EPS = 1e-6
ROPE_THETA = 10000.0
N_DEV = 8

ADAM_LR = 0.001
ADAM_B1 = 0.9
ADAM_B2 = 0.999
ADAM_EPS = 1e-08
ADAM_WD = 0.01
ADAM_STEP = 10

ROWS_IN = 840
ROWS_SQ = 128
ROWS_UQ = 48
ROWS_UKV = 64
ROWS_CW = 4
ROWS_IN_PAD = 896
OFF_TAIL = ROWS_IN_PAD
ROWS_TAIL = 128
OFF_SQ = OFF_TAIL + ROWS_TAIL
ROWS_PACK = OFF_SQ + 3 * ROWS_SQ
SMALL_SIZES = (3 * D, D, D, D, D, D, QL, KVL)
SMALL_COLS = 1152
SMALL_LEN = 8 * SMALL_COLS

MESH = pl.DeviceIdType.MESH
ANY = pl.BlockSpec(memory_space=pl.ANY)
V7X_VMEM_LIMIT = 56 * 1024 * 1024


def _cparams(n_axes, vmem=V7X_VMEM_LIMIT):
    return pltpu.CompilerParams(dimension_semantics=("arbitrary",) * n_axes, vmem_limit_bytes=vmem)


def _sig(x):
    return jax.nn.sigmoid(x)


def _nt(a, b):
    return lax.dot_general(a, b, (((1,), (1,)), ((), ())), preferred_element_type=F32)


def _tn(a, b):
    return lax.dot_general(a, b, (((0,), (0,)), ((), ())), preferred_element_type=F32)


def _nn(a, b):
    return jnp.dot(a, b, preferred_element_type=F32)


def _const(shape):
    return pl.BlockSpec(shape, lambda *_: (0,) * len(shape))


def _all_gather(block, name):
    r, c = block.shape

    def body(x_ref, out_ref, send_sems, recv_sems, local_sem):
        x, y, cc = lax.axis_index("x"), lax.axis_index("y"), lax.axis_index("c")
        me, sibling = (x, y, cc), (x, y, 1 - cc)
        chips = [(1 - x, y), (x, 1 - y), (1 - x, 1 - y)]

        def slot(px, py, pc):
            return out_ref.at[4 * px + 2 * py + pc]

        def copy(k, blk, to, src=None):
            return pltpu.make_async_remote_copy(
                src_ref=slot(*blk) if src is None else src, dst_ref=slot(*blk),
                send_sem=send_sems.at[k], recv_sem=recv_sems.at[k],
                device_id=to, device_id_type=MESH)

        mine = pltpu.make_async_copy(x_ref, slot(*me), local_sem)
        mine.start()
        first = [copy(0, me, sibling, src=x_ref)]
        first += [copy(1 + j, me, (*chip, cc), src=x_ref) for j, chip in enumerate(chips)]
        for cp in first:
            cp.start()
        passed = [copy(4 + j, (*chip, cc), sibling) for j, chip in enumerate(chips)]
        for j, chip in enumerate(chips):
            copy(1 + j, (*chip, cc), me).wait_recv()
            passed[j].start()
        copy(0, sibling, me).wait_recv()
        for j, chip in enumerate(chips):
            copy(4 + j, (*chip, 1 - cc), me).wait_recv()
        for cp in first + passed:
            cp.wait_send()
        mine.wait()

    return pl.pallas_call(
        body, name=name,
        out_shape=jax.ShapeDtypeStruct((N_DEV, r, c), block.dtype),
        in_specs=[ANY], out_specs=ANY,
        scratch_shapes=[pltpu.SemaphoreType.DMA((7,)), pltpu.SemaphoreType.DMA((7,)), pltpu.SemaphoreType.DMA],
    )(block)


N_CHIP = 4


def _pair_exchange(packed, name):
    _, _, r, c = packed.shape

    def body(src_ref, out_ref, send_sem, recv_sem):
        x, y, cc = lax.axis_index("x"), lax.axis_index("y"), lax.axis_index("c")
        cp = pltpu.make_async_remote_copy(
            src_ref=src_ref.at[1 - cc], dst_ref=out_ref, send_sem=send_sem, recv_sem=recv_sem,
            device_id=(x, y, 1 - cc), device_id_type=MESH)
        cp.start()
        cp.wait()

    return pl.pallas_call(
        body, name=name,
        out_shape=jax.ShapeDtypeStruct((N_CHIP, r, c), packed.dtype),
        in_specs=[ANY], out_specs=ANY,
        scratch_shapes=[pltpu.SemaphoreType.DMA, pltpu.SemaphoreType.DMA],
    )(packed)


def _pair_add(core, packed, got, tr, name):
    _, _, r, c = packed.shape

    def body(core_ref, own_ref, got_ref, out_ref):
        out_ref[...] = (own_ref[0].astype(F32) + got_ref[...].astype(F32)).astype(out_ref.dtype)

    blk = pl.BlockSpec((1, tr, c), lambda j, i, core_ref: (j, i, 0))
    return pl.pallas_call(
        body, name=name,
        grid_spec=pltpu.PrefetchScalarGridSpec(
            num_scalar_prefetch=1, grid=(N_CHIP, r // tr),
            in_specs=[pl.BlockSpec((1, 1, tr, c), lambda j, i, core_ref: (core_ref[0], j, i, 0)), blk],
            out_specs=blk),
        out_shape=jax.ShapeDtypeStruct((N_CHIP, r, c), packed.dtype),
        compiler_params=_cparams(2),
    )(core, packed, got)


HBM = pl.BlockSpec(memory_space=pltpu.HBM)
SEM = pl.BlockSpec(memory_space=pltpu.SEMAPHORE)
EFFECT = pltpu.SideEffectType.DATAFLOW_SIDE_EFFECTING


def _chip_copies(src_ref, land_ref, send_sems, recv_sems):
    x, y, cc = lax.axis_index("x"), lax.axis_index("y"), lax.axis_index("c")
    me = 2 * x + y
    copies = []
    for k in range(1, N_CHIP):
        px, py = (1 - x if k & 2 else x), (1 - y if k & 1 else y)
        copies.append(pltpu.make_async_remote_copy(
            src_ref=src_ref.at[2 * px + py], dst_ref=land_ref.at[me],
            send_sem=send_sems.at[k - 1], recv_sem=recv_sems.at[k - 1],
            device_id=(px, py, cc), device_id_type=MESH))
    return copies


def _direct_copies(src_ref, land_ref, send_sems, recv_sems):
    x, y, cc = lax.axis_index("x"), lax.axis_index("y"), lax.axis_index("c")
    me = 4 * x + 2 * y + cc
    copies = []
    for k in range(1, N_DEV):
        peer = ((1 - x if k & 4 else x), (1 - y if k & 2 else y), (1 - cc if k & 1 else cc))
        copies.append(pltpu.make_async_remote_copy(
            src_ref=src_ref, dst_ref=land_ref.at[me], send_sem=send_sems.at[k - 1], recv_sem=recv_sems.at[k - 1],
            device_id=peer, device_id_type=MESH))
    return copies


def _exchange_start(src, land_shape, copies_of, n_copies, name):
    def body(src_ref, land_ref, send_sems, recv_sems, src_thru, land_thru, token):
        for cp in copies_of(src_ref, land_ref, send_sems, recv_sems):
            cp.start()
        token[...] = jnp.zeros_like(token)

    return pl.pallas_call(
        body, name=name,
        out_shape=(pltpu.SemaphoreType.DMA((n_copies,)), pltpu.SemaphoreType.DMA((n_copies,)),
                   pltpu.HBM(src.shape, src.dtype), pltpu.HBM(land_shape, src.dtype), jax.ShapeDtypeStruct((8, 128), F32)),
        in_specs=(HBM, HBM), out_specs=(SEM, SEM, HBM, HBM, pl.BlockSpec(memory_space=pltpu.VMEM)),
        input_output_aliases={0: 2, 1: 3},
        compiler_params=pltpu.CompilerParams(has_side_effects=EFFECT),
    )(pltpu.with_memory_space_constraint(src, pltpu.HBM),
      pltpu.with_memory_space_constraint(lax.empty(land_shape, src.dtype), pltpu.HBM))


def _exchange_wait(send_sems, recv_sems, src_thru, land_thru, after, copies_of, name):
    def body(src_ref, land_ref, send_sems, recv_sems, after_ref, src_dead, got_ref):
        copies = copies_of(src_ref, land_ref, send_sems, recv_sems)
        for cp in copies:
            cp.wait_send()
        for cp in copies:
            cp.wait_recv()

    return pl.pallas_call(
        body, name=name,
        out_shape=(pltpu.HBM(src_thru.shape, src_thru.dtype), pltpu.HBM(land_thru.shape, land_thru.dtype)),
        in_specs=(HBM, HBM, SEM, SEM, ANY), out_specs=(HBM, HBM), input_output_aliases={0: 0, 1: 1},
        compiler_params=pltpu.CompilerParams(has_side_effects=EFFECT),
    )(src_thru, land_thru, send_sems, recv_sems, after)


def _adam(g, w, m, v):
    m = ADAM_B1 * m + (1.0 - ADAM_B1) * g
    v = ADAM_B2 * v + (1.0 - ADAM_B2) * (g * g)
    m_hat = m / (1.0 - ADAM_B1 ** ADAM_STEP)
    v_hat = v / (1.0 - ADAM_B2 ** ADAM_STEP)
    delta = -ADAM_LR * (m_hat / (jnp.sqrt(v_hat) + ADAM_EPS) + ADAM_WD * w)
    return delta, m, v


def _adam_w_in(parts, wt, mt, vt):
    n = parts.shape[0]
    tc = 256

    def body(p_ref, w_ref, m_ref, v_ref, g_out, d_out, m_out, v_out):
        g = p_ref[0].astype(F32)
        for j in range(1, n):
            g = g + p_ref[j].astype(F32)
        g = g[:ROWS_IN]
        delta, nm, nv = _adam(g, w_ref[...], m_ref[...], v_ref[...])
        g_out[...] = g
        d_out[...] = delta
        m_out[...] = nm
        v_out[...] = nv

    col = pl.BlockSpec((ROWS_IN, tc), lambda i: (0, i))
    return pl.pallas_call(
        body, name="adam_w_in", grid=(D // tc,),
        in_specs=[pl.BlockSpec((n, ROWS_IN_PAD, tc), lambda i: (0, 0, i)), col, col, col],
        out_specs=[col] * 4,
        out_shape=[jax.ShapeDtypeStruct((ROWS_IN, D), F32)] * 4,
        compiler_params=_cparams(1),
    )(parts, wt, mt, vt)


def _reduce_adam(parts, w, m, v, name, tr, first_block):
    n, _, c = parts.shape
    r = w.shape[0]

    def body(p_ref, w_ref, m_ref, v_ref, g_out, d_out, m_out, v_out):
        g = p_ref[0].astype(F32)
        for j in range(1, n):
            g = g + p_ref[j].astype(F32)
        delta, nm, nv = _adam(g, w_ref[...], m_ref[...], v_ref[...])
        g_out[...] = g
        d_out[...] = delta
        m_out[...] = nm
        v_out[...] = nv

    row = pl.BlockSpec((tr, c), lambda i: (i, 0))
    return pl.pallas_call(
        body, name=name, grid=(r // tr,),
        in_specs=[pl.BlockSpec((n, tr, c), lambda i: (0, first_block + i, 0)), row, row, row],
        out_specs=[row] * 4,
        out_shape=[jax.ShapeDtypeStruct((r, c), F32)] * 4,
        compiler_params=_cparams(1),
    )(parts, w, m, v)


def _ada_mod(c_all, w_ada, b_cols):
    def body(c_ref, w_ref, b_ref, o_ref):
        cv = c_ref[...]
        act = (cv * _sig(cv)).astype(BF16)
        o_ref[...] = _nn(act, w_ref[...].astype(BF16)) + b_ref[...]

    return pl.pallas_call(body, name="ada_mod", out_shape=jax.ShapeDtypeStruct((N_DEV, w_ada.shape[1]), F32))(c_all, w_ada, b_cols)


def _ada_bwd(c_all, dmod_cols, w, m, v):
    def body(c_ref, d_ref, w_ref, m_ref, v_ref, g_out, d_out, m_out, v_out):
        cv = c_ref[...]
        act = (cv * _sig(cv)).astype(BF16)
        g = _tn(act, d_ref[...].astype(BF16))
        delta, nm, nv = _adam(g, w_ref[...], m_ref[...], v_ref[...])
        g_out[...] = g
        d_out[...] = delta
        m_out[...] = nm
        v_out[...] = nv

    return pl.pallas_call(body, name="ada_bwd", out_shape=[jax.ShapeDtypeStruct(w.shape, F32)] * 4)(c_all, dmod_cols, w, m, v)


def _in_proj(x, norm_w, mod8, wt):
    s = x.shape[0]
    tm = min(2048, s)
    nk = wt.shape[0] // D

    def body(x_ref, nw_ref, mod_ref, w_ref, proj_ref, h_ref, hs_ref):
        @pl.when(pl.program_id(1) == 0)
        def _():
            xv = x_ref[...]
            rstd = lax.rsqrt(jnp.mean(xv * xv, axis=-1, keepdims=True) + EPS)
            h = (xv * rstd) * nw_ref[...] * (1.0 + mod_ref[1:2, :]) + mod_ref[0:1, :]
            hs_ref[...] = h.astype(BF16)
            h_ref[...] = hs_ref[...]

        proj_ref[...] = _nt(hs_ref[...], w_ref[...]).astype(proj_ref.dtype)

    return pl.pallas_call(
        body, name="in_proj", grid=(s // tm, nk),
        in_specs=[pl.BlockSpec((tm, D), lambda i, k: (i, 0)), _const((1, D)), _const((8, D)),
                  pl.BlockSpec((D, D), lambda i, k: (k, 0))],
        out_specs=[pl.BlockSpec((tm, D), lambda i, k: (i, k)), pl.BlockSpec((tm, D), lambda i, k: (i, 0))],
        out_shape=[jax.ShapeDtypeStruct((s, nk * D), BF16), jax.ShapeDtypeStruct((s, D), BF16)],
        scratch_shapes=[pltpu.VMEM((tm, D), BF16)],
        compiler_params=_cparams(2),
    )(x, norm_w, mod8, wt)


CONV_RC = 128
CONV_LC = 128
GW_RC = 32
SUBLANES = 8


def _shifted_taps(win_ref, weight_of, offsets, r0, lanes):
    acc = jnp.zeros((CONV_RC, CONV_LC), F32)
    for b in range(SUBLANES):
        group = [o for o in offsets if o % SUBLANES == b]
        if not group:
            continue
        rows = CONV_RC if b == 0 else CONV_RC + SUBLANES
        part = jnp.zeros((rows, CONV_LC), F32)
        for o in group:
            part = part + win_ref[pl.ds(r0 + o - b, rows), lanes] * weight_of(o)
        acc = acc + (part if b == 0 else part[b:b + CONV_RC])
    return acc


def _conv_fwd(proj, conv_w, conv_b, ln_w, ln_b):
    s = proj.shape[0]
    tm = min(256, s)
    hb = tm // HALO

    def body(av_ref, ag_ref, avh_ref, agh_ref, gate_ref, cw_ref, cb_ref, lw_ref, lb_ref, u1_ref, ya_ref, win_ref):
        i = pl.program_id(0)
        halo = avh_ref[...].astype(F32) * _sig(agh_ref[...].astype(F32))
        win_ref[0:HALO, :] = jnp.where(i > 0, halo, 0.0)
        win_ref[HALO:HALO + tm, :] = av_ref[...].astype(F32) * _sig(ag_ref[...].astype(F32))
        first = HALO - (KCONV - 1)
        for r0 in range(0, tm, CONV_RC):
            for c0 in range(0, D, CONV_LC):
                acc = _shifted_taps(win_ref, lambda o: cw_ref[o - first:o - first + 1, c0:c0 + CONV_LC],
                                    range(first, first + KCONV), r0, pl.ds(c0, CONV_LC))
                u1_ref[r0:r0 + CONV_RC, c0:c0 + CONV_LC] = acc + cb_ref[:, c0:c0 + CONV_LC]
        u1 = u1_ref[...]
        mu = jnp.mean(u1, axis=-1, keepdims=True)
        xc = u1 - mu
        var = jnp.mean(xc * xc, axis=-1, keepdims=True)
        ln = xc * lax.rsqrt(var + EPS) * lw_ref[...] + lb_ref[...]
        gate = gate_ref[...].astype(F32)
        ya_ref[...] = ((ln * _sig(ln)) * (gate * _sig(gate))).astype(BF16)

    row = lambda k: pl.BlockSpec((tm, D), lambda i: (i, k))
    prev = lambda k: pl.BlockSpec((HALO, D), lambda i: (jnp.maximum(i * hb - 1, 0), k))
    return pl.pallas_call(
        body, name="conv_fwd", grid=(s // tm,),
        in_specs=[row(0), row(1), prev(0), prev(1), row(2), _const((KPAD, D)), _const((1, D)), _const((1, D)), _const((1, D))],
        out_specs=[pl.BlockSpec((tm, D), lambda i: (i, 0))] * 2,
        out_shape=[jax.ShapeDtypeStruct((s, D), F32), jax.ShapeDtypeStruct((s, D), BF16)],
        scratch_shapes=[pltpu.VMEM((tm + HALO, D), F32)],
        compiler_params=_cparams(1),
    )(proj, proj, proj, proj, proj, conv_w, conv_b, ln_w, ln_b)


def _rope_tables(pos_ref, if_ref):
    ang = pos_ref[...].astype(F32) * if_ref[...]
    return jnp.cos(ang), jnp.sin(ang)


def _rms_parts(x):
    rstd = lax.rsqrt(jnp.mean(x * x, axis=-1, keepdims=True) + EPS)
    return x * rstd, rstd


def _mla_prep(proj, pos, inv_freq, qnw, kvnw, wuq, wukv):
    s = proj.shape[0]
    tm = min(512, s)

    def body(p_ref, pos_ref, if_ref, qnw_ref, kvnw_ref, wuq_ref, wukv_ref, q_ref, k_ref, v_ref):
        blk = p_ref[...].astype(F32)
        cos, sin = _rope_tables(pos_ref, if_ref)

        def rope(r):
            x1, x2 = r[:, :HALF], r[:, HALF:]
            return jnp.concatenate([x1 * cos - x2 * sin, x1 * sin + x2 * cos], axis=-1)

        qlat = _rms_parts(blk[:, :QL])[0] * qnw_ref[...]
        kvlat = _rms_parts(blk[:, QL:QL + KVL])[0] * kvnw_ref[...]
        q = _nn(qlat.astype(BF16), wuq_ref[...])
        kv = _nn(kvlat.astype(BF16), wukv_ref[...])
        kr = rope(blk[:, QL + KVL:MLA_COLS])
        for h in range(NH):
            qh = q[:, h * DQK:(h + 1) * DQK]
            q_ref[h] = (jnp.concatenate([qh[:, :NOPE], rope(qh[:, NOPE:])], axis=-1) * ATTN_SCALE).astype(BF16)
            k_ref[h] = jnp.concatenate([kv[:, h * 256:h * 256 + NOPE], kr], axis=-1).astype(BF16)
            v_ref[h] = kv[:, h * 256 + NOPE:(h + 1) * 256].astype(BF16)

    hm = lambda d: pl.BlockSpec((NH, tm, d), lambda i: (0, i, 0))
    return pl.pallas_call(
        body, name="mla_prep", grid=(s // tm,),
        in_specs=[pl.BlockSpec((tm, D), lambda i: (i, 3)), pl.BlockSpec((tm, 1), lambda i: (i, 0)), _const((1, HALF)),
                  _const((1, QL)), _const((1, KVL)), _const((QL, NH * DQK)), _const((KVL, NH * 256))],
        out_specs=[hm(DQK), hm(DQK), hm(DV)],
        out_shape=[jax.ShapeDtypeStruct((NH, s, DQK), BF16), jax.ShapeDtypeStruct((NH, s, DQK), BF16),
                   jax.ShapeDtypeStruct((NH, s, DV), BF16)],
        compiler_params=_cparams(1),
    )(proj, pos, inv_freq, qnw, kvnw, wuq, wukv)


ATTN_SCALE = DQK ** -0.5


def _causal_mask(s, t):
    rows = lax.broadcasted_iota(jnp.int32, (t, t), 0)
    cols = lax.broadcasted_iota(jnp.int32, (t, t), 1)
    return jnp.where(cols <= rows, s, -jnp.inf)


def _attn_tile(s):
    return min(1024, s // 2)


def _attn_fwd(q, k, v):
    nh, s, _ = q.shape
    t = _attn_tile(s)

    def body(q_ref, k_ref, v_ref, o_ref, lse_ref):
        qi = pl.program_id(1)
        qv = q_ref[0]

        def chunk(c, carry, diag):
            m, l, acc = carry
            rows = pl.ds(pl.multiple_of(c * t, t), t)
            sc = _nt(qv, k_ref[0, rows, :])
            if diag:
                sc = _causal_mask(sc, t)
            m_new = jnp.maximum(m, jnp.max(sc, axis=-1, keepdims=True))
            alpha = jnp.exp(m - m_new)
            p = jnp.exp(sc - m_new)
            l = alpha * l + jnp.sum(p, axis=-1, keepdims=True)
            acc = alpha * acc + _nn(p.astype(BF16), v_ref[0, rows, :])
            return m_new, l, acc

        init = (jnp.full((t, 1), -jnp.inf, F32), jnp.zeros((t, 1), F32), jnp.zeros((t, DV), F32))
        carry = lax.fori_loop(0, qi, lambda c, cr: chunk(c, cr, False), init)
        m, l, acc = chunk(qi, carry, True)
        o_ref[...] = acc / l
        lse_ref[0] = jnp.broadcast_to(m + jnp.log(l), (t, DV))

    head = lambda d: pl.BlockSpec((1, s, d), lambda h, i: (h, 0, 0))
    return pl.pallas_call(
        body, name="attn_fwd", grid=(nh, s // t),
        in_specs=[pl.BlockSpec((1, t, DQK), lambda h, i: (h, i, 0)), head(DQK), head(DV)],
        out_specs=[pl.BlockSpec((t, DV), lambda h, i: (i, h)), pl.BlockSpec((1, t, DV), lambda h, i: (h, i, 0))],
        out_shape=[jax.ShapeDtypeStruct((s, nh * DV), F32), jax.ShapeDtypeStruct((nh, s, DV), F32)],
        compiler_params=_cparams(2),
    )(q, k, v)


def _merge_loss(x, target, ya, o, proj, mod8, fnw, gathered):
    s = x.shape[0]
    tm = min(256, s)
    n = s // tm

    def body(x_ref, t_ref, ya_ref, o_ref, bg_ref, ga_ref, gb_ref, mod_ref, fnw_ref, wco_ref, wao_ref, wout_ref,
             dya_ref, do_ref, delta_ref, dpg_ref, dx2_ref, gw_ref, small_ref, acc_ref, cast_ref):
        i = pl.program_id(0)

        @pl.when(i == 0)
        def _():
            acc_ref[...] = jnp.zeros(acc_ref.shape, F32)
            small_ref[...] = jnp.zeros(small_ref.shape, F32)

        bg = bg_ref[...].astype(F32)
        sbg = _sig(bg)
        sb = bg * sbg
        ov = o_ref[...]
        ya = ya_ref[...]
        yb = (ov * sb).astype(BF16)
        square = lambda ref: ref[...].reshape(D, D)
        y_a = _nn(ya, square(wco_ref))
        y_b = _nn(yb, square(wao_ref))
        sa = _sig(ga_ref[...].astype(F32))
        sgb = _sig(gb_ref[...].astype(F32))
        merged = (sa * y_a + sgb * y_b).astype(BF16)
        z = _nn(merged, square(wout_ref))
        gate = mod_ref[2:3, :]
        x2 = x_ref[...] + gate * z
        xn, rstd = _rms_parts(x2)
        fnw = fnw_ref[...]
        err = xn * fnw - t_ref[...]
        loss = jnp.sum(jnp.sum(err * err, axis=-1, keepdims=True), axis=0, keepdims=True) * (0.5 / D)
        dy = err * (1.0 / D)
        small_ref[0:1, :] += jnp.sum(dy * xn, axis=0, keepdims=True)
        dxn = dy * fnw
        dx2 = rstd * (dxn - xn * jnp.mean(dxn * xn, axis=-1, keepdims=True))
        dx2_ref[...] = dx2
        small_ref[1:2, :] += jnp.sum(dx2 * z, axis=0, keepdims=True)
        small_ref[2:3, :] += jnp.broadcast_to(loss, (1, D))
        dz = (dx2 * gate).astype(BF16)
        dmerged = _nt(dz, square(wout_ref))
        acc_ref[2] += _tn(merged, dz)
        dy_a = (dmerged * sa).astype(BF16)
        dy_b = (dmerged * sgb).astype(BF16)
        dpg_ref[:, D:2 * D] = (dmerged * y_a * (sa * (1.0 - sa))).astype(BF16)
        dpg_ref[:, 2 * D:3 * D] = (dmerged * y_b * (sgb * (1.0 - sgb))).astype(BF16)
        dya_ref[...] = _nt(dy_a, square(wco_ref))
        acc_ref[0] += _tn(ya, dy_a)
        dyb = _nt(dy_b, square(wao_ref))
        acc_ref[1] += _tn(yb, dy_b)
        do = dyb * sb
        do_ref[...] = do.astype(BF16)
        dpg_ref[:, 0:D] = (dyb * ov * (sbg * (1.0 + bg * (1.0 - sbg)))).astype(BF16)
        prod = do * ov
        for h in range(NH):
            delta_ref[h] = jnp.broadcast_to(jnp.sum(prod[:, h * DV:(h + 1) * DV], axis=-1, keepdims=True), (tm, DV))

        @pl.when(i == n - 1)
        def _():
            for j in range(3):
                cast_ref[...] = acc_ref[j].astype(cast_ref.dtype)
                pltpu.sync_copy(cast_ref, gw_ref.at[j])

    row = pl.BlockSpec((tm, D), lambda i: (i, 0))
    col = lambda k: pl.BlockSpec((tm, D), lambda i: (i, k))
    wspec = lambda j: pl.BlockSpec((N_DEV, ROWS_SQ, D), lambda i: (0, j, 0), pipeline_mode=pl.Buffered(1))
    return pl.pallas_call(
        body, name="merge_loss", grid=(n,),
        in_specs=[row, row, row, row, col(4), col(5), col(6), _const((8, D)), _const((1, D)), wspec(0), wspec(1), wspec(2)],
        out_specs=[row, row, pl.BlockSpec((NH, tm, DV), lambda i: (0, i, 0)), pl.BlockSpec((tm, 3 * D), lambda i: (i, 0)),
                   row, ANY, _const((8, D))],
        out_shape=[jax.ShapeDtypeStruct((s, D), F32), jax.ShapeDtypeStruct((s, D), BF16),
                   jax.ShapeDtypeStruct((NH, s, DV), F32), jax.ShapeDtypeStruct((s, 3 * D), BF16),
                   jax.ShapeDtypeStruct((s, D), F32), jax.ShapeDtypeStruct((3, D, D), BF16),
                   jax.ShapeDtypeStruct((8, D), F32)],
        scratch_shapes=[pltpu.VMEM((3, D, D), F32), pltpu.VMEM((D, D), BF16)],
        compiler_params=_cparams(1),
    )(x, target, ya, o, proj, proj, proj, mod8, fnw, gathered, gathered, gathered)


def _attn_bwd(q, k, v, do, lse, delta):
    nh, s, _ = q.shape
    t = _attn_tile(s)
    nb = s // t

    def body(q_ref, k_ref, v_ref, do_ref, lse_ref, dl_ref, dq_ref, dk_ref, dv_ref):
        kj = pl.program_id(1)

        @pl.when(kj == 0)
        def _():
            dq_ref[...] = jnp.zeros(dq_ref.shape, F32)

        kv_, vv = k_ref[0], v_ref[0]

        def chunk(c, carry, diag):
            dk, dv = carry
            rows = pl.ds(pl.multiple_of(c * t, t), t)
            qv = q_ref[0, rows, :]
            dov = do_ref[rows, :]
            sc = _nt(qv, kv_)
            if diag:
                sc = _causal_mask(sc, t)
            p = jnp.exp(sc - lse_ref[0, rows, 0:1])
            dv = dv + _tn(p.astype(BF16), dov)
            dp = _nt(dov, vv)
            ds = (p * (dp - dl_ref[0, rows, 0:1])).astype(BF16)
            dk = dk + _tn(ds, qv)
            dq_ref[0, rows, :] += _nn(ds, kv_)
            return dk, dv

        carry = chunk(kj, (jnp.zeros((t, DQK), F32), jnp.zeros((t, DV), F32)), True)
        dk, dv = lax.fori_loop(kj + 1, nb, lambda c, cr: chunk(c, cr, False), carry)
        dk_ref[0] = dk
        dv_ref[0] = dv

    head = lambda d: pl.BlockSpec((1, s, d), lambda h, j: (h, 0, 0))
    blk = lambda d: pl.BlockSpec((1, t, d), lambda h, j: (h, j, 0))
    return pl.pallas_call(
        body, name="attn_bwd", grid=(nh, nb),
        in_specs=[head(DQK), blk(DQK), blk(DV), pl.BlockSpec((s, DV), lambda h, j: (0, h)), head(DV), head(DV)],
        out_specs=[head(DQK), blk(DQK), blk(DV)],
        out_shape=[jax.ShapeDtypeStruct((nh, s, DQK), F32), jax.ShapeDtypeStruct((nh, s, DQK), F32),
                   jax.ShapeDtypeStruct((nh, s, DV), F32)],
        compiler_params=_cparams(2),
    )(q, k, v, do, lse, delta)


def _mla_bwd(dq, dk, dv, proj, pos, inv_freq, qnw, kvnw, wuq, wukv):
    s = proj.shape[0]
    tm = min(512, s)

    def body(dq_ref, dk_ref, dv_ref, p_ref, pos_ref, if_ref, qnw_ref, kvnw_ref, wuq_ref, wukv_ref,
             dp_ref, guq_ref, gukv_ref, small_ref):
        @pl.when(pl.program_id(0) == 0)
        def _():
            guq_ref[...] = jnp.zeros(guq_ref.shape, F32)
            gukv_ref[...] = jnp.zeros(gukv_ref.shape, F32)
            small_ref[...] = jnp.zeros(small_ref.shape, F32)

        blk = p_ref[...].astype(F32)
        cos, sin = _rope_tables(pos_ref, if_ref)

        def unrope(g):
            g1, g2 = g[:, :HALF], g[:, HALF:]
            return jnp.concatenate([g1 * cos + g2 * sin, g2 * cos - g1 * sin], axis=-1)

        dq_cols, dkv_cols = [], []
        dkr = jnp.zeros((tm, ROPE), F32)
        for h in range(NH):
            dqh, dkh = dq_ref[h] * ATTN_SCALE, dk_ref[h]
            dq_cols += [dqh[:, :NOPE], unrope(dqh[:, NOPE:])]
            dkv_cols += [dkh[:, :NOPE], dv_ref[h]]
            dkr = dkr + dkh[:, NOPE:]
        dq_full = jnp.concatenate(dq_cols, axis=-1).astype(BF16)
        dkv_full = jnp.concatenate(dkv_cols, axis=-1).astype(BF16)

        def latent_bwd(c, nw_ref, d_up, w_ref, g_ref, srow):
            nrm, rstd = _rms_parts(c)
            nw = nw_ref[...]
            lat = (nrm * nw).astype(BF16)
            g_ref[...] += _tn(lat, d_up)
            dlat = _nt(d_up, w_ref[...])
            small_ref[srow:srow + 1, :] += jnp.sum(dlat * nrm, axis=0, keepdims=True)
            dn = dlat * nw
            return rstd * (dn - nrm * jnp.mean(dn * nrm, axis=-1, keepdims=True))

        dcq = latent_bwd(blk[:, :QL], qnw_ref, dq_full, wuq_ref, guq_ref, 0)
        dckv = latent_bwd(blk[:, QL:QL + KVL], kvnw_ref, dkv_full, wukv_ref, gukv_ref, 1)
        dp_ref[...] = jnp.concatenate([dcq, dckv, unrope(dkr), jnp.zeros((tm, D - MLA_COLS), F32)], axis=-1).astype(BF16)

    hm = lambda d: pl.BlockSpec((NH, tm, d), lambda i: (0, i, 0))
    return pl.pallas_call(
        body, name="mla_bwd", grid=(s // tm,),
        in_specs=[hm(DQK), hm(DQK), hm(DV), pl.BlockSpec((tm, D), lambda i: (i, 3)), pl.BlockSpec((tm, 1), lambda i: (i, 0)),
                  _const((1, HALF)), _const((1, QL)), _const((1, KVL)), _const((QL, NH * DQK)), _const((KVL, NH * 256))],
        out_specs=[pl.BlockSpec((tm, D), lambda i: (i, 0)), _const((QL, NH * DQK)), _const((KVL, NH * 256)), _const((8, QL))],
        out_shape=[jax.ShapeDtypeStruct((s, D), BF16), jax.ShapeDtypeStruct((QL, NH * DQK), F32),
                   jax.ShapeDtypeStruct((KVL, NH * 256), F32), jax.ShapeDtypeStruct((8, QL), F32)],
        compiler_params=_cparams(1),
    )(dq, dk, dv, proj, pos, inv_freq, qnw, kvnw, wuq, wukv)


def _conv_rows_bwd(dya, u1, proj, ln_w, ln_b):
    s = dya.shape[0]
    tm = min(512, s)

    def body(dya_ref, u1_ref, gate_ref, lw_ref, lb_ref, du1_ref, dag_ref, small_ref):
        @pl.when(pl.program_id(0) == 0)
        def _():
            small_ref[...] = jnp.zeros(small_ref.shape, F32)

        u1 = u1_ref[...]
        mu = jnp.mean(u1, axis=-1, keepdims=True)
        xc = u1 - mu
        rstd = lax.rsqrt(jnp.mean(xc * xc, axis=-1, keepdims=True) + EPS)
        xhat = xc * rstd
        lw = lw_ref[...]
        ln = xhat * lw + lb_ref[...]
        sl = _sig(ln)
        u2 = ln * sl
        gate = gate_ref[...].astype(F32)
        sg = _sig(gate)
        dya = dya_ref[...]
        dag_ref[...] = (dya * u2 * (sg * (1.0 + gate * (1.0 - sg)))).astype(BF16)
        dln = dya * (gate * sg) * (sl * (1.0 + ln * (1.0 - sl)))
        small_ref[0:1, :] += jnp.sum(dln * xhat, axis=0, keepdims=True)
        small_ref[1:2, :] += jnp.sum(dln, axis=0, keepdims=True)
        dxh = dln * lw
        du1_ref[...] = rstd * (dxh - jnp.mean(dxh, axis=-1, keepdims=True) - xhat * jnp.mean(dxh * xhat, axis=-1, keepdims=True))

    row = pl.BlockSpec((tm, D), lambda i: (i, 0))
    return pl.pallas_call(
        body, name="conv_rows_bwd", grid=(s // tm,),
        in_specs=[row, row, pl.BlockSpec((tm, D), lambda i: (i, 2)), _const((1, D)), _const((1, D))],
        out_specs=[row, row, _const((8, D))],
        out_shape=[jax.ShapeDtypeStruct((s, D), F32), jax.ShapeDtypeStruct((s, D), BF16), jax.ShapeDtypeStruct((8, D), F32)],
        compiler_params=_cparams(1),
    )(dya, u1, proj, ln_w, ln_b)


def _conv_bwd(du1, proj, conv_w):
    s = du1.shape[0]
    tm = min(256, s)
    hb = tm // HALO
    n = s // tm
    last32 = s // HALO - 1

    def body(d_ref, dn_ref, av_ref, ag_ref, avh_ref, agh_ref, cw_ref, dp_ref, gcw_ref, small_ref,
             dwin_ref, dpad_ref, dsh_ref, uwin_ref, acc_ref):
        i = pl.program_id(0)

        @pl.when(i == 0)
        def _():
            acc_ref[...] = jnp.zeros(acc_ref.shape, F32)
            small_ref[...] = jnp.zeros(small_ref.shape, F32)
            dpad_ref[...] = jnp.zeros(dpad_ref.shape, F32)
            uwin_ref[...] = jnp.zeros(uwin_ref.shape, F32)

        dv = d_ref[...]
        dwin_ref[0:tm, :] = dv
        dwin_ref[tm:tm + HALO, :] = jnp.where(i < n - 1, dn_ref[...], 0.0)
        dpad_ref[SUBLANES:SUBLANES + tm, :] = dv
        halo = avh_ref[...].astype(F32) * _sig(agh_ref[...].astype(F32))
        uwin_ref[0:HALO, :] = jnp.where(i > 0, halo, 0.0)
        av = av_ref[...].astype(F32)
        sg = _sig(ag_ref[...].astype(F32))
        uwin_ref[HALO:HALO + tm, :] = av * sg
        small_ref[0:1, :] += jnp.sum(dv, axis=0, keepdims=True)

        for b in range(SUBLANES):
            dsh_ref[b] = dpad_ref[pl.ds(SUBLANES - b, tm + SUBLANES), :]

        first = HALO - (KCONV - 1)
        for c0 in range(0, D, CONV_LC):
            lanes = pl.ds(c0, CONV_LC)
            for r0 in range(0, tm, CONV_RC):
                acc = _shifted_taps(dwin_ref, lambda o: cw_ref[KCONV - 1 - o:KCONV - o, c0:c0 + CONV_LC], range(KCONV), r0, lanes)
                a = av[r0:r0 + CONV_RC, c0:c0 + CONV_LC]
                g = sg[r0:r0 + CONV_RC, c0:c0 + CONV_LC]
                dp_ref[r0:r0 + CONV_RC, c0:c0 + CONV_LC] = (acc * g).astype(BF16)
                dp_ref[r0:r0 + CONV_RC, D + c0:D + c0 + CONV_LC] = (acc * a * (g * (1.0 - g))).astype(BF16)
            for b in range(SUBLANES):
                group = [o for o in range(first, first + KCONV) if o % SUBLANES == b]
                parts = [jnp.zeros((SUBLANES, CONV_LC), F32) for _ in group]
                chunks = [(i0, GW_RC) for i0 in range(0, tm, GW_RC)] + ([(tm, SUBLANES)] if b else [])
                for i0, rows in chunks:
                    dsh = dsh_ref[b, pl.ds(i0, rows), lanes]
                    for n_, o in enumerate(group):
                        prod = dsh * uwin_ref[pl.ds(i0 + o - b, rows), lanes]
                        parts[n_] = parts[n_] + jnp.sum(prod.reshape(rows // SUBLANES, SUBLANES, CONV_LC), axis=0)
                for n_, o in enumerate(group):
                    acc_ref[o - first, :, c0:c0 + CONV_LC] += parts[n_]

        @pl.when(i == n - 1)
        def _():
            gcw_ref[...] = jnp.sum(acc_ref[...], axis=1)

    row = lambda k: pl.BlockSpec((tm, D), lambda i: (i, k))
    prev = lambda k: pl.BlockSpec((HALO, D), lambda i: (jnp.maximum(i * hb - 1, 0), k))
    return pl.pallas_call(
        body, name="conv_bwd", grid=(n,),
        in_specs=[row(0), pl.BlockSpec((HALO, D), lambda i: (jnp.minimum((i + 1) * hb, last32), 0)),
                  row(0), row(1), prev(0), prev(1), _const((KPAD, D))],
        out_specs=[pl.BlockSpec((tm, 2 * D), lambda i: (i, 0)), _const((KPAD, D)), _const((8, D))],
        out_shape=[jax.ShapeDtypeStruct((s, 2 * D), BF16), jax.ShapeDtypeStruct((KPAD, D), F32), jax.ShapeDtypeStruct((8, D), F32)],
        scratch_shapes=[pltpu.VMEM((tm + HALO, D), F32), pltpu.VMEM((tm + 2 * SUBLANES, D), F32),
                        pltpu.VMEM((SUBLANES, tm + SUBLANES, D), F32),
                        pltpu.VMEM((tm + HALO + SUBLANES, D), F32), pltpu.VMEM((KPAD, SUBLANES, D), F32)],
        compiler_params=_cparams(1),
    )(du1, du1, proj, proj, proj, proj, conv_w)


def _dproj_specs(tm, rows_first):
    def spec(lo, hi):
        def idx(a, b):
            i, k = (a, b) if rows_first else (b, a)
            col = jnp.clip(k - lo, 0, hi - lo - 1)
            if rows_first:
                return (i, col)
            return (jnp.where((k >= lo) & (k < hi), i, 0), col)
        return pl.BlockSpec((tm, D), idx)
    return [spec(0, 2), spec(2, 3), spec(3, 4), spec(4, 7)]


def _pick_dproj(k, refs, fn):
    vg, ag, mla, gates = refs

    @pl.when(k < 2)
    def _():
        fn(vg)

    @pl.when(k == 2)
    def _():
        fn(ag)

    @pl.when(k == 3)
    def _():
        fn(mla)

    @pl.when(k > 3)
    def _():
        fn(gates)


def _in_proj_bwd_x(dps, wt, x, dx2, norm_w, mod8):
    s = x.shape[0]
    tm = min(1024, s)
    nk = wt.shape[0] // D

    def body(vg_ref, ag_ref, mla_ref, g_ref, w_ref, x_ref, dx2_ref, nw_ref, mod_ref, gx_ref, small_ref, acc_ref):
        i, k = pl.program_id(0), pl.program_id(1)

        @pl.when((i == 0) & (k == 0))
        def _():
            small_ref[...] = jnp.zeros(small_ref.shape, F32)

        @pl.when(k == 0)
        def _():
            acc_ref[...] = jnp.zeros(acc_ref.shape, F32)

        def add(ref):
            acc_ref[...] += _nn(ref[...], w_ref[...])

        _pick_dproj(k, (vg_ref, ag_ref, mla_ref, g_ref), add)

        @pl.when(k == nk - 1)
        def _():
            dh = acc_ref[...]
            xn, rstd = _rms_parts(x_ref[...])
            nw = nw_ref[...]
            hn = xn * nw
            small_ref[0:1, :] += jnp.sum(dh, axis=0, keepdims=True)
            small_ref[1:2, :] += jnp.sum(dh * hn, axis=0, keepdims=True)
            dhn = dh * (1.0 + mod_ref[1:2, :])
            small_ref[2:3, :] += jnp.sum(dhn * xn, axis=0, keepdims=True)
            dxn = dhn * nw
            gx_ref[...] = rstd * (dxn - xn * jnp.mean(dxn * xn, axis=-1, keepdims=True)) + dx2_ref[...]

    row = pl.BlockSpec((tm, D), lambda i, k: (i, 0))
    return pl.pallas_call(
        body, name="in_proj_bwd_x", grid=(s // tm, nk),
        in_specs=_dproj_specs(tm, True) + [pl.BlockSpec((D, D), lambda i, k: (k, 0)), row, row, _const((1, D)), _const((8, D))],
        out_specs=[row, _const((8, D))],
        out_shape=[jax.ShapeDtypeStruct((s, D), F32), jax.ShapeDtypeStruct((8, D), F32)],
        scratch_shapes=[pltpu.VMEM((tm, D), F32)],
        compiler_params=_cparams(2),
    )(*dps, wt, x, dx2, norm_w, mod8)


def _in_proj_bwd_w(dps, h, nk, token):
    s = h.shape[0]
    tm = min(1024, s)

    n = s // tm

    def body(vg_ref, ag_ref, mla_ref, g_ref, h_ref, token_ref, gw_ref, acc_ref):
        k, i = pl.program_id(0), pl.program_id(1)

        @pl.when(i == 0)
        def _():
            acc_ref[...] = jnp.zeros(acc_ref.shape, F32)

        def add(ref):
            acc_ref[...] += _tn(ref[...], h_ref[...])

        _pick_dproj(k, (vg_ref, ag_ref, mla_ref, g_ref), add)

        @pl.when(i == n - 1)
        def _():
            gw_ref[...] = acc_ref[...].astype(gw_ref.dtype)

    return pl.pallas_call(
        body, name="in_proj_bwd_w", grid=(nk, n),
        in_specs=_dproj_specs(tm, False) + [pl.BlockSpec((tm, D), lambda k, i: (i, 0)), _const((8, 128))],
        out_specs=pl.BlockSpec((D, D), lambda k, i: (k, 0)),
        out_shape=jax.ShapeDtypeStruct((nk * D, D), BF16),
        scratch_shapes=[pltpu.VMEM((D, D), F32)],
        compiler_params=_cparams(2),
    )(*dps, h, token)


def _small_slab(wuq, wukv, conv_w):
    cw = jnp.pad(conv_w.reshape(-1), (0, (ROWS_TAIL - ROWS_UQ - ROWS_UKV) * D - KCONV * 128)).reshape(-1, D)
    return jnp.concatenate([wuq.reshape(ROWS_UQ, D), wukv.reshape(ROWS_UKV, D), cw], axis=0)


def _split_small_slab(slab):
    return (slab[..., :ROWS_UQ, :], slab[..., ROWS_UQ:ROWS_UQ + ROWS_UKV, :],
            slab[..., ROWS_UQ + ROWS_UKV:ROWS_UQ + ROWS_UKV + ROWS_CW, :])


def _unpack_small_slab(slab):
    wuq, wukv, cw = _split_small_slab(slab)
    return (wuq.reshape(QL, NH * DQK // N_DEV), wukv.reshape(KVL, NH * 256 // N_DEV),
            cw.reshape(-1)[:KCONV * 128].reshape(KCONV, 128))


def _pack_shard(w_in, wco, wao, wout, wuq, wukv, conv_w):
    bf = lambda a: a.astype(BF16)
    first = jnp.concatenate([jnp.pad(bf(w_in).T, ((0, ROWS_IN_PAD - ROWS_IN), (0, 0))), bf(_small_slab(wuq, wukv, conv_w))], axis=0)
    return first, jnp.concatenate([bf(wco), bf(wao), bf(wout)], axis=0)


def _unpack_gathered(g):
    wt = g[:, :ROWS_IN].reshape(IN_COLS, D)
    split = 3 * D + MLA_COLS
    wt = jnp.concatenate([wt[:split], jnp.zeros((D - MLA_COLS, D), g.dtype), wt[split:]], axis=0)
    wuq, wukv, cw = _split_small_slab(g[:, OFF_TAIL:OFF_TAIL + ROWS_TAIL])
    wuq = wuq.reshape(N_DEV, QL, NH * DQK // N_DEV).transpose(1, 0, 2).reshape(QL, NH * DQK)
    wukv = wukv.reshape(N_DEV, KVL, NH * 256 // N_DEV).transpose(1, 0, 2).reshape(KVL, NH * 256)
    cw = cw.reshape(N_DEV, ROWS_CW * D)[:, :KCONV * 128].reshape(N_DEV, KCONV, 128).transpose(1, 0, 2).reshape(KCONV, D)
    return wt, wuq, wukv, cw


def _by_side(slabs):
    return slabs.reshape(N_CHIP, 2, slabs.shape[1], D).transpose(1, 0, 2, 3)


def _pack_grads_late(gwt):
    split = 3 * D + MLA_COLS
    g_in = jnp.concatenate([gwt[:split], gwt[4 * D:]], axis=0).reshape(N_DEV, ROWS_IN, D)
    return _by_side(jnp.pad(g_in, ((0, 0), (0, ROWS_IN_PAD - ROWS_IN), (0, 0))))


def _pack_grads_early(gw3, guq, gukv, gcw):
    guq = guq.reshape(QL, N_DEV, -1).transpose(1, 0, 2).reshape(N_DEV, ROWS_UQ, D)
    gukv = gukv.reshape(KVL, N_DEV, -1).transpose(1, 0, 2).reshape(N_DEV, ROWS_UKV, D)
    gcw = gcw.reshape(KCONV, N_DEV, 128).transpose(1, 0, 2).reshape(N_DEV, KCONV * 128)
    gcw = jnp.pad(gcw, ((0, 0), (0, (ROWS_TAIL - ROWS_UQ - ROWS_UKV) * D - KCONV * 128))).reshape(N_DEV, -1, D)
    gsq = gw3.reshape(3, N_DEV, ROWS_SQ, D).transpose(1, 0, 2, 3).reshape(N_DEV, 3 * ROWS_SQ, D)
    return _by_side(jnp.concatenate([a.astype(BF16) for a in (guq, gukv, gcw, gsq)], axis=1))


def _pack_small(vecs):
    flat = jnp.concatenate([v.reshape(-1) for v in vecs])
    return jnp.pad(flat, (0, SMALL_LEN - flat.shape[0])).reshape(8, SMALL_COLS)


def _unpack_small(a, shapes):
    flat = a.reshape(-1)
    out, off = [], 0
    for shp, n in zip(shapes, SMALL_SIZES):
        out.append(flat[off:off + n].reshape(shp))
        off += n
    return out


def kernel(x, c, positions, w_ada, b_ada, norm_w, w_in, conv_w, conv_b, conv_ln_w, conv_ln_b, w_conv_out, q_norm_w, w_uq, kv_norm_w, w_ukv, w_attn_out, w_out, final_norm_w, loss_target, m_w_ada, m_b_ada, m_norm_w, m_w_in, m_conv_w, m_conv_b, m_conv_ln_w, m_conv_ln_b, m_w_conv_out, m_q_norm_w, m_w_uq, m_kv_norm_w, m_w_ukv, m_w_attn_out, m_w_out, m_final_norm_w, v_w_ada, v_b_ada, v_norm_w, v_w_in, v_conv_w, v_conv_b, v_conv_ln_w, v_conv_ln_b, v_w_conv_out, v_q_norm_w, v_w_uq, v_kv_norm_w, v_w_ukv, v_w_attn_out, v_w_out, v_final_norm_w):
    me = 4 * lax.axis_index("x") + 2 * lax.axis_index("y") + lax.axis_index("c")
    xs, tgt = x[0], loss_target[0]
    s = xs.shape[0]
    ada_cols = w_ada.shape[2]

    sharded = lambda t: tuple(a[0] for a in t)
    slab_first, slab_sq = _pack_shard(*sharded((w_in, w_conv_out, w_attn_out, w_out, w_uq, w_ukv, conv_w)))
    wt, wuq, wukv, cw = _unpack_gathered(_all_gather(slab_first, "gather_weights"))
    cw32 = jnp.pad(cw.astype(F32), ((0, KPAD - KCONV), (0, 0)))

    c_all = _all_gather(jnp.broadcast_to(c, (8, D)), "gather_c")[:, 0, :]
    b_cols = lax.dynamic_slice(b_ada, (0, me * ada_cols), (1, ada_cols))
    mod_cols = _all_gather(_ada_mod(c_all, w_ada[0], b_cols), "gather_mod")
    mod = lax.dynamic_index_in_dim(mod_cols, me, axis=1, keepdims=False).reshape(3, D)
    mod8 = jnp.pad(mod, ((0, 5), (0, 0)))

    slab_sq, mod8 = lax.optimization_barrier((slab_sq, mod8))
    sq_send, sq_recv, sq_src, sq_land, sq_token = _exchange_start(slab_sq, (N_DEV,) + slab_sq.shape, _direct_copies, N_DEV - 1,
                                                                  "square_gather_start")

    pos = positions.reshape(s, 1)
    inv_freq = (ROPE_THETA ** (-jnp.arange(0, ROPE, 2, dtype=F32) / ROPE)).reshape(1, HALF)
    proj, h = _in_proj(xs, norm_w + sq_token[0:1, 0:1], mod8, wt)
    u1, ya = _conv_fwd(proj, cw32, conv_b, conv_ln_w, conv_ln_b)
    q, k, v = _mla_prep(proj, pos, inv_freq, q_norm_w, kv_norm_w, wuq, wukv)
    o, lse = _attn_fwd(q, k, v)

    slab_sq, gathered_sq = _exchange_wait(sq_send, sq_recv, sq_src, sq_land, lse, _direct_copies, "square_gather_wait")
    gathered_sq = lax.dynamic_update_slice(gathered_sq, slab_sq[None], (me, 0, 0))
    dya, do, delta, dp_gates, dx2, gw3, small_a = _merge_loss(xs, tgt, ya, o, proj, mod8, final_norm_w.reshape(1, D), gathered_sq)
    dq, dk, dv = _attn_bwd(q, k, v, do, lse, delta)
    dp_mla, guq, gukv, small_b = _mla_bwd(dq, dk, dv, proj, pos, inv_freq, q_norm_w, kv_norm_w, wuq, wukv)
    du1, dp_ag, small_c = _conv_rows_bwd(dya, u1, proj, conv_ln_w, conv_ln_b)
    dp_vg, gcw, small_d = _conv_bwd(du1, proj, cw32)
    dps = [dp_vg, dp_ag, dp_mla, dp_gates]
    core = lax.axis_index("c").astype(jnp.int32).reshape(1)
    chip = 2 * lax.axis_index("x") + lax.axis_index("y")

    def pair_sum(packed, tag):
        return _pair_add(core, packed, _pair_exchange(packed, "pair_exchange_" + tag), packed.shape[2] // 2, "pair_add_" + tag)

    def own_slot(half, recv):
        return lax.dynamic_update_slice(recv, lax.dynamic_slice(half, (chip, 0, 0), (1,) + half.shape[1:]), (chip, 0, 0))

    half_e = pair_sum(_pack_grads_early(gw3, guq, gukv, gcw[:KCONV]), "early")
    send_e, recv_e, half_e, land_e, token_e = _exchange_start(half_e, half_e.shape, _chip_copies, N_CHIP - 1, "chip_exchange_start_early")
    gwt = _in_proj_bwd_w(dps, h, wt.shape[0] // D, token_e)
    half_l = pair_sum(_pack_grads_late(gwt), "late")
    send_l, recv_l, half_l, land_l, token_l = _exchange_start(half_l, half_l.shape, _chip_copies, N_CHIP - 1, "chip_exchange_start_late")
    grad_x, small_e = _in_proj_bwd_x(dps, wt, xs, dx2, norm_w + token_l[0:1, 0:1], mod8)
    half_e, land_e = _exchange_wait(send_e, recv_e, half_e, land_e, small_e, _chip_copies, "chip_exchange_wait_early")
    half_l, land_l = _exchange_wait(send_l, recv_l, half_l, land_l, small_e, _chip_copies, "chip_exchange_wait_late")
    got_early, got_late = own_slot(half_e, land_e), own_slot(half_l, land_l)
    big_in = [a.T for a in _adam_w_in(got_late, w_in[0].T, m_w_in[0].T, v_w_in[0].T)]
    squares = (("w_conv_out", w_conv_out, m_w_conv_out, v_w_conv_out), ("w_attn_out", w_attn_out, m_w_attn_out, v_w_attn_out),
               ("w_out", w_out, m_w_out, v_w_out))
    big_sq = [_reduce_adam(got_early, w[0], m[0], v[0], "adam_" + nm, ROWS_SQ, ROWS_TAIL // ROWS_SQ + j) for j, (nm, w, m, v) in enumerate(squares)]
    tail = _reduce_adam(got_early, _small_slab(w_uq[0], w_ukv[0], conv_w[0]), _small_slab(m_w_uq[0], m_w_ukv[0], m_conv_w[0]),
                        _small_slab(v_w_uq[0], v_w_ukv[0], v_conv_w[0]), "adam_small_sharded", ROWS_TAIL, 0)
    tail = [_unpack_small_slab(a) for a in tail]
    big = [(big_in[i], big_sq[0][i], big_sq[1][i], big_sq[2][i], *tail[i]) for i in range(4)]

    dmod = jnp.concatenate([small_e[0], small_e[1], small_a[1]])
    payload = _pack_small([dmod, small_e[2], small_d[0], small_c[0], small_c[1], small_a[0], small_b[0], small_b[1], small_a[2, 0:1]])
    pay_all = _all_gather(payload, "gather_small")
    small_w = (b_ada, norm_w, conv_b, conv_ln_w, conv_ln_b, final_norm_w, q_norm_w, kv_norm_w)
    small_m = (m_b_ada, m_norm_w, m_conv_b, m_conv_ln_w, m_conv_ln_b, m_final_norm_w, m_q_norm_w, m_kv_norm_w)
    small_v = (v_b_ada, v_norm_w, v_conv_b, v_conv_ln_w, v_conv_ln_b, v_final_norm_w, v_q_norm_w, v_kv_norm_w)
    sm = _reduce_adam(pay_all, _pack_small(small_w), _pack_small(small_m), _pack_small(small_v), "adam_replicated", 8, 0)
    loss = sm[0].reshape(-1)[sum(SMALL_SIZES)]
    shapes = [t.shape for t in small_w]
    sm = [_unpack_small(a, shapes) for a in sm]

    dmod_all = pay_all.reshape(N_DEV, SMALL_LEN)[:, :3 * D]
    dmod_cols = lax.dynamic_slice(dmod_all, (0, me * ada_cols), (N_DEV, ada_cols))
    ada = _ada_bwd(c_all, dmod_cols, w_ada[0], m_w_ada[0], v_w_ada[0])

    def group(i):
        b_in, b_co, b_ao, b_out, b_uq, b_ukv, b_cw = big[i]
        s_bada, s_nw, s_cb, s_clw, s_clb, s_fnw, s_qnw, s_kvnw = sm[i]
        return (ada[i][None], s_bada, s_nw, b_in[None], b_cw[None], s_cb, s_clw, s_clb, b_co[None], s_qnw, b_uq[None], s_kvnw,
                b_ukv[None], b_ao[None], b_out[None], s_fnw)

    return (loss, grad_x[None], *group(0), *group(1), *group(2), *group(3))
```

```python
import functools

import jax
import jax.numpy as jnp
from jax import lax
from jax.experimental import pallas as pl
from jax.experimental.pallas import tpu as pltpu

F32 = jnp.float32
BF16 = jnp.bfloat16

D = 1024
NH = 8
NOPE = 128
ROPE = 64
HALF = ROPE // 2
DQK = NOPE + ROPE
DV = 128
QL = 256
KVL = 256
KCONV = 31
KPAD = 32
HALO = 32
IN_COLS = 6720
MLA_COLS = QL + KVL + ROPE
PROJ_COLS = 7 * D
EPS = 1e-6
ROPE_THETA = 10000.0
N_DEV = 8

ADAM_LR = 0.001
ADAM_B1 = 0.9
ADAM_B2 = 0.999
ADAM_EPS = 1e-08
ADAM_WD = 0.01
ADAM_STEP = 10

ROWS_IN = 840
ROWS_SQ = 128
ROWS_UQ = 48
ROWS_UKV = 64
ROWS_CW = 4
ROWS_IN_PAD = 896
OFF_TAIL = ROWS_IN_PAD
ROWS_TAIL = 128
OFF_SQ = OFF_TAIL + ROWS_TAIL
ROWS_PACK = OFF_SQ + 3 * ROWS_SQ
SMALL_SIZES = (3 * D, D, D, D, D, D, QL, KVL)
SMALL_COLS = 1152
SMALL_LEN = 8 * SMALL_COLS

MESH = pl.DeviceIdType.MESH
ANY = pl.BlockSpec(memory_space=pl.ANY)
V7X_VMEM_LIMIT = 56 * 1024 * 1024


def _cparams(n_axes, vmem=V7X_VMEM_LIMIT):
    return pltpu.CompilerParams(dimension_semantics=("arbitrary",) * n_axes, vmem_limit_bytes=vmem)


def _sig(x):
    return jax.nn.sigmoid(x)


def _nt(a, b):
    return lax.dot_general(a, b, (((1,), (1,)), ((), ())), preferred_element_type=F32)


def _tn(a, b):
    return lax.dot_general(a, b, (((0,), (0,)), ((), ())), preferred_element_type=F32)


def _nn(a, b):
    return jnp.dot(a, b, preferred_element_type=F32)


def _const(shape):
    return pl.BlockSpec(shape, lambda *_: (0,) * len(shape))


def _all_gather(block, name):
    r, c = block.shape

    def body(x_ref, out_ref, send_sems, recv_sems, local_sem):
        x, y, cc = lax.axis_index("x"), lax.axis_index("y"), lax.axis_index("c")
        me, sibling = (x, y, cc), (x, y, 1 - cc)
        chips = [(1 - x, y), (x, 1 - y), (1 - x, 1 - y)]

        def slot(px, py, pc):
            return out_ref.at[4 * px + 2 * py + pc]

        def copy(k, blk, to, src=None):
            return pltpu.make_async_remote_copy(
                src_ref=slot(*blk) if src is None else src, dst_ref=slot(*blk),
                send_sem=send_sems.at[k], recv_sem=recv_sems.at[k],
                device_id=to, device_id_type=MESH)

        mine = pltpu.make_async_copy(x_ref, slot(*me), local_sem)
        mine.start()
        first = [copy(0, me, sibling, src=x_ref)]
        first += [copy(1 + j, me, (*chip, cc), src=x_ref) for j, chip in enumerate(chips)]
        for cp in first:
            cp.start()
        passed = [copy(4 + j, (*chip, cc), sibling) for j, chip in enumerate(chips)]
        for j, chip in enumerate(chips):
            copy(1 + j, (*chip, cc), me).wait_recv()
            passed[j].start()
        copy(0, sibling, me).wait_recv()
        for j, chip in enumerate(chips):
            copy(4 + j, (*chip, 1 - cc), me).wait_recv()
        for cp in first + passed:
            cp.wait_send()
        mine.wait()

    return pl.pallas_call(
        body, name=name,
        out_shape=jax.ShapeDtypeStruct((N_DEV, r, c), block.dtype),
        in_specs=[ANY], out_specs=ANY,
        scratch_shapes=[pltpu.SemaphoreType.DMA((7,)), pltpu.SemaphoreType.DMA((7,)), pltpu.SemaphoreType.DMA],
    )(block)


N_CHIP = 4


def _pair_exchange(packed, name):
    _, _, r, c = packed.shape

    def body(src_ref, out_ref, send_sem, recv_sem):
        x, y, cc = lax.axis_index("x"), lax.axis_index("y"), lax.axis_index("c")
        cp = pltpu.make_async_remote_copy(
            src_ref=src_ref.at[1 - cc], dst_ref=out_ref, send_sem=send_sem, recv_sem=recv_sem,
            device_id=(x, y, 1 - cc), device_id_type=MESH)
        cp.start()
        cp.wait()

    return pl.pallas_call(
        body, name=name,
        out_shape=jax.ShapeDtypeStruct((N_CHIP, r, c), packed.dtype),
        in_specs=[ANY], out_specs=ANY,
        scratch_shapes=[pltpu.SemaphoreType.DMA, pltpu.SemaphoreType.DMA],
    )(packed)


def _pair_add(core, packed, got, tr, name):
    _, _, r, c = packed.shape

    def body(core_ref, own_ref, got_ref, out_ref):
        out_ref[...] = (own_ref[0].astype(F32) + got_ref[...].astype(F32)).astype(out_ref.dtype)

    blk = pl.BlockSpec((1, tr, c), lambda j, i, core_ref: (j, i, 0))
    return pl.pallas_call(
        body, name=name,
        grid_spec=pltpu.PrefetchScalarGridSpec(
            num_scalar_prefetch=1, grid=(N_CHIP, r // tr),
            in_specs=[pl.BlockSpec((1, 1, tr, c), lambda j, i, core_ref: (core_ref[0], j, i, 0)), blk],
            out_specs=blk),
        out_shape=jax.ShapeDtypeStruct((N_CHIP, r, c), packed.dtype),
        compiler_params=_cparams(2),
    )(core, packed, got)


HBM = pl.BlockSpec(memory_space=pltpu.HBM)
SEM = pl.BlockSpec(memory_space=pltpu.SEMAPHORE)
EFFECT = pltpu.SideEffectType.DATAFLOW_SIDE_EFFECTING


def _chip_copies(src_ref, land_ref, send_sems, recv_sems):
    x, y, cc = lax.axis_index("x"), lax.axis_index("y"), lax.axis_index("c")
    me = 2 * x + y
    copies = []
    for k in range(1, N_CHIP):
        px, py = (1 - x if k & 2 else x), (1 - y if k & 1 else y)
        copies.append(pltpu.make_async_remote_copy(
            src_ref=src_ref.at[2 * px + py], dst_ref=land_ref.at[me],
            send_sem=send_sems.at[k - 1], recv_sem=recv_sems.at[k - 1],
            device_id=(px, py, cc), device_id_type=MESH))
    return copies


def _direct_copies(src_ref, land_ref, send_sems, recv_sems):
    x, y, cc = lax.axis_index("x"), lax.axis_index("y"), lax.axis_index("c")
    me = 4 * x + 2 * y + cc
    copies = []
    for k in range(1, N_DEV):
        peer = ((1 - x if k & 4 else x), (1 - y if k & 2 else y), (1 - cc if k & 1 else cc))
        copies.append(pltpu.make_async_remote_copy(
            src_ref=src_ref, dst_ref=land_ref.at[me], send_sem=send_sems.at[k - 1], recv_sem=recv_sems.at[k - 1],
            device_id=peer, device_id_type=MESH))
    return copies


def _exchange_start(src, land_shape, copies_of, n_copies, name):
    def body(src_ref, land_ref, send_sems, recv_sems, src_thru, land_thru, token):
        for cp in copies_of(src_ref, land_ref, send_sems, recv_sems):
            cp.start()
        token[...] = jnp.zeros_like(token)

    return pl.pallas_call(
        body, name=name,
        out_shape=(pltpu.SemaphoreType.DMA((n_copies,)), pltpu.SemaphoreType.DMA((n_copies,)),
                   pltpu.HBM(src.shape, src.dtype), pltpu.HBM(land_shape, src.dtype), jax.ShapeDtypeStruct((8, 128), F32)),
        in_specs=(HBM, HBM), out_specs=(SEM, SEM, HBM, HBM, pl.BlockSpec(memory_space=pltpu.VMEM)),
        input_output_aliases={0: 2, 1: 3},
        compiler_params=pltpu.CompilerParams(has_side_effects=EFFECT),
    )(pltpu.with_memory_space_constraint(src, pltpu.HBM),
      pltpu.with_memory_space_constraint(lax.empty(land_shape, src.dtype), pltpu.HBM))


def _exchange_wait(send_sems, recv_sems, src_thru, land_thru, after, copies_of, name):
    def body(src_ref, land_ref, send_sems, recv_sems, after_ref, src_dead, got_ref):
        copies = copies_of(src_ref, land_ref, send_sems, recv_sems)
        for cp in copies:
            cp.wait_send()
        for cp in copies:
            cp.wait_recv()

    return pl.pallas_call(
        body, name=name,
        out_shape=(pltpu.HBM(src_thru.shape, src_thru.dtype), pltpu.HBM(land_thru.shape, land_thru.dtype)),
        in_specs=(HBM, HBM, SEM, SEM, ANY), out_specs=(HBM, HBM), input_output_aliases={0: 0, 1: 1},
        compiler_params=pltpu.CompilerParams(has_side_effects=EFFECT),
    )(src_thru, land_thru, send_sems, recv_sems, after)


def _adam(g, w, m, v):
    m = ADAM_B1 * m + (1.0 - ADAM_B1) * g
    v = ADAM_B2 * v + (1.0 - ADAM_B2) * (g * g)
    m_hat = m / (1.0 - ADAM_B1 ** ADAM_STEP)
    v_hat = v / (1.0 - ADAM_B2 ** ADAM_STEP)
    delta = -ADAM_LR * (m_hat / (jnp.sqrt(v_hat) + ADAM_EPS) + ADAM_WD * w)
    return delta, m, v


def _adam_w_in(parts, wt, mt, vt):
    n = parts.shape[0]
    tc = 256

    def body(p_ref, w_ref, m_ref, v_ref, g_out, d_out, m_out, v_out):
        g = p_ref[0].astype(F32)
        for j in range(1, n):
            g = g + p_ref[j].astype(F32)
        g = g[:ROWS_IN]
        delta, nm, nv = _adam(g, w_ref[...], m_ref[...], v_ref[...])
        g_out[...] = g
        d_out[...] = delta
        m_out[...] = nm
        v_out[...] = nv

    col = pl.BlockSpec((ROWS_IN, tc), lambda i: (0, i))
    return pl.pallas_call(
        body, name="adam_w_in", grid=(D // tc,),
        in_specs=[pl.BlockSpec((n, ROWS_IN_PAD, tc), lambda i: (0, 0, i)), col, col, col],
        out_specs=[col] * 4,
        out_shape=[jax.ShapeDtypeStruct((ROWS_IN, D), F32)] * 4,
        compiler_params=_cparams(1),
    )(parts, wt, mt, vt)


def _reduce_adam(parts, w, m, v, name, tr, first_block):
    n, _, c = parts.shape
    r = w.shape[0]

    def body(p_ref, w_ref, m_ref, v_ref, g_out, d_out, m_out, v_out):
        g = p_ref[0].astype(F32)
        for j in range(1, n):
            g = g + p_ref[j].astype(F32)
        delta, nm, nv = _adam(g, w_ref[...], m_ref[...], v_ref[...])
        g_out[...] = g
        d_out[...] = delta
        m_out[...] = nm
        v_out[...] = nv

    row = pl.BlockSpec((tr, c), lambda i: (i, 0))
    return pl.pallas_call(
        body, name=name, grid=(r // tr,),
        in_specs=[pl.BlockSpec((n, tr, c), lambda i: (0, first_block + i, 0)), row, row, row],
        out_specs=[row] * 4,
        out_shape=[jax.ShapeDtypeStruct((r, c), F32)] * 4,
        compiler_params=_cparams(1),
    )(parts, w, m, v)


def _ada_mod(c_all, w_ada, b_cols):
    def body(c_ref, w_ref, b_ref, o_ref):
        cv = c_ref[...]
        act = (cv * _sig(cv)).astype(BF16)
        o_ref[...] = _nn(act, w_ref[...].astype(BF16)) + b_ref[...]

    return pl.pallas_call(body, name="ada_mod", out_shape=jax.ShapeDtypeStruct((N_DEV, w_ada.shape[1]), F32))(c_all, w_ada, b_cols)


def _ada_bwd(c_all, dmod_cols, w, m, v):
    def body(c_ref, d_ref, w_ref, m_ref, v_ref, g_out, d_out, m_out, v_out):
        cv = c_ref[...]
        act = (cv * _sig(cv)).astype(BF16)
        g = _tn(act, d_ref[...].astype(BF16))
        delta, nm, nv = _adam(g, w_ref[...], m_ref[...], v_ref[...])
        g_out[...] = g
        d_out[...] = delta
        m_out[...] = nm
        v_out[...] = nv

    return pl.pallas_call(body, name="ada_bwd", out_shape=[jax.ShapeDtypeStruct(w.shape, F32)] * 4)(c_all, dmod_cols, w, m, v)


def _in_proj(x, norm_w, mod8, wt):
    s = x.shape[0]
    tm = min(2048, s)
    nk = wt.shape[0] // D

    def body(x_ref, nw_ref, mod_ref, w_ref, proj_ref, h_ref, hs_ref):
        @pl.when(pl.program_id(1) == 0)
        def _():
            xv = x_ref[...]
            rstd = lax.rsqrt(jnp.mean(xv * xv, axis=-1, keepdims=True) + EPS)
            h = (xv * rstd) * nw_ref[...] * (1.0 + mod_ref[1:2, :]) + mod_ref[0:1, :]
            hs_ref[...] = h.astype(BF16)
            h_ref[...] = hs_ref[...]

        proj_ref[...] = _nt(hs_ref[...], w_ref[...]).astype(proj_ref.dtype)

    return pl.pallas_call(
        body, name="in_proj", grid=(s // tm, nk),
        in_specs=[pl.BlockSpec((tm, D), lambda i, k: (i, 0)), _const((1, D)), _const((8, D)),
                  pl.BlockSpec((D, D), lambda i, k: (k, 0))],
        out_specs=[pl.BlockSpec((tm, D), lambda i, k: (i, k)), pl.BlockSpec((tm, D), lambda i, k: (i, 0))],
        out_shape=[jax.ShapeDtypeStruct((s, nk * D), BF16), jax.ShapeDtypeStruct((s, D), BF16)],
        scratch_shapes=[pltpu.VMEM((tm, D), BF16)],
        compiler_params=_cparams(2),
    )(x, norm_w, mod8, wt)


CONV_RC = 128
CONV_LC = 128
GW_RC = 32
SUBLANES = 8


def _shifted_taps(win_ref, weight_of, offsets, r0, lanes):
    acc = jnp.zeros((CONV_RC, CONV_LC), F32)
    for b in range(SUBLANES):
        group = [o for o in offsets if o % SUBLANES == b]
        if not group:
            continue
        rows = CONV_RC if b == 0 else CONV_RC + SUBLANES
        part = jnp.zeros((rows, CONV_LC), F32)
        for o in group:
            part = part + win_ref[pl.ds(r0 + o - b, rows), lanes] * weight_of(o)
        acc = acc + (part if b == 0 else part[b:b + CONV_RC])
    return acc


def _conv_fwd(proj, conv_w, conv_b, ln_w, ln_b):
    s = proj.shape[0]
    tm = min(256, s)
    hb = tm // HALO

    def body(av_ref, ag_ref, avh_ref, agh_ref, gate_ref, cw_ref, cb_ref, lw_ref, lb_ref, u1_ref, ya_ref, win_ref):
        i = pl.program_id(0)
        halo = avh_ref[...].astype(F32) * _sig(agh_ref[...].astype(F32))
        win_ref[0:HALO, :] = jnp.where(i > 0, halo, 0.0)
        win_ref[HALO:HALO + tm, :] = av_ref[...].astype(F32) * _sig(ag_ref[...].astype(F32))
        first = HALO - (KCONV - 1)
        for r0 in range(0, tm, CONV_RC):
            for c0 in range(0, D, CONV_LC):
                acc = _shifted_taps(win_ref, lambda o: cw_ref[o - first:o - first + 1, c0:c0 + CONV_LC],
                                    range(first, first + KCONV), r0, pl.ds(c0, CONV_LC))
                u1_ref[r0:r0 + CONV_RC, c0:c0 + CONV_LC] = acc + cb_ref[:, c0:c0 + CONV_LC]
        u1 = u1_ref[...]
        mu = jnp.mean(u1, axis=-1, keepdims=True)
        xc = u1 - mu
        var = jnp.mean(xc * xc, axis=-1, keepdims=True)
        ln = xc * lax.rsqrt(var + EPS) * lw_ref[...] + lb_ref[...]
        gate = gate_ref[...].astype(F32)
        ya_ref[...] = ((ln * _sig(ln)) * (gate * _sig(gate))).astype(BF16)

    row = lambda k: pl.BlockSpec((tm, D), lambda i: (i, k))
    prev = lambda k: pl.BlockSpec((HALO, D), lambda i: (jnp.maximum(i * hb - 1, 0), k))
    return pl.pallas_call(
        body, name="conv_fwd", grid=(s // tm,),
        in_specs=[row(0), row(1), prev(0), prev(1), row(2), _const((KPAD, D)), _const((1, D)), _const((1, D)), _const((1, D))],
        out_specs=[pl.BlockSpec((tm, D), lambda i: (i, 0))] * 2,
        out_shape=[jax.ShapeDtypeStruct((s, D), F32), jax.ShapeDtypeStruct((s, D), BF16)],
        scratch_shapes=[pltpu.VMEM((tm + HALO, D), F32)],
        compiler_params=_cparams(1),
    )(proj, proj, proj, proj, proj, conv_w, conv_b, ln_w, ln_b)


def _rope_tables(pos_ref, if_ref):
    ang = pos_ref[...].astype(F32) * if_ref[...]
    return jnp.cos(ang), jnp.sin(ang)


def _rms_parts(x):
    rstd = lax.rsqrt(jnp.mean(x * x, axis=-1, keepdims=True) + EPS)
    return x * rstd, rstd


def _mla_prep(proj, pos, inv_freq, qnw, kvnw, wuq, wukv):
    s = proj.shape[0]
    tm = min(512, s)

    def body(p_ref, pos_ref, if_ref, qnw_ref, kvnw_ref, wuq_ref, wukv_ref, q_ref, k_ref, v_ref):
        blk = p_ref[...].astype(F32)
        cos, sin = _rope_tables(pos_ref, if_ref)

        def rope(r):
            x1, x2 = r[:, :HALF], r[:, HALF:]
            return jnp.concatenate([x1 * cos - x2 * sin, x1 * sin + x2 * cos], axis=-1)

        qlat = _rms_parts(blk[:, :QL])[0] * qnw_ref[...]
        kvlat = _rms_parts(blk[:, QL:QL + KVL])[0] * kvnw_ref[...]
        q = _nn(qlat.astype(BF16), wuq_ref[...])
        kv = _nn(kvlat.astype(BF16), wukv_ref[...])
        kr = rope(blk[:, QL + KVL:MLA_COLS])
        for h in range(NH):
            qh = q[:, h * DQK:(h + 1) * DQK]
            q_ref[h] = (jnp.concatenate([qh[:, :NOPE], rope(qh[:, NOPE:])], axis=-1) * ATTN_SCALE).astype(BF16)
            k_ref[h] = jnp.concatenate([kv[:, h * 256:h * 256 + NOPE], kr], axis=-1).astype(BF16)
            v_ref[h] = kv[:, h * 256 + NOPE:(h + 1) * 256].astype(BF16)

    hm = lambda d: pl.BlockSpec((NH, tm, d), lambda i: (0, i, 0))
    return pl.pallas_call(
        body, name="mla_prep", grid=(s // tm,),
        in_specs=[pl.BlockSpec((tm, D), lambda i: (i, 3)), pl.BlockSpec((tm, 1), lambda i: (i, 0)), _const((1, HALF)),
                  _const((1, QL)), _const((1, KVL)), _const((QL, NH * DQK)), _const((KVL, NH * 256))],
        out_specs=[hm(DQK), hm(DQK), hm(DV)],
        out_shape=[jax.ShapeDtypeStruct((NH, s, DQK), BF16), jax.ShapeDtypeStruct((NH, s, DQK), BF16),
                   jax.ShapeDtypeStruct((NH, s, DV), BF16)],
        compiler_params=_cparams(1),
    )(proj, pos, inv_freq, qnw, kvnw, wuq, wukv)


ATTN_SCALE = DQK ** -0.5


def _causal_mask(s, t):
    rows = lax.broadcasted_iota(jnp.int32, (t, t), 0)
    cols = lax.broadcasted_iota(jnp.int32, (t, t), 1)
    return jnp.where(cols <= rows, s, -jnp.inf)


def _attn_tile(s):
    return min(1024, s // 2)


def _attn_fwd(q, k, v):
    nh, s, _ = q.shape
    t = _attn_tile(s)

    def body(q_ref, k_ref, v_ref, o_ref, lse_ref):
        qi = pl.program_id(1)
        qv = q_ref[0]

        def chunk(c, carry, diag):
            m, l, acc = carry
            rows = pl.ds(pl.multiple_of(c * t, t), t)
            sc = _nt(qv, k_ref[0, rows, :])
            if diag:
                sc = _causal_mask(sc, t)
            m_new = jnp.maximum(m, jnp.max(sc, axis=-1, keepdims=True))
            alpha = jnp.exp(m - m_new)
            p = jnp.exp(sc - m_new)
            l = alpha * l + jnp.sum(p, axis=-1, keepdims=True)
            acc = alpha * acc + _nn(p.astype(BF16), v_ref[0, rows, :])
            return m_new, l, acc

        init = (jnp.full((t, 1), -jnp.inf, F32), jnp.zeros((t, 1), F32), jnp.zeros((t, DV), F32))
        carry = lax.fori_loop(0, qi, lambda c, cr: chunk(c, cr, False), init)
        m, l, acc = chunk(qi, carry, True)
        o_ref[...] = acc / l
        lse_ref[0] = jnp.broadcast_to(m + jnp.log(l), (t, DV))

    head = lambda d: pl.BlockSpec((1, s, d), lambda h, i: (h, 0, 0))
    return pl.pallas_call(
        body, name="attn_fwd", grid=(nh, s // t),
        in_specs=[pl.BlockSpec((1, t, DQK), lambda h, i: (h, i, 0)), head(DQK), head(DV)],
        out_specs=[pl.BlockSpec((t, DV), lambda h, i: (i, h)), pl.BlockSpec((1, t, DV), lambda h, i: (h, i, 0))],
        out_shape=[jax.ShapeDtypeStruct((s, nh * DV), F32), jax.ShapeDtypeStruct((nh, s, DV), F32)],
        compiler_params=_cparams(2),
    )(q, k, v)


def _merge_loss(x, target, ya, o, proj, mod8, fnw, gathered):
    s = x.shape[0]
    tm = min(256, s)
    n = s // tm

    def body(x_ref, t_ref, ya_ref, o_ref, bg_ref, ga_ref, gb_ref, mod_ref, fnw_ref, wco_ref, wao_ref, wout_ref,
             dya_ref, do_ref, delta_ref, dpg_ref, dx2_ref, gw_ref, small_ref, acc_ref, cast_ref):
        i = pl.program_id(0)

        @pl.when(i == 0)
        def _():
            acc_ref[...] = jnp.zeros(acc_ref.shape, F32)
            small_ref[...] = jnp.zeros(small_ref.shape, F32)

        bg = bg_ref[...].astype(F32)
        sbg = _sig(bg)
        sb = bg * sbg
        ov = o_ref[...]
        ya = ya_ref[...]
        yb = (ov * sb).astype(BF16)
        square = lambda ref: ref[...].reshape(D, D)
        y_a = _nn(ya, square(wco_ref))
        y_b = _nn(yb, square(wao_ref))
        sa = _sig(ga_ref[...].astype(F32))
        sgb = _sig(gb_ref[...].astype(F32))
        merged = (sa * y_a + sgb * y_b).astype(BF16)
        z = _nn(merged, square(wout_ref))
        gate = mod_ref[2:3, :]
        x2 = x_ref[...] + gate * z
        xn, rstd = _rms_parts(x2)
        fnw = fnw_ref[...]
        err = xn * fnw - t_ref[...]
        loss = jnp.sum(jnp.sum(err * err, axis=-1, keepdims=True), axis=0, keepdims=True) * (0.5 / D)
        dy = err * (1.0 / D)
        small_ref[0:1, :] += jnp.sum(dy * xn, axis=0, keepdims=True)
        dxn = dy * fnw
        dx2 = rstd * (dxn - xn * jnp.mean(dxn * xn, axis=-1, keepdims=True))
        dx2_ref[...] = dx2
        small_ref[1:2, :] += jnp.sum(dx2 * z, axis=0, keepdims=True)
        small_ref[2:3, :] += jnp.broadcast_to(loss, (1, D))
        dz = (dx2 * gate).astype(BF16)
        dmerged = _nt(dz, square(wout_ref))
        acc_ref[2] += _tn(merged, dz)
        dy_a = (dmerged * sa).astype(BF16)
        dy_b = (dmerged * sgb).astype(BF16)
        dpg_ref[:, D:2 * D] = (dmerged * y_a * (sa * (1.0 - sa))).astype(BF16)
        dpg_ref[:, 2 * D:3 * D] = (dmerged * y_b * (sgb * (1.0 - sgb))).astype(BF16)
        dya_ref[...] = _nt(dy_a, square(wco_ref))
        acc_ref[0] += _tn(ya, dy_a)
        dyb = _nt(dy_b, square(wao_ref))
        acc_ref[1] += _tn(yb, dy_b)
        do = dyb * sb
        do_ref[...] = do.astype(BF16)
        dpg_ref[:, 0:D] = (dyb * ov * (sbg * (1.0 + bg * (1.0 - sbg)))).astype(BF16)
        prod = do * ov
        for h in range(NH):
            delta_ref[h] = jnp.broadcast_to(jnp.sum(prod[:, h * DV:(h + 1) * DV], axis=-1, keepdims=True), (tm, DV))

        @pl.when(i == n - 1)
        def _():
            for j in range(3):
                cast_ref[...] = acc_ref[j].astype(cast_ref.dtype)
                pltpu.sync_copy(cast_ref, gw_ref.at[j])

    row = pl.BlockSpec((tm, D), lambda i: (i, 0))
    col = lambda k: pl.BlockSpec((tm, D), lambda i: (i, k))
    wspec = lambda j: pl.BlockSpec((N_DEV, ROWS_SQ, D), lambda i: (0, j, 0), pipeline_mode=pl.Buffered(1))
    return pl.pallas_call(
        body, name="merge_loss", grid=(n,),
        in_specs=[row, row, row, row, col(4), col(5), col(6), _const((8, D)), _const((1, D)), wspec(0), wspec(1), wspec(2)],
        out_specs=[row, row, pl.BlockSpec((NH, tm, DV), lambda i: (0, i, 0)), pl.BlockSpec((tm, 3 * D), lambda i: (i, 0)),
                   row, ANY, _const((8, D))],
        out_shape=[jax.ShapeDtypeStruct((s, D), F32), jax.ShapeDtypeStruct((s, D), BF16),
                   jax.ShapeDtypeStruct((NH, s, DV), F32), jax.ShapeDtypeStruct((s, 3 * D), BF16),
                   jax.ShapeDtypeStruct((s, D), F32), jax.ShapeDtypeStruct((3, D, D), BF16),
                   jax.ShapeDtypeStruct((8, D), F32)],
        scratch_shapes=[pltpu.VMEM((3, D, D), F32), pltpu.VMEM((D, D), BF16)],
        compiler_params=_cparams(1),
    )(x, target, ya, o, proj, proj, proj, mod8, fnw, gathered, gathered, gathered)


def _attn_bwd(q, k, v, do, lse, delta):
    nh, s, _ = q.shape
    t = _attn_tile(s)
    nb = s // t

    def body(q_ref, k_ref, v_ref, do_ref, lse_ref, dl_ref, dq_ref, dk_ref, dv_ref):
        kj = pl.program_id(1)

        @pl.when(kj == 0)
        def _():
            dq_ref[...] = jnp.zeros(dq_ref.shape, F32)

        kv_, vv = k_ref[0], v_ref[0]

        def chunk(c, carry, diag):
            dk, dv = carry
            rows = pl.ds(pl.multiple_of(c * t, t), t)
            qv = q_ref[0, rows, :]
            dov = do_ref[rows, :]
            sc = _nt(qv, kv_)
            if diag:
                sc = _causal_mask(sc, t)
            p = jnp.exp(sc - lse_ref[0, rows, 0:1])
            dv = dv + _tn(p.astype(BF16), dov)
            dp = _nt(dov, vv)
            ds = (p * (dp - dl_ref[0, rows, 0:1])).astype(BF16)
            dk = dk + _tn(ds, qv)
            dq_ref[0, rows, :] += _nn(ds, kv_)
            return dk, dv

        carry = chunk(kj, (jnp.zeros((t, DQK), F32), jnp.zeros((t, DV), F32)), True)
        dk, dv = lax.fori_loop(kj + 1, nb, lambda c, cr: chunk(c, cr, False), carry)
        dk_ref[0] = dk
        dv_ref[0] = dv

    head = lambda d: pl.BlockSpec((1, s, d), lambda h, j: (h, 0, 0))
    blk = lambda d: pl.BlockSpec((1, t, d), lambda h, j: (h, j, 0))
    return pl.pallas_call(
        body, name="attn_bwd", grid=(nh, nb),
        in_specs=[head(DQK), blk(DQK), blk(DV), pl.BlockSpec((s, DV), lambda h, j: (0, h)), head(DV), head(DV)],
        out_specs=[head(DQK), blk(DQK), blk(DV)],
        out_shape=[jax.ShapeDtypeStruct((nh, s, DQK), F32), jax.ShapeDtypeStruct((nh, s, DQK), F32),
                   jax.ShapeDtypeStruct((nh, s, DV), F32)],
        compiler_params=_cparams(2),
    )(q, k, v, do, lse, delta)


def _mla_bwd(dq, dk, dv, proj, pos, inv_freq, qnw, kvnw, wuq, wukv):
    s = proj.shape[0]
    tm = min(512, s)

    def body(dq_ref, dk_ref, dv_ref, p_ref, pos_ref, if_ref, qnw_ref, kvnw_ref, wuq_ref, wukv_ref,
             dp_ref, guq_ref, gukv_ref, small_ref):
        @pl.when(pl.program_id(0) == 0)
        def _():
            guq_ref[...] = jnp.zeros(guq_ref.shape, F32)
            gukv_ref[...] = jnp.zeros(gukv_ref.shape, F32)
            small_ref[...] = jnp.zeros(small_ref.shape, F32)

        blk = p_ref[...].astype(F32)
        cos, sin = _rope_tables(pos_ref, if_ref)

        def unrope(g):
            g1, g2 = g[:, :HALF], g[:, HALF:]
            return jnp.concatenate([g1 * cos + g2 * sin, g2 * cos - g1 * sin], axis=-1)

        dq_cols, dkv_cols = [], []
        dkr = jnp.zeros((tm, ROPE), F32)
        for h in range(NH):
            dqh, dkh = dq_ref[h] * ATTN_SCALE, dk_ref[h]
            dq_cols += [dqh[:, :NOPE], unrope(dqh[:, NOPE:])]
            dkv_cols += [dkh[:, :NOPE], dv_ref[h]]
            dkr = dkr + dkh[:, NOPE:]
        dq_full = jnp.concatenate(dq_cols, axis=-1).astype(BF16)
        dkv_full = jnp.concatenate(dkv_cols, axis=-1).astype(BF16)

        def latent_bwd(c, nw_ref, d_up, w_ref, g_ref, srow):
            nrm, rstd = _rms_parts(c)
            nw = nw_ref[...]
            lat = (nrm * nw).astype(BF16)
            g_ref[...] += _tn(lat, d_up)
            dlat = _nt(d_up, w_ref[...])
            small_ref[srow:srow + 1, :] += jnp.sum(dlat * nrm, axis=0, keepdims=True)
            dn = dlat * nw
            return rstd * (dn - nrm * jnp.mean(dn * nrm, axis=-1, keepdims=True))

        dcq = latent_bwd(blk[:, :QL], qnw_ref, dq_full, wuq_ref, guq_ref, 0)
        dckv = latent_bwd(blk[:, QL:QL + KVL], kvnw_ref, dkv_full, wukv_ref, gukv_ref, 1)
        dp_ref[...] = jnp.concatenate([dcq, dckv, unrope(dkr), jnp.zeros((tm, D - MLA_COLS), F32)], axis=-1).astype(BF16)

    hm = lambda d: pl.BlockSpec((NH, tm, d), lambda i: (0, i, 0))
    return pl.pallas_call(
        body, name="mla_bwd", grid=(s // tm,),
        in_specs=[hm(DQK), hm(DQK), hm(DV), pl.BlockSpec((tm, D), lambda i: (i, 3)), pl.BlockSpec((tm, 1), lambda i: (i, 0)),
                  _const((1, HALF)), _const((1, QL)), _const((1, KVL)), _const((QL, NH * DQK)), _const((KVL, NH * 256))],
        out_specs=[pl.BlockSpec((tm, D), lambda i: (i, 0)), _const((QL, NH * DQK)), _const((KVL, NH * 256)), _const((8, QL))],
        out_shape=[jax.ShapeDtypeStruct((s, D), BF16), jax.ShapeDtypeStruct((QL, NH * DQK), F32),
                   jax.ShapeDtypeStruct((KVL, NH * 256), F32), jax.ShapeDtypeStruct((8, QL), F32)],
        compiler_params=_cparams(1),
    )(dq, dk, dv, proj, pos, inv_freq, qnw, kvnw, wuq, wukv)


def _conv_rows_bwd(dya, u1, proj, ln_w, ln_b):
    s = dya.shape[0]
    tm = min(512, s)

    def body(dya_ref, u1_ref, gate_ref, lw_ref, lb_ref, du1_ref, dag_ref, small_ref):
        @pl.when(pl.program_id(0) == 0)
        def _():
            small_ref[...] = jnp.zeros(small_ref.shape, F32)

        u1 = u1_ref[...]
        mu = jnp.mean(u1, axis=-1, keepdims=True)
        xc = u1 - mu
        rstd = lax.rsqrt(jnp.mean(xc * xc, axis=-1, keepdims=True) + EPS)
        xhat = xc * rstd
        lw = lw_ref[...]
        ln = xhat * lw + lb_ref[...]
        sl = _sig(ln)
        u2 = ln * sl
        gate = gate_ref[...].astype(F32)
        sg = _sig(gate)
        dya = dya_ref[...]
        dag_ref[...] = (dya * u2 * (sg * (1.0 + gate * (1.0 - sg)))).astype(BF16)
        dln = dya * (gate * sg) * (sl * (1.0 + ln * (1.0 - sl)))
        small_ref[0:1, :] += jnp.sum(dln * xhat, axis=0, keepdims=True)
        small_ref[1:2, :] += jnp.sum(dln, axis=0, keepdims=True)
        dxh = dln * lw
        du1_ref[...] = rstd * (dxh - jnp.mean(dxh, axis=-1, keepdims=True) - xhat * jnp.mean(dxh * xhat, axis=-1, keepdims=True))

    row = pl.BlockSpec((tm, D), lambda i: (i, 0))
    return pl.pallas_call(
        body, name="conv_rows_bwd", grid=(s // tm,),
        in_specs=[row, row, pl.BlockSpec((tm, D), lambda i: (i, 2)), _const((1, D)), _const((1, D))],
        out_specs=[row, row, _const((8, D))],
        out_shape=[jax.ShapeDtypeStruct((s, D), F32), jax.ShapeDtypeStruct((s, D), BF16), jax.ShapeDtypeStruct((8, D), F32)],
        compiler_params=_cparams(1),
    )(dya, u1, proj, ln_w, ln_b)


def _conv_bwd(du1, proj, conv_w):
    s = du1.shape[0]
    tm = min(256, s)
    hb = tm // HALO
    n = s // tm
    last32 = s // HALO - 1

    def body(d_ref, dn_ref, av_ref, ag_ref, avh_ref, agh_ref, cw_ref, dp_ref, gcw_ref, small_ref,
             dwin_ref, dpad_ref, dsh_ref, uwin_ref, acc_ref):
        i = pl.program_id(0)

        @pl.when(i == 0)
        def _():
            acc_ref[...] = jnp.zeros(acc_ref.shape, F32)
            small_ref[...] = jnp.zeros(small_ref.shape, F32)
            dpad_ref[...] = jnp.zeros(dpad_ref.shape, F32)
            uwin_ref[...] = jnp.zeros(uwin_ref.shape, F32)

        dv = d_ref[...]
        dwin_ref[0:tm, :] = dv
        dwin_ref[tm:tm + HALO, :] = jnp.where(i < n - 1, dn_ref[...], 0.0)
        dpad_ref[SUBLANES:SUBLANES + tm, :] = dv
        halo = avh_ref[...].astype(F32) * _sig(agh_ref[...].astype(F32))
        uwin_ref[0:HALO, :] = jnp.where(i > 0, halo, 0.0)
        av = av_ref[...].astype(F32)
        sg = _sig(ag_ref[...].astype(F32))
        uwin_ref[HALO:HALO + tm, :] = av * sg
        small_ref[0:1, :] += jnp.sum(dv, axis=0, keepdims=True)

        for b in range(SUBLANES):
            dsh_ref[b] = dpad_ref[pl.ds(SUBLANES - b, tm + SUBLANES), :]

        first = HALO - (KCONV - 1)
        for c0 in range(0, D, CONV_LC):
            lanes = pl.ds(c0, CONV_LC)
            for r0 in range(0, tm, CONV_RC):
                acc = _shifted_taps(dwin_ref, lambda o: cw_ref[KCONV - 1 - o:KCONV - o, c0:c0 + CONV_LC], range(KCONV), r0, lanes)
                a = av[r0:r0 + CONV_RC, c0:c0 + CONV_LC]
                g = sg[r0:r0 + CONV_RC, c0:c0 + CONV_LC]
                dp_ref[r0:r0 + CONV_RC, c0:c0 + CONV_LC] = (acc * g).astype(BF16)
                dp_ref[r0:r0 + CONV_RC, D + c0:D + c0 + CONV_LC] = (acc * a * (g * (1.0 - g))).astype(BF16)
            for b in range(SUBLANES):
                group = [o for o in range(first, first + KCONV) if o % SUBLANES == b]
                parts = [jnp.zeros((SUBLANES, CONV_LC), F32) for _ in group]
                chunks = [(i0, GW_RC) for i0 in range(0, tm, GW_RC)] + ([(tm, SUBLANES)] if b else [])
                for i0, rows in chunks:
                    dsh = dsh_ref[b, pl.ds(i0, rows), lanes]
                    for n_, o in enumerate(group):
                        prod = dsh * uwin_ref[pl.ds(i0 + o - b, rows), lanes]
                        parts[n_] = parts[n_] + jnp.sum(prod.reshape(rows // SUBLANES, SUBLANES, CONV_LC), axis=0)
                for n_, o in enumerate(group):
                    acc_ref[o - first, :, c0:c0 + CONV_LC] += parts[n_]

        @pl.when(i == n - 1)
        def _():
            gcw_ref[...] = jnp.sum(acc_ref[...], axis=1)

    row = lambda k: pl.BlockSpec((tm, D), lambda i: (i, k))
    prev = lambda k: pl.BlockSpec((HALO, D), lambda i: (jnp.maximum(i * hb - 1, 0), k))
    return pl.pallas_call(
        body, name="conv_bwd", grid=(n,),
        in_specs=[row(0), pl.BlockSpec((HALO, D), lambda i: (jnp.minimum((i + 1) * hb, last32), 0)),
                  row(0), row(1), prev(0), prev(1), _const((KPAD, D))],
        out_specs=[pl.BlockSpec((tm, 2 * D), lambda i: (i, 0)), _const((KPAD, D)), _const((8, D))],
        out_shape=[jax.ShapeDtypeStruct((s, 2 * D), BF16), jax.ShapeDtypeStruct((KPAD, D), F32), jax.ShapeDtypeStruct((8, D), F32)],
        scratch_shapes=[pltpu.VMEM((tm + HALO, D), F32), pltpu.VMEM((tm + 2 * SUBLANES, D), F32),
                        pltpu.VMEM((SUBLANES, tm + SUBLANES, D), F32),
                        pltpu.VMEM((tm + HALO + SUBLANES, D), F32), pltpu.VMEM((KPAD, SUBLANES, D), F32)],
        compiler_params=_cparams(1),
    )(du1, du1, proj, proj, proj, proj, conv_w)


def _dproj_specs(tm, rows_first):
    def spec(lo, hi):
        def idx(a, b):
            i, k = (a, b) if rows_first else (b, a)
            col = jnp.clip(k - lo, 0, hi - lo - 1)
            if rows_first:
                return (i, col)
            return (jnp.where((k >= lo) & (k < hi), i, 0), col)
        return pl.BlockSpec((tm, D), idx)
    return [spec(0, 2), spec(2, 3), spec(3, 4), spec(4, 7)]


def _pick_dproj(k, refs, fn):
    vg, ag, mla, gates = refs

    @pl.when(k < 2)
    def _():
        fn(vg)

    @pl.when(k == 2)
    def _():
        fn(ag)

    @pl.when(k == 3)
    def _():
        fn(mla)

    @pl.when(k > 3)
    def _():
        fn(gates)


def _in_proj_bwd_x(dps, wt, x, dx2, norm_w, mod8):
    s = x.shape[0]
    tm = min(1024, s)
    nk = wt.shape[0] // D

    def body(vg_ref, ag_ref, mla_ref, g_ref, w_ref, x_ref, dx2_ref, nw_ref, mod_ref, gx_ref, small_ref, acc_ref):
        i, k = pl.program_id(0), pl.program_id(1)

        @pl.when((i == 0) & (k == 0))
        def _():
            small_ref[...] = jnp.zeros(small_ref.shape, F32)

        @pl.when(k == 0)
        def _():
            acc_ref[...] = jnp.zeros(acc_ref.shape, F32)

        def add(ref):
            acc_ref[...] += _nn(ref[...], w_ref[...])

        _pick_dproj(k, (vg_ref, ag_ref, mla_ref, g_ref), add)

        @pl.when(k == nk - 1)
        def _():
            dh = acc_ref[...]
            xn, rstd = _rms_parts(x_ref[...])
            nw = nw_ref[...]
            hn = xn * nw
            small_ref[0:1, :] += jnp.sum(dh, axis=0, keepdims=True)
            small_ref[1:2, :] += jnp.sum(dh * hn, axis=0, keepdims=True)
            dhn = dh * (1.0 + mod_ref[1:2, :])
            small_ref[2:3, :] += jnp.sum(dhn * xn, axis=0, keepdims=True)
            dxn = dhn * nw
            gx_ref[...] = rstd * (dxn - xn * jnp.mean(dxn * xn, axis=-1, keepdims=True)) + dx2_ref[...]

    row = pl.BlockSpec((tm, D), lambda i, k: (i, 0))
    return pl.pallas_call(
        body, name="in_proj_bwd_x", grid=(s // tm, nk),
        in_specs=_dproj_specs(tm, True) + [pl.BlockSpec((D, D), lambda i, k: (k, 0)), row, row, _const((1, D)), _const((8, D))],
        out_specs=[row, _const((8, D))],
        out_shape=[jax.ShapeDtypeStruct((s, D), F32), jax.ShapeDtypeStruct((8, D), F32)],
        scratch_shapes=[pltpu.VMEM((tm, D), F32)],
        compiler_params=_cparams(2),
    )(*dps, wt, x, dx2, norm_w, mod8)


def _in_proj_bwd_w(dps, h, nk, token):
    s = h.shape[0]
    tm = min(2048, s)

    n = s // tm

    def body(vg_ref, ag_ref, mla_ref, g_ref, h_ref, token_ref, gw_ref, acc_ref):
        k, i = pl.program_id(0), pl.program_id(1)

        @pl.when(i == 0)
        def _():
            acc_ref[...] = jnp.zeros(acc_ref.shape, F32)

        def add(ref):
            acc_ref[...] += _tn(ref[...], h_ref[...])

        _pick_dproj(k, (vg_ref, ag_ref, mla_ref, g_ref), add)

        @pl.when(i == n - 1)
        def _():
            gw_ref[...] = acc_ref[...].astype(gw_ref.dtype)

    return pl.pallas_call(
        body, name="in_proj_bwd_w", grid=(nk, n),
        in_specs=_dproj_specs(tm, False) + [pl.BlockSpec((tm, D), lambda k, i: (i, 0)), _const((8, 128))],
        out_specs=pl.BlockSpec((D, D), lambda k, i: (k, 0)),
        out_shape=jax.ShapeDtypeStruct((nk * D, D), BF16),
        scratch_shapes=[pltpu.VMEM((D, D), F32)],
        compiler_params=_cparams(2),
    )(*dps, h, token)


def _small_slab(wuq, wukv, conv_w):
    cw = jnp.pad(conv_w.reshape(-1), (0, (ROWS_TAIL - ROWS_UQ - ROWS_UKV) * D - KCONV * 128)).reshape(-1, D)
    return jnp.concatenate([wuq.reshape(ROWS_UQ, D), wukv.reshape(ROWS_UKV, D), cw], axis=0)


def _split_small_slab(slab):
    return (slab[..., :ROWS_UQ, :], slab[..., ROWS_UQ:ROWS_UQ + ROWS_UKV, :],
            slab[..., ROWS_UQ + ROWS_UKV:ROWS_UQ + ROWS_UKV + ROWS_CW, :])


def _unpack_small_slab(slab):
    wuq, wukv, cw = _split_small_slab(slab)
    return (wuq.reshape(QL, NH * DQK // N_DEV), wukv.reshape(KVL, NH * 256 // N_DEV),
            cw.reshape(-1)[:KCONV * 128].reshape(KCONV, 128))


def _pack_shard(w_in, wco, wao, wout, wuq, wukv, conv_w):
    bf = lambda a: a.astype(BF16)
    first = jnp.concatenate([jnp.pad(bf(w_in).T, ((0, ROWS_IN_PAD - ROWS_IN), (0, 0))), bf(_small_slab(wuq, wukv, conv_w))], axis=0)
    return first, jnp.concatenate([bf(wco), bf(wao), bf(wout)], axis=0)


def _unpack_gathered(g):
    wt = g[:, :ROWS_IN].reshape(IN_COLS, D)
    split = 3 * D + MLA_COLS
    wt = jnp.concatenate([wt[:split], jnp.zeros((D - MLA_COLS, D), g.dtype), wt[split:]], axis=0)
    wuq, wukv, cw = _split_small_slab(g[:, OFF_TAIL:OFF_TAIL + ROWS_TAIL])
    wuq = wuq.reshape(N_DEV, QL, NH * DQK // N_DEV).transpose(1, 0, 2).reshape(QL, NH * DQK)
    wukv = wukv.reshape(N_DEV, KVL, NH * 256 // N_DEV).transpose(1, 0, 2).reshape(KVL, NH * 256)
    cw = cw.reshape(N_DEV, ROWS_CW * D)[:, :KCONV * 128].reshape(N_DEV, KCONV, 128).transpose(1, 0, 2).reshape(KCONV, D)
    return wt, wuq, wukv, cw


def _by_side(slabs):
    return slabs.reshape(N_CHIP, 2, slabs.shape[1], D).transpose(1, 0, 2, 3)


def _pack_grads_late(gwt):
    split = 3 * D + MLA_COLS
    g_in = jnp.concatenate([gwt[:split], gwt[4 * D:]], axis=0).reshape(N_DEV, ROWS_IN, D)
    return _by_side(jnp.pad(g_in, ((0, 0), (0, ROWS_IN_PAD - ROWS_IN), (0, 0))))


def _pack_grads_early(gw3, guq, gukv, gcw):
    guq = guq.reshape(QL, N_DEV, -1).transpose(1, 0, 2).reshape(N_DEV, ROWS_UQ, D)
    gukv = gukv.reshape(KVL, N_DEV, -1).transpose(1, 0, 2).reshape(N_DEV, ROWS_UKV, D)
    gcw = gcw.reshape(KCONV, N_DEV, 128).transpose(1, 0, 2).reshape(N_DEV, KCONV * 128)
    gcw = jnp.pad(gcw, ((0, 0), (0, (ROWS_TAIL - ROWS_UQ - ROWS_UKV) * D - KCONV * 128))).reshape(N_DEV, -1, D)
    gsq = gw3.reshape(3, N_DEV, ROWS_SQ, D).transpose(1, 0, 2, 3).reshape(N_DEV, 3 * ROWS_SQ, D)
    return _by_side(jnp.concatenate([a.astype(BF16) for a in (guq, gukv, gcw, gsq)], axis=1))


def _pack_small(vecs):
    flat = jnp.concatenate([v.reshape(-1) for v in vecs])
    return jnp.pad(flat, (0, SMALL_LEN - flat.shape[0])).reshape(8, SMALL_COLS)


def _unpack_small(a, shapes):
    flat = a.reshape(-1)
    out, off = [], 0
    for shp, n in zip(shapes, SMALL_SIZES):
        out.append(flat[off:off + n].reshape(shp))
        off += n
    return out


def kernel(x, c, positions, w_ada, b_ada, norm_w, w_in, conv_w, conv_b, conv_ln_w, conv_ln_b, w_conv_out, q_norm_w, w_uq, kv_norm_w, w_ukv, w_attn_out, w_out, final_norm_w, loss_target, m_w_ada, m_b_ada, m_norm_w, m_w_in, m_conv_w, m_conv_b, m_conv_ln_w, m_conv_ln_b, m_w_conv_out, m_q_norm_w, m_w_uq, m_kv_norm_w, m_w_ukv, m_w_attn_out, m_w_out, m_final_norm_w, v_w_ada, v_b_ada, v_norm_w, v_w_in, v_conv_w, v_conv_b, v_conv_ln_w, v_conv_ln_b, v_w_conv_out, v_q_norm_w, v_w_uq, v_kv_norm_w, v_w_ukv, v_w_attn_out, v_w_out, v_final_norm_w):
    me = 4 * lax.axis_index("x") + 2 * lax.axis_index("y") + lax.axis_index("c")
    xs, tgt = x[0], loss_target[0]
    s = xs.shape[0]
    ada_cols = w_ada.shape[2]

    sharded = lambda t: tuple(a[0] for a in t)
    slab_first, slab_sq = _pack_shard(*sharded((w_in, w_conv_out, w_attn_out, w_out, w_uq, w_ukv, conv_w)))
    wt, wuq, wukv, cw = _unpack_gathered(_all_gather(slab_first, "gather_weights"))
    cw32 = jnp.pad(cw.astype(F32), ((0, KPAD - KCONV), (0, 0)))

    c_all = _all_gather(jnp.broadcast_to(c, (8, D)), "gather_c")[:, 0, :]
    b_cols = lax.dynamic_slice(b_ada, (0, me * ada_cols), (1, ada_cols))
    mod_cols = _all_gather(_ada_mod(c_all, w_ada[0], b_cols), "gather_mod")
    mod = lax.dynamic_index_in_dim(mod_cols, me, axis=1, keepdims=False).reshape(3, D)
    mod8 = jnp.pad(mod, ((0, 5), (0, 0)))

    slab_sq, mod8 = lax.optimization_barrier((slab_sq, mod8))
    sq_send, sq_recv, sq_src, sq_land, sq_token = _exchange_start(slab_sq, (N_DEV,) + slab_sq.shape, _direct_copies, N_DEV - 1,
                                                                  "square_gather_start")

    pos = positions.reshape(s, 1)
    inv_freq = (ROPE_THETA ** (-jnp.arange(0, ROPE, 2, dtype=F32) / ROPE)).reshape(1, HALF)
    proj, h = _in_proj(xs, norm_w + sq_token[0:1, 0:1], mod8, wt)
    u1, ya = _conv_fwd(proj, cw32, conv_b, conv_ln_w, conv_ln_b)
    q, k, v = _mla_prep(proj, pos, inv_freq, q_norm_w, kv_norm_w, wuq, wukv)
    o, lse = _attn_fwd(q, k, v)

    slab_sq, gathered_sq = _exchange_wait(sq_send, sq_recv, sq_src, sq_land, lse, _direct_copies, "square_gather_wait")
    gathered_sq = lax.dynamic_update_slice(gathered_sq, slab_sq[None], (me, 0, 0))
    dya, do, delta, dp_gates, dx2, gw3, small_a = _merge_loss(xs, tgt, ya, o, proj, mod8, final_norm_w.reshape(1, D), gathered_sq)
    dq, dk, dv = _attn_bwd(q, k, v, do, lse, delta)
    dp_mla, guq, gukv, small_b = _mla_bwd(dq, dk, dv, proj, pos, inv_freq, q_norm_w, kv_norm_w, wuq, wukv)
    du1, dp_ag, small_c = _conv_rows_bwd(dya, u1, proj, conv_ln_w, conv_ln_b)
    dp_vg, gcw, small_d = _conv_bwd(du1, proj, cw32)
    dps = [dp_vg, dp_ag, dp_mla, dp_gates]
    core = lax.axis_index("c").astype(jnp.int32).reshape(1)
    chip = 2 * lax.axis_index("x") + lax.axis_index("y")

    def pair_sum(packed, tag):
        return _pair_add(core, packed, _pair_exchange(packed, "pair_exchange_" + tag), packed.shape[2] // 2, "pair_add_" + tag)

    def own_slot(half, recv):
        return lax.dynamic_update_slice(recv, lax.dynamic_slice(half, (chip, 0, 0), (1,) + half.shape[1:]), (chip, 0, 0))

    half_e = pair_sum(_pack_grads_early(gw3, guq, gukv, gcw[:KCONV]), "early")
    send_e, recv_e, half_e, land_e, token_e = _exchange_start(half_e, half_e.shape, _chip_copies, N_CHIP - 1, "chip_exchange_start_early")
    gwt = _in_proj_bwd_w(dps, h, wt.shape[0] // D, token_e)
    half_l = pair_sum(_pack_grads_late(gwt), "late")
    send_l, recv_l, half_l, land_l, token_l = _exchange_start(half_l, half_l.shape, _chip_copies, N_CHIP - 1, "chip_exchange_start_late")
    grad_x, small_e = _in_proj_bwd_x(dps, wt, xs, dx2, norm_w + token_l[0:1, 0:1], mod8)
    half_e, land_e = _exchange_wait(send_e, recv_e, half_e, land_e, small_e, _chip_copies, "chip_exchange_wait_early")
    half_l, land_l = _exchange_wait(send_l, recv_l, half_l, land_l, small_e, _chip_copies, "chip_exchange_wait_late")
    got_early, got_late = own_slot(half_e, land_e), own_slot(half_l, land_l)
    big_in = [a.T for a in _adam_w_in(got_late, w_in[0].T, m_w_in[0].T, v_w_in[0].T)]
    squares = (("w_conv_out", w_conv_out, m_w_conv_out, v_w_conv_out), ("w_attn_out", w_attn_out, m_w_attn_out, v_w_attn_out),
               ("w_out", w_out, m_w_out, v_w_out))
    big_sq = [_reduce_adam(got_early, w[0], m[0], v[0], "adam_" + nm, ROWS_SQ, ROWS_TAIL // ROWS_SQ + j) for j, (nm, w, m, v) in enumerate(squares)]
    tail = _reduce_adam(got_early, _small_slab(w_uq[0], w_ukv[0], conv_w[0]), _small_slab(m_w_uq[0], m_w_ukv[0], m_conv_w[0]),
                        _small_slab(v_w_uq[0], v_w_ukv[0], v_conv_w[0]), "adam_small_sharded", ROWS_TAIL, 0)
    tail = [_unpack_small_slab(a) for a in tail]
    big = [(big_in[i], big_sq[0][i], big_sq[1][i], big_sq[2][i], *tail[i]) for i in range(4)]

    dmod = jnp.concatenate([small_e[0], small_e[1], small_a[1]])
    payload = _pack_small([dmod, small_e[2], small_d[0], small_c[0], small_c[1], small_a[0], small_b[0], small_b[1], small_a[2, 0:1]])
    pay_all = _all_gather(payload, "gather_small")
    small_w = (b_ada, norm_w, conv_b, conv_ln_w, conv_ln_b, final_norm_w, q_norm_w, kv_norm_w)
    small_m = (m_b_ada, m_norm_w, m_conv_b, m_conv_ln_w, m_conv_ln_b, m_final_norm_w, m_q_norm_w, m_kv_norm_w)
    small_v = (v_b_ada, v_norm_w, v_conv_b, v_conv_ln_w, v_conv_ln_b, v_final_norm_w, v_q_norm_w, v_kv_norm_w)
    sm = _reduce_adam(pay_all, _pack_small(small_w), _pack_small(small_m), _pack_small(small_v), "adam_replicated", 8, 0)
    loss = sm[0].reshape(-1)[sum(SMALL_SIZES)]
    shapes = [t.shape for t in small_w]
    sm = [_unpack_small(a, shapes) for a in sm]

    dmod_all = pay_all.reshape(N_DEV, SMALL_LEN)[:, :3 * D]
    dmod_cols = lax.dynamic_slice(dmod_all, (0, me * ada_cols), (N_DEV, ada_cols))
    ada = _ada_bwd(c_all, dmod_cols, w_ada[0], m_w_ada[0], v_w_ada[0])

    def group(i):
        b_in, b_co, b_ao, b_out, b_uq, b_ukv, b_cw = big[i]
        s_bada, s_nw, s_cb, s_clw, s_clb, s_fnw, s_qnw, s_kvnw = sm[i]
        return (ada[i][None], s_bada, s_nw, b_in[None], b_cw[None], s_cb, s_clw, s_clb, b_co[None], s_qnw, b_uq[None], s_kvnw,
                b_ukv[None], b_ao[None], b_out[None], s_fnw)

    return (loss, grad_x[None], *group(0), *group(1), *group(2), *group(3))
```
